```python
import math, functools
import jax, jax.numpy as jnp
from jax import lax
import numpy as np

D_MODEL = 1024
BATCH = 8
SEQ = 2048
DEPTH = 1
DEC_BATCH = 128
DEC_SEQ = 4
PAST_LEN = 16384
PAGE_SIZE = 128

N_META = 16
CHUNK = 64
H_RET = 4
DK_RET = D_MODEL // 8
DV_RET = D_MODEL // 8
H_GDN = 4
DK_GDN = D_MODEL // 8
DV_GDN = D_MODEL // 8
MIX_WIDTH = H_RET * DV_RET + H_GDN * DV_GDN
GDN_QKV = 2 * H_GDN * DK_GDN + H_GDN * DV_GDN
IN_WIDTH = 2 * H_RET * DK_RET + 2 * H_RET * DV_RET + GDN_QKV + H_GDN * DV_GDN + 2 * H_GDN
CONV_GDN = 4
CONV_FFN = 3
D_FF = ((8 * D_MODEL // 3 + 127) // 128) * 128
ROPE_THETA = 10000.0
EPS = 1e-6

kernel_name = 'hybrid_retention_gdn_convffn_step'


def rms_norm(x, w):
    xf = x.astype(jnp.float32)
    y = xf * lax.rsqrt(jnp.mean(xf * xf, axis=-1, keepdims=True) + EPS)
    return (y * w.astype(jnp.float32)).astype(x.dtype)


def head_layer_norm(o, w):
    mu = jnp.mean(o, axis=-1, keepdims=True)
    c = o - mu
    var = jnp.mean(c * c, axis=-1, keepdims=True)
    return c * lax.rsqrt(var + EPS) * w.astype(jnp.float32)


def head_rms_norm(o, w):
    return o * lax.rsqrt(jnp.mean(o * o, axis=-1, keepdims=True) + EPS) * w.astype(jnp.float32)


def l2_normalize(x):
    return x * lax.rsqrt(jnp.sum(x * x, axis=-1, keepdims=True) + EPS)


def rotary(x, pos):
    half = x.shape[-1] // 2
    inv_freq = ROPE_THETA ** (-jnp.arange(half, dtype=jnp.float32) / half)
    ang = pos.astype(jnp.float32)[:, None] * inv_freq[None, :]
    cos = jnp.cos(ang)[None, :, None, :]
    sin = jnp.sin(ang)[None, :, None, :]
    xf = x.astype(jnp.float32)
    x1, x2 = xf[..., :half], xf[..., half:]
    return jnp.concatenate([x1 * cos - x2 * sin, x1 * sin + x2 * cos], axis=-1)


def causal_depthwise_conv(x, buf, w):
    width = w.shape[0]
    length = x.shape[1]
    full = jnp.concatenate([buf.astype(x.dtype), x], axis=1)
    y = full[:, 0:length] * w[0]
    for i in range(1, width):
        y = y + full[:, i:i + length] * w[i]
    return y, full[:, full.shape[1] - (width - 1):]


def retention_log_gamma():
    return jnp.log1p(-jnp.power(2.0, -5.0 - jnp.arange(H_RET, dtype=jnp.float32)))


def to_chunks(x, chunk):
    b, h, l = x.shape[:3]
    return x.reshape((b, h, l // chunk, chunk) + x.shape[3:])


def from_chunks(x):
    b, h, n, c = x.shape[:4]
    return x.reshape((b, h, n * c) + x.shape[4:])


def run_leading_then_chunks(chunk_fn, n_lead, chunk, s0, tensors):
    outs = []
    s = s0
    if n_lead > 0:
        o, s = chunk_fn(*[to_chunks(t[:, :, :n_lead], n_lead) for t in tensors], s)
        outs.append(from_chunks(o))
    o, s = chunk_fn(*[to_chunks(t[:, :, n_lead:], chunk) for t in tensors], s)
    outs.append(from_chunks(o))
    return jnp.concatenate(outs, axis=2), s


def retention_chunked(q, k, v, s0, log_gamma):
    c = q.shape[3]
    idx = jnp.arange(c, dtype=jnp.float32)
    lg = log_gamma[:, None]
    diff = idx[:, None] - idx[None, :]
    dec_intra = jnp.exp(jnp.where(diff[None] >= 0, lg[:, :, None] * diff[None], -jnp.inf))
    q_dec = jnp.exp(lg * (idx + 1.0))[None, :, None, :, None]
    k_dec = jnp.exp(lg * (c - 1.0 - idx))[None, :, None, :, None]
    chunk_dec = jnp.exp(log_gamma * c)[None, :, None, None]
    scores = jnp.einsum('bhncd,bhnsd->bhncs', q, k) * dec_intra[None, :, None]
    o_intra = jnp.einsum('bhncs,bhnse->bhnce', scores, v)
    kv = jnp.einsum('bhncd,bhnce->bhnde', k * k_dec, v)
    qd = q * q_dec

    def step(s, inp):
        qn, kvn = inp
        o = jnp.einsum('bhcd,bhde->bhce', qn, s)
        return chunk_dec * s + kvn, o

    s_fin, o_cross = lax.scan(step, s0, (jnp.moveaxis(qd, 2, 0), jnp.moveaxis(kv, 2, 0)))
    return o_intra + jnp.moveaxis(o_cross, 0, 2), s_fin


def gdn_chunked(q, k, v, g, beta, s0):
    c = q.shape[3]
    cum = jnp.cumsum(g, axis=-1)
    ii = jnp.arange(c)
    tri = ii[:, None] >= ii[None, :]
    strict = ii[:, None] > ii[None, :]
    gdiff = cum[..., :, None] - cum[..., None, :]
    dmask = jnp.exp(jnp.where(tri, gdiff, -jnp.inf))
    kk = jnp.einsum('bhncd,bhnsd->bhncs', k, k)
    a = jnp.where(strict, beta[..., :, None] * kk * dmask, 0.0)
    eye = jnp.eye(c, dtype=q.dtype)
    rhs = jnp.concatenate([v * beta[..., None], k * (beta * jnp.exp(cum))[..., None]], axis=-1)
    sol = lax.linalg.triangular_solve(a + eye, rhs, left_side=True, lower=True, unit_diagonal=True)
    u, wk = sol[..., :v.shape[-1]], sol[..., v.shape[-1]:]
    qk = jnp.where(tri, jnp.einsum('bhncd,bhnsd->bhncs', q, k) * dmask, 0.0)
    q_dec = q * jnp.exp(cum)[..., None]
    k_dec = k * jnp.exp(cum[..., -1:] - cum)[..., None]
    chunk_dec = jnp.exp(cum[..., -1])

    def step(s, inp):
        un, wkn, qdn, qkn, kdn, cdn = inp
        w = un - jnp.einsum('bhcd,bhde->bhce', wkn, s)
        o = jnp.einsum('bhcd,bhde->bhce', qdn, s) + jnp.einsum('bhcs,bhse->bhce', qkn, w)
        s = cdn[:, :, None, None] * s + jnp.einsum('bhcd,bhce->bhde', kdn, w)
        return s, o

    xs = tuple(jnp.moveaxis(t, 2, 0) for t in (u, wk, q_dec, qk, k_dec, chunk_dec))
    s_fin, o = lax.scan(step, s0, xs)
    return jnp.moveaxis(o, 0, 2), s_fin


def decoder_layer(x, pos, n_lead, chunk, s_ret, s_gdn, buf_qkv, buf_ffn,
                  norm_mix, w_in, conv_gdn, gdn_a_log, gdn_dt_bias, norm_ret, norm_gdn, w_out,
                  norm_ffn, w_up, conv_ffn, w_down):
    b, l, _ = x.shape
    f32 = jnp.float32
    h = rms_norm(x, norm_mix)
    proj = h @ w_in
    sizes = [H_RET * DK_RET, H_RET * DK_RET, H_RET * DV_RET, H_RET * DV_RET, GDN_QKV, H_GDN * DV_GDN, H_GDN]
    offsets = [int(o) for o in np.cumsum(sizes)]
    rq, rk, rv, rg, qkv, gg, gb, ga = jnp.split(proj, offsets, axis=-1)

    def to_heads(t, nh):
        return jnp.swapaxes(t.reshape(b, l, nh, -1), 1, 2).astype(f32)

    rq_h = jnp.swapaxes(rotary(rq.reshape(b, l, H_RET, DK_RET), pos), 1, 2)
    rk_h = jnp.swapaxes(rotary(rk.reshape(b, l, H_RET, DK_RET), pos), 1, 2) * (DK_RET ** -0.5)
    rv_h = to_heads(rv, H_RET)
    ret_fn = functools.partial(retention_chunked, log_gamma=retention_log_gamma())
    o_ret, s_ret_new = run_leading_then_chunks(ret_fn, n_lead, chunk, s_ret.astype(f32), [rq_h, rk_h, rv_h])
    o_ret = head_layer_norm(jnp.swapaxes(o_ret, 1, 2), norm_ret.reshape(H_RET, DV_RET))
    o_ret = o_ret.reshape(b, l, H_RET * DV_RET).astype(x.dtype) * jax.nn.silu(rg)

    qkv, buf_qkv_new = causal_depthwise_conv(qkv, buf_qkv, conv_gdn)
    qkv = jax.nn.silu(qkv)
    gq, gk, gv = jnp.split(qkv, [H_GDN * DK_GDN, 2 * H_GDN * DK_GDN], axis=-1)
    gq_h = l2_normalize(to_heads(gq, H_GDN)) * (DK_GDN ** -0.5)
    gk_h = l2_normalize(to_heads(gk, H_GDN))
    gv_h = to_heads(gv, H_GDN)
    beta = jnp.swapaxes(jax.nn.sigmoid(gb.astype(f32)), 1, 2)
    decay = -jnp.exp(gdn_a_log.astype(f32)) * jax.nn.softplus(ga.astype(f32) + gdn_dt_bias.astype(f32))
    decay = jnp.swapaxes(decay, 1, 2)
    o_gdn, s_gdn_new = run_leading_then_chunks(gdn_chunked, n_lead, chunk, s_gdn.astype(f32), [gq_h, gk_h, gv_h, decay, beta])
    o_gdn = head_rms_norm(jnp.swapaxes(o_gdn, 1, 2), norm_gdn)
    o_gdn = o_gdn.reshape(b, l, H_GDN * DV_GDN).astype(x.dtype) * jax.nn.silu(gg)

    x = x + jnp.concatenate([o_ret, o_gdn], axis=-1) @ w_out

    u, buf_ffn_new = causal_depthwise_conv(rms_norm(x, norm_ffn) @ w_up, buf_ffn, conv_ffn)
    gate, val = jnp.split(u, 2, axis=-1)
    x = x + (jax.nn.silu(gate) * val) @ w_down
    return x, s_ret_new, s_gdn_new, buf_qkv_new, buf_ffn_new


def setup_inputs(seed: int = 0) -> dict:
    key = jax.random.key(seed)
    ks = jax.random.split(key, 24)
    f32 = jnp.float32
    n = lambda k, s, sc: jax.random.normal(k, s, f32) * sc
    dt = jnp.exp(jax.random.uniform(ks[10], (DEPTH, H_GDN), f32, math.log(0.001), math.log(0.1)))
    return {
        'x_prompt': n(ks[0], (BATCH, SEQ, D_MODEL), 1.0),
        'x_sample': n(ks[1], (DEC_BATCH, DEC_SEQ, D_MODEL), 1.0),
        'state_ret': n(ks[2], (DEPTH, DEC_BATCH, H_RET, DK_RET, DV_RET), 1.0),
        'state_gdn': n(ks[3], (DEPTH, DEC_BATCH, H_GDN, DK_GDN, DV_GDN), DK_GDN ** -0.5),
        'state_conv_qkv': n(ks[4], (DEPTH, DEC_BATCH, CONV_GDN - 1, GDN_QKV), 1.0),
        'state_ffn_conv': n(ks[5], (DEPTH, DEC_BATCH, CONV_FFN - 1, 2 * D_FF), 1.0),
        'meta_tokens': n(ks[6], (N_META, D_MODEL), 1.0),
        'norm_mix': 1.0 + n(ks[7], (DEPTH, D_MODEL), 0.02),
        'w_in': n(ks[8], (DEPTH, D_MODEL, IN_WIDTH), D_MODEL ** -0.5),
        'conv_gdn': n(ks[9], (DEPTH, CONV_GDN, GDN_QKV), CONV_GDN ** -0.5),
        'gdn_a_log': jnp.log(jax.random.uniform(ks[11], (DEPTH, H_GDN), f32, 1.0, 16.0)),
        'gdn_dt_bias': dt + jnp.log(-jnp.expm1(-dt)),
        'norm_ret': 1.0 + n(ks[12], (DEPTH, H_RET * DV_RET), 0.02),
        'norm_gdn': 1.0 + n(ks[13], (DEPTH, DV_GDN), 0.02),
        'w_out': n(ks[14], (DEPTH, MIX_WIDTH, D_MODEL), MIX_WIDTH ** -0.5),
        'norm_ffn': 1.0 + n(ks[15], (DEPTH, D_MODEL), 0.02),
        'w_up': n(ks[16], (DEPTH, D_MODEL, 2 * D_FF), D_MODEL ** -0.5),
        'conv_ffn': n(ks[17], (DEPTH, CONV_FFN, 2 * D_FF), CONV_FFN ** -0.5),
        'w_down': n(ks[18], (DEPTH, D_FF, D_MODEL), D_FF ** -0.5),
        'norm_final': 1.0 + n(ks[19], (D_MODEL,), 0.02),
    }


def reference(x_prompt, x_sample, state_ret, state_gdn, state_conv_qkv, state_ffn_conv,
              meta_tokens, norm_mix, w_in, conv_gdn, gdn_a_log, gdn_dt_bias, norm_ret, norm_gdn, w_out,
              norm_ffn, w_up, conv_ffn, w_down, norm_final):
    bp = x_prompt.shape[0]
    meta = jnp.broadcast_to(meta_tokens[None].astype(x_prompt.dtype), (bp, N_META, D_MODEL))
    xp = jnp.concatenate([meta, x_prompt], axis=1)
    xs = x_sample
    pos_p = jnp.arange(xp.shape[1], dtype=jnp.int32)
    pos_s = PAST_LEN + jnp.arange(xs.shape[1], dtype=jnp.int32)
    p_ret, p_gdn, p_cq, p_cf = [], [], [], []
    s_ret_l, s_gdn_l, s_cq, s_cf = [], [], [], []
    for layer in range(DEPTH):
        params = (norm_mix[layer], w_in[layer], conv_gdn[layer], gdn_a_log[layer], gdn_dt_bias[layer],
                  norm_ret[layer], norm_gdn[layer], w_out[layer], norm_ffn[layer], w_up[layer],
                  conv_ffn[layer], w_down[layer])
        z_ret = jnp.zeros((bp, H_RET, DK_RET, DV_RET), jnp.float32)
        z_gdn = jnp.zeros((bp, H_GDN, DK_GDN, DV_GDN), jnp.float32)
        z_cq = jnp.zeros((bp, CONV_GDN - 1, GDN_QKV), xp.dtype)
        z_cf = jnp.zeros((bp, CONV_FFN - 1, 2 * D_FF), xp.dtype)
        xp, a, bq, c, d = decoder_layer(xp, pos_p, N_META, CHUNK, z_ret, z_gdn, z_cq, z_cf, *params)
        p_ret.append(a); p_gdn.append(bq); p_cq.append(c); p_cf.append(d)
        xs, a, bq, c, d = decoder_layer(xs, pos_s, 0, xs.shape[1], state_ret[layer], state_gdn[layer],
                                        state_conv_qkv[layer], state_ffn_conv[layer], *params)
        s_ret_l.append(a); s_gdn_l.append(bq); s_cq.append(c); s_cf.append(d)
    y_prompt = rms_norm(xp[:, N_META:], norm_final)
    y_sample = rms_norm(xs, norm_final)
    return (y_prompt, y_sample,
            jnp.stack(p_ret), jnp.stack(p_gdn), jnp.stack(p_cq), jnp.stack(p_cf),
            jnp.stack(s_ret_l), jnp.stack(s_gdn_l), jnp.stack(s_cq), jnp.stack(s_cf))
```

```python
import functools
import math

import jax
import jax.numpy as jnp
from jax import lax
from jax.experimental import pallas as pl
from jax.experimental.pallas import tpu as pltpu

F32 = jnp.float32
BF16 = jnp.bfloat16
HIGHEST = lax.Precision.HIGHEST

D_MODEL = 1024
N_META = 16
PAST_LEN = 16384
N_HEADS = 4
D_HEAD = 128
D_GRP = N_HEADS * D_HEAD
GDN_QKV = 3 * D_GRP
CONV_GDN = 4
CONV_FFN = 3
D_FF = 2816
ROPE_THETA = 10000.0
EPS = 1e-6

OFF_RQ, OFF_RK, OFF_RV, OFF_RG = 0, D_GRP, 2 * D_GRP, 3 * D_GRP
OFF_QKV = 4 * D_GRP
OFF_GG = OFF_QKV + GDN_QKV
OFF_BA = OFF_GG + D_GRP
IN_WIDTH = OFF_BA + 2 * N_HEADS
LANES = 128
SUBLANES = 8
IN_PAD = OFF_BA + LANES

PROMPT_CHUNK = 64
SAMPLE_PAD = 8
SAMPLE_GROUP = 32
VMEM_LIMIT = 56 * 1024 * 1024


def _round_up(n, m):
    return (n + m - 1) // m * m


def _mm(a, b):
    return jnp.dot(a.astype(BF16), b.astype(BF16), preferred_element_type=F32)


def _mm_nt(a, b):
    return lax.dot_general(a.astype(BF16), b.astype(BF16), (((1,), (1,)), ((), ())),
                           preferred_element_type=F32)


def _mm_tn(a, b):
    return lax.dot_general(a.astype(BF16), b.astype(BF16), (((0,), (0,)), ((), ())),
                           preferred_element_type=F32)


def _hmm(a, b):
    return jnp.dot(a, b, precision=HIGHEST, preferred_element_type=F32)


def _silu(x):
    return x * jax.nn.sigmoid(x)


def _rms(x, w):
    return x * lax.rsqrt(jnp.mean(x * x, axis=-1, keepdims=True) + EPS) * w


def _proj_kernel(x_ref, nw_ref, w_ref, o_ref):
    h = _rms(x_ref[...], nw_ref[...])
    o_ref[...] = jnp.dot(h.astype(BF16), w_ref[...], preferred_element_type=F32)


def _proj(x, norm_w, w_bf, tm):
    rows = x.shape[0]
    assert rows % tm == 0
    return pl.pallas_call(
        _proj_kernel,
        out_shape=jax.ShapeDtypeStruct((rows, IN_PAD), F32),
        grid=(rows // tm,),
        in_specs=[
            pl.BlockSpec((tm, D_MODEL), lambda i: (i, 0)),
            pl.BlockSpec((1, D_MODEL), lambda i: (0, 0)),
            pl.BlockSpec((D_MODEL, IN_PAD), lambda i: (0, 0), pipeline_mode=pl.Buffered(1)),
        ],
        out_specs=pl.BlockSpec((tm, IN_PAD), lambda i: (i, 0)),
        compiler_params=pltpu.CompilerParams(dimension_semantics=("arbitrary",),
                                             vmem_limit_bytes=VMEM_LIMIT),
        name="in_proj",
    )(x, norm_w, w_bf)


def _unit_lower_inverse(a, c):
    ri = lax.broadcasted_iota(jnp.int32, (c, c), 0)
    ci = lax.broadcasted_iota(jnp.int32, (c, c), 1)
    eye = (ri == ci).astype(F32)
    ad = jnp.where((ri // SUBLANES) == (ci // SUBLANES), a, 0.0)
    a2 = _hmm(ad, ad)
    a4 = _hmm(a2, a2)
    t = eye - ad
    t = t + _hmm(t, a2)
    t = t + _hmm(t, a4)
    s = SUBLANES
    while s < c:
        off = jnp.where(((ri // (2 * s)) == (ci // (2 * s))) & ((ri // s) != (ci // s)), a, 0.0)
        t = t - _hmm(t, _hmm(off, t))
        s *= 2
    return t


def _mixer_kernel(p_ref, sret0_ref, sgdn0_ref, cq0_ref, invf_ref, dint_ref, qdec_ref, kdec_ref, cdec_ref,
                  cw_ref, alog_ref, dtb_ref, nret_ref, ngdn_ref,
                  mix_ref, sret_ref, sgdn_ref, cq_ref, xs_ref, *, bb, c, n_valid, pos0):
    ci = pl.program_id(1)
    tail = CONV_GDN - 1
    top = SUBLANES

    @pl.when(ci == 0)
    def _():
        sret_ref[...] = sret0_ref[...]
        sgdn_ref[...] = sgdn0_ref[...]
        xs_ref[:, top - tail:top, :] = cq0_ref[...]

    row = lax.broadcasted_iota(jnp.int32, (c, LANES), 0)
    pos = (pos0 + ci * c + row).astype(F32)
    ang = pos * invf_ref[...]
    lane = lax.broadcasted_iota(jnp.int32, (c, LANES), 1)
    cos2 = jnp.cos(ang)
    sin2 = jnp.where(lane < D_HEAD // 2, -jnp.sin(ang), jnp.sin(ang))

    ri = lax.broadcasted_iota(jnp.int32, (c, c), 0)
    cj = lax.broadcasted_iota(jnp.int32, (c, c), 1)
    tri = ri >= cj
    strict = ri > cj
    tril_ones = tri.astype(F32)
    all_ones = jnp.ones((c, c), F32)
    if n_valid < c:
        rowmask = (row < n_valid).astype(F32)
    scale = D_HEAD ** -0.5

    for b in range(bb):
        for h in range(N_HEADS):
            lo = h * D_HEAD
            q = p_ref[b, :, OFF_RQ + lo:OFF_RQ + lo + D_HEAD]
            k = p_ref[b, :, OFF_RK + lo:OFF_RK + lo + D_HEAD]
            v = p_ref[b, :, OFF_RV + lo:OFF_RV + lo + D_HEAD]
            gate = p_ref[b, :, OFF_RG + lo:OFF_RG + lo + D_HEAD]
            qr = q * cos2 + pltpu.roll(q, D_HEAD // 2, 1) * sin2
            kr = (k * cos2 + pltpu.roll(k, D_HEAD // 2, 1) * sin2) * scale
            s = sret_ref[b, h]
            scores = _mm_nt(qr, kr) * dint_ref[h]
            o = _mm(scores, v) + _mm(qr * qdec_ref[h], s)
            sret_ref[b, h] = cdec_ref[h] * s + _mm_tn(kr * kdec_ref[h], v)
            mu = jnp.mean(o, axis=-1, keepdims=True)
            cen = o - mu
            var = jnp.mean(cen * cen, axis=-1, keepdims=True)
            o = cen * lax.rsqrt(var + EPS) * nret_ref[h:h + 1, :]
            mix_ref[b, :, lo:lo + D_HEAD] = (o * _silu(gate)).astype(mix_ref.dtype)

        xs_ref[b, top:top + c, :] = p_ref[b, :, OFF_QKV:OFF_QKV + GDN_QKV]
        conv = xs_ref[b, top - tail:top - tail + c, :] * cw_ref[0:1, :]
        for i in range(1, CONV_GDN):
            conv = conv + xs_ref[b, top - tail + i:top - tail + i + c, :] * cw_ref[i:i + 1, :]
        new_tail = xs_ref[b, top + n_valid - tail:top + n_valid, :]
        xs_ref[b, top - tail:top, :] = new_tail
        cq_ref[b] = new_tail
        qkv = _silu(conv)

        ba = p_ref[b, :, OFF_BA:OFF_BA + LANES]
        beta_all = jax.nn.sigmoid(ba)
        z = ba + dtb_ref[...]
        softplus = jnp.maximum(z, 0.0) + jnp.log1p(jnp.exp(-jnp.abs(z)))
        g_all = -jnp.exp(alog_ref[...]) * softplus
        if n_valid < c:
            beta_all = beta_all * rowmask
            g_all = g_all * rowmask

        for h in range(N_HEADS):
            lo = h * D_HEAD
            q = qkv[:, lo:lo + D_HEAD]
            k = qkv[:, D_GRP + lo:D_GRP + lo + D_HEAD]
            v = qkv[:, 2 * D_GRP + lo:2 * D_GRP + lo + D_HEAD]
            gate = p_ref[b, :, OFF_GG + lo:OFF_GG + lo + D_HEAD]
            q = q * lax.rsqrt(jnp.sum(q * q, axis=-1, keepdims=True) + EPS) * scale
            k = k * lax.rsqrt(jnp.sum(k * k, axis=-1, keepdims=True) + EPS)
            beta = jnp.broadcast_to(beta_all[:, h:h + 1], (c, LANES))
            gcol = jnp.broadcast_to(g_all[:, N_HEADS + h:N_HEADS + h + 1], (c, LANES))
            cum = _hmm(tril_ones, gcol)
            cum_row = _hmm(all_ones, jnp.where(cj >= ri, gcol[:, :c], 0.0))
            dmask = jnp.exp(jnp.where(tri, cum[:, :c] - cum_row, -jnp.inf))
            cum_last = cum[c - 1:c, :]
            ecum = jnp.exp(cum)
            a = jnp.where(strict, beta[:, :c] * _mm_nt(k, k) * dmask, 0.0)
            t = _unit_lower_inverse(a, c)
            u = _hmm(t, v * beta)
            wk = _hmm(t, k * (beta * ecum))
            qk = jnp.where(tri, _mm_nt(q, k) * dmask, 0.0)
            s = sgdn_ref[b, h]
            w = u - _mm(wk, s)
            o = _mm(q * ecum, s) + _mm(qk, w)
            sgdn_ref[b, h] = jnp.exp(cum_last) * s + _mm_tn(k * jnp.exp(cum_last - cum), w)
            o = o * lax.rsqrt(jnp.mean(o * o, axis=-1, keepdims=True) + EPS) * ngdn_ref[...]
            mix_ref[b, :, D_GRP + lo:D_GRP + lo + D_HEAD] = (o * _silu(gate)).astype(mix_ref.dtype)


def _retention_decay_tables(c, n_valid):
    lg = jnp.log1p(-jnp.power(2.0, -5.0 - jnp.arange(N_HEADS, dtype=F32)))[:, None]
    idx = jnp.arange(c, dtype=F32)
    diff = idx[:, None] - idx[None, :]
    dint = jnp.exp(jnp.where(diff[None] >= 0, lg[:, :, None] * diff[None], -jnp.inf))
    qdec = jnp.exp(lg * (idx + 1.0))
    kdec = jnp.where(idx[None, :] < n_valid, jnp.exp(lg * (n_valid - 1.0 - idx)), 0.0)
    cdec = jnp.exp(lg * float(n_valid))
    bc = lambda t: jnp.broadcast_to(t[:, :, None], t.shape + (LANES,))
    return dint, bc(qdec), bc(kdec), jnp.broadcast_to(cdec[:, :, None], (N_HEADS, 1, LANES))


def _mixer(proj, sret0, sgdn0, cq0, consts, *, bb, c, n_valid, pos0, shared_init):
    nb, length, _ = proj.shape
    assert nb % bb == 0 and length % c == 0
    assert not shared_init or bb == 1
    invf2, cw, alog, dtb, nret, ngdn = consts
    dint, qdec, kdec, cdec = _retention_decay_tables(c, n_valid)
    init_idx = (lambda b, i: (0, 0, 0, 0)) if shared_init else (lambda b, i: (b, 0, 0, 0))
    init_idx3 = (lambda b, i: (0, 0, 0)) if shared_init else (lambda b, i: (b, 0, 0))
    const2 = lambda shape: pl.BlockSpec(shape, lambda b, i: (0, 0))
    const3 = lambda shape: pl.BlockSpec(shape, lambda b, i: (0, 0, 0))
    state_shape = (bb, N_HEADS, D_HEAD, D_HEAD)
    tail = CONV_GDN - 1
    kern = functools.partial(_mixer_kernel, bb=bb, c=c, n_valid=n_valid, pos0=pos0)
    return pl.pallas_call(
        kern,
        out_shape=(
            jax.ShapeDtypeStruct((nb, length, D_MODEL), BF16),
            jax.ShapeDtypeStruct((nb, N_HEADS, D_HEAD, D_HEAD), F32),
            jax.ShapeDtypeStruct((nb, N_HEADS, D_HEAD, D_HEAD), F32),
            jax.ShapeDtypeStruct((nb, tail, GDN_QKV), F32),
        ),
        grid=(nb // bb, length // c),
        in_specs=[
            pl.BlockSpec((bb, c, IN_PAD), lambda b, i: (b, i, 0)),
            pl.BlockSpec(state_shape, init_idx),
            pl.BlockSpec(state_shape, init_idx),
            pl.BlockSpec((bb, tail, GDN_QKV), init_idx3),
            const2((1, LANES)),
            const3((N_HEADS, c, c)),
            const3((N_HEADS, c, LANES)),
            const3((N_HEADS, c, LANES)),
            const3((N_HEADS, 1, LANES)),
            const2((CONV_GDN, GDN_QKV)),
            const2((1, LANES)),
            const2((1, LANES)),
            const2((N_HEADS, D_HEAD)),
            const2((1, D_HEAD)),
        ],
        out_specs=(
            pl.BlockSpec((bb, c, D_MODEL), lambda b, i: (b, i, 0)),
            pl.BlockSpec(state_shape, lambda b, i: (b, 0, 0, 0)),
            pl.BlockSpec(state_shape, lambda b, i: (b, 0, 0, 0)),
            pl.BlockSpec((bb, tail, GDN_QKV), lambda b, i: (b, 0, 0)),
        ),
        scratch_shapes=[pltpu.VMEM((bb, SUBLANES + c, GDN_QKV), F32)],
        compiler_params=pltpu.CompilerParams(dimension_semantics=("arbitrary", "arbitrary"),
                                             vmem_limit_bytes=VMEM_LIMIT),
        name="mixer",
    )(proj, sret0, sgdn0, cq0, invf2, dint, qdec, kdec, cdec, cw, alog, dtb, nret, ngdn)


FFN_COL_CHUNK = D_FF // 2


def _ffn_kernel(x_ref, mix_ref, tail0_ref, wout_ref, nffn_ref, wup_ref, cw_ref, wdn_ref, nfin_ref,
                y_ref, tail_ref, full_ref, *, tm, stride):
    t = pl.program_id(1)
    carry = (CONV_FFN - 1) * stride
    base = _round_up(carry, SUBLANES)

    @pl.when(t == 0)
    def _():
        full_ref[base - carry:base, :] = tail0_ref[0]

    x1 = x_ref[...] + jnp.dot(mix_ref[...], wout_ref[...], preferred_element_type=F32)
    h = _rms(x1, nffn_ref[...]).astype(BF16)
    full_ref[base:base + tm, :] = jnp.dot(h, wup_ref[...], preferred_element_type=F32)

    def conv_cols(lo):
        acc = full_ref[base - carry:base - carry + tm, lo:lo + FFN_COL_CHUNK] * cw_ref[0:1, lo:lo + FFN_COL_CHUNK]
        for i in range(1, CONV_FFN):
            r0 = base - carry + i * stride
            acc = acc + full_ref[r0:r0 + tm, lo:lo + FFN_COL_CHUNK] * cw_ref[i:i + 1, lo:lo + FFN_COL_CHUNK]
        return acc

    x2 = x1
    for j in range(D_FF // FFN_COL_CHUNK):
        lo = j * FFN_COL_CHUNK
        act = (_silu(conv_cols(lo)) * conv_cols(D_FF + lo)).astype(BF16)
        x2 = x2 + jnp.dot(act, wdn_ref[lo:lo + FFN_COL_CHUNK, :], preferred_element_type=F32)
    y_ref[...] = _rms(x2, nfin_ref[...])

    new_tail = full_ref[base + tm - carry:base + tm, :]
    full_ref[base - carry:base, :] = new_tail
    tail_ref[0] = new_tail


def _ffn(x, mix, tail0, weights, *, nseq, tm, stride, shared_init):
    wout, nffn, wup, cw, wdn, nfin = weights
    rows = x.shape[0]
    assert rows % (nseq * tm) == 0
    nt = rows // (nseq * tm)
    carry = (CONV_FFN - 1) * stride
    base = _round_up(carry, SUBLANES)
    assert tm >= carry
    resident = lambda shape: pl.BlockSpec(shape, lambda b, t: (0, 0), pipeline_mode=pl.Buffered(1))
    small = lambda shape: pl.BlockSpec(shape, lambda b, t: (0, 0))
    tail_idx = (lambda b, t: (0, 0, 0)) if shared_init else (lambda b, t: (b, 0, 0))
    kern = functools.partial(_ffn_kernel, tm=tm, stride=stride)
    return pl.pallas_call(
        kern,
        out_shape=(
            jax.ShapeDtypeStruct((rows, D_MODEL), F32),
            jax.ShapeDtypeStruct((nseq, carry, 2 * D_FF), F32),
        ),
        grid=(nseq, nt),
        in_specs=[
            pl.BlockSpec((tm, D_MODEL), lambda b, t: (b * nt + t, 0)),
            pl.BlockSpec((tm, D_MODEL), lambda b, t: (b * nt + t, 0)),
            pl.BlockSpec((1, carry, 2 * D_FF), tail_idx),
            resident((D_MODEL, D_MODEL)),
            small((1, D_MODEL)),
            resident((D_MODEL, 2 * D_FF)),
            small((CONV_FFN, 2 * D_FF)),
            resident((D_FF, D_MODEL)),
            small((1, D_MODEL)),
        ],
        out_specs=(
            pl.BlockSpec((tm, D_MODEL), lambda b, t: (b * nt + t, 0)),
            pl.BlockSpec((1, carry, 2 * D_FF), lambda b, t: (b, 0, 0)),
        ),
        scratch_shapes=[pltpu.VMEM((base + tm, 2 * D_FF), F32)],
        compiler_params=pltpu.CompilerParams(dimension_semantics=("arbitrary", "arbitrary"),
                                             vmem_limit_bytes=VMEM_LIMIT),
        name="out_ffn",
    )(x, mix, tail0, wout, nffn, wup, cw, wdn, nfin)


def kernel(x_prompt, x_sample, state_ret, state_gdn, state_conv_qkv, state_ffn_conv, meta_tokens, norm_mix,
           w_in, conv_gdn, gdn_a_log, gdn_dt_bias, norm_ret, norm_gdn, w_out, norm_ffn, w_up, conv_ffn,
           w_down, norm_final):
    depth = w_in.shape[0]
    assert depth == 1
    nbp, seq, _ = x_prompt.shape
    nbs, dec_seq, _ = x_sample.shape
    assert dec_seq <= SAMPLE_PAD and nbs % SAMPLE_GROUP == 0 and seq % PROMPT_CHUNK == 0
    layer = 0

    w_in_bf = jnp.pad(w_in[layer].astype(BF16), ((0, 0), (0, IN_PAD - IN_WIDTH)))
    w_out_bf = w_out[layer].astype(BF16)
    w_up_bf = w_up[layer].astype(BF16)
    w_down_bf = w_down[layer].astype(BF16)
    row = lambda v: v.reshape(1, -1).astype(F32)
    half = D_HEAD // 2
    inv_freq = ROPE_THETA ** (-jnp.arange(half, dtype=F32) / half)
    invf2 = jnp.concatenate([inv_freq, inv_freq]).reshape(1, LANES)
    pad_ba = lambda v: jnp.zeros((1, LANES), F32).at[0, N_HEADS:2 * N_HEADS].set(v.astype(F32))
    mixer_consts = (invf2, conv_gdn[layer], pad_ba(gdn_a_log[layer]), pad_ba(gdn_dt_bias[layer]),
                    norm_ret[layer].reshape(N_HEADS, D_HEAD), row(norm_gdn[layer]))
    ffn_weights = (w_out_bf, row(norm_ffn[layer]), w_up_bf, conv_ffn[layer], w_down_bf, row(norm_final))
    nmix = row(norm_mix[layer])

    xs_pad = jnp.pad(x_sample, ((0, 0), (0, SAMPLE_PAD - dec_seq), (0, 0))).reshape(nbs * SAMPLE_PAD, D_MODEL)
    small_rows = jnp.concatenate([xs_pad, meta_tokens.astype(F32)], axis=0)
    n_small = small_rows.shape[0]
    proj_small = _proj(small_rows, nmix, w_in_bf, tm=n_small // 2)
    proj_s = proj_small[:nbs * SAMPLE_PAD].reshape(nbs, SAMPLE_PAD, IN_PAD)
    proj_m = proj_small[nbs * SAMPLE_PAD:].reshape(1, N_META, IN_PAD)

    zero_state = jnp.zeros((1, N_HEADS, D_HEAD, D_HEAD), F32)
    zero_cq = jnp.zeros((1, CONV_GDN - 1, GDN_QKV), F32)
    mix_m, sret_m, sgdn_m, cq_m = _mixer(proj_m, zero_state, zero_state, zero_cq, mixer_consts,
                                         bb=1, c=N_META, n_valid=N_META, pos0=0, shared_init=True)
    zero_cf = jnp.zeros((1, CONV_FFN - 1, 2 * D_FF), F32)
    _, cf_m = _ffn(meta_tokens.astype(F32), mix_m.reshape(N_META, D_MODEL), zero_cf, ffn_weights,
                   nseq=1, tm=N_META, stride=1, shared_init=True)

    mix_s, sret_s, sgdn_s, cq_s = _mixer(proj_s, state_ret[layer], state_gdn[layer], state_conv_qkv[layer],
                                         mixer_consts, bb=4, c=SAMPLE_PAD, n_valid=dec_seq, pos0=PAST_LEN,
                                         shared_init=False)
    ng = nbs // SAMPLE_GROUP
    to_tmajor = lambda a: a.reshape(ng, SAMPLE_GROUP, a.shape[1], a.shape[2]).transpose(0, 2, 1, 3)
    xs_t = to_tmajor(x_sample).reshape(nbs * dec_seq, D_MODEL)
    mix_t = to_tmajor(mix_s[:, :dec_seq]).reshape(nbs * dec_seq, D_MODEL)
    cf0_t = to_tmajor(state_ffn_conv[layer]).reshape(ng, (CONV_FFN - 1) * SAMPLE_GROUP, 2 * D_FF)
    y_s_t, cf_s_t = _ffn(xs_t, mix_t, cf0_t, ffn_weights, nseq=ng, tm=dec_seq * SAMPLE_GROUP,
                         stride=SAMPLE_GROUP, shared_init=False)
    y_sample = y_s_t.reshape(ng, dec_seq, SAMPLE_GROUP, D_MODEL).transpose(0, 2, 1, 3).reshape(nbs, dec_seq, D_MODEL)
    cf_s = cf_s_t.reshape(ng, CONV_FFN - 1, SAMPLE_GROUP, 2 * D_FF).transpose(0, 2, 1, 3).reshape(
        nbs, CONV_FFN - 1, 2 * D_FF)

    xp = x_prompt.reshape(nbp * seq, D_MODEL)
    proj_p = _proj(xp, nmix, w_in_bf, tm=512).reshape(nbp, seq, IN_PAD)
    mix_p, sret_p, sgdn_p, cq_p = _mixer(proj_p, sret_m, sgdn_m, cq_m, mixer_consts, bb=1, c=PROMPT_CHUNK,
                                         n_valid=PROMPT_CHUNK, pos0=N_META, shared_init=True)
    y_p, cf_p = _ffn(xp, mix_p.reshape(nbp * seq, D_MODEL), cf_m, ffn_weights, nseq=nbp, tm=256, stride=1,
                     shared_init=True)
    y_prompt = y_p.reshape(nbp, seq, D_MODEL)

    return (y_prompt, y_sample, sret_p[None], sgdn_p[None], cq_p[None], cf_p[None],
            sret_s[None], sgdn_s[None], cq_s[None], cf_s[None])
```

```python
import functools

import jax
import jax.numpy as jnp
from jax import lax
from jax.experimental import pallas as pl
from jax.experimental.pallas import tpu as pltpu

F32 = jnp.float32
BF16 = jnp.bfloat16

D_MODEL = 1024
N_META = 16
PAST_LEN = 16384
N_HEADS = 4
D_HEAD = 128
D_GRP = N_HEADS * D_HEAD
GDN_QKV = 3 * D_GRP
CONV_GDN = 4
CONV_FFN = 3
D_FF = 2816
ROPE_THETA = 10000.0
EPS = 1e-6

OFF_RQ, OFF_RK, OFF_RV, OFF_RG = 0, D_GRP, 2 * D_GRP, 3 * D_GRP
OFF_QKV = 4 * D_GRP
OFF_GG = OFF_QKV + GDN_QKV
OFF_BA = OFF_GG + D_GRP
IN_WIDTH = OFF_BA + 2 * N_HEADS
LANES = 128
SUBLANES = 8
IN_PAD = OFF_BA + LANES

PROMPT_CHUNK = 128
PROMPT_CHUNKS_PER_STEP = 2
SAMPLE_PAD = 8
SAMPLE_SEQS_PER_STEP = 8
SAMPLE_GROUP = 32
VMEM_LIMIT = 56 * 1024 * 1024


def _round_up(n, m):
    return (n + m - 1) // m * m


def _mm(a, b):
    return jnp.dot(a.astype(BF16), b.astype(BF16), preferred_element_type=F32)


def _mm_nt(a, b):
    return lax.dot_general(a.astype(BF16), b.astype(BF16), (((1,), (1,)), ((), ())),
                           preferred_element_type=F32)


def _mm_tn(a, b):
    return lax.dot_general(a.astype(BF16), b.astype(BF16), (((0,), (0,)), ((), ())),
                           preferred_element_type=F32)


def _split2(x):
    hi = x.astype(BF16)
    lo = (x - hi.astype(F32)).astype(BF16)
    return hi, lo


def _split3(x):
    p0 = x.astype(BF16)
    r = x - p0.astype(F32)
    p1 = r.astype(BF16)
    p2 = (r - p1.astype(F32)).astype(BF16)
    return p0, p1, p2


def _mm3(a2, b2):
    (ah, al), (bh, bl) = a2, b2
    dot = lambda x, y: jnp.dot(x, y, preferred_element_type=F32)
    return dot(ah, bh) + (dot(ah, bl) + dot(al, bh))


def _silu(x):
    return x * jax.nn.sigmoid(x)


def _rms(x, w):
    return x * lax.rsqrt(jnp.mean(x * x, axis=-1, keepdims=True) + EPS) * w


def _proj_kernel(x_ref, nw_ref, w_ref, o_ref):
    h = _rms(x_ref[...], nw_ref[...])
    o_ref[...] = jnp.dot(h.astype(BF16), w_ref[...], preferred_element_type=F32)


def _proj(x, norm_w, w_bf, tm):
    rows = x.shape[0]
    assert rows % tm == 0
    return pl.pallas_call(
        _proj_kernel,
        out_shape=jax.ShapeDtypeStruct((rows, IN_PAD), F32),
        grid=(rows // tm,),
        in_specs=[
            pl.BlockSpec((tm, D_MODEL), lambda i: (i, 0)),
            pl.BlockSpec((1, D_MODEL), lambda i: (0, 0)),
            pl.BlockSpec((D_MODEL, IN_PAD), lambda i: (0, 0), pipeline_mode=pl.Buffered(1)),
        ],
        out_specs=pl.BlockSpec((tm, IN_PAD), lambda i: (i, 0)),
        compiler_params=pltpu.CompilerParams(dimension_semantics=("arbitrary",),
                                             vmem_limit_bytes=VMEM_LIMIT),
        name="in_proj",
    )(x, norm_w, w_bf)


def _unit_lower_inverses(mats, c):
    ri = lax.broadcasted_iota(jnp.int32, (c, c), 0)
    ci = lax.broadcasted_iota(jnp.int32, (c, c), 1)
    eye = (ri == ci).astype(F32)
    diag_blk = (ri // SUBLANES) == (ci // SUBLANES)
    ad = [jnp.where(diag_blk, a, 0.0) for a in mats]
    ad_s = [_split2(x) for x in ad]
    a2 = [_mm3(s, s) for s in ad_s]
    a2_s = [_split2(x) for x in a2]
    a4_s = [_split2(_mm3(s, s)) for s in a2_s]
    t = [eye - x for x in ad]
    t = [x + _mm3(_split2(x), s) for x, s in zip(t, a2_s)]
    t = [x + _mm3(_split2(x), s) for x, s in zip(t, a4_s)]
    s = SUBLANES
    while s < c:
        level = ((ri // (2 * s)) == (ci // (2 * s))) & ((ri // s) != (ci // s))
        off_s = [_split2(jnp.where(level, a, 0.0)) for a in mats]
        t_s = [_split2(x) for x in t]
        lt_s = [_split2(_mm3(o, x)) for o, x in zip(off_s, t_s)]
        t = [x - _mm3(xs, y) for x, xs, y in zip(t, t_s, lt_s)]
        s *= 2
    return t


def _mixer_kernel(p_ref, sret0_ref, sgdn0_ref, cq0_ref, invf_ref, dint_ref, qdec_ref, kdec_ref, cdec_ref,
                  cw_ref, alog_ref, dtb_ref, nret_ref, ngdn_ref,
                  mix_ref, sret_ref, sgdn_ref, cq_ref, xs_ref, *, bb, c, nch, n_valid, pos0):
    step = pl.program_id(1)
    rows = nch * c
    tail = CONV_GDN - 1
    top = SUBLANES

    @pl.when(step == 0)
    def _():
        sret_ref[...] = sret0_ref[...]
        sgdn_ref[...] = sgdn0_ref[...]
        xs_ref[:, top - tail:top, :] = cq0_ref[...]

    row = lax.broadcasted_iota(jnp.int32, (rows, LANES), 0)
    pos = (pos0 + step * rows + row).astype(F32)
    ang = pos * invf_ref[...]
    lane = lax.broadcasted_iota(jnp.int32, (rows, LANES), 1)
    cos2 = jnp.cos(ang)
    sin2 = jnp.where(lane < D_HEAD // 2, -jnp.sin(ang), jnp.sin(ang))

    ri = lax.broadcasted_iota(jnp.int32, (c, c), 0)
    cj = lax.broadcasted_iota(jnp.int32, (c, c), 1)
    tri = ri >= cj
    strict = ri > cj
    tril_bf = tri.astype(BF16)
    triu_bf = (cj >= ri).astype(BF16)
    scale = D_HEAD ** -0.5
    heads = range(N_HEADS)
    chunks = [slice(j * c, (j + 1) * c) for j in range(nch)]

    for b in range(bb):
        s_ret = [sret_ref[b, h] for h in heads]
        for h in heads:
            lo = h * D_HEAD
            q = p_ref[b, :, OFF_RQ + lo:OFF_RQ + lo + D_HEAD]
            k = p_ref[b, :, OFF_RK + lo:OFF_RK + lo + D_HEAD]
            v = p_ref[b, :, OFF_RV + lo:OFF_RV + lo + D_HEAD]
            gate = p_ref[b, :, OFF_RG + lo:OFF_RG + lo + D_HEAD]
            qr = q * cos2 + pltpu.roll(q, D_HEAD // 2, 1) * sin2
            kr = (k * cos2 + pltpu.roll(k, D_HEAD // 2, 1) * sin2) * scale
            outs = []
            for rs in chunks:
                s = s_ret[h]
                scores = _mm_nt(qr[rs], kr[rs]) * dint_ref[h]
                outs.append(_mm(scores, v[rs]) + _mm(qr[rs] * qdec_ref[h], s))
                s_ret[h] = cdec_ref[h] * s + _mm_tn(kr[rs] * kdec_ref[h], v[rs])
            o = outs[0] if nch == 1 else jnp.concatenate(outs, axis=0)
            mu = jnp.mean(o, axis=-1, keepdims=True)
            cen = o - mu
            var = jnp.mean(cen * cen, axis=-1, keepdims=True)
            o = cen * lax.rsqrt(var + EPS) * nret_ref[h:h + 1, :]
            mix_ref[b, :, lo:lo + D_HEAD] = (o * _silu(gate)).astype(mix_ref.dtype)
            sret_ref[b, h] = s_ret[h]

        xs_ref[b, top:top + rows, :] = p_ref[b, :, OFF_QKV:OFF_QKV + GDN_QKV]
        conv = xs_ref[b, top - tail:top - tail + rows, :] * cw_ref[0:1, :]
        for i in range(1, CONV_GDN):
            conv = conv + xs_ref[b, top - tail + i:top - tail + i + rows, :] * cw_ref[i:i + 1, :]
        last = n_valid if nch == 1 else rows
        new_tail = xs_ref[b, top + last - tail:top + last, :]
        xs_ref[b, top - tail:top, :] = new_tail
        cq_ref[b] = new_tail
        qkv = _silu(conv)

        ba = p_ref[b, :, OFF_BA:OFF_BA + LANES]
        beta_all = jax.nn.sigmoid(ba)
        z = ba + dtb_ref[...]
        softplus = jnp.maximum(z, 0.0) + jnp.log1p(jnp.exp(-jnp.abs(z)))
        g_all = -jnp.exp(alog_ref[...]) * softplus
        if n_valid < c:
            rowmask = (row < n_valid).astype(F32)
            beta_all = beta_all * rowmask
            g_all = g_all * rowmask

        cum_cols, cum_rows = [], []
        for rs in chunks:
            parts = _split3(g_all[rs])
            cum_cols.append(sum(jnp.dot(tril_bf, g, preferred_element_type=F32) for g in parts))
            cum_rows.append(sum(lax.dot_general(g, triu_bf, (((0,), (0,)), ((), ())),
                                                preferred_element_type=F32) for g in parts))

        tasks = [(j, h) for j in range(nch) for h in heads]
        qs, ks, vs, betas, ecums, kdecs, cdecs, amats, qkms = [], [], [], [], [], [], [], [], []
        for j, h in tasks:
            rs = chunks[j]
            lo = h * D_HEAD
            q = qkv[rs, lo:lo + D_HEAD]
            k = qkv[rs, D_GRP + lo:D_GRP + lo + D_HEAD]
            v = qkv[rs, 2 * D_GRP + lo:2 * D_GRP + lo + D_HEAD]
            q = q * lax.rsqrt(jnp.sum(q * q, axis=-1, keepdims=True) + EPS) * scale
            k = k * lax.rsqrt(jnp.sum(k * k, axis=-1, keepdims=True) + EPS)
            beta = jnp.broadcast_to(beta_all[rs, h:h + 1], (c, LANES))
            cum = jnp.broadcast_to(cum_cols[j][:, N_HEADS + h:N_HEADS + h + 1], (c, LANES))
            cum_row = cum_rows[j][N_HEADS + h:N_HEADS + h + 1, :]
            dmask = jnp.exp(jnp.where(tri, cum[:, :c] - cum_row, -jnp.inf))
            cum_last = cum[c - 1:c, :]
            kq = _mm_nt(jnp.concatenate([k, q], axis=0), k)
            amats.append(jnp.where(strict, beta[:, :c] * kq[:c] * dmask, 0.0))
            qkms.append(jnp.where(tri, kq[c:] * dmask, 0.0))
            qs.append(q)
            ks.append(k)
            vs.append(v)
            betas.append(beta)
            ecums.append(jnp.exp(cum))
            kdecs.append(jnp.exp(cum_last - cum))
            cdecs.append(jnp.exp(cum_last))

        tinv = _unit_lower_inverses(amats, c)
        sols = []
        for i in range(len(tasks)):
            rhs = jnp.concatenate([vs[i] * betas[i], ks[i] * (betas[i] * ecums[i])], axis=1)
            sols.append(_mm3(_split2(tinv[i]), _split2(rhs)))

        s_gdn = [sgdn_ref[b, h] for h in heads]
        outs = [[] for _ in heads]
        for i, (j, h) in enumerate(tasks):
            s = s_gdn[h]
            u, wk = sols[i][:, :D_HEAD], sols[i][:, D_HEAD:]
            both = _mm(jnp.concatenate([wk, qs[i] * ecums[i]], axis=0), s)
            w = u - both[:c]
            outs[h].append(both[c:] + _mm(qkms[i], w))
            s_gdn[h] = cdecs[i] * s + _mm_tn(ks[i] * kdecs[i], w)
        for h in heads:
            lo = h * D_HEAD
            o = outs[h][0] if nch == 1 else jnp.concatenate(outs[h], axis=0)
            gate = p_ref[b, :, OFF_GG + lo:OFF_GG + lo + D_HEAD]
            o = o * lax.rsqrt(jnp.mean(o * o, axis=-1, keepdims=True) + EPS) * ngdn_ref[...]
            mix_ref[b, :, D_GRP + lo:D_GRP + lo + D_HEAD] = (o * _silu(gate)).astype(mix_ref.dtype)
            sgdn_ref[b, h] = s_gdn[h]


def _retention_decay_tables(c, n_valid):
    lg = jnp.log1p(-jnp.power(2.0, -5.0 - jnp.arange(N_HEADS, dtype=F32)))[:, None]
    idx = jnp.arange(c, dtype=F32)
    diff = idx[:, None] - idx[None, :]
    dint = jnp.exp(jnp.where(diff[None] >= 0, lg[:, :, None] * diff[None], -jnp.inf))
    qdec = jnp.exp(lg * (idx + 1.0))
    kdec = jnp.where(idx[None, :] < n_valid, jnp.exp(lg * (n_valid - 1.0 - idx)), 0.0)
    cdec = jnp.exp(lg * float(n_valid))
    bc = lambda t: jnp.broadcast_to(t[:, :, None], t.shape + (LANES,))
    return dint, bc(qdec), bc(kdec), jnp.broadcast_to(cdec[:, :, None], (N_HEADS, 1, LANES))


def _mixer(proj, sret0, sgdn0, cq0, consts, *, bb, c, nch, n_valid, pos0, shared_init):
    nb, length, _ = proj.shape
    rows = nch * c
    assert nb % bb == 0 and length % rows == 0
    assert not shared_init or bb == 1
    assert n_valid == c or nch == 1
    invf2, cw, alog, dtb, nret, ngdn = consts
    dint, qdec, kdec, cdec = _retention_decay_tables(c, n_valid)
    init_idx = (lambda b, i: (0, 0, 0, 0)) if shared_init else (lambda b, i: (b, 0, 0, 0))
    init_idx3 = (lambda b, i: (0, 0, 0)) if shared_init else (lambda b, i: (b, 0, 0))
    const2 = lambda shape: pl.BlockSpec(shape, lambda b, i: (0, 0))
    const3 = lambda shape: pl.BlockSpec(shape, lambda b, i: (0, 0, 0))
    state_shape = (bb, N_HEADS, D_HEAD, D_HEAD)
    tail = CONV_GDN - 1
    kern = functools.partial(_mixer_kernel, bb=bb, c=c, nch=nch, n_valid=n_valid, pos0=pos0)
    return pl.pallas_call(
        kern,
        out_shape=(
            jax.ShapeDtypeStruct((nb, length, D_MODEL), BF16),
            jax.ShapeDtypeStruct((nb, N_HEADS, D_HEAD, D_HEAD), F32),
            jax.ShapeDtypeStruct((nb, N_HEADS, D_HEAD, D_HEAD), F32),
            jax.ShapeDtypeStruct((nb, tail, GDN_QKV), F32),
        ),
        grid=(nb // bb, length // rows),
        in_specs=[
            pl.BlockSpec((bb, rows, IN_PAD), lambda b, i: (b, i, 0)),
            pl.BlockSpec(state_shape, init_idx),
            pl.BlockSpec(state_shape, init_idx),
            pl.BlockSpec((bb, tail, GDN_QKV), init_idx3),
            const2((1, LANES)),
            const3((N_HEADS, c, c)),
            const3((N_HEADS, c, LANES)),
            const3((N_HEADS, c, LANES)),
            const3((N_HEADS, 1, LANES)),
            const2((CONV_GDN, GDN_QKV)),
            const2((1, LANES)),
            const2((1, LANES)),
            const2((N_HEADS, D_HEAD)),
            const2((1, D_HEAD)),
        ],
        out_specs=(
            pl.BlockSpec((bb, rows, D_MODEL), lambda b, i: (b, i, 0)),
            pl.BlockSpec(state_shape, lambda b, i: (b, 0, 0, 0)),
            pl.BlockSpec(state_shape, lambda b, i: (b, 0, 0, 0)),
            pl.BlockSpec((bb, tail, GDN_QKV), lambda b, i: (b, 0, 0)),
        ),
        scratch_shapes=[pltpu.VMEM((bb, SUBLANES + rows, GDN_QKV), F32)],
        compiler_params=pltpu.CompilerParams(dimension_semantics=("arbitrary", "arbitrary"),
                                             vmem_limit_bytes=VMEM_LIMIT),
        name="mixer",
    )(proj, sret0, sgdn0, cq0, invf2, dint, qdec, kdec, cdec, cw, alog, dtb, nret, ngdn)


FFN_COL_CHUNK = D_FF // 2


def _ffn_kernel(x_ref, mix_ref, tail0_ref, wout_ref, nffn_ref, wup_ref, cw_ref, wdn_ref, nfin_ref,
                y_ref, tail_ref, full_ref, *, tm, stride):
    t = pl.program_id(1)
    carry = (CONV_FFN - 1) * stride
    base = _round_up(carry, SUBLANES)

    @pl.when(t == 0)
    def _():
        full_ref[base - carry:base, :] = tail0_ref[0]

    x1 = x_ref[...] + jnp.dot(mix_ref[...], wout_ref[...], preferred_element_type=F32)
    h = _rms(x1, nffn_ref[...]).astype(BF16)
    full_ref[base:base + tm, :] = jnp.dot(h, wup_ref[...], preferred_element_type=F32)

    def conv_cols(lo):
        acc = full_ref[base - carry:base - carry + tm, lo:lo + FFN_COL_CHUNK] * cw_ref[0:1, lo:lo + FFN_COL_CHUNK]
        for i in range(1, CONV_FFN):
            r0 = base - carry + i * stride
            acc = acc + full_ref[r0:r0 + tm, lo:lo + FFN_COL_CHUNK] * cw_ref[i:i + 1, lo:lo + FFN_COL_CHUNK]
        return acc

    x2 = x1
    for j in range(D_FF // FFN_COL_CHUNK):
        lo = j * FFN_COL_CHUNK
        act = (_silu(conv_cols(lo)) * conv_cols(D_FF + lo)).astype(BF16)
        x2 = x2 + jnp.dot(act, wdn_ref[lo:lo + FFN_COL_CHUNK, :], preferred_element_type=F32)
    y_ref[...] = _rms(x2, nfin_ref[...])

    new_tail = full_ref[base + tm - carry:base + tm, :]
    full_ref[base - carry:base, :] = new_tail
    tail_ref[0] = new_tail


def _ffn(x, mix, tail0, weights, *, nseq, tm, stride, shared_init):
    wout, nffn, wup, cw, wdn, nfin = weights
    rows = x.shape[0]
    assert rows % (nseq * tm) == 0
    nt = rows // (nseq * tm)
    carry = (CONV_FFN - 1) * stride
    base = _round_up(carry, SUBLANES)
    assert tm >= carry
    resident = lambda shape: pl.BlockSpec(shape, lambda b, t: (0, 0), pipeline_mode=pl.Buffered(1))
    small = lambda shape: pl.BlockSpec(shape, lambda b, t: (0, 0))
    tail_idx = (lambda b, t: (0, 0, 0)) if shared_init else (lambda b, t: (b, 0, 0))
    kern = functools.partial(_ffn_kernel, tm=tm, stride=stride)
    return pl.pallas_call(
        kern,
        out_shape=(
            jax.ShapeDtypeStruct((rows, D_MODEL), F32),
            jax.ShapeDtypeStruct((nseq, carry, 2 * D_FF), F32),
        ),
        grid=(nseq, nt),
        in_specs=[
            pl.BlockSpec((tm, D_MODEL), lambda b, t: (b * nt + t, 0)),
            pl.BlockSpec((tm, D_MODEL), lambda b, t: (b * nt + t, 0)),
            pl.BlockSpec((1, carry, 2 * D_FF), tail_idx),
            resident((D_MODEL, D_MODEL)),
            small((1, D_MODEL)),
            resident((D_MODEL, 2 * D_FF)),
            small((CONV_FFN, 2 * D_FF)),
            resident((D_FF, D_MODEL)),
            small((1, D_MODEL)),
        ],
        out_specs=(
            pl.BlockSpec((tm, D_MODEL), lambda b, t: (b * nt + t, 0)),
            pl.BlockSpec((1, carry, 2 * D_FF), lambda b, t: (b, 0, 0)),
        ),
        scratch_shapes=[pltpu.VMEM((base + tm, 2 * D_FF), F32)],
        compiler_params=pltpu.CompilerParams(dimension_semantics=("arbitrary", "arbitrary"),
                                             vmem_limit_bytes=VMEM_LIMIT),
        name="out_ffn",
    )(x, mix, tail0, wout, nffn, wup, cw, wdn, nfin)


def kernel(x_prompt, x_sample, state_ret, state_gdn, state_conv_qkv, state_ffn_conv, meta_tokens, norm_mix,
           w_in, conv_gdn, gdn_a_log, gdn_dt_bias, norm_ret, norm_gdn, w_out, norm_ffn, w_up, conv_ffn,
           w_down, norm_final):
    depth = w_in.shape[0]
    assert depth == 1
    nbp, seq, _ = x_prompt.shape
    nbs, dec_seq, _ = x_sample.shape
    assert dec_seq <= SAMPLE_PAD and nbs % SAMPLE_GROUP == 0
    assert seq % (PROMPT_CHUNK * PROMPT_CHUNKS_PER_STEP) == 0
    layer = 0

    w_in_bf = jnp.pad(w_in[layer].astype(BF16), ((0, 0), (0, IN_PAD - IN_WIDTH)))
    w_out_bf = w_out[layer].astype(BF16)
    w_up_bf = w_up[layer].astype(BF16)
    w_down_bf = w_down[layer].astype(BF16)
    row = lambda v: v.reshape(1, -1).astype(F32)
    half = D_HEAD // 2
    inv_freq = ROPE_THETA ** (-jnp.arange(half, dtype=F32) / half)
    invf2 = jnp.concatenate([inv_freq, inv_freq]).reshape(1, LANES)
    pad_ba = lambda v: jnp.zeros((1, LANES), F32).at[0, N_HEADS:2 * N_HEADS].set(v.astype(F32))
    mixer_consts = (invf2, conv_gdn[layer], pad_ba(gdn_a_log[layer]), pad_ba(gdn_dt_bias[layer]),
                    norm_ret[layer].reshape(N_HEADS, D_HEAD), row(norm_gdn[layer]))
    ffn_weights = (w_out_bf, row(norm_ffn[layer]), w_up_bf, conv_ffn[layer], w_down_bf, row(norm_final))
    nmix = row(norm_mix[layer])

    xs_pad = jnp.pad(x_sample, ((0, 0), (0, SAMPLE_PAD - dec_seq), (0, 0))).reshape(nbs * SAMPLE_PAD, D_MODEL)
    small_rows = jnp.concatenate([xs_pad, meta_tokens.astype(F32)], axis=0)
    n_small = small_rows.shape[0]
    proj_small = _proj(small_rows, nmix, w_in_bf, tm=n_small // 2)
    proj_s = proj_small[:nbs * SAMPLE_PAD].reshape(nbs, SAMPLE_PAD, IN_PAD)
    proj_m = proj_small[nbs * SAMPLE_PAD:].reshape(1, N_META, IN_PAD)

    zero_state = jnp.zeros((1, N_HEADS, D_HEAD, D_HEAD), F32)
    zero_cq = jnp.zeros((1, CONV_GDN - 1, GDN_QKV), F32)
    mix_m, sret_m, sgdn_m, cq_m = _mixer(proj_m, zero_state, zero_state, zero_cq, mixer_consts,
                                         bb=1, c=N_META, nch=1, n_valid=N_META, pos0=0, shared_init=True)
    zero_cf = jnp.zeros((1, CONV_FFN - 1, 2 * D_FF), F32)
    _, cf_m = _ffn(meta_tokens.astype(F32), mix_m.reshape(N_META, D_MODEL), zero_cf, ffn_weights,
                   nseq=1, tm=N_META, stride=1, shared_init=True)

    mix_s, sret_s, sgdn_s, cq_s = _mixer(proj_s, state_ret[layer], state_gdn[layer], state_conv_qkv[layer],
                                         mixer_consts, bb=SAMPLE_SEQS_PER_STEP, c=SAMPLE_PAD, nch=1,
                                         n_valid=dec_seq, pos0=PAST_LEN, shared_init=False)
    ng = nbs // SAMPLE_GROUP
    to_tmajor = lambda a: a.reshape(ng, SAMPLE_GROUP, a.shape[1], a.shape[2]).transpose(0, 2, 1, 3)
    xs_t = to_tmajor(x_sample).reshape(nbs * dec_seq, D_MODEL)
    mix_t = to_tmajor(mix_s[:, :dec_seq]).reshape(nbs * dec_seq, D_MODEL)
    cf0_t = to_tmajor(state_ffn_conv[layer]).reshape(ng, (CONV_FFN - 1) * SAMPLE_GROUP, 2 * D_FF)
    y_s_t, cf_s_t = _ffn(xs_t, mix_t, cf0_t, ffn_weights, nseq=ng, tm=dec_seq * SAMPLE_GROUP,
                         stride=SAMPLE_GROUP, shared_init=False)
    y_sample = y_s_t.reshape(ng, dec_seq, SAMPLE_GROUP, D_MODEL).transpose(0, 2, 1, 3).reshape(nbs, dec_seq, D_MODEL)
    cf_s = cf_s_t.reshape(ng, CONV_FFN - 1, SAMPLE_GROUP, 2 * D_FF).transpose(0, 2, 1, 3).reshape(
        nbs, CONV_FFN - 1, 2 * D_FF)

    xp = x_prompt.reshape(nbp * seq, D_MODEL)
    proj_p = _proj(xp, nmix, w_in_bf, tm=512).reshape(nbp, seq, IN_PAD)
    mix_p, sret_p, sgdn_p, cq_p = _mixer(proj_p, sret_m, sgdn_m, cq_m, mixer_consts, bb=1, c=PROMPT_CHUNK,
                                         nch=PROMPT_CHUNKS_PER_STEP, n_valid=PROMPT_CHUNK, pos0=N_META,
                                         shared_init=True)
    y_p, cf_p = _ffn(xp, mix_p.reshape(nbp * seq, D_MODEL), cf_m, ffn_weights, nseq=nbp, tm=256, stride=1,
                     shared_init=True)
    y_prompt = y_p.reshape(nbp, seq, D_MODEL)

    return (y_prompt, y_sample, sret_p[None], sgdn_p[None], cq_p[None], cf_p[None],
            sret_s[None], sgdn_s[None], cq_s[None], cf_s[None])
```

```python
import functools

import jax
import jax.numpy as jnp
from jax import lax
from jax.experimental import pallas as pl
from jax.experimental.pallas import tpu as pltpu

F32 = jnp.float32
BF16 = jnp.bfloat16

D_MODEL = 1024
N_META = 16
PAST_LEN = 16384
N_HEADS = 4
D_HEAD = 128
D_GRP = N_HEADS * D_HEAD
GDN_QKV = 3 * D_GRP
CONV_GDN = 4
CONV_FFN = 3
D_FF = 2816
ROPE_THETA = 10000.0
EPS = 1e-6

OFF_RQ, OFF_RK, OFF_RV, OFF_RG = 0, D_GRP, 2 * D_GRP, 3 * D_GRP
OFF_QKV = 4 * D_GRP
OFF_GG = OFF_QKV + GDN_QKV
OFF_BA = OFF_GG + D_GRP
IN_WIDTH = OFF_BA + 2 * N_HEADS
LANES = 128
SUBLANES = 8
IN_PAD = OFF_BA + LANES

PROMPT_CHUNK = 128
PROMPT_CHUNKS_PER_STEP = 2
SAMPLE_PAD = 8
SAMPLE_SEQS_PER_STEP = 8
SAMPLE_GROUP = 32
VMEM_LIMIT = 56 * 1024 * 1024


def _round_up(n, m):
    return (n + m - 1) // m * m


def _mm(a, b):
    return jnp.dot(a.astype(BF16), b.astype(BF16), preferred_element_type=F32)


def _mm_nt(a, b):
    return lax.dot_general(a.astype(BF16), b.astype(BF16), (((1,), (1,)), ((), ())),
                           preferred_element_type=F32)


def _mm_tn(a, b):
    return lax.dot_general(a.astype(BF16), b.astype(BF16), (((0,), (0,)), ((), ())),
                           preferred_element_type=F32)


def _split2(x):
    hi = x.astype(BF16)
    lo = (x - hi.astype(F32)).astype(BF16)
    return hi, lo


def _split3(x):
    p0 = x.astype(BF16)
    r = x - p0.astype(F32)
    p1 = r.astype(BF16)
    p2 = (r - p1.astype(F32)).astype(BF16)
    return p0, p1, p2


def _mm3(a2, b2):
    (ah, al), (bh, bl) = a2, b2
    dot = lambda x, y: jnp.dot(x, y, preferred_element_type=F32)
    return dot(ah, bh) + (dot(ah, bl) + dot(al, bh))


def _silu(x):
    return x * jax.nn.sigmoid(x)


def _rms(x, w):
    return x * lax.rsqrt(jnp.mean(x * x, axis=-1, keepdims=True) + EPS) * w


def _rope_kernel(pos_ref, invf_ref, cos_ref, sin_ref):
    ang = pos_ref[...] * invf_ref[...]
    lane = lax.broadcasted_iota(jnp.int32, ang.shape, 1)
    sin = jnp.sin(ang)
    cos_ref[...] = jnp.cos(ang)
    sin_ref[...] = jnp.where(lane < D_HEAD // 2, -sin, sin)


def _rope_tables(pos):
    n = pos.shape[0]
    half = D_HEAD // 2
    inv_freq = ROPE_THETA ** (-jnp.arange(half, dtype=F32) / half)
    invf2 = jnp.concatenate([inv_freq, inv_freq]).reshape(1, LANES)
    pos_b = jnp.broadcast_to(pos.astype(F32)[:, None], (n, LANES))
    out = jax.ShapeDtypeStruct((n, LANES), F32)
    return pl.pallas_call(_rope_kernel, out_shape=(out, out), name="rope_tables")(pos_b, invf2)


def _proj_kernel(x_ref, nw_ref, w_ref, o_ref):
    h = _rms(x_ref[...], nw_ref[...])
    o_ref[...] = jnp.dot(h.astype(BF16), w_ref[...], preferred_element_type=F32)


def _proj(x, norm_w, w_bf, tm):
    rows = x.shape[0]
    assert rows % tm == 0
    return pl.pallas_call(
        _proj_kernel,
        out_shape=jax.ShapeDtypeStruct((rows, IN_PAD), F32),
        grid=(rows // tm,),
        in_specs=[
            pl.BlockSpec((tm, D_MODEL), lambda i: (i, 0)),
            pl.BlockSpec((1, D_MODEL), lambda i: (0, 0)),
            pl.BlockSpec((D_MODEL, IN_PAD), lambda i: (0, 0), pipeline_mode=pl.Buffered(1)),
        ],
        out_specs=pl.BlockSpec((tm, IN_PAD), lambda i: (i, 0)),
        compiler_params=pltpu.CompilerParams(dimension_semantics=("arbitrary",),
                                             vmem_limit_bytes=VMEM_LIMIT),
        name="in_proj",
    )(x, norm_w, w_bf)


def _unit_lower_inverses(mats, c):
    ri = lax.broadcasted_iota(jnp.int32, (c, c), 0)
    ci = lax.broadcasted_iota(jnp.int32, (c, c), 1)
    eye = (ri == ci).astype(F32)
    diag_blk = (ri // SUBLANES) == (ci // SUBLANES)
    ad = [jnp.where(diag_blk, a, 0.0) for a in mats]
    ad_s = [_split2(x) for x in ad]
    a2 = [_mm3(s, s) for s in ad_s]
    a2_s = [_split2(x) for x in a2]
    a4_s = [_split2(_mm3(s, s)) for s in a2_s]
    t = [eye - x for x in ad]
    t = [x + _mm3(_split2(x), s) for x, s in zip(t, a2_s)]
    t = [x + _mm3(_split2(x), s) for x, s in zip(t, a4_s)]
    s = SUBLANES
    while s < c:
        level = ((ri // (2 * s)) == (ci // (2 * s))) & ((ri // s) != (ci // s))
        off_s = [_split2(jnp.where(level, a, 0.0)) for a in mats]
        t_s = [_split2(x) for x in t]
        lt_s = [_split2(_mm3(o, x)) for o, x in zip(off_s, t_s)]
        t = [x - _mm3(xs, y) for x, xs, y in zip(t, t_s, lt_s)]
        s *= 2
    return t


def _mixer_kernel(p_ref, sret0_ref, sgdn0_ref, cq0_ref, cos_ref, sin_ref, tril_ref, triu_ref,
                  dint_ref, qdec_ref, kdec_ref, cdec_ref, cw_ref, alog_ref, dtb_ref, nret_ref, ngdn_ref,
                  mix_ref, sret_ref, sgdn_ref, cq_ref, xs_ref, *, bb, c, nch, n_valid):
    step = pl.program_id(1)
    rows = nch * c
    tail = CONV_GDN - 1
    top = SUBLANES

    @pl.when(step == 0)
    def _():
        sret_ref[...] = sret0_ref[...]
        sgdn_ref[...] = sgdn0_ref[...]
        xs_ref[:, top - tail:top, :] = cq0_ref[...]

    cos2 = cos_ref[...]
    sin2 = sin_ref[...]
    ri = lax.broadcasted_iota(jnp.int32, (c, c), 0)
    cj = lax.broadcasted_iota(jnp.int32, (c, c), 1)
    tri = ri >= cj
    strict = ri > cj
    tril_bf = tril_ref[...]
    triu_bf = triu_ref[...]
    scale = D_HEAD ** -0.5
    heads = range(N_HEADS)
    seqs = range(bb)
    chunks = [slice(j * c, (j + 1) * c) for j in range(nch)]
    join = lambda parts: parts[0] if len(parts) == 1 else jnp.concatenate(parts, axis=0)

    ret = {}
    for b in seqs:
        for h in heads:
            lo = h * D_HEAD
            q = p_ref[b, :, OFF_RQ + lo:OFF_RQ + lo + D_HEAD]
            k = p_ref[b, :, OFF_RK + lo:OFF_RK + lo + D_HEAD]
            v = p_ref[b, :, OFF_RV + lo:OFF_RV + lo + D_HEAD]
            qr = q * cos2 + pltpu.roll(q, D_HEAD // 2, 1) * sin2
            kr = (k * cos2 + pltpu.roll(k, D_HEAD // 2, 1) * sin2) * scale
            ret[b, h] = (qr, kr, v)
    rtasks = [(b, h, j) for j in range(nch) for b in seqs for h in heads]
    scores = {t: _mm_nt(ret[t[0], t[1]][0][chunks[t[2]]], ret[t[0], t[1]][1][chunks[t[2]]]) * dint_ref[t[1]]
              for t in rtasks}
    intra = {t: _mm(scores[t], ret[t[0], t[1]][2][chunks[t[2]]]) for t in rtasks}
    kv = {t: _mm_tn(ret[t[0], t[1]][1][chunks[t[2]]] * kdec_ref[t[1]], ret[t[0], t[1]][2][chunks[t[2]]])
          for t in rtasks}
    pairs = [(b, h) for b in seqs for h in heads]
    s_ret = {bh: sret_ref[bh[0], bh[1]] for bh in pairs}
    o_ret = {}
    for j in range(nch):
        for b, h in pairs:
            o_ret[b, h, j] = intra[b, h, j] + _mm(ret[b, h][0][chunks[j]] * qdec_ref[h], s_ret[b, h])
        for b, h in pairs:
            s_ret[b, h] = cdec_ref[h] * s_ret[b, h] + kv[b, h, j]
    o_ret = {bh: join([o_ret[bh[0], bh[1], j] for j in range(nch)]) for bh in pairs}
    mu = {bh: jnp.mean(o_ret[bh], axis=-1, keepdims=True) for bh in pairs}
    cen = {bh: o_ret[bh] - mu[bh] for bh in pairs}
    var = {bh: jnp.mean(cen[bh] * cen[bh], axis=-1, keepdims=True) for bh in pairs}
    for b, h in pairs:
        lo = h * D_HEAD
        gate = p_ref[b, :, OFF_RG + lo:OFF_RG + lo + D_HEAD]
        o = cen[b, h] * lax.rsqrt(var[b, h] + EPS) * nret_ref[h:h + 1, :]
        mix_ref[b, :, lo:lo + D_HEAD] = (o * _silu(gate)).astype(mix_ref.dtype)
        sret_ref[b, h] = s_ret[b, h]

    qkvs, beta_alls, cum_cols, cum_rows, cum_tots = {}, {}, {}, {}, {}
    ones_bf = jnp.ones((D_HEAD, c), BF16)
    for b in seqs:
        xs_ref[b, top:top + rows, :] = p_ref[b, :, OFF_QKV:OFF_QKV + GDN_QKV]
        conv = xs_ref[b, top - tail:top - tail + rows, :] * cw_ref[0:1, :]
        for i in range(1, CONV_GDN):
            conv = conv + xs_ref[b, top - tail + i:top - tail + i + rows, :] * cw_ref[i:i + 1, :]
        last = n_valid if nch == 1 else rows
        new_tail = xs_ref[b, top + last - tail:top + last, :]
        xs_ref[b, top - tail:top, :] = new_tail
        cq_ref[b] = new_tail
        qkvs[b] = _silu(conv)

        ba = p_ref[b, :, OFF_BA:OFF_BA + LANES]
        beta_all = jax.nn.sigmoid(ba)
        z = ba + dtb_ref[...]
        softplus = jnp.maximum(z, 0.0) + jnp.log1p(jnp.exp(-jnp.abs(z)))
        g_all = -jnp.exp(alog_ref[...]) * softplus
        if n_valid < c:
            row = lax.broadcasted_iota(jnp.int32, (rows, LANES), 0)
            rowmask = (row < n_valid).astype(F32)
            beta_all = beta_all * rowmask
            g_all = g_all * rowmask
        beta_alls[b] = beta_all
        for j, rs in enumerate(chunks):
            parts = _split3(g_all[rs])
            cum_cols[b, j] = sum(jnp.dot(tril_bf, g, preferred_element_type=F32) for g in parts)
            cum_tots[b, j] = sum(jnp.dot(ones_bf, g, preferred_element_type=F32) for g in parts)
            cum_rows[b, j] = sum(lax.dot_general(g, triu_bf, (((0,), (0,)), ((), ())),
                                                 preferred_element_type=F32) for g in parts)

    tasks = [(j, b, h) for j in range(nch) for b in seqs for h in heads]
    qs, ks, vs, betas, ecums, kdecs, cdecs, amats, qkms = [], [], [], [], [], [], [], [], []
    raw = [(qkvs[b][chunks[j], h * D_HEAD:(h + 1) * D_HEAD],
            qkvs[b][chunks[j], D_GRP + h * D_HEAD:D_GRP + (h + 1) * D_HEAD]) for j, b, h in tasks]
    sumsq = [(jnp.sum(q * q, axis=-1, keepdims=True), jnp.sum(k * k, axis=-1, keepdims=True)) for q, k in raw]
    for i, (j, b, h) in enumerate(tasks):
        rs = chunks[j]
        lo = h * D_HEAD
        v = qkvs[b][rs, 2 * D_GRP + lo:2 * D_GRP + lo + D_HEAD]
        q = raw[i][0] * lax.rsqrt(sumsq[i][0] + EPS) * scale
        k = raw[i][1] * lax.rsqrt(sumsq[i][1] + EPS)
        beta = jnp.broadcast_to(beta_alls[b][rs, h:h + 1], (c, LANES))
        cum = jnp.broadcast_to(cum_cols[b, j][:, N_HEADS + h:N_HEADS + h + 1], (c, LANES))
        cum_row = cum_rows[b, j][N_HEADS + h:N_HEADS + h + 1, :]
        dmask = jnp.exp(jnp.where(tri, cum[:, :c] - cum_row, -jnp.inf))
        cum_last = jnp.broadcast_to(cum_tots[b, j][:, N_HEADS + h:N_HEADS + h + 1], (D_HEAD, LANES))
        kq =_mm_nt(jnp.concatenate([k, q], axis=0), k)
        amats.append(jnp.where(strict, beta[:, :c] * kq[:c] * dmask, 0.0))
        qkms.append(jnp.where(tri, kq[c:] * dmask, 0.0))
        qs.append(q)
        ks.append(k)
        vs.append(v)
        betas.append(beta)
        ecums.append(jnp.exp(cum))
        kdecs.append(jnp.exp(cum_last[:c] - cum))
        cdecs.append(jnp.exp(cum_last))

    tinv = _unit_lower_inverses(amats, c)
    sols = []
    for i in range(len(tasks)):
        rhs = jnp.concatenate([vs[i] * betas[i], ks[i] * (betas[i] * ecums[i])], axis=1)
        sols.append(_mm3(_split2(tinv[i]), _split2(rhs)))

    s_gdn = {bh: sgdn_ref[bh[0], bh[1]] for bh in pairs}
    o_gdn = {}
    for j in range(nch):
        idx = {tasks[i][1:]: i for i in range(len(tasks)) if tasks[i][0] == j}
        lhs = {bh: jnp.concatenate([sols[i][:, D_HEAD:], qs[i] * ecums[i]], axis=0) for bh, i in idx.items()}
        both = {bh: _mm(lhs[bh], s_gdn[bh]) for bh in idx}
        w = {bh: sols[i][:, :D_HEAD] - both[bh][:c] for bh, i in idx.items()}
        for bh, i in idx.items():
            o_gdn[bh + (j,)] = both[bh][c:] + _mm(qkms[i], w[bh])
        upd = {bh: _mm_tn(ks[i] * kdecs[i], w[bh]) for bh, i in idx.items()}
        for bh, i in idx.items():
            s_gdn[bh] = cdecs[i] * s_gdn[bh] + upd[bh]
    o_gdn = {bh: join([o_gdn[bh + (j,)] for j in range(nch)]) for bh in pairs}
    msq = {bh: jnp.mean(o_gdn[bh] * o_gdn[bh], axis=-1, keepdims=True) for bh in pairs}
    for b, h in pairs:
        lo = h * D_HEAD
        gate = p_ref[b, :, OFF_GG + lo:OFF_GG + lo + D_HEAD]
        o = o_gdn[b, h] * lax.rsqrt(msq[b, h] + EPS) * ngdn_ref[...]
        mix_ref[b, :, D_GRP + lo:D_GRP + lo + D_HEAD] = (o * _silu(gate)).astype(mix_ref.dtype)
        sgdn_ref[b, h] = s_gdn[b, h]


def _retention_decay_tables(c, n_valid):
    lg = jnp.log1p(-jnp.power(2.0, -5.0 - jnp.arange(N_HEADS, dtype=F32)))[:, None]
    idx = jnp.arange(c, dtype=F32)
    diff = idx[:, None] - idx[None, :]
    dint = jnp.exp(jnp.where(diff[None] >= 0, lg[:, :, None] * diff[None], -jnp.inf))
    qdec = jnp.exp(lg * (idx + 1.0))
    kdec = jnp.where(idx[None, :] < n_valid, jnp.exp(lg * (n_valid - 1.0 - idx)), 0.0)
    cdec = jnp.exp(lg * float(n_valid))
    bc = lambda t: jnp.broadcast_to(t[:, :, None], t.shape + (LANES,))
    return dint, bc(qdec), bc(kdec), jnp.broadcast_to(cdec[:, :, None], (N_HEADS, 1, LANES))


def _mixer(proj, sret0, sgdn0, cq0, rope, rope_row0, consts, *, bb, c, nch, n_valid, shared_init):
    nb, length, _ = proj.shape
    rows = nch * c
    assert nb % bb == 0 and length % rows == 0 and rope_row0 % rows == 0
    assert not shared_init or bb == 1
    assert n_valid == c or nch == 1
    cw, alog, dtb, nret, ngdn = consts
    dint, qdec, kdec, cdec = _retention_decay_tables(c, n_valid)
    idx = jnp.arange(c)
    tril = (idx[:, None] >= idx[None, :]).astype(BF16)
    init_idx = (lambda b, i: (0, 0, 0, 0)) if shared_init else (lambda b, i: (b, 0, 0, 0))
    init_idx3 = (lambda b, i: (0, 0, 0)) if shared_init else (lambda b, i: (b, 0, 0))
    const2 = lambda shape: pl.BlockSpec(shape, lambda b, i: (0, 0))
    const3 = lambda shape: pl.BlockSpec(shape, lambda b, i: (0, 0, 0))
    rope_spec = pl.BlockSpec((rows, LANES), lambda b, i: (rope_row0 // rows + i, 0))
    state_shape = (bb, N_HEADS, D_HEAD, D_HEAD)
    tail = CONV_GDN - 1
    kern = functools.partial(_mixer_kernel, bb=bb, c=c, nch=nch, n_valid=n_valid)
    return pl.pallas_call(
        kern,
        out_shape=(
            jax.ShapeDtypeStruct((nb, length, D_MODEL), BF16),
            jax.ShapeDtypeStruct((nb, N_HEADS, D_HEAD, D_HEAD), F32),
            jax.ShapeDtypeStruct((nb, N_HEADS, D_HEAD, D_HEAD), F32),
            jax.ShapeDtypeStruct((nb, tail, GDN_QKV), F32),
        ),
        grid=(nb // bb, length // rows),
        in_specs=[
            pl.BlockSpec((bb, rows, IN_PAD), lambda b, i: (b, i, 0)),
            pl.BlockSpec(state_shape, init_idx),
            pl.BlockSpec(state_shape, init_idx),
            pl.BlockSpec((bb, tail, GDN_QKV), init_idx3),
            rope_spec,
            rope_spec,
            const2((c, c)),
            const2((c, c)),
            const3((N_HEADS, c, c)),
            const3((N_HEADS, c, LANES)),
            const3((N_HEADS, c, LANES)),
            const3((N_HEADS, 1, LANES)),
            const2((CONV_GDN, GDN_QKV)),
            const2((1, LANES)),
            const2((1, LANES)),
            const2((N_HEADS, D_HEAD)),
            const2((1, D_HEAD)),
        ],
        out_specs=(
            pl.BlockSpec((bb, rows, D_MODEL), lambda b, i: (b, i, 0)),
            pl.BlockSpec(state_shape, lambda b, i: (b, 0, 0, 0)),
            pl.BlockSpec(state_shape, lambda b, i: (b, 0, 0, 0)),
            pl.BlockSpec((bb, tail, GDN_QKV), lambda b, i: (b, 0, 0)),
        ),
        scratch_shapes=[pltpu.VMEM((bb, SUBLANES + rows, GDN_QKV), F32)],
        compiler_params=pltpu.CompilerParams(dimension_semantics=("arbitrary", "arbitrary"),
                                             vmem_limit_bytes=VMEM_LIMIT),
        name="mixer",
    )(proj, sret0, sgdn0, cq0, rope[0], rope[1], tril, tril.T, dint, qdec, kdec, cdec, cw, alog, dtb, nret, ngdn)


FFN_COL_CHUNK = D_FF // 2


def _ffn_kernel(x_ref, mix_ref, tail0_ref, wout_ref, nffn_ref, wup_ref, cw_ref, wdn_ref, nfin_ref,
                y_ref, tail_ref, full_ref, *, tm, stride):
    t = pl.program_id(1)
    carry = (CONV_FFN - 1) * stride
    base = _round_up(carry, SUBLANES)

    @pl.when(t == 0)
    def _():
        full_ref[base - carry:base, :] = tail0_ref[0]

    x1 = x_ref[...] + jnp.dot(mix_ref[...], wout_ref[...], preferred_element_type=F32)
    h = _rms(x1, nffn_ref[...]).astype(BF16)
    full_ref[base:base + tm, :] = jnp.dot(h, wup_ref[...], preferred_element_type=F32)

    def conv_cols(lo):
        acc = full_ref[base - carry:base - carry + tm, lo:lo + FFN_COL_CHUNK] * cw_ref[0:1, lo:lo + FFN_COL_CHUNK]
        for i in range(1, CONV_FFN):
            r0 = base - carry + i * stride
            acc = acc + full_ref[r0:r0 + tm, lo:lo + FFN_COL_CHUNK] * cw_ref[i:i + 1, lo:lo + FFN_COL_CHUNK]
        return acc

    x2 = x1
    for j in range(D_FF // FFN_COL_CHUNK):
        lo = j * FFN_COL_CHUNK
        act = (_silu(conv_cols(lo)) * conv_cols(D_FF + lo)).astype(BF16)
        x2 = x2 + jnp.dot(act, wdn_ref[lo:lo + FFN_COL_CHUNK, :], preferred_element_type=F32)
    y_ref[...] = _rms(x2, nfin_ref[...])

    new_tail = full_ref[base + tm - carry:base + tm, :]
    full_ref[base - carry:base, :] = new_tail
    tail_ref[0] = new_tail


def _ffn(x, mix, tail0, weights, *, nseq, tm, stride, shared_init):
    wout, nffn, wup, cw, wdn, nfin = weights
    rows = x.shape[0]
    assert rows % (nseq * tm) == 0
    nt = rows // (nseq * tm)
    carry = (CONV_FFN - 1) * stride
    base = _round_up(carry, SUBLANES)
    assert tm >= carry
    resident = lambda shape: pl.BlockSpec(shape, lambda b, t: (0, 0), pipeline_mode=pl.Buffered(1))
    small = lambda shape: pl.BlockSpec(shape, lambda b, t: (0, 0))
    tail_idx = (lambda b, t: (0, 0, 0)) if shared_init else (lambda b, t: (b, 0, 0))
    kern = functools.partial(_ffn_kernel, tm=tm, stride=stride)
    return pl.pallas_call(
        kern,
        out_shape=(
            jax.ShapeDtypeStruct((rows, D_MODEL), F32),
            jax.ShapeDtypeStruct((nseq, carry, 2 * D_FF), F32),
        ),
        grid=(nseq, nt),
        in_specs=[
            pl.BlockSpec((tm, D_MODEL), lambda b, t: (b * nt + t, 0)),
            pl.BlockSpec((tm, D_MODEL), lambda b, t: (b * nt + t, 0)),
            pl.BlockSpec((1, carry, 2 * D_FF), tail_idx),
            resident((D_MODEL, D_MODEL)),
            small((1, D_MODEL)),
            resident((D_MODEL, 2 * D_FF)),
            small((CONV_FFN, 2 * D_FF)),
            resident((D_FF, D_MODEL)),
            small((1, D_MODEL)),
        ],
        out_specs=(
            pl.BlockSpec((tm, D_MODEL), lambda b, t: (b * nt + t, 0)),
            pl.BlockSpec((1, carry, 2 * D_FF), lambda b, t: (b, 0, 0)),
        ),
        scratch_shapes=[pltpu.VMEM((base + tm, 2 * D_FF), F32)],
        compiler_params=pltpu.CompilerParams(dimension_semantics=("arbitrary", "arbitrary"),
                                             vmem_limit_bytes=VMEM_LIMIT),
        name="out_ffn",
    )(x, mix, tail0, wout, nffn, wup, cw, wdn, nfin)


def kernel(x_prompt, x_sample, state_ret, state_gdn, state_conv_qkv, state_ffn_conv, meta_tokens, norm_mix,
           w_in, conv_gdn, gdn_a_log, gdn_dt_bias, norm_ret, norm_gdn, w_out, norm_ffn, w_up, conv_ffn,
           w_down, norm_final):
    depth = w_in.shape[0]
    assert depth == 1
    nbp, seq, _ = x_prompt.shape
    nbs, dec_seq, _ = x_sample.shape
    assert dec_seq <= SAMPLE_PAD and nbs % SAMPLE_GROUP == 0
    assert seq % (PROMPT_CHUNK * PROMPT_CHUNKS_PER_STEP) == 0
    layer = 0

    w_in_bf = jnp.pad(w_in[layer].astype(BF16), ((0, 0), (0, IN_PAD - IN_WIDTH)))
    w_out_bf = w_out[layer].astype(BF16)
    w_up_bf = w_up[layer].astype(BF16)
    w_down_bf = w_down[layer].astype(BF16)
    row = lambda v: v.reshape(1, -1).astype(F32)
    pad_ba = lambda v: jnp.zeros((1, LANES), F32).at[0, N_HEADS:2 * N_HEADS].set(v.astype(F32))
    mixer_consts = (conv_gdn[layer], pad_ba(gdn_a_log[layer]), pad_ba(gdn_dt_bias[layer]),
                    norm_ret[layer].reshape(N_HEADS, D_HEAD), row(norm_gdn[layer]))
    ffn_weights = (w_out_bf, row(norm_ffn[layer]), w_up_bf, conv_ffn[layer], w_down_bf, row(norm_final))
    nmix = row(norm_mix[layer])

    assert seq % N_META == 0 and (seq + N_META) % SAMPLE_PAD == 0
    rope = _rope_tables(jnp.concatenate([
        N_META + jnp.arange(seq, dtype=jnp.int32),
        jnp.arange(N_META, dtype=jnp.int32),
        PAST_LEN + jnp.arange(SAMPLE_PAD, dtype=jnp.int32)]))
    rope_meta_row0, rope_sample_row0 = seq, seq + N_META

    xs_pad = jnp.pad(x_sample, ((0, 0), (0, SAMPLE_PAD - dec_seq), (0, 0))).reshape(nbs * SAMPLE_PAD, D_MODEL)
    small_rows = jnp.concatenate([xs_pad, meta_tokens.astype(F32)], axis=0)
    n_small = small_rows.shape[0]
    proj_small = _proj(small_rows, nmix, w_in_bf, tm=n_small // 2)
    proj_s = proj_small[:nbs * SAMPLE_PAD].reshape(nbs, SAMPLE_PAD, IN_PAD)
    proj_m = proj_small[nbs * SAMPLE_PAD:].reshape(1, N_META, IN_PAD)

    zero_state = jnp.zeros((1, N_HEADS, D_HEAD, D_HEAD), F32)
    zero_cq = jnp.zeros((1, CONV_GDN - 1, GDN_QKV), F32)
    mix_m, sret_m, sgdn_m, cq_m = _mixer(proj_m, zero_state, zero_state, zero_cq, rope, rope_meta_row0,
                                         mixer_consts, bb=1, c=N_META, nch=1, n_valid=N_META, shared_init=True)
    zero_cf = jnp.zeros((1, CONV_FFN - 1, 2 * D_FF), F32)
    _, cf_m = _ffn(meta_tokens.astype(F32), mix_m.reshape(N_META, D_MODEL), zero_cf, ffn_weights,
                   nseq=1, tm=N_META, stride=1, shared_init=True)

    mix_s, sret_s, sgdn_s, cq_s = _mixer(proj_s, state_ret[layer], state_gdn[layer], state_conv_qkv[layer],
                                         rope, rope_sample_row0, mixer_consts, bb=SAMPLE_SEQS_PER_STEP,
                                         c=SAMPLE_PAD, nch=1, n_valid=dec_seq, shared_init=False)
    ng = nbs // SAMPLE_GROUP
    to_tmajor = lambda a: a.reshape(ng, SAMPLE_GROUP, a.shape[1], a.shape[2]).transpose(0, 2, 1, 3)
    xs_t = to_tmajor(x_sample).reshape(nbs * dec_seq, D_MODEL)
    mix_t = to_tmajor(mix_s[:, :dec_seq]).reshape(nbs * dec_seq, D_MODEL)
    cf0_t = to_tmajor(state_ffn_conv[layer]).reshape(ng, (CONV_FFN - 1) * SAMPLE_GROUP, 2 * D_FF)
    y_s_t, cf_s_t = _ffn(xs_t, mix_t, cf0_t, ffn_weights, nseq=ng, tm=dec_seq * SAMPLE_GROUP,
                         stride=SAMPLE_GROUP, shared_init=False)
    y_sample = y_s_t.reshape(ng, dec_seq, SAMPLE_GROUP, D_MODEL).transpose(0, 2, 1, 3).reshape(nbs, dec_seq, D_MODEL)
    cf_s = cf_s_t.reshape(ng, CONV_FFN - 1, SAMPLE_GROUP, 2 * D_FF).transpose(0, 2, 1, 3).reshape(
        nbs, CONV_FFN - 1, 2 * D_FF)

    xp = x_prompt.reshape(nbp * seq, D_MODEL)
    proj_p = _proj(xp, nmix, w_in_bf, tm=512).reshape(nbp, seq, IN_PAD)
    mix_p, sret_p, sgdn_p, cq_p = _mixer(proj_p, sret_m, sgdn_m, cq_m, rope, 0, mixer_consts, bb=1,
                                         c=PROMPT_CHUNK, nch=PROMPT_CHUNKS_PER_STEP, n_valid=PROMPT_CHUNK,
                                         shared_init=True)
    y_p, cf_p = _ffn(xp, mix_p.reshape(nbp * seq, D_MODEL), cf_m, ffn_weights, nseq=nbp, tm=256, stride=1,
                     shared_init=True)
    y_prompt = y_p.reshape(nbp, seq, D_MODEL)

    return (y_prompt, y_sample, sret_p[None], sgdn_p[None], cq_p[None], cf_p[None],
            sret_s[None], sgdn_s[None], cq_s[None], cf_s[None])
```

```python
import functools

import jax
import jax.numpy as jnp
from jax import lax
from jax.experimental import pallas as pl
from jax.experimental.pallas import tpu as pltpu

F32 = jnp.float32
BF16 = jnp.bfloat16

D_MODEL = 1024
N_META = 16
PAST_LEN = 16384
N_HEADS = 4
D_HEAD = 128
D_GRP = N_HEADS * D_HEAD
GDN_QKV = 3 * D_GRP
CONV_GDN = 4
CONV_FFN = 3
D_FF = 2816
ROPE_THETA = 10000.0
EPS = 1e-6

OFF_RQ, OFF_RK, OFF_RV, OFF_RG = 0, D_GRP, 2 * D_GRP, 3 * D_GRP
OFF_QKV = 4 * D_GRP
OFF_GG = OFF_QKV + GDN_QKV
OFF_BA = OFF_GG + D_GRP
IN_WIDTH = OFF_BA + 2 * N_HEADS
LANES = 128
SUBLANES = 8
IN_PAD = OFF_BA + LANES

PROMPT_CHUNK = 128
PROMPT_CHUNKS_PER_STEP = 2
SAMPLE_PAD = 8
SAMPLE_SEQS_PER_STEP = 8
SAMPLE_GROUP = 32
VMEM_LIMIT = 56 * 1024 * 1024


def _round_up(n, m):
    return (n + m - 1) // m * m


def _mm(a, b):
    return jnp.dot(a.astype(BF16), b.astype(BF16), preferred_element_type=F32)


def _mm_nt(a, b):
    return lax.dot_general(a.astype(BF16), b.astype(BF16), (((1,), (1,)), ((), ())),
                           preferred_element_type=F32)


def _mm_tn(a, b):
    return lax.dot_general(a.astype(BF16), b.astype(BF16), (((0,), (0,)), ((), ())),
                           preferred_element_type=F32)


def _split2(x):
    hi = x.astype(BF16)
    lo = (x - hi.astype(F32)).astype(BF16)
    return hi, lo


def _split3(x):
    p0 = x.astype(BF16)
    r = x - p0.astype(F32)
    p1 = r.astype(BF16)
    p2 = (r - p1.astype(F32)).astype(BF16)
    return p0, p1, p2


def _mm3(a2, b2):
    (ah, al), (bh, bl) = a2, b2
    dot = lambda x, y: jnp.dot(x, y, preferred_element_type=F32)
    return dot(ah, bh) + (dot(ah, bl) + dot(al, bh))


def _silu(x):
    return x * jax.nn.sigmoid(x)


def _rms(x, w):
    return x * lax.rsqrt(jnp.mean(x * x, axis=-1, keepdims=True) + EPS) * w


def _rope_kernel(pos_ref, invf_ref, cos_ref, sin_ref):
    ang = pos_ref[...] * invf_ref[...]
    lane = lax.broadcasted_iota(jnp.int32, ang.shape, 1)
    sin = jnp.sin(ang)
    cos_ref[...] = jnp.cos(ang)
    sin_ref[...] = jnp.where(lane < D_HEAD // 2, -sin, sin)


def _rope_tables(pos):
    n = pos.shape[0]
    half = D_HEAD // 2
    inv_freq = ROPE_THETA ** (-jnp.arange(half, dtype=F32) / half)
    invf2 = jnp.concatenate([inv_freq, inv_freq]).reshape(1, LANES)
    pos_b = jnp.broadcast_to(pos.astype(F32)[:, None], (n, LANES))
    out = jax.ShapeDtypeStruct((n, LANES), F32)
    return pl.pallas_call(_rope_kernel, out_shape=(out, out), name="rope_tables")(pos_b, invf2)


def _proj_kernel(x_ref, nw_ref, w_ref, o_ref):
    h = _rms(x_ref[...], nw_ref[...])
    o_ref[...] = jnp.dot(h.astype(BF16), w_ref[...], preferred_element_type=F32)


def _proj(x, norm_w, w_bf, tm):
    rows = x.shape[0]
    assert rows % tm == 0
    return pl.pallas_call(
        _proj_kernel,
        out_shape=jax.ShapeDtypeStruct((rows, IN_PAD), F32),
        grid=(rows // tm,),
        in_specs=[
            pl.BlockSpec((tm, D_MODEL), lambda i: (i, 0)),
            pl.BlockSpec((1, D_MODEL), lambda i: (0, 0)),
            pl.BlockSpec((D_MODEL, IN_PAD), lambda i: (0, 0), pipeline_mode=pl.Buffered(1)),
        ],
        out_specs=pl.BlockSpec((tm, IN_PAD), lambda i: (i, 0)),
        compiler_params=pltpu.CompilerParams(dimension_semantics=("arbitrary",),
                                             vmem_limit_bytes=VMEM_LIMIT),
        name="in_proj",
    )(x, norm_w, w_bf)


def _unit_lower_inverses(mats, c):
    ri = lax.broadcasted_iota(jnp.int32, (c, c), 0)
    ci = lax.broadcasted_iota(jnp.int32, (c, c), 1)
    eye = (ri == ci).astype(F32)
    diag_blk = (ri // SUBLANES) == (ci // SUBLANES)
    ad = [jnp.where(diag_blk, a, 0.0) for a in mats]
    a2 = [_mm(x, x) for x in ad]
    a4 = [_mm(x, x) for x in a2]
    t = [eye - x for x in ad]
    t = [x + _mm(x, s) for x, s in zip(t, a2)]
    t = [x + _mm(x, s) for x, s in zip(t, a4)]
    s = SUBLANES
    while s < c:
        level = ((ri // (2 * s)) == (ci // (2 * s))) & ((ri // s) != (ci // s))
        off = [jnp.where(level, a, 0.0) for a in mats]
        lt = [_mm(o, x) for o, x in zip(off, t)]
        t = [x - _mm(x, y) for x, y in zip(t, lt)]
        s *= 2
    return t


def _mixer_kernel(p_ref, sret0_ref, sgdn0_ref, cq0_ref, cos_ref, sin_ref, tril_ref, triu_ref,
                  dint_ref, qdec_ref, kdec_ref, cdec_ref, cw_ref, alog_ref, dtb_ref, nret_ref, ngdn_ref,
                  mix_ref, sret_ref, sgdn_ref, cq_ref, xs_ref, *, bb, c, nch, n_valid):
    step = pl.program_id(1)
    rows = nch * c
    tail = CONV_GDN - 1
    top = SUBLANES

    @pl.when(step == 0)
    def _():
        sret_ref[...] = sret0_ref[...]
        sgdn_ref[...] = sgdn0_ref[...]
        xs_ref[:, top - tail:top, :] = cq0_ref[...]

    cos2 = cos_ref[...]
    sin2 = sin_ref[...]
    ri = lax.broadcasted_iota(jnp.int32, (c, c), 0)
    cj = lax.broadcasted_iota(jnp.int32, (c, c), 1)
    tri = ri >= cj
    strict = ri > cj
    tril_bf = tril_ref[...]
    triu_bf = triu_ref[...]
    scale = D_HEAD ** -0.5
    heads = range(N_HEADS)
    seqs = range(bb)
    chunks = [slice(j * c, (j + 1) * c) for j in range(nch)]
    join = lambda parts: parts[0] if len(parts) == 1 else jnp.concatenate(parts, axis=0)

    ret = {}
    for b in seqs:
        for h in heads:
            lo = h * D_HEAD
            q = p_ref[b, :, OFF_RQ + lo:OFF_RQ + lo + D_HEAD]
            k = p_ref[b, :, OFF_RK + lo:OFF_RK + lo + D_HEAD]
            v = p_ref[b, :, OFF_RV + lo:OFF_RV + lo + D_HEAD]
            qr = q * cos2 + pltpu.roll(q, D_HEAD // 2, 1) * sin2
            kr = (k * cos2 + pltpu.roll(k, D_HEAD // 2, 1) * sin2) * scale
            ret[b, h] = (qr, kr, v)
    rtasks = [(b, h, j) for j in range(nch) for b in seqs for h in heads]
    scores = {t: _mm_nt(ret[t[0], t[1]][0][chunks[t[2]]], ret[t[0], t[1]][1][chunks[t[2]]]) * dint_ref[t[1]]
              for t in rtasks}
    intra = {t: _mm(scores[t], ret[t[0], t[1]][2][chunks[t[2]]]) for t in rtasks}
    kv = {t: _mm_tn(ret[t[0], t[1]][1][chunks[t[2]]] * kdec_ref[t[1]], ret[t[0], t[1]][2][chunks[t[2]]])
          for t in rtasks}
    pairs = [(b, h) for b in seqs for h in heads]
    s_ret = {bh: sret_ref[bh[0], bh[1]] for bh in pairs}
    o_ret = {}
    for j in range(nch):
        for b, h in pairs:
            o_ret[b, h, j] = intra[b, h, j] + _mm(ret[b, h][0][chunks[j]] * qdec_ref[h], s_ret[b, h])
        for b, h in pairs:
            s_ret[b, h] = cdec_ref[h] * s_ret[b, h] + kv[b, h, j]
    o_ret = {bh: join([o_ret[bh[0], bh[1], j] for j in range(nch)]) for bh in pairs}
    mu = {bh: jnp.mean(o_ret[bh], axis=-1, keepdims=True) for bh in pairs}
    cen = {bh: o_ret[bh] - mu[bh] for bh in pairs}
    var = {bh: jnp.mean(cen[bh] * cen[bh], axis=-1, keepdims=True) for bh in pairs}
    for b, h in pairs:
        lo = h * D_HEAD
        gate = p_ref[b, :, OFF_RG + lo:OFF_RG + lo + D_HEAD]
        o = cen[b, h] * lax.rsqrt(var[b, h] + EPS) * nret_ref[h:h + 1, :]
        mix_ref[b, :, lo:lo + D_HEAD] = (o * _silu(gate)).astype(mix_ref.dtype)
        sret_ref[b, h] = s_ret[b, h]

    qkvs, beta_alls, cum_cols, cum_rows, cum_tots = {}, {}, {}, {}, {}
    ones_bf = jnp.ones((D_HEAD, c), BF16)
    for b in seqs:
        xs_ref[b, top:top + rows, :] = p_ref[b, :, OFF_QKV:OFF_QKV + GDN_QKV]
        conv = xs_ref[b, top - tail:top - tail + rows, :] * cw_ref[0:1, :]
        for i in range(1, CONV_GDN):
            conv = conv + xs_ref[b, top - tail + i:top - tail + i + rows, :] * cw_ref[i:i + 1, :]
        last = n_valid if nch == 1 else rows
        new_tail = xs_ref[b, top + last - tail:top + last, :]
        xs_ref[b, top - tail:top, :] = new_tail
        cq_ref[b] = new_tail
        qkvs[b] = _silu(conv)

        ba = p_ref[b, :, OFF_BA:OFF_BA + LANES]
        beta_all = jax.nn.sigmoid(ba)
        z = ba + dtb_ref[...]
        softplus = jnp.maximum(z, 0.0) + jnp.log1p(jnp.exp(-jnp.abs(z)))
        g_all = -jnp.exp(alog_ref[...]) * softplus
        if n_valid < c:
            row = lax.broadcasted_iota(jnp.int32, (rows, LANES), 0)
            rowmask = (row < n_valid).astype(F32)
            beta_all = beta_all * rowmask
            g_all = g_all * rowmask
        beta_alls[b] = beta_all
        for j, rs in enumerate(chunks):
            parts = _split3(g_all[rs])
            cum_cols[b, j] = sum(jnp.dot(tril_bf, g, preferred_element_type=F32) for g in parts)
            cum_tots[b, j] = sum(jnp.dot(ones_bf, g, preferred_element_type=F32) for g in parts)
            cum_rows[b, j] = sum(lax.dot_general(g, triu_bf, (((0,), (0,)), ((), ())),
                                                 preferred_element_type=F32) for g in parts)

    tasks = [(j, b, h) for j in range(nch) for b in seqs for h in heads]
    qs, ks, vs, betas, ecums, kdecs, cdecs, amats, qkms = [], [], [], [], [], [], [], [], []
    raw = [(qkvs[b][chunks[j], h * D_HEAD:(h + 1) * D_HEAD],
            qkvs[b][chunks[j], D_GRP + h * D_HEAD:D_GRP + (h + 1) * D_HEAD]) for j, b, h in tasks]
    sumsq = [(jnp.sum(q * q, axis=-1, keepdims=True), jnp.sum(k * k, axis=-1, keepdims=True)) for q, k in raw]
    for i, (j, b, h) in enumerate(tasks):
        rs = chunks[j]
        lo = h * D_HEAD
        v = qkvs[b][rs, 2 * D_GRP + lo:2 * D_GRP + lo + D_HEAD]
        q = raw[i][0] * lax.rsqrt(sumsq[i][0] + EPS) * scale
        k = raw[i][1] * lax.rsqrt(sumsq[i][1] + EPS)
        beta = jnp.broadcast_to(beta_alls[b][rs, h:h + 1], (c, LANES))
        cum = jnp.broadcast_to(cum_cols[b, j][:, N_HEADS + h:N_HEADS + h + 1], (c, LANES))
        cum_row = cum_rows[b, j][N_HEADS + h:N_HEADS + h + 1, :]
        dmask = jnp.exp(jnp.where(tri, cum[:, :c] - cum_row, -jnp.inf))
        cum_last = jnp.broadcast_to(cum_tots[b, j][:, N_HEADS + h:N_HEADS + h + 1], (D_HEAD, LANES))
        kq =_mm_nt(jnp.concatenate([k, q], axis=0), k)
        amats.append(jnp.where(strict, beta[:, :c] * kq[:c] * dmask, 0.0))
        qkms.append(jnp.where(tri, kq[c:] * dmask, 0.0))
        qs.append(q)
        ks.append(k)
        vs.append(v)
        betas.append(beta)
        ecums.append(jnp.exp(cum))
        kdecs.append(jnp.exp(cum_last[:c] - cum))
        cdecs.append(jnp.exp(cum_last))

    tinv = _unit_lower_inverses(amats, c)
    sols = []
    for i in range(len(tasks)):
        rhs = jnp.concatenate([vs[i] * betas[i], ks[i] * (betas[i] * ecums[i])], axis=1)
        sols.append(_mm3(_split2(tinv[i]), _split2(rhs)))

    s_gdn = {bh: sgdn_ref[bh[0], bh[1]] for bh in pairs}
    o_gdn = {}
    for j in range(nch):
        idx = {tasks[i][1:]: i for i in range(len(tasks)) if tasks[i][0] == j}
        lhs = {bh: jnp.concatenate([sols[i][:, D_HEAD:], qs[i] * ecums[i]], axis=0) for bh, i in idx.items()}
        both = {bh: _mm(lhs[bh], s_gdn[bh]) for bh in idx}
        w = {bh: sols[i][:, :D_HEAD] - both[bh][:c] for bh, i in idx.items()}
        for bh, i in idx.items():
            o_gdn[bh + (j,)] = both[bh][c:] + _mm(qkms[i], w[bh])
        upd = {bh: _mm_tn(ks[i] * kdecs[i], w[bh]) for bh, i in idx.items()}
        for bh, i in idx.items():
            s_gdn[bh] = cdecs[i] * s_gdn[bh] + upd[bh]
    o_gdn = {bh: join([o_gdn[bh + (j,)] for j in range(nch)]) for bh in pairs}
    msq = {bh: jnp.mean(o_gdn[bh] * o_gdn[bh], axis=-1, keepdims=True) for bh in pairs}
    for b, h in pairs:
        lo = h * D_HEAD
        gate = p_ref[b, :, OFF_GG + lo:OFF_GG + lo + D_HEAD]
        o = o_gdn[b, h] * lax.rsqrt(msq[b, h] + EPS) * ngdn_ref[...]
        mix_ref[b, :, D_GRP + lo:D_GRP + lo + D_HEAD] = (o * _silu(gate)).astype(mix_ref.dtype)
        sgdn_ref[b, h] = s_gdn[b, h]


def _retention_decay_tables(c, n_valid):
    lg = jnp.log1p(-jnp.power(2.0, -5.0 - jnp.arange(N_HEADS, dtype=F32)))[:, None]
    idx = jnp.arange(c, dtype=F32)
    diff = idx[:, None] - idx[None, :]
    dint = jnp.exp(jnp.where(diff[None] >= 0, lg[:, :, None] * diff[None], -jnp.inf))
    qdec = jnp.exp(lg * (idx + 1.0))
    kdec = jnp.where(idx[None, :] < n_valid, jnp.exp(lg * (n_valid - 1.0 - idx)), 0.0)
    cdec = jnp.exp(lg * float(n_valid))
    bc = lambda t: jnp.broadcast_to(t[:, :, None], t.shape + (LANES,))
    return dint, bc(qdec), bc(kdec), jnp.broadcast_to(cdec[:, :, None], (N_HEADS, 1, LANES))


def _mixer(proj, sret0, sgdn0, cq0, rope, rope_row0, consts, *, bb, c, nch, n_valid, shared_init):
    nb, length, _ = proj.shape
    rows = nch * c
    assert nb % bb == 0 and length % rows == 0 and rope_row0 % rows == 0
    assert not shared_init or bb == 1
    assert n_valid == c or nch == 1
    cw, alog, dtb, nret, ngdn = consts
    dint, qdec, kdec, cdec = _retention_decay_tables(c, n_valid)
    idx = jnp.arange(c)
    tril = (idx[:, None] >= idx[None, :]).astype(BF16)
    init_idx = (lambda b, i: (0, 0, 0, 0)) if shared_init else (lambda b, i: (b, 0, 0, 0))
    init_idx3 = (lambda b, i: (0, 0, 0)) if shared_init else (lambda b, i: (b, 0, 0))
    const2 = lambda shape: pl.BlockSpec(shape, lambda b, i: (0, 0))
    const3 = lambda shape: pl.BlockSpec(shape, lambda b, i: (0, 0, 0))
    rope_spec = pl.BlockSpec((rows, LANES), lambda b, i: (rope_row0 // rows + i, 0))
    state_shape = (bb, N_HEADS, D_HEAD, D_HEAD)
    tail = CONV_GDN - 1
    kern = functools.partial(_mixer_kernel, bb=bb, c=c, nch=nch, n_valid=n_valid)
    return pl.pallas_call(
        kern,
        out_shape=(
            jax.ShapeDtypeStruct((nb, length, D_MODEL), BF16),
            jax.ShapeDtypeStruct((nb, N_HEADS, D_HEAD, D_HEAD), F32),
            jax.ShapeDtypeStruct((nb, N_HEADS, D_HEAD, D_HEAD), F32),
            jax.ShapeDtypeStruct((nb, tail, GDN_QKV), F32),
        ),
        grid=(nb // bb, length // rows),
        in_specs=[
            pl.BlockSpec((bb, rows, IN_PAD), lambda b, i: (b, i, 0)),
            pl.BlockSpec(state_shape, init_idx),
            pl.BlockSpec(state_shape, init_idx),
            pl.BlockSpec((bb, tail, GDN_QKV), init_idx3),
            rope_spec,
            rope_spec,
            const2((c, c)),
            const2((c, c)),
            const3((N_HEADS, c, c)),
            const3((N_HEADS, c, LANES)),
            const3((N_HEADS, c, LANES)),
            const3((N_HEADS, 1, LANES)),
            const2((CONV_GDN, GDN_QKV)),
            const2((1, LANES)),
            const2((1, LANES)),
            const2((N_HEADS, D_HEAD)),
            const2((1, D_HEAD)),
        ],
        out_specs=(
            pl.BlockSpec((bb, rows, D_MODEL), lambda b, i: (b, i, 0)),
            pl.BlockSpec(state_shape, lambda b, i: (b, 0, 0, 0)),
            pl.BlockSpec(state_shape, lambda b, i: (b, 0, 0, 0)),
            pl.BlockSpec((bb, tail, GDN_QKV), lambda b, i: (b, 0, 0)),
        ),
        scratch_shapes=[pltpu.VMEM((bb, SUBLANES + rows, GDN_QKV), F32)],
        compiler_params=pltpu.CompilerParams(dimension_semantics=("arbitrary", "arbitrary"),
                                             vmem_limit_bytes=VMEM_LIMIT),
        name="mixer",
    )(proj, sret0, sgdn0, cq0, rope[0], rope[1], tril, tril.T, dint, qdec, kdec, cdec, cw, alog, dtb, nret, ngdn)


FFN_COL_CHUNK = D_FF // 2


def _ffn_kernel(x_ref, mix_ref, tail0_ref, wout_ref, nffn_ref, wup_ref, cw_ref, wdn_ref, nfin_ref,
                y_ref, tail_ref, full_ref, *, tm, stride):
    t = pl.program_id(1)
    carry = (CONV_FFN - 1) * stride
    base = _round_up(carry, SUBLANES)

    @pl.when(t == 0)
    def _():
        full_ref[base - carry:base, :] = tail0_ref[0]

    x1 = x_ref[...] + jnp.dot(mix_ref[...], wout_ref[...], preferred_element_type=F32)
    h = _rms(x1, nffn_ref[...]).astype(BF16)
    full_ref[base:base + tm, :] = jnp.dot(h, wup_ref[...], preferred_element_type=F32)

    def conv_cols(lo):
        acc = full_ref[base - carry:base - carry + tm, lo:lo + FFN_COL_CHUNK] * cw_ref[0:1, lo:lo + FFN_COL_CHUNK]
        for i in range(1, CONV_FFN):
            r0 = base - carry + i * stride
            acc = acc + full_ref[r0:r0 + tm, lo:lo + FFN_COL_CHUNK] * cw_ref[i:i + 1, lo:lo + FFN_COL_CHUNK]
        return acc

    x2 = x1
    for j in range(D_FF // FFN_COL_CHUNK):
        lo = j * FFN_COL_CHUNK
        act = (_silu(conv_cols(lo)) * conv_cols(D_FF + lo)).astype(BF16)
        x2 = x2 + jnp.dot(act, wdn_ref[lo:lo + FFN_COL_CHUNK, :], preferred_element_type=F32)
    y_ref[...] = _rms(x2, nfin_ref[...])

    new_tail = full_ref[base + tm - carry:base + tm, :]
    full_ref[base - carry:base, :] = new_tail
    tail_ref[0] = new_tail


def _ffn(x, mix, tail0, weights, *, nseq, tm, stride, shared_init):
    wout, nffn, wup, cw, wdn, nfin = weights
    rows = x.shape[0]
    assert rows % (nseq * tm) == 0
    nt = rows // (nseq * tm)
    carry = (CONV_FFN - 1) * stride
    base = _round_up(carry, SUBLANES)
    assert tm >= carry
    resident = lambda shape: pl.BlockSpec(shape, lambda b, t: (0, 0), pipeline_mode=pl.Buffered(1))
    small = lambda shape: pl.BlockSpec(shape, lambda b, t: (0, 0))
    tail_idx = (lambda b, t: (0, 0, 0)) if shared_init else (lambda b, t: (b, 0, 0))
    kern = functools.partial(_ffn_kernel, tm=tm, stride=stride)
    return pl.pallas_call(
        kern,
        out_shape=(
            jax.ShapeDtypeStruct((rows, D_MODEL), F32),
            jax.ShapeDtypeStruct((nseq, carry, 2 * D_FF), F32),
        ),
        grid=(nseq, nt),
        in_specs=[
            pl.BlockSpec((tm, D_MODEL), lambda b, t: (b * nt + t, 0)),
            pl.BlockSpec((tm, D_MODEL), lambda b, t: (b * nt + t, 0)),
            pl.BlockSpec((1, carry, 2 * D_FF), tail_idx),
            resident((D_MODEL, D_MODEL)),
            small((1, D_MODEL)),
            resident((D_MODEL, 2 * D_FF)),
            small((CONV_FFN, 2 * D_FF)),
            resident((D_FF, D_MODEL)),
            small((1, D_MODEL)),
        ],
        out_specs=(
            pl.BlockSpec((tm, D_MODEL), lambda b, t: (b * nt + t, 0)),
            pl.BlockSpec((1, carry, 2 * D_FF), lambda b, t: (b, 0, 0)),
        ),
        scratch_shapes=[pltpu.VMEM((base + tm, 2 * D_FF), F32)],
        compiler_params=pltpu.CompilerParams(dimension_semantics=("arbitrary", "arbitrary"),
                                             vmem_limit_bytes=VMEM_LIMIT),
        name="out_ffn",
    )(x, mix, tail0, wout, nffn, wup, cw, wdn, nfin)


def kernel(x_prompt, x_sample, state_ret, state_gdn, state_conv_qkv, state_ffn_conv, meta_tokens, norm_mix,
           w_in, conv_gdn, gdn_a_log, gdn_dt_bias, norm_ret, norm_gdn, w_out, norm_ffn, w_up, conv_ffn,
           w_down, norm_final):
    depth = w_in.shape[0]
    assert depth == 1
    nbp, seq, _ = x_prompt.shape
    nbs, dec_seq, _ = x_sample.shape
    assert dec_seq <= SAMPLE_PAD and nbs % SAMPLE_GROUP == 0
    assert seq % (PROMPT_CHUNK * PROMPT_CHUNKS_PER_STEP) == 0
    layer = 0

    w_in_bf = jnp.pad(w_in[layer].astype(BF16), ((0, 0), (0, IN_PAD - IN_WIDTH)))
    w_out_bf = w_out[layer].astype(BF16)
    w_up_bf = w_up[layer].astype(BF16)
    w_down_bf = w_down[layer].astype(BF16)
    row = lambda v: v.reshape(1, -1).astype(F32)
    pad_ba = lambda v: jnp.zeros((1, LANES), F32).at[0, N_HEADS:2 * N_HEADS].set(v.astype(F32))
    mixer_consts = (conv_gdn[layer], pad_ba(gdn_a_log[layer]), pad_ba(gdn_dt_bias[layer]),
                    norm_ret[layer].reshape(N_HEADS, D_HEAD), row(norm_gdn[layer]))
    ffn_weights = (w_out_bf, row(norm_ffn[layer]), w_up_bf, conv_ffn[layer], w_down_bf, row(norm_final))
    nmix = row(norm_mix[layer])

    assert seq % N_META == 0 and (seq + N_META) % SAMPLE_PAD == 0
    rope = _rope_tables(jnp.concatenate([
        N_META + jnp.arange(seq, dtype=jnp.int32),
        jnp.arange(N_META, dtype=jnp.int32),
        PAST_LEN + jnp.arange(SAMPLE_PAD, dtype=jnp.int32)]))
    rope_meta_row0, rope_sample_row0 = seq, seq + N_META

    xs_pad = jnp.pad(x_sample, ((0, 0), (0, SAMPLE_PAD - dec_seq), (0, 0))).reshape(nbs * SAMPLE_PAD, D_MODEL)
    small_rows = jnp.concatenate([xs_pad, meta_tokens.astype(F32)], axis=0)
    n_small = small_rows.shape[0]
    proj_small = _proj(small_rows, nmix, w_in_bf, tm=n_small // 2)
    proj_s = proj_small[:nbs * SAMPLE_PAD].reshape(nbs, SAMPLE_PAD, IN_PAD)
    proj_m = proj_small[nbs * SAMPLE_PAD:].reshape(1, N_META, IN_PAD)

    zero_state = jnp.zeros((1, N_HEADS, D_HEAD, D_HEAD), F32)
    zero_cq = jnp.zeros((1, CONV_GDN - 1, GDN_QKV), F32)
    mix_m, sret_m, sgdn_m, cq_m = _mixer(proj_m, zero_state, zero_state, zero_cq, rope, rope_meta_row0,
                                         mixer_consts, bb=1, c=N_META, nch=1, n_valid=N_META, shared_init=True)
    zero_cf = jnp.zeros((1, CONV_FFN - 1, 2 * D_FF), F32)
    _, cf_m = _ffn(meta_tokens.astype(F32), mix_m.reshape(N_META, D_MODEL), zero_cf, ffn_weights,
                   nseq=1, tm=N_META, stride=1, shared_init=True)

    mix_s, sret_s, sgdn_s, cq_s = _mixer(proj_s, state_ret[layer], state_gdn[layer], state_conv_qkv[layer],
                                         rope, rope_sample_row0, mixer_consts, bb=SAMPLE_SEQS_PER_STEP,
                                         c=SAMPLE_PAD, nch=1, n_valid=dec_seq, shared_init=False)
    ng = nbs // SAMPLE_GROUP
    to_tmajor = lambda a: a.reshape(ng, SAMPLE_GROUP, a.shape[1], a.shape[2]).transpose(0, 2, 1, 3)
    xs_t = to_tmajor(x_sample).reshape(nbs * dec_seq, D_MODEL)
    mix_t = to_tmajor(mix_s[:, :dec_seq]).reshape(nbs * dec_seq, D_MODEL)
    cf0_t = to_tmajor(state_ffn_conv[layer]).reshape(ng, (CONV_FFN - 1) * SAMPLE_GROUP, 2 * D_FF)
    y_s_t, cf_s_t = _ffn(xs_t, mix_t, cf0_t, ffn_weights, nseq=ng, tm=dec_seq * SAMPLE_GROUP,
                         stride=SAMPLE_GROUP, shared_init=False)
    y_sample = y_s_t.reshape(ng, dec_seq, SAMPLE_GROUP, D_MODEL).transpose(0, 2, 1, 3).reshape(nbs, dec_seq, D_MODEL)
    cf_s = cf_s_t.reshape(ng, CONV_FFN - 1, SAMPLE_GROUP, 2 * D_FF).transpose(0, 2, 1, 3).reshape(
        nbs, CONV_FFN - 1, 2 * D_FF)

    xp = x_prompt.reshape(nbp * seq, D_MODEL)
    proj_p = _proj(xp, nmix, w_in_bf, tm=512).reshape(nbp, seq, IN_PAD)
    mix_p, sret_p, sgdn_p, cq_p = _mixer(proj_p, sret_m, sgdn_m, cq_m, rope, 0, mixer_consts, bb=1,
                                         c=PROMPT_CHUNK, nch=PROMPT_CHUNKS_PER_STEP, n_valid=PROMPT_CHUNK,
                                         shared_init=True)
    y_p, cf_p = _ffn(xp, mix_p.reshape(nbp * seq, D_MODEL), cf_m, ffn_weights, nseq=nbp, tm=256, stride=1,
                     shared_init=True)
    y_prompt = y_p.reshape(nbp, seq, D_MODEL)

    return (y_prompt, y_sample, sret_p[None], sgdn_p[None], cq_p[None], cf_p[None],
            sret_s[None], sgdn_s[None], cq_s[None], cf_s[None])
```

```python
import functools

import jax
import jax.numpy as jnp
from jax import lax
from jax.experimental import pallas as pl
from jax.experimental.pallas import tpu as pltpu

F32 = jnp.float32
BF16 = jnp.bfloat16

D_MODEL = 1024
N_META = 16
PAST_LEN = 16384
N_HEADS = 4
D_HEAD = 128
D_GRP = N_HEADS * D_HEAD
GDN_QKV = 3 * D_GRP
CONV_GDN = 4
CONV_FFN = 3
D_FF = 2816
ROPE_THETA = 10000.0
EPS = 1e-6

OFF_RQ, OFF_RK, OFF_RV, OFF_RG = 0, D_GRP, 2 * D_GRP, 3 * D_GRP
OFF_QKV = 4 * D_GRP
OFF_GG = OFF_QKV + GDN_QKV
OFF_BA = OFF_GG + D_GRP
IN_WIDTH = OFF_BA + 2 * N_HEADS
LANES = 128
SUBLANES = 8
IN_PAD = OFF_BA + LANES

PROMPT_CHUNK = 128
PROMPT_CHUNKS_PER_STEP = 2
SAMPLE_PAD = 8
SAMPLE_SEQS_PER_STEP = 8
SAMPLE_GROUP = 32
VMEM_LIMIT = 56 * 1024 * 1024


def _round_up(n, m):
    return (n + m - 1) // m * m


def _mm(a, b):
    return jnp.dot(a.astype(BF16), b.astype(BF16), preferred_element_type=F32)


def _mm_nt(a, b):
    return lax.dot_general(a.astype(BF16), b.astype(BF16), (((1,), (1,)), ((), ())),
                           preferred_element_type=F32)


def _mm_tn(a, b):
    return lax.dot_general(a.astype(BF16), b.astype(BF16), (((0,), (0,)), ((), ())),
                           preferred_element_type=F32)


def _split2(x):
    hi = x.astype(BF16)
    lo = (x - hi.astype(F32)).astype(BF16)
    return hi, lo


def _split3(x):
    p0 = x.astype(BF16)
    r = x - p0.astype(F32)
    p1 = r.astype(BF16)
    p2 = (r - p1.astype(F32)).astype(BF16)
    return p0, p1, p2


def _mm3(a2, b2):
    (ah, al), (bh, bl) = a2, b2
    dot = lambda x, y: jnp.dot(x, y, preferred_element_type=F32)
    return dot(ah, bh) + (dot(ah, bl) + dot(al, bh))


def _silu(x):
    return x * jax.nn.sigmoid(x)


def _rms(x, w):
    return x * lax.rsqrt(jnp.mean(x * x, axis=-1, keepdims=True) + EPS) * w


def _rope_kernel(pos_ref, invf_ref, cos_ref, sin_ref):
    ang = pos_ref[...] * invf_ref[...]
    lane = lax.broadcasted_iota(jnp.int32, ang.shape, 1)
    sin = jnp.sin(ang)
    cos_ref[...] = jnp.cos(ang)
    sin_ref[...] = jnp.where(lane < D_HEAD // 2, -sin, sin)


def _rope_tables(pos):
    n = pos.shape[0]
    half = D_HEAD // 2
    inv_freq = ROPE_THETA ** (-jnp.arange(half, dtype=F32) / half)
    invf2 = jnp.concatenate([inv_freq, inv_freq]).reshape(1, LANES)
    pos_b = jnp.broadcast_to(pos.astype(F32)[:, None], (n, LANES))
    out = jax.ShapeDtypeStruct((n, LANES), F32)
    return pl.pallas_call(_rope_kernel, out_shape=(out, out), name="rope_tables")(pos_b, invf2)


def _proj_kernel(x_ref, nw_ref, w_ref, o_ref):
    h = _rms(x_ref[...], nw_ref[...])
    o_ref[...] = jnp.dot(h.astype(BF16), w_ref[...], preferred_element_type=F32)


def _proj(x, norm_w, w_bf, tm):
    rows = x.shape[0]
    assert rows % tm == 0
    return pl.pallas_call(
        _proj_kernel,
        out_shape=jax.ShapeDtypeStruct((rows, IN_PAD), F32),
        grid=(rows // tm,),
        in_specs=[
            pl.BlockSpec((tm, D_MODEL), lambda i: (i, 0)),
            pl.BlockSpec((1, D_MODEL), lambda i: (0, 0)),
            pl.BlockSpec((D_MODEL, IN_PAD), lambda i: (0, 0), pipeline_mode=pl.Buffered(1)),
        ],
        out_specs=pl.BlockSpec((tm, IN_PAD), lambda i: (i, 0)),
        compiler_params=pltpu.CompilerParams(dimension_semantics=("arbitrary",),
                                             vmem_limit_bytes=VMEM_LIMIT),
        name="in_proj",
    )(x, norm_w, w_bf)


def _unit_lower_inverses(mats, c, tick=lambda: None):
    ri = lax.broadcasted_iota(jnp.int32, (c, c), 0)
    ci = lax.broadcasted_iota(jnp.int32, (c, c), 1)
    eye = (ri == ci).astype(F32)
    diag_blk = (ri // SUBLANES) == (ci // SUBLANES)
    ad = [jnp.where(diag_blk, a, 0.0) for a in mats]
    a2 = [_mm(x, x) for x in ad]
    tick()
    a4 = [_mm(x, x) for x in a2]
    t = [eye - x for x in ad]
    t = [x + _mm(x, s) for x, s in zip(t, a2)]
    tick()
    t = [x + _mm(x, s) for x, s in zip(t, a4)]
    tick()
    s = SUBLANES
    while s < c:
        level = ((ri // (2 * s)) == (ci // (2 * s))) & ((ri // s) != (ci // s))
        off = [jnp.where(level, a, 0.0) for a in mats]
        lt = [_mm(o, x) for o, x in zip(off, t)]
        tick()
        t = [x - _mm(x, y) for x, y in zip(t, lt)]
        tick()
        s *= 2
    return t


XS_TOP = SUBLANES


def _mixer_init(sret0_ref, sgdn0_ref, cq0_ref, sret_ref, sgdn_ref, xs_ref):
    tail = CONV_GDN - 1
    sret_ref[...] = sret0_ref[...]
    sgdn_ref[...] = sgdn0_ref[...]
    xs_ref[:, XS_TOP - tail:XS_TOP, :] = cq0_ref[...]


def _mixer_block(getp, put_mix, cos2, sin2, const_refs, sret_ref, sgdn_ref, cq_ref, xs_ref, *,
                 bb, c, nch, n_valid, tick=lambda: None):
    (tril_ref, triu_ref, dint_ref, qdec_ref, kdec_ref, cdec_ref, cw_ref, alog_ref, dtb_ref, nret_ref,
     ngdn_ref) = const_refs
    rows = nch * c
    tail = CONV_GDN - 1
    top = XS_TOP
    ri = lax.broadcasted_iota(jnp.int32, (c, c), 0)
    cj = lax.broadcasted_iota(jnp.int32, (c, c), 1)
    tri = ri >= cj
    strict = ri > cj
    tril_bf = tril_ref[...]
    triu_bf = triu_ref[...]
    scale = D_HEAD ** -0.5
    heads = range(N_HEADS)
    seqs = range(bb)
    chunks = [slice(j * c, (j + 1) * c) for j in range(nch)]
    join = lambda parts: parts[0] if len(parts) == 1 else jnp.concatenate(parts, axis=0)

    ret = {}
    for b in seqs:
        for h in heads:
            lo = h * D_HEAD
            q = getp(b, OFF_RQ + lo, OFF_RQ + lo + D_HEAD)
            k = getp(b, OFF_RK + lo, OFF_RK + lo + D_HEAD)
            v = getp(b, OFF_RV + lo, OFF_RV + lo + D_HEAD)
            qr = q * cos2 + pltpu.roll(q, D_HEAD // 2, 1) * sin2
            kr = (k * cos2 + pltpu.roll(k, D_HEAD // 2, 1) * sin2) * scale
            ret[b, h] = (qr, kr, v)
    tick()
    rtasks = [(b, h, j) for j in range(nch) for b in seqs for h in heads]
    scores = {t: _mm_nt(ret[t[0], t[1]][0][chunks[t[2]]], ret[t[0], t[1]][1][chunks[t[2]]]) * dint_ref[t[1]]
              for t in rtasks}
    tick()
    intra = {t: _mm(scores[t], ret[t[0], t[1]][2][chunks[t[2]]]) for t in rtasks}
    tick()
    kv = {t: _mm_tn(ret[t[0], t[1]][1][chunks[t[2]]] * kdec_ref[t[1]], ret[t[0], t[1]][2][chunks[t[2]]])
          for t in rtasks}
    tick()
    pairs = [(b, h) for b in seqs for h in heads]
    s_ret = {bh: sret_ref[bh[0], bh[1]] for bh in pairs}
    o_ret = {}
    for j in range(nch):
        for b, h in pairs:
            o_ret[b, h, j] = intra[b, h, j] + _mm(ret[b, h][0][chunks[j]] * qdec_ref[h], s_ret[b, h])
        for b, h in pairs:
            s_ret[b, h] = cdec_ref[h] * s_ret[b, h] + kv[b, h, j]
        tick()
    o_ret = {bh: join([o_ret[bh[0], bh[1], j] for j in range(nch)]) for bh in pairs}
    mu = {bh: jnp.mean(o_ret[bh], axis=-1, keepdims=True) for bh in pairs}
    cen = {bh: o_ret[bh] - mu[bh] for bh in pairs}
    var = {bh: jnp.mean(cen[bh] * cen[bh], axis=-1, keepdims=True) for bh in pairs}
    for b, h in pairs:
        lo = h * D_HEAD
        gate = getp(b, OFF_RG + lo, OFF_RG + lo + D_HEAD)
        o = cen[b, h] * lax.rsqrt(var[b, h] + EPS) * nret_ref[h:h + 1, :]
        put_mix(b, lo, o * _silu(gate))
        sret_ref[b, h] = s_ret[b, h]
    tick()

    qkvs, beta_alls, cum_cols, cum_rows, cum_tots = {}, {}, {}, {}, {}
    ones_bf = jnp.ones((D_HEAD, c), BF16)
    for b in seqs:
        xs_ref[b, top:top + rows, :] = getp(b, OFF_QKV, OFF_QKV + GDN_QKV)
        conv = xs_ref[b, top - tail:top - tail + rows, :] * cw_ref[0:1, :]
        for i in range(1, CONV_GDN):
            conv = conv + xs_ref[b, top - tail + i:top - tail + i + rows, :] * cw_ref[i:i + 1, :]
        last = n_valid if nch == 1 else rows
        new_tail = xs_ref[b, top + last - tail:top + last, :]
        xs_ref[b, top - tail:top, :] = new_tail
        cq_ref[b] = new_tail
        qkvs[b] = _silu(conv)
        tick()

        ba = getp(b, OFF_BA, OFF_BA + LANES)
        beta_all = jax.nn.sigmoid(ba)
        z = ba + dtb_ref[...]
        softplus = jnp.maximum(z, 0.0) + jnp.log1p(jnp.exp(-jnp.abs(z)))
        g_all = -jnp.exp(alog_ref[...]) * softplus
        if n_valid < c:
            row = lax.broadcasted_iota(jnp.int32, (rows, LANES), 0)
            rowmask = (row < n_valid).astype(F32)
            beta_all = beta_all * rowmask
            g_all = g_all * rowmask
        beta_alls[b] = beta_all
        for j, rs in enumerate(chunks):
            parts = _split3(g_all[rs])
            cum_cols[b, j] = sum(jnp.dot(tril_bf, g, preferred_element_type=F32) for g in parts)
            cum_tots[b, j] = sum(jnp.dot(ones_bf, g, preferred_element_type=F32) for g in parts)
            cum_rows[b, j] = sum(lax.dot_general(g, triu_bf, (((0,), (0,)), ((), ())),
                                                 preferred_element_type=F32) for g in parts)
    tick()

    tasks = [(j, b, h) for j in range(nch) for b in seqs for h in heads]
    qs, ks, vs, betas, ecums, kdecs, cdecs, amats, qkms = [], [], [], [], [], [], [], [], []
    raw = [(qkvs[b][chunks[j], h * D_HEAD:(h + 1) * D_HEAD],
            qkvs[b][chunks[j], D_GRP + h * D_HEAD:D_GRP + (h + 1) * D_HEAD]) for j, b, h in tasks]
    sumsq = [(jnp.sum(q * q, axis=-1, keepdims=True), jnp.sum(k * k, axis=-1, keepdims=True)) for q, k in raw]
    for i, (j, b, h) in enumerate(tasks):
        rs = chunks[j]
        lo = h * D_HEAD
        v = qkvs[b][rs, 2 * D_GRP + lo:2 * D_GRP + lo + D_HEAD]
        q = raw[i][0] * lax.rsqrt(sumsq[i][0] + EPS) * scale
        k = raw[i][1] * lax.rsqrt(sumsq[i][1] + EPS)
        beta = jnp.broadcast_to(beta_alls[b][rs, h:h + 1], (c, LANES))
        cum = jnp.broadcast_to(cum_cols[b, j][:, N_HEADS + h:N_HEADS + h + 1], (c, LANES))
        cum_row = cum_rows[b, j][N_HEADS + h:N_HEADS + h + 1, :]
        dmask = jnp.exp(jnp.where(tri, cum[:, :c] - cum_row, -jnp.inf))
        cum_last = jnp.broadcast_to(cum_tots[b, j][:, N_HEADS + h:N_HEADS + h + 1], (D_HEAD, LANES))
        kq = _mm_nt(jnp.concatenate([k, q], axis=0), k)
        amats.append(jnp.where(strict, beta[:, :c] * kq[:c] * dmask, 0.0))
        qkms.append(jnp.where(tri, kq[c:] * dmask, 0.0))
        qs.append(q)
        ks.append(k)
        vs.append(v)
        betas.append(beta)
        ecums.append(jnp.exp(cum))
        kdecs.append(jnp.exp(cum_last[:c] - cum))
        cdecs.append(jnp.exp(cum_last))
        if i % N_HEADS == N_HEADS - 1:
            tick()

    tinv = _unit_lower_inverses(amats, c, tick)
    sols = []
    for i in range(len(tasks)):
        rhs = jnp.concatenate([vs[i] * betas[i], ks[i] * (betas[i] * ecums[i])], axis=1)
        sols.append(_mm3(_split2(tinv[i]), _split2(rhs)))
    tick()

    s_gdn = {bh: sgdn_ref[bh[0], bh[1]] for bh in pairs}
    o_gdn = {}
    for j in range(nch):
        idx = {tasks[i][1:]: i for i in range(len(tasks)) if tasks[i][0] == j}
        lhs = {bh: jnp.concatenate([sols[i][:, D_HEAD:], qs[i] * ecums[i]], axis=0) for bh, i in idx.items()}
        both = {bh: _mm(lhs[bh], s_gdn[bh]) for bh in idx}
        tick()
        w = {bh: sols[i][:, :D_HEAD] - both[bh][:c] for bh, i in idx.items()}
        for bh, i in idx.items():
            o_gdn[bh + (j,)] = both[bh][c:] + _mm(qkms[i], w[bh])
        tick()
        upd = {bh: _mm_tn(ks[i] * kdecs[i], w[bh]) for bh, i in idx.items()}
        for bh, i in idx.items():
            s_gdn[bh] = cdecs[i] * s_gdn[bh] + upd[bh]
        tick()
    o_gdn = {bh: join([o_gdn[bh + (j,)] for j in range(nch)]) for bh in pairs}
    msq = {bh: jnp.mean(o_gdn[bh] * o_gdn[bh], axis=-1, keepdims=True) for bh in pairs}
    for b, h in pairs:
        lo = h * D_HEAD
        gate = getp(b, OFF_GG + lo, OFF_GG + lo + D_HEAD)
        o = o_gdn[b, h] * lax.rsqrt(msq[b, h] + EPS) * ngdn_ref[...]
        put_mix(b, D_GRP + lo, o * _silu(gate))
        sgdn_ref[b, h] = s_gdn[b, h]


N_MIXER_CONSTS = 11


def _mixer_kernel(p_ref, sret0_ref, sgdn0_ref, cq0_ref, cos_ref, sin_ref, *rest, bb, c, nch, n_valid):
    const_refs, (mix_ref, sret_ref, sgdn_ref, cq_ref, xs_ref) = rest[:N_MIXER_CONSTS], rest[N_MIXER_CONSTS:]
    rows = nch * c

    @pl.when(pl.program_id(1) == 0)
    def _():
        _mixer_init(sret0_ref, sgdn0_ref, cq0_ref, sret_ref, sgdn_ref, xs_ref)

    def put_mix(b, lo, value):
        mix_ref[b, :, lo:lo + D_HEAD] = value.astype(mix_ref.dtype)

    _mixer_block(lambda b, lo, hi: p_ref[b * rows:(b + 1) * rows, lo:hi], put_mix, cos_ref[...], sin_ref[...],
                 const_refs, sret_ref, sgdn_ref, cq_ref, xs_ref, bb=bb, c=c, nch=nch, n_valid=n_valid)


PROJ_PANEL = 2 * LANES


def _mixer_block_ticks(bb, c, nch):
    levels = (c // SUBLANES).bit_length() - 1
    retention = 4 + nch + 1
    gdn_prep = bb + 1 + bb * nch
    inverse = 3 + 2 * levels
    return retention + gdn_prep + inverse + 1 + 3 * nch


def _proj_mixer_kernel(x0_ref, xa_ref, xb_ref, nw_ref, w_ref, sret0_ref, sgdn0_ref, cq0_ref, cos_ref, sin_ref,
                       *rest, c, nch, steps_per_seq):
    const_refs = rest[:N_MIXER_CONSTS]
    mix_ref, sret_ref, sgdn_ref, cq_ref, pja_ref, pjb_ref, xs_ref = rest[N_MIXER_CONSTS:]
    step = pl.program_id(0)
    rows = nch * c

    def project(x_ref, dst_ref, n_ticks):
        h = _rms(x_ref[...], nw_ref[...]).astype(BF16)
        panels = [(lo, min(lo + PROJ_PANEL, IN_PAD)) for lo in range(0, IN_PAD, PROJ_PANEL)]
        n_panels = len(panels)
        calls = [0]

        def emit():
            lo, hi = panels.pop(0)
            dst_ref[:, lo:hi] = jnp.dot(h, w_ref[:, lo:hi], preferred_element_type=F32)

        def tick():
            calls[0] += 1
            due = min(n_panels, -(-calls[0] * n_panels // n_ticks))
            while n_panels - len(panels) < due:
                emit()

        def flush():
            while panels:
                emit()

        return tick, flush

    @pl.when(step == 0)
    def _():
        project(x0_ref, pja_ref, 1)[1]()

    @pl.when(lax.rem(step, steps_per_seq) == 0)
    def _():
        _mixer_init(sret0_ref, sgdn0_ref, cq0_ref, sret_ref, sgdn_ref, xs_ref)

    for half, (cur_ref, x_next_ref, nxt_ref) in enumerate(((pja_ref, xa_ref, pjb_ref), (pjb_ref, xb_ref, pja_ref))):
        r0 = half * rows
        tick, flush = project(x_next_ref, nxt_ref, _mixer_block_ticks(1, c, nch))

        def put_mix(b, lo, value, r0=r0):
            mix_ref[r0:r0 + rows, lo:lo + D_HEAD] = value.astype(mix_ref.dtype)

        _mixer_block(lambda b, lo, hi, cur_ref=cur_ref: cur_ref[:, lo:hi], put_mix,
                     cos_ref[r0:r0 + rows, :], sin_ref[r0:r0 + rows, :], const_refs,
                     sret_ref, sgdn_ref, cq_ref, xs_ref, bb=1, c=c, nch=nch, n_valid=c, tick=tick)
        flush()


def _retention_decay_tables(c, n_valid):
    lg = jnp.log1p(-jnp.power(2.0, -5.0 - jnp.arange(N_HEADS, dtype=F32)))[:, None]
    idx = jnp.arange(c, dtype=F32)
    diff = idx[:, None] - idx[None, :]
    dint = jnp.exp(jnp.where(diff[None] >= 0, lg[:, :, None] * diff[None], -jnp.inf))
    qdec = jnp.exp(lg * (idx + 1.0))
    kdec = jnp.where(idx[None, :] < n_valid, jnp.exp(lg * (n_valid - 1.0 - idx)), 0.0)
    cdec = jnp.exp(lg * float(n_valid))
    bc = lambda t: jnp.broadcast_to(t[:, :, None], t.shape + (LANES,))
    return dint, bc(qdec), bc(kdec), jnp.broadcast_to(cdec[:, :, None], (N_HEADS, 1, LANES))


def _mixer_const_operands(c, n_valid, consts):
    cw, alog, dtb, nret, ngdn = consts
    dint, qdec, kdec, cdec = _retention_decay_tables(c, n_valid)
    idx = jnp.arange(c)
    tril = (idx[:, None] >= idx[None, :]).astype(BF16)
    arrays = (tril, tril.T, dint, qdec, kdec, cdec, cw, alog, dtb, nret, ngdn)
    assert len(arrays) == N_MIXER_CONSTS
    return arrays, [a.shape for a in arrays]


def _mixer(proj, row0, nb, length, sret0, sgdn0, cq0, rope, rope_row0, consts, *, bb, c, nch, n_valid,
           shared_init):
    rows = nch * c
    assert nb % bb == 0 and length % rows == 0 and rope_row0 % rows == 0 and row0 % (bb * length) == 0
    assert not shared_init or bb == 1
    assert n_valid == c or nch == 1
    assert bb == 1 or length == rows
    const_arrays, const_shapes = _mixer_const_operands(c, n_valid, consts)
    nsteps = length // rows
    blk0 = row0 // (bb * rows)
    init_idx = (lambda b, i: (0, 0, 0, 0)) if shared_init else (lambda b, i: (b, 0, 0, 0))
    init_idx3 = (lambda b, i: (0, 0, 0)) if shared_init else (lambda b, i: (b, 0, 0))
    whole = lambda shape: pl.BlockSpec(shape, lambda b, i: (0,) * len(shape))
    rope_spec = pl.BlockSpec((rows, LANES), lambda b, i: (rope_row0 // rows + i, 0))
    state_shape = (bb, N_HEADS, D_HEAD, D_HEAD)
    tail = CONV_GDN - 1
    kern = functools.partial(_mixer_kernel, bb=bb, c=c, nch=nch, n_valid=n_valid)
    return pl.pallas_call(
        kern,
        out_shape=(
            jax.ShapeDtypeStruct((nb, length, D_MODEL), BF16),
            jax.ShapeDtypeStruct((nb, N_HEADS, D_HEAD, D_HEAD), F32),
            jax.ShapeDtypeStruct((nb, N_HEADS, D_HEAD, D_HEAD), F32),
            jax.ShapeDtypeStruct((nb, tail, GDN_QKV), F32),
        ),
        grid=(nb // bb, nsteps),
        in_specs=[
            pl.BlockSpec((bb * rows, IN_PAD), lambda b, i: (blk0 + b * nsteps + i, 0)),
            pl.BlockSpec(state_shape, init_idx),
            pl.BlockSpec(state_shape, init_idx),
            pl.BlockSpec((bb, tail, GDN_QKV), init_idx3),
            rope_spec,
            rope_spec,
        ] + [whole(shape) for shape in const_shapes],
        out_specs=(
            pl.BlockSpec((bb, rows, D_MODEL), lambda b, i: (b, i, 0)),
            pl.BlockSpec(state_shape, lambda b, i: (b, 0, 0, 0)),
            pl.BlockSpec(state_shape, lambda b, i: (b, 0, 0, 0)),
            pl.BlockSpec((bb, tail, GDN_QKV), lambda b, i: (b, 0, 0)),
        ),
        scratch_shapes=[pltpu.VMEM((bb, XS_TOP + rows, GDN_QKV), F32)],
        compiler_params=pltpu.CompilerParams(dimension_semantics=("arbitrary", "arbitrary"),
                                             vmem_limit_bytes=VMEM_LIMIT),
        name="mixer",
    )(proj, sret0, sgdn0, cq0, rope[0], rope[1], *const_arrays)


def _proj_mixer(x, norm_w, w_bf, nseq, sret0, sgdn0, cq0, rope, consts, *, c, nch):
    rows = nch * c
    total = x.shape[0]
    length = total // nseq
    assert total % nseq == 0 and length % (2 * rows) == 0
    nblk = total // rows
    steps_per_seq = length // (2 * rows)
    const_arrays, const_shapes = _mixer_const_operands(c, c, consts)
    whole = lambda shape, **kw: pl.BlockSpec(shape, lambda s: (0,) * len(shape), **kw)
    rope_spec = pl.BlockSpec((2 * rows, LANES), lambda s: (lax.rem(s, steps_per_seq), 0))
    state_shape = (1, N_HEADS, D_HEAD, D_HEAD)
    tail = CONV_GDN - 1
    kern = functools.partial(_proj_mixer_kernel, c=c, nch=nch, steps_per_seq=steps_per_seq)
    return pl.pallas_call(
        kern,
        out_shape=(
            jax.ShapeDtypeStruct((total, D_MODEL), BF16),
            jax.ShapeDtypeStruct((nseq, N_HEADS, D_HEAD, D_HEAD), F32),
            jax.ShapeDtypeStruct((nseq, N_HEADS, D_HEAD, D_HEAD), F32),
            jax.ShapeDtypeStruct((nseq, tail, GDN_QKV), F32),
        ),
        grid=(nblk // 2,),
        in_specs=[
            pl.BlockSpec((rows, D_MODEL), lambda s: (0, 0)),
            pl.BlockSpec((rows, D_MODEL), lambda s: (2 * s + 1, 0)),
            pl.BlockSpec((rows, D_MODEL), lambda s: (jnp.minimum(2 * s + 2, nblk - 1), 0)),
            whole((1, D_MODEL)),
            whole((D_MODEL, IN_PAD), pipeline_mode=pl.Buffered(1)),
            whole(state_shape),
            whole(state_shape),
            whole((1, tail, GDN_QKV)),
            rope_spec,
            rope_spec,
        ] + [whole(shape) for shape in const_shapes],
        out_specs=(
            pl.BlockSpec((2 * rows, D_MODEL), lambda s: (s, 0)),
            pl.BlockSpec(state_shape, lambda s: (s // steps_per_seq, 0, 0, 0)),
            pl.BlockSpec(state_shape, lambda s: (s // steps_per_seq, 0, 0, 0)),
            pl.BlockSpec((1, tail, GDN_QKV), lambda s: (s // steps_per_seq, 0, 0)),
        ),
        scratch_shapes=[pltpu.VMEM((rows, IN_PAD), F32), pltpu.VMEM((rows, IN_PAD), F32),
                        pltpu.VMEM((1, XS_TOP + rows, GDN_QKV), F32)],
        compiler_params=pltpu.CompilerParams(dimension_semantics=("arbitrary",),
                                             vmem_limit_bytes=VMEM_LIMIT),
        name="proj_mixer",
    )(x, x, x, norm_w, w_bf, sret0, sgdn0, cq0, rope[0], rope[1], *const_arrays)


FFN_COL_CHUNK = D_FF // 2


def _ffn_kernel(x_ref, mix_ref, tail0_ref, wout_ref, nffn_ref, wup_ref, cw_ref, wdn_ref, nfin_ref,
                y_ref, tail_ref, full_ref, *, tm, stride):
    t = pl.program_id(1)
    carry = (CONV_FFN - 1) * stride
    base = _round_up(carry, SUBLANES)

    @pl.when(t == 0)
    def _():
        full_ref[base - carry:base, :] = tail0_ref[0]

    x1 = x_ref[...] + jnp.dot(mix_ref[...], wout_ref[...], preferred_element_type=F32)
    h = _rms(x1, nffn_ref[...]).astype(BF16)
    full_ref[base:base + tm, :] = jnp.dot(h, wup_ref[...], preferred_element_type=F32)

    def conv_cols(lo):
        acc = full_ref[base - carry:base - carry + tm, lo:lo + FFN_COL_CHUNK] * cw_ref[0:1, lo:lo + FFN_COL_CHUNK]
        for i in range(1, CONV_FFN):
            r0 = base - carry + i * stride
            acc = acc + full_ref[r0:r0 + tm, lo:lo + FFN_COL_CHUNK] * cw_ref[i:i + 1, lo:lo + FFN_COL_CHUNK]
        return acc

    x2 = x1
    for j in range(D_FF // FFN_COL_CHUNK):
        lo = j * FFN_COL_CHUNK
        act = (_silu(conv_cols(lo)) * conv_cols(D_FF + lo)).astype(BF16)
        x2 = x2 + jnp.dot(act, wdn_ref[lo:lo + FFN_COL_CHUNK, :], preferred_element_type=F32)
    y_ref[...] = _rms(x2, nfin_ref[...])

    new_tail = full_ref[base + tm - carry:base + tm, :]
    full_ref[base - carry:base, :] = new_tail
    tail_ref[0] = new_tail


def _ffn(x, mix, tail0, weights, *, nseq, tm, stride, shared_init):
    wout, nffn, wup, cw, wdn, nfin = weights
    rows = x.shape[0]
    assert rows % (nseq * tm) == 0
    nt = rows // (nseq * tm)
    carry = (CONV_FFN - 1) * stride
    base = _round_up(carry, SUBLANES)
    assert tm >= carry
    resident = lambda shape: pl.BlockSpec(shape, lambda b, t: (0, 0), pipeline_mode=pl.Buffered(1))
    small = lambda shape: pl.BlockSpec(shape, lambda b, t: (0, 0))
    tail_idx = (lambda b, t: (0, 0, 0)) if shared_init else (lambda b, t: (b, 0, 0))
    kern = functools.partial(_ffn_kernel, tm=tm, stride=stride)
    return pl.pallas_call(
        kern,
        out_shape=(
            jax.ShapeDtypeStruct((rows, D_MODEL), F32),
            jax.ShapeDtypeStruct((nseq, carry, 2 * D_FF), F32),
        ),
        grid=(nseq, nt),
        in_specs=[
            pl.BlockSpec((tm, D_MODEL), lambda b, t: (b * nt + t, 0)),
            pl.BlockSpec((tm, D_MODEL), lambda b, t: (b * nt + t, 0)),
            pl.BlockSpec((1, carry, 2 * D_FF), tail_idx),
            resident((D_MODEL, D_MODEL)),
            small((1, D_MODEL)),
            resident((D_MODEL, 2 * D_FF)),
            small((CONV_FFN, 2 * D_FF)),
            resident((D_FF, D_MODEL)),
            small((1, D_MODEL)),
        ],
        out_specs=(
            pl.BlockSpec((tm, D_MODEL), lambda b, t: (b * nt + t, 0)),
            pl.BlockSpec((1, carry, 2 * D_FF), lambda b, t: (b, 0, 0)),
        ),
        scratch_shapes=[pltpu.VMEM((base + tm, 2 * D_FF), F32)],
        compiler_params=pltpu.CompilerParams(dimension_semantics=("arbitrary", "arbitrary"),
                                             vmem_limit_bytes=VMEM_LIMIT),
        name="out_ffn",
    )(x, mix, tail0, wout, nffn, wup, cw, wdn, nfin)


def kernel(x_prompt, x_sample, state_ret, state_gdn, state_conv_qkv, state_ffn_conv, meta_tokens, norm_mix,
           w_in, conv_gdn, gdn_a_log, gdn_dt_bias, norm_ret, norm_gdn, w_out, norm_ffn, w_up, conv_ffn,
           w_down, norm_final):
    depth = w_in.shape[0]
    assert depth == 1
    nbp, seq, _ = x_prompt.shape
    nbs, dec_seq, _ = x_sample.shape
    assert dec_seq <= SAMPLE_PAD and nbs % SAMPLE_GROUP == 0
    assert seq % (2 * PROMPT_CHUNK * PROMPT_CHUNKS_PER_STEP) == 0
    layer = 0

    w_in_bf = jnp.pad(w_in[layer].astype(BF16), ((0, 0), (0, IN_PAD - IN_WIDTH)))
    w_out_bf = w_out[layer].astype(BF16)
    w_up_bf = w_up[layer].astype(BF16)
    w_down_bf = w_down[layer].astype(BF16)
    row = lambda v: v.reshape(1, -1).astype(F32)
    pad_ba = lambda v: jnp.zeros((1, LANES), F32).at[0, N_HEADS:2 * N_HEADS].set(v.astype(F32))
    mixer_consts = (conv_gdn[layer], pad_ba(gdn_a_log[layer]), pad_ba(gdn_dt_bias[layer]),
                    norm_ret[layer].reshape(N_HEADS, D_HEAD), row(norm_gdn[layer]))
    ffn_weights = (w_out_bf, row(norm_ffn[layer]), w_up_bf, conv_ffn[layer], w_down_bf, row(norm_final))
    nmix = row(norm_mix[layer])

    assert seq % N_META == 0 and (seq + N_META) % SAMPLE_PAD == 0
    rope = _rope_tables(jnp.concatenate([
        N_META + jnp.arange(seq, dtype=jnp.int32),
        jnp.arange(N_META, dtype=jnp.int32),
        PAST_LEN + jnp.arange(SAMPLE_PAD, dtype=jnp.int32)]))
    rope_meta_row0, rope_sample_row0 = seq, seq + N_META

    xs_pad = jnp.pad(x_sample, ((0, 0), (0, SAMPLE_PAD - dec_seq), (0, 0))).reshape(nbs * SAMPLE_PAD, D_MODEL)
    small_rows = jnp.concatenate([xs_pad, meta_tokens.astype(F32)], axis=0)
    n_small = small_rows.shape[0]
    proj_small = _proj(small_rows, nmix, w_in_bf, tm=n_small // 2)
    meta_row0 = nbs * SAMPLE_PAD

    zero_state = jnp.zeros((1, N_HEADS, D_HEAD, D_HEAD), F32)
    zero_cq = jnp.zeros((1, CONV_GDN - 1, GDN_QKV), F32)
    mix_m, sret_m, sgdn_m, cq_m = _mixer(proj_small, meta_row0, 1, N_META, zero_state, zero_state, zero_cq, rope,
                                         rope_meta_row0, mixer_consts, bb=1, c=N_META, nch=1, n_valid=N_META,
                                         shared_init=True)
    zero_cf = jnp.zeros((1, CONV_FFN - 1, 2 * D_FF), F32)
    _, cf_m = _ffn(meta_tokens.astype(F32), mix_m.reshape(N_META, D_MODEL), zero_cf, ffn_weights,
                   nseq=1, tm=N_META, stride=1, shared_init=True)

    mix_s, sret_s, sgdn_s, cq_s = _mixer(proj_small, 0, nbs, SAMPLE_PAD, state_ret[layer], state_gdn[layer],
                                         state_conv_qkv[layer], rope, rope_sample_row0, mixer_consts,
                                         bb=SAMPLE_SEQS_PER_STEP, c=SAMPLE_PAD, nch=1, n_valid=dec_seq,
                                         shared_init=False)
    ng = nbs // SAMPLE_GROUP
    to_tmajor = lambda a: a.reshape(ng, SAMPLE_GROUP, a.shape[1], a.shape[2]).transpose(0, 2, 1, 3)
    xs_t = to_tmajor(x_sample).reshape(nbs * dec_seq, D_MODEL)
    mix_t = to_tmajor(mix_s[:, :dec_seq]).reshape(nbs * dec_seq, D_MODEL)
    cf0_t = to_tmajor(state_ffn_conv[layer]).reshape(ng, (CONV_FFN - 1) * SAMPLE_GROUP, 2 * D_FF)
    y_s_t, cf_s_t = _ffn(xs_t, mix_t, cf0_t, ffn_weights, nseq=ng, tm=dec_seq * SAMPLE_GROUP,
                         stride=SAMPLE_GROUP, shared_init=False)
    y_sample = y_s_t.reshape(ng, dec_seq, SAMPLE_GROUP, D_MODEL).transpose(0, 2, 1, 3).reshape(nbs, dec_seq, D_MODEL)
    cf_s = cf_s_t.reshape(ng, CONV_FFN - 1, SAMPLE_GROUP, 2 * D_FF).transpose(0, 2, 1, 3).reshape(
        nbs, CONV_FFN - 1, 2 * D_FF)

    xp = x_prompt.reshape(nbp * seq, D_MODEL)
    mix_p, sret_p, sgdn_p, cq_p = _proj_mixer(xp, nmix, w_in_bf, nbp, sret_m, sgdn_m, cq_m, rope, mixer_consts,
                                              c=PROMPT_CHUNK, nch=PROMPT_CHUNKS_PER_STEP)
    y_p, cf_p = _ffn(xp, mix_p, cf_m, ffn_weights, nseq=nbp, tm=256, stride=1, shared_init=True)
    y_prompt = y_p.reshape(nbp, seq, D_MODEL)

    return (y_prompt, y_sample, sret_p[None], sgdn_p[None], cq_p[None], cf_p[None],
            sret_s[None], sgdn_s[None], cq_s[None], cf_s[None])
```

```python
import functools

import jax
import numpy as np
import jax.numpy as jnp
from jax import lax
from jax.experimental import pallas as pl
from jax.experimental.pallas import tpu as pltpu

F32 = jnp.float32
BF16 = jnp.bfloat16

D_MODEL = 1024
N_META = 16
PAST_LEN = 16384
N_HEADS = 4
D_HEAD = 128
D_GRP = N_HEADS * D_HEAD
GDN_QKV = 3 * D_GRP
CONV_GDN = 4
CONV_FFN = 3
D_FF = 2816
ROPE_THETA = 10000.0
EPS = 1e-6

OFF_RQ, OFF_RK, OFF_RV, OFF_RG = 0, D_GRP, 2 * D_GRP, 3 * D_GRP
OFF_QKV = 4 * D_GRP
OFF_GG = OFF_QKV + GDN_QKV
OFF_BA = OFF_GG + D_GRP
IN_WIDTH = OFF_BA + 2 * N_HEADS
LANES = 128
SUBLANES = 8
IN_PAD = OFF_BA + LANES

PROMPT_CHUNK = 128
PROMPT_CHUNKS_PER_STEP = 2
SAMPLE_PAD = 8
SAMPLE_SEQS_PER_STEP = 8
SAMPLE_GROUP = 32
VMEM_LIMIT = 56 * 1024 * 1024


def _round_up(n, m):
    return (n + m - 1) // m * m


def _mm(a, b):
    return jnp.dot(a.astype(BF16), b.astype(BF16), preferred_element_type=F32)


def _mm_nt(a, b):
    return lax.dot_general(a.astype(BF16), b.astype(BF16), (((1,), (1,)), ((), ())),
                           preferred_element_type=F32)


def _mm_tn(a, b):
    return lax.dot_general(a.astype(BF16), b.astype(BF16), (((0,), (0,)), ((), ())),
                           preferred_element_type=F32)


def _split2(x):
    hi = x.astype(BF16)
    lo = (x - hi.astype(F32)).astype(BF16)
    return hi, lo


def _split3(x):
    p0 = x.astype(BF16)
    r = x - p0.astype(F32)
    p1 = r.astype(BF16)
    p2 = (r - p1.astype(F32)).astype(BF16)
    return p0, p1, p2


def _mm3(a2, b2):
    (ah, al), (bh, bl) = a2, b2
    dot = lambda x, y: jnp.dot(x, y, preferred_element_type=F32)
    return dot(ah, bh) + (dot(ah, bl) + dot(al, bh))


def _silu(x):
    return x * jax.nn.sigmoid(x)


def _rms(x, w):
    return x * lax.rsqrt(jnp.mean(x * x, axis=-1, keepdims=True) + EPS) * w


def _rope_kernel(invf_ref, cos_ref, sin_ref, *, segments):
    shape = cos_ref.shape
    r = lax.broadcasted_iota(jnp.int32, shape, 0)
    pos = r + (segments[0][1] - segments[0][0])
    for row0, pos0 in segments[1:]:
        pos = jnp.where(r >= row0, r + (pos0 - row0), pos)
    ang = pos.astype(F32) * invf_ref[...]
    lane = lax.broadcasted_iota(jnp.int32, shape, 1)
    sin = jnp.sin(ang)
    cos_ref[...] = jnp.cos(ang)
    sin_ref[...] = jnp.where(lane < D_HEAD // 2, -sin, sin)


def _rope_tables(n, segments):
    half = D_HEAD // 2
    inv_freq = ROPE_THETA ** (-jnp.arange(half, dtype=F32) / half)
    invf2 = jnp.concatenate([inv_freq, inv_freq]).reshape(1, LANES)
    out = jax.ShapeDtypeStruct((n, LANES), F32)
    return pl.pallas_call(functools.partial(_rope_kernel, segments=tuple(segments)), out_shape=(out, out),
                          name="rope_tables")(invf2)


def _proj_kernel(x_ref, nw_ref, w_ref, o_ref):
    h = _rms(x_ref[...], nw_ref[...])
    o_ref[...] = jnp.dot(h.astype(BF16), w_ref[...], preferred_element_type=F32)


def _proj(x, norm_w, w_bf, tm):
    rows = x.shape[0]
    assert rows % tm == 0
    return pl.pallas_call(
        _proj_kernel,
        out_shape=jax.ShapeDtypeStruct((rows, IN_PAD), F32),
        grid=(rows // tm,),
        in_specs=[
            pl.BlockSpec((tm, D_MODEL), lambda i: (i, 0)),
            pl.BlockSpec((1, D_MODEL), lambda i: (0, 0)),
            pl.BlockSpec((D_MODEL, IN_PAD), lambda i: (0, 0), pipeline_mode=pl.Buffered(1)),
        ],
        out_specs=pl.BlockSpec((tm, IN_PAD), lambda i: (i, 0)),
        compiler_params=pltpu.CompilerParams(dimension_semantics=("arbitrary",),
                                             vmem_limit_bytes=VMEM_LIMIT),
        name="in_proj",
    )(x, norm_w, w_bf)


def _unit_lower_inverses(mats, c, tick=lambda: None):
    ri = lax.broadcasted_iota(jnp.int32, (c, c), 0)
    ci = lax.broadcasted_iota(jnp.int32, (c, c), 1)
    eye = (ri == ci).astype(F32)
    diag_blk = (ri // SUBLANES) == (ci // SUBLANES)
    ad = [jnp.where(diag_blk, a, 0.0) for a in mats]
    a2 = [_mm(x, x) for x in ad]
    tick()
    a4 = [_mm(x, x) for x in a2]
    t = [eye - x for x in ad]
    t = [x + _mm(x, s) for x, s in zip(t, a2)]
    tick()
    t = [x + _mm(x, s) for x, s in zip(t, a4)]
    tick()
    s = SUBLANES
    while s < c:
        level = ((ri // (2 * s)) == (ci // (2 * s))) & ((ri // s) != (ci // s))
        off = [jnp.where(level, a, 0.0) for a in mats]
        lt = [_mm(o, x) for o, x in zip(off, t)]
        tick()
        t = [x - _mm(x, y) for x, y in zip(t, lt)]
        tick()
        s *= 2
    return t


XS_TOP = SUBLANES


def _mixer_init(sret0_ref, sgdn0_ref, cq0_ref, sret_ref, sgdn_ref, xs_ref):
    tail = CONV_GDN - 1
    sret_ref[...] = sret0_ref[...]
    sgdn_ref[...] = sgdn0_ref[...]
    xs_ref[:, XS_TOP - tail:XS_TOP, :] = cq0_ref[...]


def _mixer_block(getp, put_mix, cos2, sin2, const_refs, sret_ref, sgdn_ref, cq_ref, xs_ref, *,
                 bb, c, nch, n_valid, tick=lambda: None):
    (tril_ref, triu_ref, dint_ref, qdec_ref, kdec_ref, cdec_ref, cw_ref, alog_ref, dtb_ref, nret_ref,
     ngdn_ref) = const_refs
    rows = nch * c
    tail = CONV_GDN - 1
    top = XS_TOP
    ri = lax.broadcasted_iota(jnp.int32, (c, c), 0)
    cj = lax.broadcasted_iota(jnp.int32, (c, c), 1)
    tri = ri >= cj
    strict = ri > cj
    tril_bf = tril_ref[...]
    triu_bf = triu_ref[...]
    scale = D_HEAD ** -0.5
    heads = range(N_HEADS)
    seqs = range(bb)
    chunks = [slice(j * c, (j + 1) * c) for j in range(nch)]
    join = lambda parts: parts[0] if len(parts) == 1 else jnp.concatenate(parts, axis=0)

    ret = {}
    for b in seqs:
        for h in heads:
            lo = h * D_HEAD
            q = getp(b, OFF_RQ + lo, OFF_RQ + lo + D_HEAD)
            k = getp(b, OFF_RK + lo, OFF_RK + lo + D_HEAD)
            v = getp(b, OFF_RV + lo, OFF_RV + lo + D_HEAD)
            qr = q * cos2 + pltpu.roll(q, D_HEAD // 2, 1) * sin2
            kr = (k * cos2 + pltpu.roll(k, D_HEAD // 2, 1) * sin2) * scale
            ret[b, h] = (qr, kr, v)
    tick()
    rtasks = [(b, h, j) for j in range(nch) for b in seqs for h in heads]
    scores = {t: _mm_nt(ret[t[0], t[1]][0][chunks[t[2]]], ret[t[0], t[1]][1][chunks[t[2]]]) * dint_ref[t[1]]
              for t in rtasks}
    tick()
    intra = {t: _mm(scores[t], ret[t[0], t[1]][2][chunks[t[2]]]) for t in rtasks}
    tick()
    kv = {t: _mm_tn(ret[t[0], t[1]][1][chunks[t[2]]] * kdec_ref[t[1]], ret[t[0], t[1]][2][chunks[t[2]]])
          for t in rtasks}
    tick()
    pairs = [(b, h) for b in seqs for h in heads]
    s_ret = {bh: sret_ref[bh[0], bh[1]] for bh in pairs}
    o_ret = {}
    for j in range(nch):
        for b, h in pairs:
            o_ret[b, h, j] = intra[b, h, j] + _mm(ret[b, h][0][chunks[j]] * qdec_ref[h], s_ret[b, h])
        for b, h in pairs:
            s_ret[b, h] = cdec_ref[h] * s_ret[b, h] + kv[b, h, j]
        tick()
    o_ret = {bh: join([o_ret[bh[0], bh[1], j] for j in range(nch)]) for bh in pairs}
    mu = {bh: jnp.mean(o_ret[bh], axis=-1, keepdims=True) for bh in pairs}
    cen = {bh: o_ret[bh] - mu[bh] for bh in pairs}
    var = {bh: jnp.mean(cen[bh] * cen[bh], axis=-1, keepdims=True) for bh in pairs}
    for b, h in pairs:
        lo = h * D_HEAD
        gate = getp(b, OFF_RG + lo, OFF_RG + lo + D_HEAD)
        o = cen[b, h] * lax.rsqrt(var[b, h] + EPS) * nret_ref[h:h + 1, :]
        put_mix(b, lo, o * _silu(gate))
        sret_ref[b, h] = s_ret[b, h]
    tick()

    qkvs, beta_alls, cum_cols, cum_rows, cum_tots = {}, {}, {}, {}, {}
    ones_bf = jnp.ones((D_HEAD, c), BF16)
    for b in seqs:
        xs_ref[b, top:top + rows, :] = getp(b, OFF_QKV, OFF_QKV + GDN_QKV)
        conv = xs_ref[b, top - tail:top - tail + rows, :] * cw_ref[0:1, :]
        for i in range(1, CONV_GDN):
            conv = conv + xs_ref[b, top - tail + i:top - tail + i + rows, :] * cw_ref[i:i + 1, :]
        last = n_valid if nch == 1 else rows
        new_tail = xs_ref[b, top + last - tail:top + last, :]
        xs_ref[b, top - tail:top, :] = new_tail
        cq_ref[b] = new_tail
        qkvs[b] = _silu(conv)
        tick()

        ba = getp(b, OFF_BA, OFF_BA + LANES)
        beta_all = jax.nn.sigmoid(ba)
        z = ba + dtb_ref[...]
        softplus = jnp.maximum(z, 0.0) + jnp.log1p(jnp.exp(-jnp.abs(z)))
        g_all = -jnp.exp(alog_ref[...]) * softplus
        if n_valid < c:
            row = lax.broadcasted_iota(jnp.int32, (rows, LANES), 0)
            rowmask = (row < n_valid).astype(F32)
            beta_all = beta_all * rowmask
            g_all = g_all * rowmask
        beta_alls[b] = beta_all
        for j, rs in enumerate(chunks):
            parts = _split3(g_all[rs])
            cum_cols[b, j] = sum(jnp.dot(tril_bf, g, preferred_element_type=F32) for g in parts)
            cum_tots[b, j] = sum(jnp.dot(ones_bf, g, preferred_element_type=F32) for g in parts)
            cum_rows[b, j] = sum(lax.dot_general(g, triu_bf, (((0,), (0,)), ((), ())),
                                                 preferred_element_type=F32) for g in parts)
    tick()

    tasks = [(j, b, h) for j in range(nch) for b in seqs for h in heads]
    qs, ks, vs, betas, ecums, kdecs, cdecs, amats, qkms = [], [], [], [], [], [], [], [], []
    raw = [(qkvs[b][chunks[j], h * D_HEAD:(h + 1) * D_HEAD],
            qkvs[b][chunks[j], D_GRP + h * D_HEAD:D_GRP + (h + 1) * D_HEAD]) for j, b, h in tasks]
    sumsq = [(jnp.sum(q * q, axis=-1, keepdims=True), jnp.sum(k * k, axis=-1, keepdims=True)) for q, k in raw]
    for i, (j, b, h) in enumerate(tasks):
        rs = chunks[j]
        lo = h * D_HEAD
        v = qkvs[b][rs, 2 * D_GRP + lo:2 * D_GRP + lo + D_HEAD]
        q = raw[i][0] * lax.rsqrt(sumsq[i][0] + EPS) * scale
        k = raw[i][1] * lax.rsqrt(sumsq[i][1] + EPS)
        beta = jnp.broadcast_to(beta_alls[b][rs, h:h + 1], (c, LANES))
        cum = jnp.broadcast_to(cum_cols[b, j][:, N_HEADS + h:N_HEADS + h + 1], (c, LANES))
        cum_row = cum_rows[b, j][N_HEADS + h:N_HEADS + h + 1, :]
        dmask = jnp.exp(jnp.where(tri, cum[:, :c] - cum_row, -jnp.inf))
        cum_last = jnp.broadcast_to(cum_tots[b, j][:, N_HEADS + h:N_HEADS + h + 1], (D_HEAD, LANES))
        kq = _mm_nt(jnp.concatenate([k, q], axis=0), k)
        amats.append(jnp.where(strict, beta[:, :c] * kq[:c] * dmask, 0.0))
        qkms.append(jnp.where(tri, kq[c:] * dmask, 0.0))
        qs.append(q)
        ks.append(k)
        vs.append(v)
        betas.append(beta)
        ecums.append(jnp.exp(cum))
        kdecs.append(jnp.exp(cum_last[:c] - cum))
        cdecs.append(jnp.exp(cum_last))
        if i % N_HEADS == N_HEADS - 1:
            tick()

    tinv = _unit_lower_inverses(amats, c, tick)
    sols = []
    for i in range(len(tasks)):
        rhs = jnp.concatenate([vs[i] * betas[i], ks[i] * (betas[i] * ecums[i])], axis=1)
        sols.append(_mm3(_split2(tinv[i]), _split2(rhs)))
    tick()

    s_gdn = {bh: sgdn_ref[bh[0], bh[1]] for bh in pairs}
    o_gdn = {}
    for j in range(nch):
        idx = {tasks[i][1:]: i for i in range(len(tasks)) if tasks[i][0] == j}
        lhs = {bh: jnp.concatenate([sols[i][:, D_HEAD:], qs[i] * ecums[i]], axis=0) for bh, i in idx.items()}
        both = {bh: _mm(lhs[bh], s_gdn[bh]) for bh in idx}
        tick()
        w = {bh: sols[i][:, :D_HEAD] - both[bh][:c] for bh, i in idx.items()}
        for bh, i in idx.items():
            o_gdn[bh + (j,)] = both[bh][c:] + _mm(qkms[i], w[bh])
        tick()
        upd = {bh: _mm_tn(ks[i] * kdecs[i], w[bh]) for bh, i in idx.items()}
        for bh, i in idx.items():
            s_gdn[bh] = cdecs[i] * s_gdn[bh] + upd[bh]
        tick()
    o_gdn = {bh: join([o_gdn[bh + (j,)] for j in range(nch)]) for bh in pairs}
    msq = {bh: jnp.mean(o_gdn[bh] * o_gdn[bh], axis=-1, keepdims=True) for bh in pairs}
    for b, h in pairs:
        lo = h * D_HEAD
        gate = getp(b, OFF_GG + lo, OFF_GG + lo + D_HEAD)
        o = o_gdn[b, h] * lax.rsqrt(msq[b, h] + EPS) * ngdn_ref[...]
        put_mix(b, D_GRP + lo, o * _silu(gate))
        sgdn_ref[b, h] = s_gdn[b, h]


N_MIXER_CONSTS = 11


def _mixer_kernel(p_ref, sret0_ref, sgdn0_ref, cq0_ref, cos_ref, sin_ref, *rest, bb, c, nch, n_valid):
    const_refs, (mix_ref, sret_ref, sgdn_ref, cq_ref, xs_ref) = rest[:N_MIXER_CONSTS], rest[N_MIXER_CONSTS:]
    rows = nch * c

    @pl.when(pl.program_id(1) == 0)
    def _():
        _mixer_init(sret0_ref, sgdn0_ref, cq0_ref, sret_ref, sgdn_ref, xs_ref)

    def put_mix(b, lo, value):
        mix_ref[b, :, lo:lo + D_HEAD] = value.astype(mix_ref.dtype)

    _mixer_block(lambda b, lo, hi: p_ref[b * rows:(b + 1) * rows, lo:hi], put_mix, cos_ref[...], sin_ref[...],
                 const_refs, sret_ref, sgdn_ref, cq_ref, xs_ref, bb=bb, c=c, nch=nch, n_valid=n_valid)


PROJ_PANEL = 2 * LANES


def _mixer_block_ticks(bb, c, nch):
    levels = (c // SUBLANES).bit_length() - 1
    retention = 4 + nch + 1
    gdn_prep = bb + 1 + bb * nch
    inverse = 3 + 2 * levels
    return retention + gdn_prep + inverse + 1 + 3 * nch


def _proj_mixer_kernel(x0_ref, xa_ref, xb_ref, nw_ref, w_ref, sret0_ref, sgdn0_ref, cq0_ref, cos_ref, sin_ref,
                       *rest, c, nch, steps_per_seq):
    const_refs = rest[:N_MIXER_CONSTS]
    mix_ref, sret_ref, sgdn_ref, cq_ref, pja_ref, pjb_ref, xs_ref = rest[N_MIXER_CONSTS:]
    step = pl.program_id(0)
    rows = nch * c

    def project(x_ref, dst_ref, n_ticks):
        h = _rms(x_ref[...], nw_ref[...]).astype(BF16)
        panels = [(lo, min(lo + PROJ_PANEL, IN_PAD)) for lo in range(0, IN_PAD, PROJ_PANEL)]
        n_panels = len(panels)
        calls = [0]

        def emit():
            lo, hi = panels.pop(0)
            dst_ref[:, lo:hi] = jnp.dot(h, w_ref[:, lo:hi], preferred_element_type=F32)

        def tick():
            calls[0] += 1
            due = min(n_panels, -(-calls[0] * n_panels // n_ticks))
            while n_panels - len(panels) < due:
                emit()

        def flush():
            while panels:
                emit()

        return tick, flush

    @pl.when(step == 0)
    def _():
        project(x0_ref, pja_ref, 1)[1]()

    @pl.when(lax.rem(step, steps_per_seq) == 0)
    def _():
        _mixer_init(sret0_ref, sgdn0_ref, cq0_ref, sret_ref, sgdn_ref, xs_ref)

    for half, (cur_ref, x_next_ref, nxt_ref) in enumerate(((pja_ref, xa_ref, pjb_ref), (pjb_ref, xb_ref, pja_ref))):
        r0 = half * rows
        tick, flush = project(x_next_ref, nxt_ref, _mixer_block_ticks(1, c, nch))

        def put_mix(b, lo, value, r0=r0):
            mix_ref[r0:r0 + rows, lo:lo + D_HEAD] = value.astype(mix_ref.dtype)

        _mixer_block(lambda b, lo, hi, cur_ref=cur_ref: cur_ref[:, lo:hi], put_mix,
                     cos_ref[r0:r0 + rows, :], sin_ref[r0:r0 + rows, :], const_refs,
                     sret_ref, sgdn_ref, cq_ref, xs_ref, bb=1, c=c, nch=nch, n_valid=c, tick=tick)
        flush()


def _retention_decay_tables(c, n_valid):
    f32 = np.float32
    lg = np.log1p(-np.power(f32(2.0), f32(-5.0) - np.arange(N_HEADS, dtype=f32)))[:, None].astype(f32)
    idx = np.arange(c, dtype=f32)
    diff = idx[:, None] - idx[None, :]
    dint = np.where(diff[None] >= 0, np.exp(lg[:, :, None] * np.maximum(diff[None], 0)), f32(0.0)).astype(f32)
    qdec = np.exp(lg * (idx + f32(1.0))).astype(f32)
    kdec = np.where(idx[None, :] < n_valid, np.exp(lg * np.minimum(f32(n_valid) - f32(1.0) - idx, c)), f32(0.0))
    cdec = np.exp(lg * f32(n_valid)).astype(f32)
    bc = lambda t: np.broadcast_to(t.astype(f32)[:, :, None], t.shape + (LANES,))
    return dint, bc(qdec), bc(kdec), np.broadcast_to(cdec[:, :, None], (N_HEADS, 1, LANES))


def _mixer_const_operands(c, n_valid, consts):
    cw, alog, dtb, nret, ngdn = consts
    dint, qdec, kdec, cdec = _retention_decay_tables(c, n_valid)
    idx = np.arange(c)
    tril = (idx[:, None] >= idx[None, :]).astype(np.float32)
    arrays = (jnp.asarray(tril, BF16), jnp.asarray(tril.T, BF16), jnp.asarray(dint), jnp.asarray(qdec),
              jnp.asarray(kdec), jnp.asarray(cdec), cw, alog, dtb, nret, ngdn)
    assert len(arrays) == N_MIXER_CONSTS
    return arrays, [a.shape for a in arrays]


def _mixer(proj, row0, nb, length, sret0, sgdn0, cq0, rope, rope_row0, consts, *, bb, c, nch, n_valid,
           shared_init):
    rows = nch * c
    assert nb % bb == 0 and length % rows == 0 and rope_row0 % rows == 0 and row0 % (bb * length) == 0
    assert not shared_init or bb == 1
    assert n_valid == c or nch == 1
    assert bb == 1 or length == rows
    const_arrays, const_shapes = _mixer_const_operands(c, n_valid, consts)
    nsteps = length // rows
    blk0 = row0 // (bb * rows)
    init_idx = (lambda b, i: (0, 0, 0, 0)) if shared_init else (lambda b, i: (b, 0, 0, 0))
    init_idx3 = (lambda b, i: (0, 0, 0)) if shared_init else (lambda b, i: (b, 0, 0))
    whole = lambda shape: pl.BlockSpec(shape, lambda b, i: (0,) * len(shape))
    rope_spec = pl.BlockSpec((rows, LANES), lambda b, i: (rope_row0 // rows + i, 0))
    state_shape = (bb, N_HEADS, D_HEAD, D_HEAD)
    tail = CONV_GDN - 1
    kern = functools.partial(_mixer_kernel, bb=bb, c=c, nch=nch, n_valid=n_valid)
    return pl.pallas_call(
        kern,
        out_shape=(
            jax.ShapeDtypeStruct((nb, length, D_MODEL), BF16),
            jax.ShapeDtypeStruct((nb, N_HEADS, D_HEAD, D_HEAD), F32),
            jax.ShapeDtypeStruct((nb, N_HEADS, D_HEAD, D_HEAD), F32),
            jax.ShapeDtypeStruct((nb, tail, GDN_QKV), F32),
        ),
        grid=(nb // bb, nsteps),
        in_specs=[
            pl.BlockSpec((bb * rows, IN_PAD), lambda b, i: (blk0 + b * nsteps + i, 0)),
            pl.BlockSpec(state_shape, init_idx),
            pl.BlockSpec(state_shape, init_idx),
            pl.BlockSpec((bb, tail, GDN_QKV), init_idx3),
            rope_spec,
            rope_spec,
        ] + [whole(shape) for shape in const_shapes],
        out_specs=(
            pl.BlockSpec((bb, rows, D_MODEL), lambda b, i: (b, i, 0)),
            pl.BlockSpec(state_shape, lambda b, i: (b, 0, 0, 0)),
            pl.BlockSpec(state_shape, lambda b, i: (b, 0, 0, 0)),
            pl.BlockSpec((bb, tail, GDN_QKV), lambda b, i: (b, 0, 0)),
        ),
        scratch_shapes=[pltpu.VMEM((bb, XS_TOP + rows, GDN_QKV), F32)],
        compiler_params=pltpu.CompilerParams(dimension_semantics=("arbitrary", "arbitrary"),
                                             vmem_limit_bytes=VMEM_LIMIT),
        name="mixer",
    )(proj, sret0, sgdn0, cq0, rope[0], rope[1], *const_arrays)


def _proj_mixer(x, norm_w, w_bf, nseq, sret0, sgdn0, cq0, rope, consts, *, c, nch):
    rows = nch * c
    total = x.shape[0]
    length = total // nseq
    assert total % nseq == 0 and length % (2 * rows) == 0
    nblk = total // rows
    steps_per_seq = length // (2 * rows)
    const_arrays, const_shapes = _mixer_const_operands(c, c, consts)
    whole = lambda shape, **kw: pl.BlockSpec(shape, lambda s: (0,) * len(shape), **kw)
    rope_spec = pl.BlockSpec((2 * rows, LANES), lambda s: (lax.rem(s, steps_per_seq), 0))
    state_shape = (1, N_HEADS, D_HEAD, D_HEAD)
    tail = CONV_GDN - 1
    kern = functools.partial(_proj_mixer_kernel, c=c, nch=nch, steps_per_seq=steps_per_seq)
    return pl.pallas_call(
        kern,
        out_shape=(
            jax.ShapeDtypeStruct((total, D_MODEL), BF16),
            jax.ShapeDtypeStruct((nseq, N_HEADS, D_HEAD, D_HEAD), F32),
            jax.ShapeDtypeStruct((nseq, N_HEADS, D_HEAD, D_HEAD), F32),
            jax.ShapeDtypeStruct((nseq, tail, GDN_QKV), F32),
        ),
        grid=(nblk // 2,),
        in_specs=[
            pl.BlockSpec((rows, D_MODEL), lambda s: (0, 0), pipeline_mode=pl.Buffered(1)),
            pl.BlockSpec((rows, D_MODEL), lambda s: (2 * s + 1, 0)),
            pl.BlockSpec((rows, D_MODEL), lambda s: (jnp.minimum(2 * s + 2, nblk - 1), 0)),
            whole((1, D_MODEL)),
            whole((D_MODEL, IN_PAD), pipeline_mode=pl.Buffered(1)),
            whole(state_shape),
            whole(state_shape),
            whole((1, tail, GDN_QKV)),
            rope_spec,
            rope_spec,
        ] + [whole(shape) for shape in const_shapes],
        out_specs=(
            pl.BlockSpec((2 * rows, D_MODEL), lambda s: (s, 0)),
            pl.BlockSpec(state_shape, lambda s: (s // steps_per_seq, 0, 0, 0)),
            pl.BlockSpec(state_shape, lambda s: (s // steps_per_seq, 0, 0, 0)),
            pl.BlockSpec((1, tail, GDN_QKV), lambda s: (s // steps_per_seq, 0, 0)),
        ),
        scratch_shapes=[pltpu.VMEM((rows, IN_PAD), F32), pltpu.VMEM((rows, IN_PAD), F32),
                        pltpu.VMEM((1, XS_TOP + rows, GDN_QKV), F32)],
        compiler_params=pltpu.CompilerParams(dimension_semantics=("arbitrary",),
                                             vmem_limit_bytes=VMEM_LIMIT),
        name="proj_mixer",
    )(x, x, x, norm_w, w_bf, sret0, sgdn0, cq0, rope[0], rope[1], *const_arrays)


FFN_COL_CHUNK = D_FF // 2


def _ffn_kernel(x_ref, mix_ref, tail0_ref, wout_ref, nffn_ref, wup_ref, cw_ref, wdn_ref, nfin_ref,
                y_ref, tail_ref, full_ref, *, tm, stride):
    t = pl.program_id(1)
    carry = (CONV_FFN - 1) * stride
    base = _round_up(carry, SUBLANES)

    @pl.when(t == 0)
    def _():
        full_ref[base - carry:base, :] = tail0_ref[0]

    x1 = x_ref[...] + jnp.dot(mix_ref[...], wout_ref[...], preferred_element_type=F32)
    h = _rms(x1, nffn_ref[...]).astype(BF16)
    full_ref[base:base + tm, :] = jnp.dot(h, wup_ref[...], preferred_element_type=F32)

    def conv_cols(lo):
        acc = full_ref[base - carry:base - carry + tm, lo:lo + FFN_COL_CHUNK] * cw_ref[0:1, lo:lo + FFN_COL_CHUNK]
        for i in range(1, CONV_FFN):
            r0 = base - carry + i * stride
            acc = acc + full_ref[r0:r0 + tm, lo:lo + FFN_COL_CHUNK] * cw_ref[i:i + 1, lo:lo + FFN_COL_CHUNK]
        return acc

    x2 = x1
    for j in range(D_FF // FFN_COL_CHUNK):
        lo = j * FFN_COL_CHUNK
        act = (_silu(conv_cols(lo)) * conv_cols(D_FF + lo)).astype(BF16)
        x2 = x2 + jnp.dot(act, wdn_ref[lo:lo + FFN_COL_CHUNK, :], preferred_element_type=F32)
    y_ref[...] = _rms(x2, nfin_ref[...])

    new_tail = full_ref[base + tm - carry:base + tm, :]
    full_ref[base - carry:base, :] = new_tail
    tail_ref[0] = new_tail


def _ffn(x, mix, tail0, weights, *, nseq, tm, stride, shared_init):
    wout, nffn, wup, cw, wdn, nfin = weights
    rows = x.shape[0]
    assert rows % (nseq * tm) == 0
    nt = rows // (nseq * tm)
    carry = (CONV_FFN - 1) * stride
    base = _round_up(carry, SUBLANES)
    assert tm >= carry
    resident = lambda shape: pl.BlockSpec(shape, lambda b, t: (0, 0), pipeline_mode=pl.Buffered(1))
    small = lambda shape: pl.BlockSpec(shape, lambda b, t: (0, 0))
    tail_idx = (lambda b, t: (0, 0, 0)) if shared_init else (lambda b, t: (b, 0, 0))
    kern = functools.partial(_ffn_kernel, tm=tm, stride=stride)
    return pl.pallas_call(
        kern,
        out_shape=(
            jax.ShapeDtypeStruct((rows, D_MODEL), F32),
            jax.ShapeDtypeStruct((nseq, carry, 2 * D_FF), F32),
        ),
        grid=(nseq, nt),
        in_specs=[
            pl.BlockSpec((tm, D_MODEL), lambda b, t: (b * nt + t, 0)),
            pl.BlockSpec((tm, D_MODEL), lambda b, t: (b * nt + t, 0)),
            pl.BlockSpec((1, carry, 2 * D_FF), tail_idx),
            resident((D_MODEL, D_MODEL)),
            small((1, D_MODEL)),
            resident((D_MODEL, 2 * D_FF)),
            small((CONV_FFN, 2 * D_FF)),
            resident((D_FF, D_MODEL)),
            small((1, D_MODEL)),
        ],
        out_specs=(
            pl.BlockSpec((tm, D_MODEL), lambda b, t: (b * nt + t, 0)),
            pl.BlockSpec((1, carry, 2 * D_FF), lambda b, t: (b, 0, 0)),
        ),
        scratch_shapes=[pltpu.VMEM((base + tm, 2 * D_FF), F32)],
        compiler_params=pltpu.CompilerParams(dimension_semantics=("arbitrary", "arbitrary"),
                                             vmem_limit_bytes=VMEM_LIMIT),
        name="out_ffn",
    )(x, mix, tail0, wout, nffn, wup, cw, wdn, nfin)


def kernel(x_prompt, x_sample, state_ret, state_gdn, state_conv_qkv, state_ffn_conv, meta_tokens, norm_mix,
           w_in, conv_gdn, gdn_a_log, gdn_dt_bias, norm_ret, norm_gdn, w_out, norm_ffn, w_up, conv_ffn,
           w_down, norm_final):
    depth = w_in.shape[0]
    assert depth == 1
    nbp, seq, _ = x_prompt.shape
    nbs, dec_seq, _ = x_sample.shape
    assert dec_seq <= SAMPLE_PAD and nbs % SAMPLE_GROUP == 0
    assert seq % (2 * PROMPT_CHUNK * PROMPT_CHUNKS_PER_STEP) == 0
    layer = 0

    w_in_bf = jnp.concatenate([w_in[layer].astype(BF16), jnp.zeros((D_MODEL, IN_PAD - IN_WIDTH), BF16)], axis=1)
    w_out_bf = w_out[layer].astype(BF16)
    w_up_bf = w_up[layer].astype(BF16)
    w_down_bf = w_down[layer].astype(BF16)
    row = lambda v: v.reshape(1, -1).astype(F32)
    pad_ba = lambda v: jnp.pad(v.astype(F32), (N_HEADS, LANES - 2 * N_HEADS)).reshape(1, LANES)
    mixer_consts = (conv_gdn[layer], pad_ba(gdn_a_log[layer]), pad_ba(gdn_dt_bias[layer]),
                    norm_ret[layer].reshape(N_HEADS, D_HEAD), row(norm_gdn[layer]))
    ffn_weights = (w_out_bf, row(norm_ffn[layer]), w_up_bf, conv_ffn[layer], w_down_bf, row(norm_final))
    nmix = row(norm_mix[layer])

    assert seq % N_META == 0 and (seq + N_META) % SAMPLE_PAD == 0
    rope_meta_row0, rope_sample_row0 = seq, seq + N_META
    rope = _rope_tables(seq + N_META + SAMPLE_PAD,
                        [(0, N_META), (rope_meta_row0, 0), (rope_sample_row0, PAST_LEN)])

    xs_pad = jnp.pad(x_sample, ((0, 0), (0, SAMPLE_PAD - dec_seq), (0, 0))).reshape(nbs * SAMPLE_PAD, D_MODEL)
    small_rows = jnp.concatenate([xs_pad, meta_tokens.astype(F32)], axis=0)
    n_small = small_rows.shape[0]
    proj_small = _proj(small_rows, nmix, w_in_bf, tm=n_small // 2)
    meta_row0 = nbs * SAMPLE_PAD

    zero_state = jnp.zeros((1, N_HEADS, D_HEAD, D_HEAD), F32)
    zero_cq = jnp.zeros((1, CONV_GDN - 1, GDN_QKV), F32)
    mix_m, sret_m, sgdn_m, cq_m = _mixer(proj_small, meta_row0, 1, N_META, zero_state, zero_state, zero_cq, rope,
                                         rope_meta_row0, mixer_consts, bb=1, c=N_META, nch=1, n_valid=N_META,
                                         shared_init=True)
    zero_cf = jnp.zeros((1, CONV_FFN - 1, 2 * D_FF), F32)
    _, cf_m = _ffn(meta_tokens.astype(F32), mix_m.reshape(N_META, D_MODEL), zero_cf, ffn_weights,
                   nseq=1, tm=N_META, stride=1, shared_init=True)

    mix_s, sret_s, sgdn_s, cq_s = _mixer(proj_small, 0, nbs, SAMPLE_PAD, state_ret[layer], state_gdn[layer],
                                         state_conv_qkv[layer], rope, rope_sample_row0, mixer_consts,
                                         bb=SAMPLE_SEQS_PER_STEP, c=SAMPLE_PAD, nch=1, n_valid=dec_seq,
                                         shared_init=False)
    ng = nbs // SAMPLE_GROUP
    to_tmajor = lambda a: a.reshape(ng, SAMPLE_GROUP, a.shape[1], a.shape[2]).transpose(0, 2, 1, 3)
    xs_t = to_tmajor(x_sample).reshape(nbs * dec_seq, D_MODEL)
    mix_t = to_tmajor(mix_s[:, :dec_seq]).reshape(nbs * dec_seq, D_MODEL)
    cf0_t = to_tmajor(state_ffn_conv[layer]).reshape(ng, (CONV_FFN - 1) * SAMPLE_GROUP, 2 * D_FF)
    y_s_t, cf_s_t = _ffn(xs_t, mix_t, cf0_t, ffn_weights, nseq=ng, tm=dec_seq * SAMPLE_GROUP,
                         stride=SAMPLE_GROUP, shared_init=False)
    y_sample = y_s_t.reshape(ng, dec_seq, SAMPLE_GROUP, D_MODEL).transpose(0, 2, 1, 3).reshape(nbs, dec_seq, D_MODEL)
    cf_s = cf_s_t.reshape(ng, CONV_FFN - 1, SAMPLE_GROUP, 2 * D_FF).transpose(0, 2, 1, 3).reshape(
        nbs, CONV_FFN - 1, 2 * D_FF)

    xp = x_prompt.reshape(nbp * seq, D_MODEL)
    mix_p, sret_p, sgdn_p, cq_p = _proj_mixer(xp, nmix, w_in_bf, nbp, sret_m, sgdn_m, cq_m, rope, mixer_consts,
                                              c=PROMPT_CHUNK, nch=PROMPT_CHUNKS_PER_STEP)
    y_p, cf_p = _ffn(xp, mix_p, cf_m, ffn_weights, nseq=nbp, tm=256, stride=1, shared_init=True)
    y_prompt = y_p.reshape(nbp, seq, D_MODEL)

    return (y_prompt, y_sample, sret_p[None], sgdn_p[None], cq_p[None], cf_p[None],
            sret_s[None], sgdn_s[None], cq_s[None], cf_s[None])
```

```python
import functools

import jax
import numpy as np
import jax.numpy as jnp
from jax import lax
from jax.experimental import pallas as pl
from jax.experimental.pallas import tpu as pltpu

F32 = jnp.float32
BF16 = jnp.bfloat16

D_MODEL = 1024
N_META = 16
PAST_LEN = 16384
N_HEADS = 4
D_HEAD = 128
D_GRP = N_HEADS * D_HEAD
GDN_QKV = 3 * D_GRP
CONV_GDN = 4
CONV_FFN = 3
D_FF = 2816
ROPE_THETA = 10000.0
EPS = 1e-6

OFF_RQ, OFF_RK, OFF_RV, OFF_RG = 0, D_GRP, 2 * D_GRP, 3 * D_GRP
OFF_QKV = 4 * D_GRP
OFF_GG = OFF_QKV + GDN_QKV
OFF_BA = OFF_GG + D_GRP
IN_WIDTH = OFF_BA + 2 * N_HEADS
LANES = 128
SUBLANES = 8
IN_PAD = OFF_BA + LANES

PROMPT_CHUNK = 128
PROMPT_CHUNKS_PER_STEP = 2
SAMPLE_PAD = 8
SAMPLE_SEQS_PER_STEP = 8
SAMPLE_GROUP = 32
VMEM_LIMIT = 56 * 1024 * 1024


def _round_up(n, m):
    return (n + m - 1) // m * m


def _mm(a, b):
    return jnp.dot(a.astype(BF16), b.astype(BF16), preferred_element_type=F32)


def _mm_nt(a, b):
    return lax.dot_general(a.astype(BF16), b.astype(BF16), (((1,), (1,)), ((), ())),
                           preferred_element_type=F32)


def _mm_tn(a, b):
    return lax.dot_general(a.astype(BF16), b.astype(BF16), (((0,), (0,)), ((), ())),
                           preferred_element_type=F32)


def _split2(x):
    hi = x.astype(BF16)
    lo = (x - hi.astype(F32)).astype(BF16)
    return hi, lo


def _split3(x):
    p0 = x.astype(BF16)
    r = x - p0.astype(F32)
    p1 = r.astype(BF16)
    p2 = (r - p1.astype(F32)).astype(BF16)
    return p0, p1, p2


def _mm3(a2, b2):
    (ah, al), (bh, bl) = a2, b2
    dot = lambda x, y: jnp.dot(x, y, preferred_element_type=F32)
    return dot(ah, bh) + (dot(ah, bl) + dot(al, bh))


def _silu(x):
    return x * jax.nn.sigmoid(x)


def _rms(x, w):
    return x * lax.rsqrt(jnp.mean(x * x, axis=-1, keepdims=True) + EPS) * w


def _rope_kernel(invf_ref, cos_ref, sin_ref, *, segments):
    shape = cos_ref.shape
    r = lax.broadcasted_iota(jnp.int32, shape, 0)
    pos = r + (segments[0][1] - segments[0][0])
    for row0, pos0 in segments[1:]:
        pos = jnp.where(r >= row0, r + (pos0 - row0), pos)
    ang = pos.astype(F32) * invf_ref[...]
    lane = lax.broadcasted_iota(jnp.int32, shape, 1)
    sin = jnp.sin(ang)
    cos_ref[...] = jnp.cos(ang)
    sin_ref[...] = jnp.where(lane < D_HEAD // 2, -sin, sin)


def _rope_tables(n, segments):
    half = D_HEAD // 2
    inv_freq = ROPE_THETA ** (-jnp.arange(half, dtype=F32) / half)
    invf2 = jnp.concatenate([inv_freq, inv_freq]).reshape(1, LANES)
    out = jax.ShapeDtypeStruct((n, LANES), F32)
    return pl.pallas_call(functools.partial(_rope_kernel, segments=tuple(segments)), out_shape=(out, out),
                          name="rope_tables")(invf2)


def _proj_kernel(x_ref, nw_ref, w_ref, o_ref):
    h = _rms(x_ref[...], nw_ref[...])
    o_ref[...] = jnp.dot(h.astype(BF16), w_ref[...], preferred_element_type=F32)


def _proj(x, norm_w, w_bf, tm):
    rows = x.shape[0]
    assert rows % tm == 0
    return pl.pallas_call(
        _proj_kernel,
        out_shape=jax.ShapeDtypeStruct((rows, IN_PAD), F32),
        grid=(rows // tm,),
        in_specs=[
            pl.BlockSpec((tm, D_MODEL), lambda i: (i, 0)),
            pl.BlockSpec((1, D_MODEL), lambda i: (0, 0)),
            pl.BlockSpec((D_MODEL, IN_PAD), lambda i: (0, 0), pipeline_mode=pl.Buffered(1)),
        ],
        out_specs=pl.BlockSpec((tm, IN_PAD), lambda i: (i, 0)),
        compiler_params=pltpu.CompilerParams(dimension_semantics=("arbitrary",),
                                             vmem_limit_bytes=VMEM_LIMIT),
        name="in_proj",
    )(x, norm_w, w_bf)


def _unit_lower_inverses(mats, c, tick=lambda: None):
    ri = lax.broadcasted_iota(jnp.int32, (c, c), 0)
    ci = lax.broadcasted_iota(jnp.int32, (c, c), 1)
    eye = (ri == ci).astype(F32)
    diag_blk = (ri // SUBLANES) == (ci // SUBLANES)
    ad = [jnp.where(diag_blk, a, 0.0) for a in mats]
    a2 = [_mm(x, x) for x in ad]
    tick()
    a4 = [_mm(x, x) for x in a2]
    t = [eye - x for x in ad]
    t = [x + _mm(x, s) for x, s in zip(t, a2)]
    tick()
    t = [x + _mm(x, s) for x, s in zip(t, a4)]
    tick()
    s = SUBLANES
    while s < c:
        level = ((ri // (2 * s)) == (ci // (2 * s))) & ((ri // s) != (ci // s))
        off = [jnp.where(level, a, 0.0) for a in mats]
        lt = [_mm(o, x) for o, x in zip(off, t)]
        tick()
        t = [x - _mm(x, y) for x, y in zip(t, lt)]
        tick()
        s *= 2
    return t


XS_TOP = SUBLANES


def _mixer_init(sret0_ref, sgdn0_ref, cq0_ref, sret_ref, sgdn_ref, xs_ref):
    tail = CONV_GDN - 1
    sret_ref[...] = sret0_ref[...]
    sgdn_ref[...] = sgdn0_ref[...]
    xs_ref[:, XS_TOP - tail:XS_TOP, :] = cq0_ref[...]


def _mixer_block(getp, put_mix, cos2, sin2, const_refs, sret_ref, sgdn_ref, cq_ref, xs_ref, *,
                 bb, c, nch, n_valid, tick=lambda: None):
    (tril_ref, triu_ref, dint_ref, qdec_ref, kdec_ref, cdec_ref, cw_ref, alog_ref, dtb_ref, nret_ref,
     ngdn_ref) = const_refs
    rows = nch * c
    tail = CONV_GDN - 1
    top = XS_TOP
    ri = lax.broadcasted_iota(jnp.int32, (c, c), 0)
    cj = lax.broadcasted_iota(jnp.int32, (c, c), 1)
    tri = ri >= cj
    strict = ri > cj
    tril_bf = tril_ref[...]
    triu_bf = triu_ref[...]
    scale = D_HEAD ** -0.5
    heads = range(N_HEADS)
    seqs = range(bb)
    chunks = [slice(j * c, (j + 1) * c) for j in range(nch)]
    join = lambda parts: parts[0] if len(parts) == 1 else jnp.concatenate(parts, axis=0)

    ret = {}
    for b in seqs:
        for h in heads:
            lo = h * D_HEAD
            q = getp(b, OFF_RQ + lo, OFF_RQ + lo + D_HEAD)
            k = getp(b, OFF_RK + lo, OFF_RK + lo + D_HEAD)
            v = getp(b, OFF_RV + lo, OFF_RV + lo + D_HEAD)
            qr = q * cos2 + pltpu.roll(q, D_HEAD // 2, 1) * sin2
            kr = (k * cos2 + pltpu.roll(k, D_HEAD // 2, 1) * sin2) * scale
            ret[b, h] = (qr, kr, v)
    tick()
    rtasks = [(b, h, j) for j in range(nch) for b in seqs for h in heads]
    scores = {t: _mm_nt(ret[t[0], t[1]][0][chunks[t[2]]], ret[t[0], t[1]][1][chunks[t[2]]]) * dint_ref[t[1]]
              for t in rtasks}
    tick()
    intra = {t: _mm(scores[t], ret[t[0], t[1]][2][chunks[t[2]]]) for t in rtasks}
    tick()
    kv = {t: _mm_tn(ret[t[0], t[1]][1][chunks[t[2]]] * kdec_ref[t[1]], ret[t[0], t[1]][2][chunks[t[2]]])
          for t in rtasks}
    tick()
    pairs = [(b, h) for b in seqs for h in heads]
    s_ret = {bh: sret_ref[bh[0], bh[1]] for bh in pairs}
    o_ret = {}
    for j in range(nch):
        for b, h in pairs:
            o_ret[b, h, j] = intra[b, h, j] + _mm(ret[b, h][0][chunks[j]] * qdec_ref[h], s_ret[b, h])
        for b, h in pairs:
            s_ret[b, h] = cdec_ref[h] * s_ret[b, h] + kv[b, h, j]
        tick()
    o_ret = {bh: join([o_ret[bh[0], bh[1], j] for j in range(nch)]) for bh in pairs}
    mu = {bh: jnp.mean(o_ret[bh], axis=-1, keepdims=True) for bh in pairs}
    cen = {bh: o_ret[bh] - mu[bh] for bh in pairs}
    var = {bh: jnp.mean(cen[bh] * cen[bh], axis=-1, keepdims=True) for bh in pairs}
    for b, h in pairs:
        lo = h * D_HEAD
        gate = getp(b, OFF_RG + lo, OFF_RG + lo + D_HEAD)
        o = cen[b, h] * lax.rsqrt(var[b, h] + EPS) * nret_ref[h:h + 1, :]
        put_mix(b, lo, o * _silu(gate))
        sret_ref[b, h] = s_ret[b, h]
    tick()

    qkvs, beta_alls, cum_cols, cum_rows, cum_tots = {}, {}, {}, {}, {}
    ones_bf = jnp.ones((D_HEAD, c), BF16)
    for b in seqs:
        xs_ref[b, top:top + rows, :] = getp(b, OFF_QKV, OFF_QKV + GDN_QKV)
        conv = xs_ref[b, top - tail:top - tail + rows, :] * cw_ref[0:1, :]
        for i in range(1, CONV_GDN):
            conv = conv + xs_ref[b, top - tail + i:top - tail + i + rows, :] * cw_ref[i:i + 1, :]
        last = n_valid if nch == 1 else rows
        new_tail = xs_ref[b, top + last - tail:top + last, :]
        xs_ref[b, top - tail:top, :] = new_tail
        cq_ref[b] = new_tail
        qkvs[b] = _silu(conv)
        tick()

        ba = getp(b, OFF_BA, OFF_BA + LANES)
        beta_all = jax.nn.sigmoid(ba)
        z = ba + dtb_ref[...]
        softplus = jnp.maximum(z, 0.0) + jnp.log1p(jnp.exp(-jnp.abs(z)))
        g_all = -jnp.exp(alog_ref[...]) * softplus
        if n_valid < c:
            row = lax.broadcasted_iota(jnp.int32, (rows, LANES), 0)
            rowmask = (row < n_valid).astype(F32)
            beta_all = beta_all * rowmask
            g_all = g_all * rowmask
        beta_alls[b] = beta_all
        for j, rs in enumerate(chunks):
            parts = _split3(g_all[rs])
            cum_cols[b, j] = sum(jnp.dot(tril_bf, g, preferred_element_type=F32) for g in parts)
            cum_tots[b, j] = sum(jnp.dot(ones_bf, g, preferred_element_type=F32) for g in parts)
            cum_rows[b, j] = sum(lax.dot_general(g, triu_bf, (((0,), (0,)), ((), ())),
                                                 preferred_element_type=F32) for g in parts)
    tick()

    tasks = [(j, b, h) for j in range(nch) for b in seqs for h in heads]
    qs, ks, vs, betas, ecums, kdecs, cdecs, amats, qkms = [], [], [], [], [], [], [], [], []
    raw = [(qkvs[b][chunks[j], h * D_HEAD:(h + 1) * D_HEAD],
            qkvs[b][chunks[j], D_GRP + h * D_HEAD:D_GRP + (h + 1) * D_HEAD]) for j, b, h in tasks]
    sumsq = [(jnp.sum(q * q, axis=-1, keepdims=True), jnp.sum(k * k, axis=-1, keepdims=True)) for q, k in raw]
    for i, (j, b, h) in enumerate(tasks):
        rs = chunks[j]
        lo = h * D_HEAD
        v = qkvs[b][rs, 2 * D_GRP + lo:2 * D_GRP + lo + D_HEAD]
        q = raw[i][0] * lax.rsqrt(sumsq[i][0] + EPS) * scale
        k = raw[i][1] * lax.rsqrt(sumsq[i][1] + EPS)
        beta = jnp.broadcast_to(beta_alls[b][rs, h:h + 1], (c, LANES))
        cum = jnp.broadcast_to(cum_cols[b, j][:, N_HEADS + h:N_HEADS + h + 1], (c, LANES))
        cum_row = cum_rows[b, j][N_HEADS + h:N_HEADS + h + 1, :]
        dmask = jnp.exp(jnp.where(tri, cum[:, :c] - cum_row, -jnp.inf))
        cum_last = jnp.broadcast_to(cum_tots[b, j][:, N_HEADS + h:N_HEADS + h + 1], (D_HEAD, LANES))
        kq = _mm_nt(jnp.concatenate([k, q], axis=0), k)
        amats.append(jnp.where(strict, beta[:, :c] * kq[:c] * dmask, 0.0))
        qkms.append(jnp.where(tri, kq[c:] * dmask, 0.0))
        qs.append(q)
        ks.append(k)
        vs.append(v)
        betas.append(beta)
        ecums.append(jnp.exp(cum))
        kdecs.append(jnp.exp(cum_last[:c] - cum))
        cdecs.append(jnp.exp(cum_last))
        if i % N_HEADS == N_HEADS - 1:
            tick()

    tinv = _unit_lower_inverses(amats, c, tick)
    sols = []
    for i in range(len(tasks)):
        rhs = jnp.concatenate([vs[i] * betas[i], ks[i] * (betas[i] * ecums[i])], axis=1)
        sols.append(_mm3(_split2(tinv[i]), _split2(rhs)))
    tick()

    s_gdn = {bh: sgdn_ref[bh[0], bh[1]] for bh in pairs}
    o_gdn = {}
    for j in range(nch):
        idx = {tasks[i][1:]: i for i in range(len(tasks)) if tasks[i][0] == j}
        lhs = {bh: jnp.concatenate([sols[i][:, D_HEAD:], qs[i] * ecums[i]], axis=0) for bh, i in idx.items()}
        both = {bh: _mm(lhs[bh], s_gdn[bh]) for bh in idx}
        tick()
        w = {bh: sols[i][:, :D_HEAD] - both[bh][:c] for bh, i in idx.items()}
        for bh, i in idx.items():
            o_gdn[bh + (j,)] = both[bh][c:] + _mm(qkms[i], w[bh])
        tick()
        upd = {bh: _mm_tn(ks[i] * kdecs[i], w[bh]) for bh, i in idx.items()}
        for bh, i in idx.items():
            s_gdn[bh] = cdecs[i] * s_gdn[bh] + upd[bh]
        tick()
    o_gdn = {bh: join([o_gdn[bh + (j,)] for j in range(nch)]) for bh in pairs}
    msq = {bh: jnp.mean(o_gdn[bh] * o_gdn[bh], axis=-1, keepdims=True) for bh in pairs}
    for b, h in pairs:
        lo = h * D_HEAD
        gate = getp(b, OFF_GG + lo, OFF_GG + lo + D_HEAD)
        o = o_gdn[b, h] * lax.rsqrt(msq[b, h] + EPS) * ngdn_ref[...]
        put_mix(b, D_GRP + lo, o * _silu(gate))
        sgdn_ref[b, h] = s_gdn[b, h]


N_MIXER_CONSTS = 11


def _mixer_kernel(p_ref, sret0_ref, sgdn0_ref, cq0_ref, cos_ref, sin_ref, *rest, bb, c, nch, n_valid):
    const_refs, (mix_ref, sret_ref, sgdn_ref, cq_ref, xs_ref) = rest[:N_MIXER_CONSTS], rest[N_MIXER_CONSTS:]
    rows = nch * c

    @pl.when(pl.program_id(1) == 0)
    def _():
        _mixer_init(sret0_ref, sgdn0_ref, cq0_ref, sret_ref, sgdn_ref, xs_ref)

    def put_mix(b, lo, value):
        mix_ref[b, :, lo:lo + D_HEAD] = value.astype(mix_ref.dtype)

    _mixer_block(lambda b, lo, hi: p_ref[b * rows:(b + 1) * rows, lo:hi], put_mix, cos_ref[...], sin_ref[...],
                 const_refs, sret_ref, sgdn_ref, cq_ref, xs_ref, bb=bb, c=c, nch=nch, n_valid=n_valid)


PROJ_PANEL = 2 * LANES


def _mixer_block_ticks(bb, c, nch):
    levels = (c // SUBLANES).bit_length() - 1
    retention = 4 + nch + 1
    gdn_prep = bb + 1 + bb * nch
    inverse = 3 + 2 * levels
    return retention + gdn_prep + inverse + 1 + 3 * nch


def _proj_mixer_kernel(x0_ref, xa_ref, xb_ref, nw_ref, w_ref, sret0_ref, sgdn0_ref, cq0_ref, cos_ref, sin_ref,
                       *rest, c, nch, steps_per_seq):
    const_refs = rest[:N_MIXER_CONSTS]
    mix_ref, sret_ref, sgdn_ref, cq_ref, pja_ref, pjb_ref, xs_ref = rest[N_MIXER_CONSTS:]
    step = pl.program_id(0)
    rows = nch * c

    def project(x_ref, dst_ref, n_ticks):
        h = _rms(x_ref[...], nw_ref[...]).astype(BF16)
        panels = [(lo, min(lo + PROJ_PANEL, IN_PAD)) for lo in range(0, IN_PAD, PROJ_PANEL)]
        n_panels = len(panels)
        calls = [0]

        def emit():
            lo, hi = panels.pop(0)
            dst_ref[:, lo:hi] = jnp.dot(h, w_ref[:, lo:hi], preferred_element_type=F32)

        def tick():
            calls[0] += 1
            due = min(n_panels, -(-calls[0] * n_panels // n_ticks))
            while n_panels - len(panels) < due:
                emit()

        def flush():
            while panels:
                emit()

        return tick, flush

    @pl.when(step == 0)
    def _():
        project(x0_ref, pja_ref, 1)[1]()

    @pl.when(lax.rem(step, steps_per_seq) == 0)
    def _():
        _mixer_init(sret0_ref, sgdn0_ref, cq0_ref, sret_ref, sgdn_ref, xs_ref)

    for half, (cur_ref, x_next_ref, nxt_ref) in enumerate(((pja_ref, xa_ref, pjb_ref), (pjb_ref, xb_ref, pja_ref))):
        r0 = half * rows
        tick, flush = project(x_next_ref, nxt_ref, _mixer_block_ticks(1, c, nch))

        def put_mix(b, lo, value, r0=r0):
            mix_ref[r0:r0 + rows, lo:lo + D_HEAD] = value.astype(mix_ref.dtype)

        _mixer_block(lambda b, lo, hi, cur_ref=cur_ref: cur_ref[:, lo:hi], put_mix,
                     cos_ref[r0:r0 + rows, :], sin_ref[r0:r0 + rows, :], const_refs,
                     sret_ref, sgdn_ref, cq_ref, xs_ref, bb=1, c=c, nch=nch, n_valid=c, tick=tick)
        flush()


def _retention_decay_tables(c, n_valid):
    f32 = np.float32
    lg = np.log1p(-np.power(f32(2.0), f32(-5.0) - np.arange(N_HEADS, dtype=f32)))[:, None].astype(f32)
    idx = np.arange(c, dtype=f32)
    diff = idx[:, None] - idx[None, :]
    dint = np.where(diff[None] >= 0, np.exp(lg[:, :, None] * np.maximum(diff[None], 0)), f32(0.0)).astype(f32)
    qdec = np.exp(lg * (idx + f32(1.0))).astype(f32)
    kdec = np.where(idx[None, :] < n_valid, np.exp(lg * np.minimum(f32(n_valid) - f32(1.0) - idx, c)), f32(0.0))
    cdec = np.exp(lg * f32(n_valid)).astype(f32)
    bc = lambda t: np.broadcast_to(t.astype(f32)[:, :, None], t.shape + (LANES,))
    return dint, bc(qdec), bc(kdec), np.broadcast_to(cdec[:, :, None], (N_HEADS, 1, LANES))


def _mixer_const_operands(c, n_valid, consts):
    cw, alog, dtb, nret, ngdn = consts
    dint, qdec, kdec, cdec = _retention_decay_tables(c, n_valid)
    idx = np.arange(c)
    tril = (idx[:, None] >= idx[None, :]).astype(np.float32)
    arrays = (jnp.asarray(tril, BF16), jnp.asarray(tril.T, BF16), jnp.asarray(dint), jnp.asarray(qdec),
              jnp.asarray(kdec), jnp.asarray(cdec), cw, alog, dtb, nret, ngdn)
    assert len(arrays) == N_MIXER_CONSTS
    return arrays, [a.shape for a in arrays]


def _mixer(proj, row0, nb, length, sret0, sgdn0, cq0, rope, rope_row0, consts, *, bb, c, nch, n_valid,
           shared_init):
    rows = nch * c
    assert nb % bb == 0 and length % rows == 0 and rope_row0 % rows == 0 and row0 % (bb * length) == 0
    assert not shared_init or bb == 1
    assert n_valid == c or nch == 1
    assert bb == 1 or length == rows
    const_arrays, const_shapes = _mixer_const_operands(c, n_valid, consts)
    nsteps = length // rows
    blk0 = row0 // (bb * rows)
    init_idx = (lambda b, i: (0, 0, 0, 0)) if shared_init else (lambda b, i: (b, 0, 0, 0))
    init_idx3 = (lambda b, i: (0, 0, 0)) if shared_init else (lambda b, i: (b, 0, 0))
    whole = lambda shape: pl.BlockSpec(shape, lambda b, i: (0,) * len(shape))
    rope_spec = pl.BlockSpec((rows, LANES), lambda b, i: (rope_row0 // rows + i, 0))
    state_shape = (bb, N_HEADS, D_HEAD, D_HEAD)
    tail = CONV_GDN - 1
    kern = functools.partial(_mixer_kernel, bb=bb, c=c, nch=nch, n_valid=n_valid)
    return pl.pallas_call(
        kern,
        out_shape=(
            jax.ShapeDtypeStruct((nb, length, D_MODEL), BF16),
            jax.ShapeDtypeStruct((nb, N_HEADS, D_HEAD, D_HEAD), F32),
            jax.ShapeDtypeStruct((nb, N_HEADS, D_HEAD, D_HEAD), F32),
            jax.ShapeDtypeStruct((nb, tail, GDN_QKV), F32),
        ),
        grid=(nb // bb, nsteps),
        in_specs=[
            pl.BlockSpec((bb * rows, IN_PAD), lambda b, i: (blk0 + b * nsteps + i, 0)),
            pl.BlockSpec(state_shape, init_idx),
            pl.BlockSpec(state_shape, init_idx),
            pl.BlockSpec((bb, tail, GDN_QKV), init_idx3),
            rope_spec,
            rope_spec,
        ] + [whole(shape) for shape in const_shapes],
        out_specs=(
            pl.BlockSpec((bb, rows, D_MODEL), lambda b, i: (b, i, 0)),
            pl.BlockSpec(state_shape, lambda b, i: (b, 0, 0, 0)),
            pl.BlockSpec(state_shape, lambda b, i: (b, 0, 0, 0)),
            pl.BlockSpec((bb, tail, GDN_QKV), lambda b, i: (b, 0, 0)),
        ),
        scratch_shapes=[pltpu.VMEM((bb, XS_TOP + rows, GDN_QKV), F32)],
        compiler_params=pltpu.CompilerParams(dimension_semantics=("arbitrary", "arbitrary"),
                                             vmem_limit_bytes=VMEM_LIMIT),
        name="mixer",
    )(proj, sret0, sgdn0, cq0, rope[0], rope[1], *const_arrays)


def _proj_mixer(x, norm_w, w_bf, nseq, sret0, sgdn0, cq0, rope, consts, *, c, nch):
    rows = nch * c
    total = x.shape[0]
    length = total // nseq
    assert total % nseq == 0 and length % (2 * rows) == 0
    nblk = total // rows
    steps_per_seq = length // (2 * rows)
    const_arrays, const_shapes = _mixer_const_operands(c, c, consts)
    whole = lambda shape, **kw: pl.BlockSpec(shape, lambda s: (0,) * len(shape), **kw)
    rope_spec = pl.BlockSpec((2 * rows, LANES), lambda s: (lax.rem(s, steps_per_seq), 0))
    state_shape = (1, N_HEADS, D_HEAD, D_HEAD)
    tail = CONV_GDN - 1
    kern = functools.partial(_proj_mixer_kernel, c=c, nch=nch, steps_per_seq=steps_per_seq)
    return pl.pallas_call(
        kern,
        out_shape=(
            jax.ShapeDtypeStruct((total, D_MODEL), BF16),
            jax.ShapeDtypeStruct((nseq, N_HEADS, D_HEAD, D_HEAD), F32),
            jax.ShapeDtypeStruct((nseq, N_HEADS, D_HEAD, D_HEAD), F32),
            jax.ShapeDtypeStruct((nseq, tail, GDN_QKV), F32),
        ),
        grid=(nblk // 2,),
        in_specs=[
            pl.BlockSpec((rows, D_MODEL), lambda s: (0, 0), pipeline_mode=pl.Buffered(1)),
            pl.BlockSpec((rows, D_MODEL), lambda s: (2 * s + 1, 0)),
            pl.BlockSpec((rows, D_MODEL), lambda s: (jnp.minimum(2 * s + 2, nblk - 1), 0)),
            whole((1, D_MODEL)),
            whole((D_MODEL, IN_PAD), pipeline_mode=pl.Buffered(1)),
            whole(state_shape),
            whole(state_shape),
            whole((1, tail, GDN_QKV)),
            rope_spec,
            rope_spec,
        ] + [whole(shape) for shape in const_shapes],
        out_specs=(
            pl.BlockSpec((2 * rows, D_MODEL), lambda s: (s, 0)),
            pl.BlockSpec(state_shape, lambda s: (s // steps_per_seq, 0, 0, 0)),
            pl.BlockSpec(state_shape, lambda s: (s // steps_per_seq, 0, 0, 0)),
            pl.BlockSpec((1, tail, GDN_QKV), lambda s: (s // steps_per_seq, 0, 0)),
        ),
        scratch_shapes=[pltpu.VMEM((rows, IN_PAD), F32), pltpu.VMEM((rows, IN_PAD), F32),
                        pltpu.VMEM((1, XS_TOP + rows, GDN_QKV), F32)],
        compiler_params=pltpu.CompilerParams(dimension_semantics=("arbitrary",),
                                             vmem_limit_bytes=VMEM_LIMIT),
        name="proj_mixer",
    )(x, x, x, norm_w, w_bf, sret0, sgdn0, cq0, rope[0], rope[1], *const_arrays)


FFN_COL_CHUNK = D_FF // 11


def _ffn_kernel(x_ref, mix_ref, tail0_ref, wout_ref, nffn_ref, wup_ref, cw_ref, wdn_ref, nfin_ref,
                y_ref, tail_ref, full_ref, *, tm, stride):
    t = pl.program_id(1)
    carry = (CONV_FFN - 1) * stride
    base = _round_up(carry, SUBLANES)

    @pl.when(t == 0)
    def _():
        full_ref[base - carry:base, :] = tail0_ref[0]

    x1 = x_ref[...] + jnp.dot(mix_ref[...], wout_ref[...], preferred_element_type=F32)
    h = _rms(x1, nffn_ref[...]).astype(BF16)
    full_ref[base:base + tm, :] = jnp.dot(h, wup_ref[...], preferred_element_type=F32)

    def conv_cols(lo):
        acc = full_ref[base - carry:base - carry + tm, lo:lo + FFN_COL_CHUNK] * cw_ref[0:1, lo:lo + FFN_COL_CHUNK]
        for i in range(1, CONV_FFN):
            r0 = base - carry + i * stride
            acc = acc + full_ref[r0:r0 + tm, lo:lo + FFN_COL_CHUNK] * cw_ref[i:i + 1, lo:lo + FFN_COL_CHUNK]
        return acc

    x2 = x1
    for j in range(D_FF // FFN_COL_CHUNK):
        lo = j * FFN_COL_CHUNK
        act = (_silu(conv_cols(lo)) * conv_cols(D_FF + lo)).astype(BF16)
        x2 = x2 + jnp.dot(act, wdn_ref[lo:lo + FFN_COL_CHUNK, :], preferred_element_type=F32)
    y_ref[...] = _rms(x2, nfin_ref[...])

    new_tail = full_ref[base + tm - carry:base + tm, :]
    full_ref[base - carry:base, :] = new_tail
    tail_ref[0] = new_tail


def _ffn(x, mix, tail0, weights, *, nseq, tm, stride, shared_init):
    wout, nffn, wup, cw, wdn, nfin = weights
    rows = x.shape[0]
    assert rows % (nseq * tm) == 0
    nt = rows // (nseq * tm)
    carry = (CONV_FFN - 1) * stride
    base = _round_up(carry, SUBLANES)
    assert tm >= carry
    resident = lambda shape: pl.BlockSpec(shape, lambda b, t: (0, 0), pipeline_mode=pl.Buffered(1))
    small = lambda shape: pl.BlockSpec(shape, lambda b, t: (0, 0))
    tail_idx = (lambda b, t: (0, 0, 0)) if shared_init else (lambda b, t: (b, 0, 0))
    kern = functools.partial(_ffn_kernel, tm=tm, stride=stride)
    return pl.pallas_call(
        kern,
        out_shape=(
            jax.ShapeDtypeStruct((rows, D_MODEL), F32),
            jax.ShapeDtypeStruct((nseq, carry, 2 * D_FF), F32),
        ),
        grid=(nseq, nt),
        in_specs=[
            pl.BlockSpec((tm, D_MODEL), lambda b, t: (b * nt + t, 0)),
            pl.BlockSpec((tm, D_MODEL), lambda b, t: (b * nt + t, 0)),
            pl.BlockSpec((1, carry, 2 * D_FF), tail_idx),
            resident((D_MODEL, D_MODEL)),
            small((1, D_MODEL)),
            resident((D_MODEL, 2 * D_FF)),
            small((CONV_FFN, 2 * D_FF)),
            resident((D_FF, D_MODEL)),
            small((1, D_MODEL)),
        ],
        out_specs=(
            pl.BlockSpec((tm, D_MODEL), lambda b, t: (b * nt + t, 0)),
            pl.BlockSpec((1, carry, 2 * D_FF), lambda b, t: (b, 0, 0)),
        ),
        scratch_shapes=[pltpu.VMEM((base + tm, 2 * D_FF), F32)],
        compiler_params=pltpu.CompilerParams(dimension_semantics=("arbitrary", "arbitrary"),
                                             vmem_limit_bytes=VMEM_LIMIT),
        name="out_ffn",
    )(x, mix, tail0, wout, nffn, wup, cw, wdn, nfin)


def kernel(x_prompt, x_sample, state_ret, state_gdn, state_conv_qkv, state_ffn_conv, meta_tokens, norm_mix,
           w_in, conv_gdn, gdn_a_log, gdn_dt_bias, norm_ret, norm_gdn, w_out, norm_ffn, w_up, conv_ffn,
           w_down, norm_final):
    depth = w_in.shape[0]
    assert depth == 1
    nbp, seq, _ = x_prompt.shape
    nbs, dec_seq, _ = x_sample.shape
    assert dec_seq <= SAMPLE_PAD and nbs % SAMPLE_GROUP == 0
    assert seq % (2 * PROMPT_CHUNK * PROMPT_CHUNKS_PER_STEP) == 0
    layer = 0

    w_in_bf = jnp.concatenate([w_in[layer].astype(BF16), jnp.zeros((D_MODEL, IN_PAD - IN_WIDTH), BF16)], axis=1)
    w_out_bf = w_out[layer].astype(BF16)
    w_up_bf = w_up[layer].astype(BF16)
    w_down_bf = w_down[layer].astype(BF16)
    row = lambda v: v.reshape(1, -1).astype(F32)
    pad_ba = lambda v: jnp.pad(v.astype(F32), (N_HEADS, LANES - 2 * N_HEADS)).reshape(1, LANES)
    mixer_consts = (conv_gdn[layer], pad_ba(gdn_a_log[layer]), pad_ba(gdn_dt_bias[layer]),
                    norm_ret[layer].reshape(N_HEADS, D_HEAD), row(norm_gdn[layer]))
    ffn_weights = (w_out_bf, row(norm_ffn[layer]), w_up_bf, conv_ffn[layer], w_down_bf, row(norm_final))
    nmix = row(norm_mix[layer])

    assert seq % N_META == 0 and (seq + N_META) % SAMPLE_PAD == 0
    rope_meta_row0, rope_sample_row0 = seq, seq + N_META
    rope = _rope_tables(seq + N_META + SAMPLE_PAD,
                        [(0, N_META), (rope_meta_row0, 0), (rope_sample_row0, PAST_LEN)])

    xs_pad = jnp.pad(x_sample, ((0, 0), (0, SAMPLE_PAD - dec_seq), (0, 0))).reshape(nbs * SAMPLE_PAD, D_MODEL)
    small_rows = jnp.concatenate([xs_pad, meta_tokens.astype(F32)], axis=0)
    n_small = small_rows.shape[0]
    proj_small = _proj(small_rows, nmix, w_in_bf, tm=n_small // 2)
    meta_row0 = nbs * SAMPLE_PAD

    zero_state = jnp.zeros((1, N_HEADS, D_HEAD, D_HEAD), F32)
    zero_cq = jnp.zeros((1, CONV_GDN - 1, GDN_QKV), F32)
    mix_m, sret_m, sgdn_m, cq_m = _mixer(proj_small, meta_row0, 1, N_META, zero_state, zero_state, zero_cq, rope,
                                         rope_meta_row0, mixer_consts, bb=1, c=N_META, nch=1, n_valid=N_META,
                                         shared_init=True)
    zero_cf = jnp.zeros((1, CONV_FFN - 1, 2 * D_FF), F32)
    _, cf_m = _ffn(meta_tokens.astype(F32), mix_m.reshape(N_META, D_MODEL), zero_cf, ffn_weights,
                   nseq=1, tm=N_META, stride=1, shared_init=True)

    mix_s, sret_s, sgdn_s, cq_s = _mixer(proj_small, 0, nbs, SAMPLE_PAD, state_ret[layer], state_gdn[layer],
                                         state_conv_qkv[layer], rope, rope_sample_row0, mixer_consts,
                                         bb=SAMPLE_SEQS_PER_STEP, c=SAMPLE_PAD, nch=1, n_valid=dec_seq,
                                         shared_init=False)
    ng = nbs // SAMPLE_GROUP
    to_tmajor = lambda a: a.reshape(ng, SAMPLE_GROUP, a.shape[1], a.shape[2]).transpose(0, 2, 1, 3)
    xs_t = to_tmajor(x_sample).reshape(nbs * dec_seq, D_MODEL)
    mix_t = to_tmajor(mix_s[:, :dec_seq]).reshape(nbs * dec_seq, D_MODEL)
    cf0_t = to_tmajor(state_ffn_conv[layer]).reshape(ng, (CONV_FFN - 1) * SAMPLE_GROUP, 2 * D_FF)
    y_s_t, cf_s_t = _ffn(xs_t, mix_t, cf0_t, ffn_weights, nseq=ng, tm=dec_seq * SAMPLE_GROUP,
                         stride=SAMPLE_GROUP, shared_init=False)
    y_sample = y_s_t.reshape(ng, dec_seq, SAMPLE_GROUP, D_MODEL).transpose(0, 2, 1, 3).reshape(nbs, dec_seq, D_MODEL)
    cf_s = cf_s_t.reshape(ng, CONV_FFN - 1, SAMPLE_GROUP, 2 * D_FF).transpose(0, 2, 1, 3).reshape(
        nbs, CONV_FFN - 1, 2 * D_FF)

    xp = x_prompt.reshape(nbp * seq, D_MODEL)
    mix_p, sret_p, sgdn_p, cq_p = _proj_mixer(xp, nmix, w_in_bf, nbp, sret_m, sgdn_m, cq_m, rope, mixer_consts,
                                              c=PROMPT_CHUNK, nch=PROMPT_CHUNKS_PER_STEP)
    y_p, cf_p = _ffn(xp, mix_p, cf_m, ffn_weights, nseq=nbp, tm=512, stride=1, shared_init=True)
    y_prompt = y_p.reshape(nbp, seq, D_MODEL)

    return (y_prompt, y_sample, sret_p[None], sgdn_p[None], cq_p[None], cf_p[None],
            sret_s[None], sgdn_s[None], cq_s[None], cf_s[None])
```

```python
import functools

import jax
import numpy as np
import jax.numpy as jnp
from jax import lax
from jax.experimental import pallas as pl
from jax.experimental.pallas import tpu as pltpu

F32 = jnp.float32
BF16 = jnp.bfloat16

D_MODEL = 1024
N_META = 16
PAST_LEN = 16384
N_HEADS = 4
D_HEAD = 128
D_GRP = N_HEADS * D_HEAD
GDN_QKV = 3 * D_GRP
CONV_GDN = 4
CONV_FFN = 3
D_FF = 2816
ROPE_THETA = 10000.0
EPS = 1e-6

OFF_RQ, OFF_RK, OFF_RV, OFF_RG = 0, D_GRP, 2 * D_GRP, 3 * D_GRP
OFF_QKV = 4 * D_GRP
OFF_GG = OFF_QKV + GDN_QKV
OFF_BA = OFF_GG + D_GRP
IN_WIDTH = OFF_BA + 2 * N_HEADS
LANES = 128
SUBLANES = 8
IN_PAD = OFF_BA + LANES

PROMPT_CHUNK = 128
PROMPT_CHUNKS_PER_STEP = 2
SAMPLE_PAD = 8
SAMPLE_SEQS_PER_STEP = 8
SAMPLE_GROUP = 32
VMEM_LIMIT = 56 * 1024 * 1024


def _round_up(n, m):
    return (n + m - 1) // m * m


def _mm(a, b):
    return jnp.dot(a.astype(BF16), b.astype(BF16), preferred_element_type=F32)


def _mm_nt(a, b):
    return lax.dot_general(a.astype(BF16), b.astype(BF16), (((1,), (1,)), ((), ())),
                           preferred_element_type=F32)


def _mm_tn(a, b):
    return lax.dot_general(a.astype(BF16), b.astype(BF16), (((0,), (0,)), ((), ())),
                           preferred_element_type=F32)


def _split2(x):
    hi = x.astype(BF16)
    lo = (x - hi.astype(F32)).astype(BF16)
    return hi, lo


def _split3(x):
    p0 = x.astype(BF16)
    r = x - p0.astype(F32)
    p1 = r.astype(BF16)
    p2 = (r - p1.astype(F32)).astype(BF16)
    return p0, p1, p2


def _mm3(a2, b2):
    (ah, al), (bh, bl) = a2, b2
    dot = lambda x, y: jnp.dot(x, y, preferred_element_type=F32)
    return dot(ah, bh) + (dot(ah, bl) + dot(al, bh))


def _silu(x):
    return x * jax.nn.sigmoid(x)


def _rms(x, w):
    return x * lax.rsqrt(jnp.mean(x * x, axis=-1, keepdims=True) + EPS) * w


def _rope_kernel(invf_ref, cos_ref, sin_ref, *, segments):
    shape = cos_ref.shape
    r = lax.broadcasted_iota(jnp.int32, shape, 0)
    pos = r + (segments[0][1] - segments[0][0])
    for row0, pos0 in segments[1:]:
        pos = jnp.where(r >= row0, r + (pos0 - row0), pos)
    ang = pos.astype(F32) * invf_ref[...]
    lane = lax.broadcasted_iota(jnp.int32, shape, 1)
    sin = jnp.sin(ang)
    cos_ref[...] = jnp.cos(ang)
    sin_ref[...] = jnp.where(lane < D_HEAD // 2, -sin, sin)


def _rope_tables(n, segments):
    half = D_HEAD // 2
    inv_freq = ROPE_THETA ** (-jnp.arange(half, dtype=F32) / half)
    invf2 = jnp.concatenate([inv_freq, inv_freq]).reshape(1, LANES)
    out = jax.ShapeDtypeStruct((n, LANES), F32)
    return pl.pallas_call(functools.partial(_rope_kernel, segments=tuple(segments)), out_shape=(out, out),
                          name="rope_tables")(invf2)


def _cast_pad_kernel(w_ref, o_ref):
    o_ref[:, :OFF_BA] = w_ref[:, :OFF_BA].astype(BF16)
    tail = jnp.concatenate([w_ref[:, OFF_BA:IN_WIDTH], jnp.zeros((w_ref.shape[0], IN_PAD - IN_WIDTH), F32)], axis=1)
    o_ref[:, OFF_BA:] = tail.astype(BF16)


def _cast_pad_w_in(w):
    tr = LANES
    return pl.pallas_call(
        _cast_pad_kernel,
        out_shape=jax.ShapeDtypeStruct((D_MODEL, IN_PAD), BF16),
        grid=(D_MODEL // tr,),
        in_specs=[pl.BlockSpec((tr, IN_WIDTH), lambda i: (i, 0))],
        out_specs=pl.BlockSpec((tr, IN_PAD), lambda i: (i, 0)),
        name="cast_w_in",
    )(w)


def _proj_kernel(x_ref, nw_ref, w_ref, o_ref):
    h = _rms(x_ref[...], nw_ref[...])
    o_ref[...] = jnp.dot(h.astype(BF16), w_ref[...], preferred_element_type=F32)


def _proj(x, norm_w, w_bf, tm):
    rows = x.shape[0]
    assert rows % tm == 0
    return pl.pallas_call(
        _proj_kernel,
        out_shape=jax.ShapeDtypeStruct((rows, IN_PAD), F32),
        grid=(rows // tm,),
        in_specs=[
            pl.BlockSpec((tm, D_MODEL), lambda i: (i, 0)),
            pl.BlockSpec((1, D_MODEL), lambda i: (0, 0)),
            pl.BlockSpec((D_MODEL, IN_PAD), lambda i: (0, 0), pipeline_mode=pl.Buffered(1)),
        ],
        out_specs=pl.BlockSpec((tm, IN_PAD), lambda i: (i, 0)),
        compiler_params=pltpu.CompilerParams(dimension_semantics=("arbitrary",),
                                             vmem_limit_bytes=VMEM_LIMIT),
        name="in_proj",
    )(x, norm_w, w_bf)


def _unit_lower_inverses(mats, c, tick=lambda: None):
    ri = lax.broadcasted_iota(jnp.int32, (c, c), 0)
    ci = lax.broadcasted_iota(jnp.int32, (c, c), 1)
    eye = (ri == ci).astype(F32)
    diag_blk = (ri // SUBLANES) == (ci // SUBLANES)
    ad = [jnp.where(diag_blk, a, 0.0) for a in mats]
    a2 = [_mm(x, x) for x in ad]
    tick()
    a4 = [_mm(x, x) for x in a2]
    t = [eye - x for x in ad]
    t = [x + _mm(x, s) for x, s in zip(t, a2)]
    tick()
    t = [x + _mm(x, s) for x, s in zip(t, a4)]
    tick()
    s = SUBLANES
    while s < c:
        level = ((ri // (2 * s)) == (ci // (2 * s))) & ((ri // s) != (ci // s))
        off = [jnp.where(level, a, 0.0) for a in mats]
        lt = [_mm(o, x) for o, x in zip(off, t)]
        tick()
        t = [x - _mm(x, y) for x, y in zip(t, lt)]
        tick()
        s *= 2
    return t


XS_TOP = SUBLANES


def _mixer_init(sret0_ref, sgdn0_ref, cq0_ref, sret_ref, sgdn_ref, xs_ref):
    tail = CONV_GDN - 1
    sret_ref[...] = sret0_ref[...]
    sgdn_ref[...] = sgdn0_ref[...]
    xs_ref[:, XS_TOP - tail:XS_TOP, :] = cq0_ref[...]


def _mixer_block(getp, put_mix, cos2, sin2, const_refs, sret_ref, sgdn_ref, cq_ref, xs_ref, *,
                 bb, c, nch, n_valid, tick=lambda: None):
    (tril_ref, triu_ref, dint_ref, qdec_ref, kdec_ref, cdec_ref, cw_ref, alog_ref, dtb_ref, nret_ref,
     ngdn_ref) = const_refs
    rows = nch * c
    tail = CONV_GDN - 1
    top = XS_TOP
    ri = lax.broadcasted_iota(jnp.int32, (c, c), 0)
    cj = lax.broadcasted_iota(jnp.int32, (c, c), 1)
    tri = ri >= cj
    strict = ri > cj
    tril_bf = tril_ref[...]
    triu_bf = triu_ref[...]
    scale = D_HEAD ** -0.5
    heads = range(N_HEADS)
    seqs = range(bb)
    chunks = [slice(j * c, (j + 1) * c) for j in range(nch)]
    join = lambda parts: parts[0] if len(parts) == 1 else jnp.concatenate(parts, axis=0)

    ret = {}
    for b in seqs:
        for h in heads:
            lo = h * D_HEAD
            q = getp(b, OFF_RQ + lo, OFF_RQ + lo + D_HEAD)
            k = getp(b, OFF_RK + lo, OFF_RK + lo + D_HEAD)
            v = getp(b, OFF_RV + lo, OFF_RV + lo + D_HEAD)
            qr = q * cos2 + pltpu.roll(q, D_HEAD // 2, 1) * sin2
            kr = (k * cos2 + pltpu.roll(k, D_HEAD // 2, 1) * sin2) * scale
            ret[b, h] = (qr, kr, v)
    tick()
    rtasks = [(b, h, j) for j in range(nch) for b in seqs for h in heads]
    scores = {t: _mm_nt(ret[t[0], t[1]][0][chunks[t[2]]], ret[t[0], t[1]][1][chunks[t[2]]]) * dint_ref[t[1]]
              for t in rtasks}
    tick()
    intra = {t: _mm(scores[t], ret[t[0], t[1]][2][chunks[t[2]]]) for t in rtasks}
    tick()
    kv = {t: _mm_tn(ret[t[0], t[1]][1][chunks[t[2]]] * kdec_ref[t[1]], ret[t[0], t[1]][2][chunks[t[2]]])
          for t in rtasks}
    tick()
    pairs = [(b, h) for b in seqs for h in heads]
    s_ret = {bh: sret_ref[bh[0], bh[1]] for bh in pairs}
    o_ret = {}
    for j in range(nch):
        for b, h in pairs:
            o_ret[b, h, j] = intra[b, h, j] + _mm(ret[b, h][0][chunks[j]] * qdec_ref[h], s_ret[b, h])
        for b, h in pairs:
            s_ret[b, h] = cdec_ref[h] * s_ret[b, h] + kv[b, h, j]
        tick()
    o_ret = {bh: join([o_ret[bh[0], bh[1], j] for j in range(nch)]) for bh in pairs}
    mu = {bh: jnp.mean(o_ret[bh], axis=-1, keepdims=True) for bh in pairs}
    cen = {bh: o_ret[bh] - mu[bh] for bh in pairs}
    var = {bh: jnp.mean(cen[bh] * cen[bh], axis=-1, keepdims=True) for bh in pairs}
    for b, h in pairs:
        lo = h * D_HEAD
        gate = getp(b, OFF_RG + lo, OFF_RG + lo + D_HEAD)
        o = cen[b, h] * lax.rsqrt(var[b, h] + EPS) * nret_ref[h:h + 1, :]
        put_mix(b, lo, o * _silu(gate))
        sret_ref[b, h] = s_ret[b, h]
    tick()

    qkvs, beta_alls, cum_cols, cum_rows, cum_tots = {}, {}, {}, {}, {}
    ones_bf = jnp.ones((D_HEAD, c), BF16)
    for b in seqs:
        xs_ref[b, top:top + rows, :] = getp(b, OFF_QKV, OFF_QKV + GDN_QKV)
        conv = xs_ref[b, top - tail:top - tail + rows, :] * cw_ref[0:1, :]
        for i in range(1, CONV_GDN):
            conv = conv + xs_ref[b, top - tail + i:top - tail + i + rows, :] * cw_ref[i:i + 1, :]
        last = n_valid if nch == 1 else rows
        new_tail = xs_ref[b, top + last - tail:top + last, :]
        xs_ref[b, top - tail:top, :] = new_tail
        cq_ref[b] = new_tail
        qkvs[b] = _silu(conv)
        tick()

        ba = getp(b, OFF_BA, OFF_BA + LANES)
        beta_all = jax.nn.sigmoid(ba)
        z = ba + dtb_ref[...]
        softplus = jnp.maximum(z, 0.0) + jnp.log1p(jnp.exp(-jnp.abs(z)))
        g_all = -jnp.exp(alog_ref[...]) * softplus
        if n_valid < c:
            row = lax.broadcasted_iota(jnp.int32, (rows, LANES), 0)
            rowmask = (row < n_valid).astype(F32)
            beta_all = beta_all * rowmask
            g_all = g_all * rowmask
        beta_alls[b] = beta_all
        for j, rs in enumerate(chunks):
            parts = _split3(g_all[rs])
            cum_cols[b, j] = sum(jnp.dot(tril_bf, g, preferred_element_type=F32) for g in parts)
            cum_tots[b, j] = sum(jnp.dot(ones_bf, g, preferred_element_type=F32) for g in parts)
            cum_rows[b, j] = sum(lax.dot_general(g, triu_bf, (((0,), (0,)), ((), ())),
                                                 preferred_element_type=F32) for g in parts)
    tick()

    tasks = [(j, b, h) for j in range(nch) for b in seqs for h in heads]
    qs, ks, vs, betas, ecums, kdecs, cdecs, amats, qkms = [], [], [], [], [], [], [], [], []
    raw = [(qkvs[b][chunks[j], h * D_HEAD:(h + 1) * D_HEAD],
            qkvs[b][chunks[j], D_GRP + h * D_HEAD:D_GRP + (h + 1) * D_HEAD]) for j, b, h in tasks]
    sumsq = [(jnp.sum(q * q, axis=-1, keepdims=True), jnp.sum(k * k, axis=-1, keepdims=True)) for q, k in raw]
    for i, (j, b, h) in enumerate(tasks):
        rs = chunks[j]
        lo = h * D_HEAD
        v = qkvs[b][rs, 2 * D_GRP + lo:2 * D_GRP + lo + D_HEAD]
        q = raw[i][0] * lax.rsqrt(sumsq[i][0] + EPS) * scale
        k = raw[i][1] * lax.rsqrt(sumsq[i][1] + EPS)
        beta = jnp.broadcast_to(beta_alls[b][rs, h:h + 1], (c, LANES))
        cum = jnp.broadcast_to(cum_cols[b, j][:, N_HEADS + h:N_HEADS + h + 1], (c, LANES))
        cum_row = cum_rows[b, j][N_HEADS + h:N_HEADS + h + 1, :]
        dmask = jnp.exp(jnp.where(tri, cum[:, :c] - cum_row, -jnp.inf))
        cum_last = jnp.broadcast_to(cum_tots[b, j][:, N_HEADS + h:N_HEADS + h + 1], (D_HEAD, LANES))
        kq = _mm_nt(jnp.concatenate([k, q], axis=0), k)
        amats.append(jnp.where(strict, beta[:, :c] * kq[:c] * dmask, 0.0))
        qkms.append(jnp.where(tri, kq[c:] * dmask, 0.0))
        qs.append(q)
        ks.append(k)
        vs.append(v)
        betas.append(beta)
        ecums.append(jnp.exp(cum))
        kdecs.append(jnp.exp(cum_last[:c] - cum))
        cdecs.append(jnp.exp(cum_last))
        if i % N_HEADS == N_HEADS - 1:
            tick()

    tinv = _unit_lower_inverses(amats, c, tick)
    sols = []
    for i in range(len(tasks)):
        rhs = jnp.concatenate([vs[i] * betas[i], ks[i] * (betas[i] * ecums[i])], axis=1)
        sols.append(_mm3(_split2(tinv[i]), _split2(rhs)))
    tick()

    s_gdn = {bh: sgdn_ref[bh[0], bh[1]] for bh in pairs}
    o_gdn = {}
    for j in range(nch):
        idx = {tasks[i][1:]: i for i in range(len(tasks)) if tasks[i][0] == j}
        lhs = {bh: jnp.concatenate([sols[i][:, D_HEAD:], qs[i] * ecums[i]], axis=0) for bh, i in idx.items()}
        both = {bh: _mm(lhs[bh], s_gdn[bh]) for bh in idx}
        tick()
        w = {bh: sols[i][:, :D_HEAD] - both[bh][:c] for bh, i in idx.items()}
        for bh, i in idx.items():
            o_gdn[bh + (j,)] = both[bh][c:] + _mm(qkms[i], w[bh])
        tick()
        upd = {bh: _mm_tn(ks[i] * kdecs[i], w[bh]) for bh, i in idx.items()}
        for bh, i in idx.items():
            s_gdn[bh] = cdecs[i] * s_gdn[bh] + upd[bh]
        tick()
    o_gdn = {bh: join([o_gdn[bh + (j,)] for j in range(nch)]) for bh in pairs}
    msq = {bh: jnp.mean(o_gdn[bh] * o_gdn[bh], axis=-1, keepdims=True) for bh in pairs}
    for b, h in pairs:
        lo = h * D_HEAD
        gate = getp(b, OFF_GG + lo, OFF_GG + lo + D_HEAD)
        o = o_gdn[b, h] * lax.rsqrt(msq[b, h] + EPS) * ngdn_ref[...]
        put_mix(b, D_GRP + lo, o * _silu(gate))
        sgdn_ref[b, h] = s_gdn[b, h]


N_MIXER_CONSTS = 11


def _mixer_kernel(p_ref, sret0_ref, sgdn0_ref, cq0_ref, cos_ref, sin_ref, *rest, bb, c, nch, n_valid):
    const_refs, (mix_ref, sret_ref, sgdn_ref, cq_ref, xs_ref) = rest[:N_MIXER_CONSTS], rest[N_MIXER_CONSTS:]
    rows = nch * c

    @pl.when(pl.program_id(1) == 0)
    def _():
        _mixer_init(sret0_ref, sgdn0_ref, cq0_ref, sret_ref, sgdn_ref, xs_ref)

    def put_mix(b, lo, value):
        mix_ref[b, :, lo:lo + D_HEAD] = value.astype(mix_ref.dtype)

    _mixer_block(lambda b, lo, hi: p_ref[b * rows:(b + 1) * rows, lo:hi], put_mix, cos_ref[...], sin_ref[...],
                 const_refs, sret_ref, sgdn_ref, cq_ref, xs_ref, bb=bb, c=c, nch=nch, n_valid=n_valid)


PROJ_PANEL = 2 * LANES


def _mixer_block_ticks(bb, c, nch):
    levels = (c // SUBLANES).bit_length() - 1
    retention = 4 + nch + 1
    gdn_prep = bb + 1 + bb * nch
    inverse = 3 + 2 * levels
    return retention + gdn_prep + inverse + 1 + 3 * nch


def _proj_mixer_kernel(x0_ref, xa_ref, xb_ref, nw_ref, w_ref, sret0_ref, sgdn0_ref, cq0_ref, cos_ref, sin_ref,
                       *rest, c, nch, steps_per_seq, n_cast):
    const_refs = rest[:N_MIXER_CONSTS]
    cast_in = rest[N_MIXER_CONSTS:N_MIXER_CONSTS + n_cast]
    mix_ref, sret_ref, sgdn_ref, cq_ref = rest[N_MIXER_CONSTS + n_cast:N_MIXER_CONSTS + n_cast + 4]
    cast_out = rest[N_MIXER_CONSTS + n_cast + 4:N_MIXER_CONSTS + 2 * n_cast + 4]
    pja_ref, pjb_ref, xs_ref = rest[N_MIXER_CONSTS + 2 * n_cast + 4:]
    step = pl.program_id(0)
    rows = nch * c

    for src_ref, dst_ref in zip(cast_in, cast_out):
        dst_ref[...] = src_ref[...].astype(dst_ref.dtype)

    def project(x_ref, dst_ref, n_ticks):
        h = _rms(x_ref[...], nw_ref[...]).astype(BF16)
        panels = [(lo, min(lo + PROJ_PANEL, IN_PAD)) for lo in range(0, IN_PAD, PROJ_PANEL)]
        n_panels = len(panels)
        calls = [0]

        def emit():
            lo, hi = panels.pop(0)
            dst_ref[:, lo:hi] = jnp.dot(h, w_ref[:, lo:hi], preferred_element_type=F32)

        def tick():
            calls[0] += 1
            due = min(n_panels, -(-calls[0] * n_panels // n_ticks))
            while n_panels - len(panels) < due:
                emit()

        def flush():
            while panels:
                emit()

        return tick, flush

    @pl.when(step == 0)
    def _():
        project(x0_ref, pja_ref, 1)[1]()

    @pl.when(lax.rem(step, steps_per_seq) == 0)
    def _():
        _mixer_init(sret0_ref, sgdn0_ref, cq0_ref, sret_ref, sgdn_ref, xs_ref)

    for half, (cur_ref, x_next_ref, nxt_ref) in enumerate(((pja_ref, xa_ref, pjb_ref), (pjb_ref, xb_ref, pja_ref))):
        r0 = half * rows
        tick, flush = project(x_next_ref, nxt_ref, _mixer_block_ticks(1, c, nch))

        def put_mix(b, lo, value, r0=r0):
            mix_ref[r0:r0 + rows, lo:lo + D_HEAD] = value.astype(mix_ref.dtype)

        _mixer_block(lambda b, lo, hi, cur_ref=cur_ref: cur_ref[:, lo:hi], put_mix,
                     cos_ref[r0:r0 + rows, :], sin_ref[r0:r0 + rows, :], const_refs,
                     sret_ref, sgdn_ref, cq_ref, xs_ref, bb=1, c=c, nch=nch, n_valid=c, tick=tick)
        flush()


def _retention_decay_tables(c, n_valid):
    f32 = np.float32
    lg = np.log1p(-np.power(f32(2.0), f32(-5.0) - np.arange(N_HEADS, dtype=f32)))[:, None].astype(f32)
    idx = np.arange(c, dtype=f32)
    diff = idx[:, None] - idx[None, :]
    dint = np.where(diff[None] >= 0, np.exp(lg[:, :, None] * np.maximum(diff[None], 0)), f32(0.0)).astype(f32)
    qdec = np.exp(lg * (idx + f32(1.0))).astype(f32)
    kdec = np.where(idx[None, :] < n_valid, np.exp(lg * np.minimum(f32(n_valid) - f32(1.0) - idx, c)), f32(0.0))
    cdec = np.exp(lg * f32(n_valid)).astype(f32)
    bc = lambda t: np.broadcast_to(t.astype(f32)[:, :, None], t.shape + (LANES,))
    return dint, bc(qdec), bc(kdec), np.broadcast_to(cdec[:, :, None], (N_HEADS, 1, LANES))


def _mixer_const_operands(c, n_valid, consts):
    cw, alog, dtb, nret, ngdn = consts
    dint, qdec, kdec, cdec = _retention_decay_tables(c, n_valid)
    idx = np.arange(c)
    tril = (idx[:, None] >= idx[None, :]).astype(np.float32)
    arrays = (jnp.asarray(tril, BF16), jnp.asarray(tril.T, BF16), jnp.asarray(dint), jnp.asarray(qdec),
              jnp.asarray(kdec), jnp.asarray(cdec), cw, alog, dtb, nret, ngdn)
    assert len(arrays) == N_MIXER_CONSTS
    return arrays, [a.shape for a in arrays]


def _mixer(proj, row0, nb, length, sret0, sgdn0, cq0, rope, rope_row0, consts, *, bb, c, nch, n_valid,
           shared_init):
    rows = nch * c
    assert nb % bb == 0 and length % rows == 0 and rope_row0 % rows == 0 and row0 % (bb * length) == 0
    assert not shared_init or bb == 1
    assert n_valid == c or nch == 1
    assert bb == 1 or length == rows
    const_arrays, const_shapes = _mixer_const_operands(c, n_valid, consts)
    nsteps = length // rows
    blk0 = row0 // (bb * rows)
    init_idx = (lambda b, i: (0, 0, 0, 0)) if shared_init else (lambda b, i: (b, 0, 0, 0))
    init_idx3 = (lambda b, i: (0, 0, 0)) if shared_init else (lambda b, i: (b, 0, 0))
    whole = lambda shape: pl.BlockSpec(shape, lambda b, i: (0,) * len(shape))
    rope_spec = pl.BlockSpec((rows, LANES), lambda b, i: (rope_row0 // rows + i, 0))
    state_shape = (bb, N_HEADS, D_HEAD, D_HEAD)
    tail = CONV_GDN - 1
    kern = functools.partial(_mixer_kernel, bb=bb, c=c, nch=nch, n_valid=n_valid)
    return pl.pallas_call(
        kern,
        out_shape=(
            jax.ShapeDtypeStruct((nb, length, D_MODEL), BF16),
            jax.ShapeDtypeStruct((nb, N_HEADS, D_HEAD, D_HEAD), F32),
            jax.ShapeDtypeStruct((nb, N_HEADS, D_HEAD, D_HEAD), F32),
            jax.ShapeDtypeStruct((nb, tail, GDN_QKV), F32),
        ),
        grid=(nb // bb, nsteps),
        in_specs=[
            pl.BlockSpec((bb * rows, IN_PAD), lambda b, i: (blk0 + b * nsteps + i, 0)),
            pl.BlockSpec(state_shape, init_idx),
            pl.BlockSpec(state_shape, init_idx),
            pl.BlockSpec((bb, tail, GDN_QKV), init_idx3),
            rope_spec,
            rope_spec,
        ] + [whole(shape) for shape in const_shapes],
        out_specs=(
            pl.BlockSpec((bb, rows, D_MODEL), lambda b, i: (b, i, 0)),
            pl.BlockSpec(state_shape, lambda b, i: (b, 0, 0, 0)),
            pl.BlockSpec(state_shape, lambda b, i: (b, 0, 0, 0)),
            pl.BlockSpec((bb, tail, GDN_QKV), lambda b, i: (b, 0, 0)),
        ),
        scratch_shapes=[pltpu.VMEM((bb, XS_TOP + rows, GDN_QKV), F32)],
        compiler_params=pltpu.CompilerParams(dimension_semantics=("arbitrary", "arbitrary"),
                                             vmem_limit_bytes=VMEM_LIMIT),
        name="mixer",
    )(proj, sret0, sgdn0, cq0, rope[0], rope[1], *const_arrays)


def _row_slab(nrows, nsteps):
    for hold in (1, 2, 4, 8):
        slabs = nsteps // hold
        if nsteps % hold == 0 and nrows % slabs == 0 and (nrows // slabs) % (2 * SUBLANES) == 0:
            return nrows // slabs, hold
    raise ValueError((nrows, nsteps))


def _proj_mixer(x, norm_w, w_bf, nseq, sret0, sgdn0, cq0, rope, consts, to_bf16, *, c, nch):
    rows = nch * c
    total = x.shape[0]
    length = total // nseq
    assert total % nseq == 0 and length % (2 * rows) == 0
    nblk = total // rows
    steps_per_seq = length // (2 * rows)
    const_arrays, const_shapes = _mixer_const_operands(c, c, consts)
    whole = lambda shape, **kw: pl.BlockSpec(shape, lambda s: (0,) * len(shape), **kw)
    rope_spec = pl.BlockSpec((2 * rows, LANES), lambda s: (lax.rem(s, steps_per_seq), 0))
    state_shape = (1, N_HEADS, D_HEAD, D_HEAD)
    tail = CONV_GDN - 1
    nsteps = nblk // 2
    slabs = [_row_slab(w.shape[0], nsteps) for w in to_bf16]
    cast_specs = [pl.BlockSpec((r, w.shape[1]), lambda s, hold=hold: (s // hold, 0))
                  for w, (r, hold) in zip(to_bf16, slabs)]
    kern = functools.partial(_proj_mixer_kernel, c=c, nch=nch, steps_per_seq=steps_per_seq, n_cast=len(to_bf16))
    return pl.pallas_call(
        kern,
        out_shape=(
            jax.ShapeDtypeStruct((total, D_MODEL), BF16),
            jax.ShapeDtypeStruct((nseq, N_HEADS, D_HEAD, D_HEAD), F32),
            jax.ShapeDtypeStruct((nseq, N_HEADS, D_HEAD, D_HEAD), F32),
            jax.ShapeDtypeStruct((nseq, tail, GDN_QKV), F32),
        ) + tuple(jax.ShapeDtypeStruct(w.shape, BF16) for w in to_bf16),
        grid=(nsteps,),
        in_specs=[
            pl.BlockSpec((rows, D_MODEL), lambda s: (0, 0), pipeline_mode=pl.Buffered(1)),
            pl.BlockSpec((rows, D_MODEL), lambda s: (2 * s + 1, 0)),
            pl.BlockSpec((rows, D_MODEL), lambda s: (jnp.minimum(2 * s + 2, nblk - 1), 0)),
            whole((1, D_MODEL)),
            whole((D_MODEL, IN_PAD), pipeline_mode=pl.Buffered(1)),
            whole(state_shape),
            whole(state_shape),
            whole((1, tail, GDN_QKV)),
            rope_spec,
            rope_spec,
        ] + [whole(shape) for shape in const_shapes] + cast_specs,
        out_specs=(
            pl.BlockSpec((2 * rows, D_MODEL), lambda s: (s, 0)),
            pl.BlockSpec(state_shape, lambda s: (s // steps_per_seq, 0, 0, 0)),
            pl.BlockSpec(state_shape, lambda s: (s // steps_per_seq, 0, 0, 0)),
            pl.BlockSpec((1, tail, GDN_QKV), lambda s: (s // steps_per_seq, 0, 0)),
        ) + tuple(cast_specs),
        scratch_shapes=[pltpu.VMEM((rows, IN_PAD), F32), pltpu.VMEM((rows, IN_PAD), F32),
                        pltpu.VMEM((1, XS_TOP + rows, GDN_QKV), F32)],
        compiler_params=pltpu.CompilerParams(dimension_semantics=("arbitrary",),
                                             vmem_limit_bytes=VMEM_LIMIT),
        name="proj_mixer",
    )(x, x, x, norm_w, w_bf, sret0, sgdn0, cq0, rope[0], rope[1], *const_arrays, *to_bf16)


FFN_COL_CHUNK = D_FF // 11


def _ffn_kernel(x_ref, mix_ref, tail0_ref, wout_ref, nffn_ref, wup_ref, cw_ref, wdn_ref, nfin_ref,
                y_ref, tail_ref, full_ref, *, tm, stride):
    t = pl.program_id(1)
    carry = (CONV_FFN - 1) * stride
    base = _round_up(carry, SUBLANES)

    @pl.when(t == 0)
    def _():
        full_ref[base - carry:base, :] = tail0_ref[0]

    x1 = x_ref[...] + jnp.dot(mix_ref[...], wout_ref[...], preferred_element_type=F32)
    h = _rms(x1, nffn_ref[...]).astype(BF16)
    full_ref[base:base + tm, :] = jnp.dot(h, wup_ref[...], preferred_element_type=F32)

    def conv_cols(lo):
        acc = full_ref[base - carry:base - carry + tm, lo:lo + FFN_COL_CHUNK] * cw_ref[0:1, lo:lo + FFN_COL_CHUNK]
        for i in range(1, CONV_FFN):
            r0 = base - carry + i * stride
            acc = acc + full_ref[r0:r0 + tm, lo:lo + FFN_COL_CHUNK] * cw_ref[i:i + 1, lo:lo + FFN_COL_CHUNK]
        return acc

    x2 = x1
    for j in range(D_FF // FFN_COL_CHUNK):
        lo = j * FFN_COL_CHUNK
        act = (_silu(conv_cols(lo)) * conv_cols(D_FF + lo)).astype(BF16)
        x2 = x2 + jnp.dot(act, wdn_ref[lo:lo + FFN_COL_CHUNK, :], preferred_element_type=F32)
    y_ref[...] = _rms(x2, nfin_ref[...])

    new_tail = full_ref[base + tm - carry:base + tm, :]
    full_ref[base - carry:base, :] = new_tail
    tail_ref[0] = new_tail


def _ffn(x, mix, tail0, weights, *, nseq, tm, stride, shared_init):
    wout, nffn, wup, cw, wdn, nfin = weights
    rows = x.shape[0]
    assert rows % (nseq * tm) == 0
    nt = rows // (nseq * tm)
    carry = (CONV_FFN - 1) * stride
    base = _round_up(carry, SUBLANES)
    assert tm >= carry
    resident = lambda shape: pl.BlockSpec(shape, lambda b, t: (0, 0), pipeline_mode=pl.Buffered(1))
    small = lambda shape: pl.BlockSpec(shape, lambda b, t: (0, 0))
    tail_idx = (lambda b, t: (0, 0, 0)) if shared_init else (lambda b, t: (b, 0, 0))
    kern = functools.partial(_ffn_kernel, tm=tm, stride=stride)
    return pl.pallas_call(
        kern,
        out_shape=(
            jax.ShapeDtypeStruct((rows, D_MODEL), F32),
            jax.ShapeDtypeStruct((nseq, carry, 2 * D_FF), F32),
        ),
        grid=(nseq, nt),
        in_specs=[
            pl.BlockSpec((tm, D_MODEL), lambda b, t: (b * nt + t, 0)),
            pl.BlockSpec((tm, D_MODEL), lambda b, t: (b * nt + t, 0)),
            pl.BlockSpec((1, carry, 2 * D_FF), tail_idx),
            resident((D_MODEL, D_MODEL)),
            small((1, D_MODEL)),
            resident((D_MODEL, 2 * D_FF)),
            small((CONV_FFN, 2 * D_FF)),
            resident((D_FF, D_MODEL)),
            small((1, D_MODEL)),
        ],
        out_specs=(
            pl.BlockSpec((tm, D_MODEL), lambda b, t: (b * nt + t, 0)),
            pl.BlockSpec((1, carry, 2 * D_FF), lambda b, t: (b, 0, 0)),
        ),
        scratch_shapes=[pltpu.VMEM((base + tm, 2 * D_FF), F32)],
        compiler_params=pltpu.CompilerParams(dimension_semantics=("arbitrary", "arbitrary"),
                                             vmem_limit_bytes=VMEM_LIMIT),
        name="out_ffn",
    )(x, mix, tail0, wout, nffn, wup, cw, wdn, nfin)


def kernel(x_prompt, x_sample, state_ret, state_gdn, state_conv_qkv, state_ffn_conv, meta_tokens, norm_mix,
           w_in, conv_gdn, gdn_a_log, gdn_dt_bias, norm_ret, norm_gdn, w_out, norm_ffn, w_up, conv_ffn,
           w_down, norm_final):
    depth = w_in.shape[0]
    assert depth == 1
    nbp, seq, _ = x_prompt.shape
    nbs, dec_seq, _ = x_sample.shape
    assert dec_seq <= SAMPLE_PAD and nbs % SAMPLE_GROUP == 0
    assert seq % (2 * PROMPT_CHUNK * PROMPT_CHUNKS_PER_STEP) == 0
    layer = 0

    w_in_bf = _cast_pad_w_in(w_in[layer])
    row = lambda v: v.reshape(1, -1).astype(F32)
    pad_ba = lambda v: jnp.pad(v.astype(F32), (N_HEADS, LANES - 2 * N_HEADS)).reshape(1, LANES)
    mixer_consts = (conv_gdn[layer], pad_ba(gdn_a_log[layer]), pad_ba(gdn_dt_bias[layer]),
                    norm_ret[layer].reshape(N_HEADS, D_HEAD), row(norm_gdn[layer]))
    nmix = row(norm_mix[layer])

    assert seq % N_META == 0 and (seq + N_META) % SAMPLE_PAD == 0
    rope_meta_row0, rope_sample_row0 = seq, seq + N_META
    rope = _rope_tables(seq + N_META + SAMPLE_PAD,
                        [(0, N_META), (rope_meta_row0, 0), (rope_sample_row0, PAST_LEN)])

    xs_pad = jnp.pad(x_sample, ((0, 0), (0, SAMPLE_PAD - dec_seq), (0, 0))).reshape(nbs * SAMPLE_PAD, D_MODEL)
    small_rows = jnp.concatenate([xs_pad, meta_tokens.astype(F32)], axis=0)
    n_small = small_rows.shape[0]
    proj_small = _proj(small_rows, nmix, w_in_bf, tm=n_small // 2)
    meta_row0 = nbs * SAMPLE_PAD

    zero_state = jnp.zeros((1, N_HEADS, D_HEAD, D_HEAD), F32)
    zero_cq = jnp.zeros((1, CONV_GDN - 1, GDN_QKV), F32)
    mix_m, sret_m, sgdn_m, cq_m = _mixer(proj_small, meta_row0, 1, N_META, zero_state, zero_state, zero_cq, rope,
                                         rope_meta_row0, mixer_consts, bb=1, c=N_META, nch=1, n_valid=N_META,
                                         shared_init=True)

    xp = x_prompt.reshape(nbp * seq, D_MODEL)
    mix_p, sret_p, sgdn_p, cq_p, w_out_bf, w_up_bf, w_down_bf = _proj_mixer(
        xp, nmix, w_in_bf, nbp, sret_m, sgdn_m, cq_m, rope, mixer_consts,
        (w_out[layer], w_up[layer], w_down[layer]), c=PROMPT_CHUNK, nch=PROMPT_CHUNKS_PER_STEP)
    ffn_weights = (w_out_bf, row(norm_ffn[layer]), w_up_bf, conv_ffn[layer], w_down_bf, row(norm_final))

    zero_cf = jnp.zeros((1, CONV_FFN - 1, 2 * D_FF), F32)
    _, cf_m = _ffn(meta_tokens.astype(F32), mix_m.reshape(N_META, D_MODEL), zero_cf, ffn_weights,
                   nseq=1, tm=N_META, stride=1, shared_init=True)
    y_p, cf_p = _ffn(xp, mix_p, cf_m, ffn_weights, nseq=nbp, tm=512, stride=1, shared_init=True)
    y_prompt = y_p.reshape(nbp, seq, D_MODEL)

    mix_s, sret_s, sgdn_s, cq_s = _mixer(proj_small, 0, nbs, SAMPLE_PAD, state_ret[layer], state_gdn[layer],
                                         state_conv_qkv[layer], rope, rope_sample_row0, mixer_consts,
                                         bb=SAMPLE_SEQS_PER_STEP, c=SAMPLE_PAD, nch=1, n_valid=dec_seq,
                                         shared_init=False)
    ng = nbs // SAMPLE_GROUP
    to_tmajor = lambda a: a.reshape(ng, SAMPLE_GROUP, a.shape[1], a.shape[2]).transpose(0, 2, 1, 3)
    xs_t = to_tmajor(x_sample).reshape(nbs * dec_seq, D_MODEL)
    mix_t = to_tmajor(mix_s[:, :dec_seq]).reshape(nbs * dec_seq, D_MODEL)
    cf0_t = to_tmajor(state_ffn_conv[layer]).reshape(ng, (CONV_FFN - 1) * SAMPLE_GROUP, 2 * D_FF)
    y_s_t, cf_s_t = _ffn(xs_t, mix_t, cf0_t, ffn_weights, nseq=ng, tm=dec_seq * SAMPLE_GROUP,
                         stride=SAMPLE_GROUP, shared_init=False)
    y_sample = y_s_t.reshape(ng, dec_seq, SAMPLE_GROUP, D_MODEL).transpose(0, 2, 1, 3).reshape(nbs, dec_seq, D_MODEL)
    cf_s = cf_s_t.reshape(ng, CONV_FFN - 1, SAMPLE_GROUP, 2 * D_FF).transpose(0, 2, 1, 3).reshape(
        nbs, CONV_FFN - 1, 2 * D_FF)

    return (y_prompt, y_sample, sret_p[None], sgdn_p[None], cq_p[None], cf_p[None],
            sret_s[None], sgdn_s[None], cq_s[None], cf_s[None])
```

```python
import functools

import jax
import numpy as np
import jax.numpy as jnp
from jax import lax
from jax.experimental import pallas as pl
from jax.experimental.pallas import tpu as pltpu

F32 = jnp.float32
BF16 = jnp.bfloat16

D_MODEL = 1024
N_META = 16
PAST_LEN = 16384
N_HEADS = 4
D_HEAD = 128
D_GRP = N_HEADS * D_HEAD
GDN_QKV = 3 * D_GRP
CONV_GDN = 4
CONV_FFN = 3
D_FF = 2816
ROPE_THETA = 10000.0
EPS = 1e-6

OFF_RQ, OFF_RK, OFF_RV, OFF_RG = 0, D_GRP, 2 * D_GRP, 3 * D_GRP
OFF_QKV = 4 * D_GRP
OFF_GG = OFF_QKV + GDN_QKV
OFF_BA = OFF_GG + D_GRP
IN_WIDTH = OFF_BA + 2 * N_HEADS
LANES = 128
SUBLANES = 8
IN_PAD = OFF_BA + LANES

PROMPT_CHUNK = 128
PROMPT_CHUNKS_PER_STEP = 2
SAMPLE_PAD = 8
SAMPLE_SEQS_PER_STEP = 8
SAMPLE_GROUP = 32
VMEM_LIMIT = 56 * 1024 * 1024


def _round_up(n, m):
    return (n + m - 1) // m * m


def _mm(a, b):
    return jnp.dot(a.astype(BF16), b.astype(BF16), preferred_element_type=F32)


def _mm_nt(a, b):
    return lax.dot_general(a.astype(BF16), b.astype(BF16), (((1,), (1,)), ((), ())),
                           preferred_element_type=F32)


def _mm_tn(a, b):
    return lax.dot_general(a.astype(BF16), b.astype(BF16), (((0,), (0,)), ((), ())),
                           preferred_element_type=F32)


def _split2(x):
    hi = x.astype(BF16)
    lo = (x - hi.astype(F32)).astype(BF16)
    return hi, lo


def _split3(x):
    p0 = x.astype(BF16)
    r = x - p0.astype(F32)
    p1 = r.astype(BF16)
    p2 = (r - p1.astype(F32)).astype(BF16)
    return p0, p1, p2


def _mm3(a2, b2):
    (ah, al), (bh, bl) = a2, b2
    dot = lambda x, y: jnp.dot(x, y, preferred_element_type=F32)
    return dot(ah, bh) + (dot(ah, bl) + dot(al, bh))


def _silu(x):
    return x * jax.nn.sigmoid(x)


def _rms(x, w):
    return x * lax.rsqrt(jnp.mean(x * x, axis=-1, keepdims=True) + EPS) * w


def _rope_kernel(invf_ref, cos_ref, sin_ref, *, segments):
    shape = cos_ref.shape
    r = lax.broadcasted_iota(jnp.int32, shape, 0)
    pos = r + (segments[0][1] - segments[0][0])
    for row0, pos0 in segments[1:]:
        pos = jnp.where(r >= row0, r + (pos0 - row0), pos)
    ang = pos.astype(F32) * invf_ref[...]
    lane = lax.broadcasted_iota(jnp.int32, shape, 1)
    sin = jnp.sin(ang)
    cos_ref[...] = jnp.cos(ang)
    sin_ref[...] = jnp.where(lane < D_HEAD // 2, -sin, sin)


def _rope_tables(n, segments):
    half = D_HEAD // 2
    inv_freq = ROPE_THETA ** (-jnp.arange(half, dtype=F32) / half)
    invf2 = jnp.concatenate([inv_freq, inv_freq]).reshape(1, LANES)
    out = jax.ShapeDtypeStruct((n, LANES), F32)
    return pl.pallas_call(functools.partial(_rope_kernel, segments=tuple(segments)), out_shape=(out, out),
                          name="rope_tables")(invf2)


def _cast_pad_kernel(wt_ref, o_ref):
    first = pl.program_id(0) * LANES
    r = lax.broadcasted_iota(jnp.int32, wt_ref.shape, 0)
    blk = jnp.where(first + r < IN_WIDTH, wt_ref[...], 0.0)
    o_ref[...] = blk.T.astype(BF16)


def _cast_pad_w_in(w):
    wt = w.T
    return pl.pallas_call(
        _cast_pad_kernel,
        out_shape=jax.ShapeDtypeStruct((D_MODEL, IN_PAD), BF16),
        grid=(IN_PAD // LANES,),
        in_specs=[pl.BlockSpec((LANES, D_MODEL), lambda i: (i, 0))],
        out_specs=pl.BlockSpec((D_MODEL, LANES), lambda i: (0, i)),
        name="cast_w_in",
    )(wt)


def _proj_kernel(x_ref, nw_ref, w_ref, o_ref):
    h = _rms(x_ref[...], nw_ref[...])
    o_ref[...] = jnp.dot(h.astype(BF16), w_ref[...], preferred_element_type=F32)


def _proj(x, norm_w, w_bf, tm):
    rows = x.shape[0]
    assert rows % tm == 0
    return pl.pallas_call(
        _proj_kernel,
        out_shape=jax.ShapeDtypeStruct((rows, IN_PAD), F32),
        grid=(rows // tm,),
        in_specs=[
            pl.BlockSpec((tm, D_MODEL), lambda i: (i, 0)),
            pl.BlockSpec((1, D_MODEL), lambda i: (0, 0)),
            pl.BlockSpec((D_MODEL, IN_PAD), lambda i: (0, 0), pipeline_mode=pl.Buffered(1)),
        ],
        out_specs=pl.BlockSpec((tm, IN_PAD), lambda i: (i, 0)),
        compiler_params=pltpu.CompilerParams(dimension_semantics=("arbitrary",),
                                             vmem_limit_bytes=VMEM_LIMIT),
        name="in_proj",
    )(x, norm_w, w_bf)


def _unit_lower_inverses(mats, c, tick=lambda: None):
    ri = lax.broadcasted_iota(jnp.int32, (c, c), 0)
    ci = lax.broadcasted_iota(jnp.int32, (c, c), 1)
    eye = (ri == ci).astype(F32)
    diag_blk = (ri // SUBLANES) == (ci // SUBLANES)
    ad = [jnp.where(diag_blk, a, 0.0) for a in mats]
    a2 = [_mm(x, x) for x in ad]
    tick()
    a4 = [_mm(x, x) for x in a2]
    t = [eye - x for x in ad]
    t = [x + _mm(x, s) for x, s in zip(t, a2)]
    tick()
    t = [x + _mm(x, s) for x, s in zip(t, a4)]
    tick()
    s = SUBLANES
    while s < c:
        level = ((ri // (2 * s)) == (ci // (2 * s))) & ((ri // s) != (ci // s))
        off = [jnp.where(level, a, 0.0) for a in mats]
        lt = [_mm(o, x) for o, x in zip(off, t)]
        tick()
        t = [x - _mm(x, y) for x, y in zip(t, lt)]
        tick()
        s *= 2
    return t


XS_TOP = SUBLANES


def _mixer_init(sret0_ref, sgdn0_ref, cq0_ref, sret_ref, sgdn_ref, xs_ref):
    tail = CONV_GDN - 1
    sret_ref[...] = sret0_ref[...]
    sgdn_ref[...] = sgdn0_ref[...]
    xs_ref[:, XS_TOP - tail:XS_TOP, :] = cq0_ref[...]


def _mixer_block(getp, put_mix, cos2, sin2, const_refs, sret_ref, sgdn_ref, cq_ref, xs_ref, *,
                 bb, c, nch, n_valid, tick=lambda: None):
    (tril_ref, triu_ref, dint_ref, qdec_ref, kdec_ref, cdec_ref, cw_ref, alog_ref, dtb_ref, nret_ref,
     ngdn_ref) = const_refs
    rows = nch * c
    tail = CONV_GDN - 1
    top = XS_TOP
    ri = lax.broadcasted_iota(jnp.int32, (c, c), 0)
    cj = lax.broadcasted_iota(jnp.int32, (c, c), 1)
    tri = ri >= cj
    strict = ri > cj
    tril_bf = tril_ref[...]
    triu_bf = triu_ref[...]
    scale = D_HEAD ** -0.5
    heads = range(N_HEADS)
    seqs = range(bb)
    chunks = [slice(j * c, (j + 1) * c) for j in range(nch)]
    join = lambda parts: parts[0] if len(parts) == 1 else jnp.concatenate(parts, axis=0)

    ret = {}
    for b in seqs:
        for h in heads:
            lo = h * D_HEAD
            q = getp(b, OFF_RQ + lo, OFF_RQ + lo + D_HEAD)
            k = getp(b, OFF_RK + lo, OFF_RK + lo + D_HEAD)
            v = getp(b, OFF_RV + lo, OFF_RV + lo + D_HEAD)
            qr = q * cos2 + pltpu.roll(q, D_HEAD // 2, 1) * sin2
            kr = (k * cos2 + pltpu.roll(k, D_HEAD // 2, 1) * sin2) * scale
            ret[b, h] = (qr, kr, v)
    tick()
    rtasks = [(b, h, j) for j in range(nch) for b in seqs for h in heads]
    scores = {t: _mm_nt(ret[t[0], t[1]][0][chunks[t[2]]], ret[t[0], t[1]][1][chunks[t[2]]]) * dint_ref[t[1]]
              for t in rtasks}
    tick()
    intra = {t: _mm(scores[t], ret[t[0], t[1]][2][chunks[t[2]]]) for t in rtasks}
    tick()
    kv = {t: _mm_tn(ret[t[0], t[1]][1][chunks[t[2]]] * kdec_ref[t[1]], ret[t[0], t[1]][2][chunks[t[2]]])
          for t in rtasks}
    tick()
    pairs = [(b, h) for b in seqs for h in heads]
    s_ret = {bh: sret_ref[bh[0], bh[1]] for bh in pairs}
    o_ret = {}
    for j in range(nch):
        for b, h in pairs:
            o_ret[b, h, j] = intra[b, h, j] + _mm(ret[b, h][0][chunks[j]] * qdec_ref[h], s_ret[b, h])
        for b, h in pairs:
            s_ret[b, h] = cdec_ref[h] * s_ret[b, h] + kv[b, h, j]
        tick()
    o_ret = {bh: join([o_ret[bh[0], bh[1], j] for j in range(nch)]) for bh in pairs}
    mu = {bh: jnp.mean(o_ret[bh], axis=-1, keepdims=True) for bh in pairs}
    cen = {bh: o_ret[bh] - mu[bh] for bh in pairs}
    var = {bh: jnp.mean(cen[bh] * cen[bh], axis=-1, keepdims=True) for bh in pairs}
    for b, h in pairs:
        lo = h * D_HEAD
        gate = getp(b, OFF_RG + lo, OFF_RG + lo + D_HEAD)
        o = cen[b, h] * lax.rsqrt(var[b, h] + EPS) * nret_ref[h:h + 1, :]
        put_mix(b, lo, o * _silu(gate))
        sret_ref[b, h] = s_ret[b, h]
    tick()

    qkvs, beta_alls, cum_cols, cum_rows, cum_tots = {}, {}, {}, {}, {}
    ones_bf = jnp.ones((D_HEAD, c), BF16)
    for b in seqs:
        xs_ref[b, top:top + rows, :] = getp(b, OFF_QKV, OFF_QKV + GDN_QKV)
        conv = xs_ref[b, top - tail:top - tail + rows, :] * cw_ref[0:1, :]
        for i in range(1, CONV_GDN):
            conv = conv + xs_ref[b, top - tail + i:top - tail + i + rows, :] * cw_ref[i:i + 1, :]
        last = n_valid if nch == 1 else rows
        new_tail = xs_ref[b, top + last - tail:top + last, :]
        xs_ref[b, top - tail:top, :] = new_tail
        cq_ref[b] = new_tail
        qkvs[b] = _silu(conv)
        tick()

        ba = getp(b, OFF_BA, OFF_BA + LANES)
        beta_all = jax.nn.sigmoid(ba)
        z = ba + dtb_ref[...]
        softplus = jnp.maximum(z, 0.0) + jnp.log1p(jnp.exp(-jnp.abs(z)))
        g_all = -jnp.exp(alog_ref[...]) * softplus
        if n_valid < c:
            row = lax.broadcasted_iota(jnp.int32, (rows, LANES), 0)
            rowmask = (row < n_valid).astype(F32)
            beta_all = beta_all * rowmask
            g_all = g_all * rowmask
        beta_alls[b] = beta_all
        for j, rs in enumerate(chunks):
            parts = _split3(g_all[rs])
            cum_cols[b, j] = sum(jnp.dot(tril_bf, g, preferred_element_type=F32) for g in parts)
            cum_tots[b, j] = sum(jnp.dot(ones_bf, g, preferred_element_type=F32) for g in parts)
            cum_rows[b, j] = sum(lax.dot_general(g, triu_bf, (((0,), (0,)), ((), ())),
                                                 preferred_element_type=F32) for g in parts)
    tick()

    tasks = [(j, b, h) for j in range(nch) for b in seqs for h in heads]
    qs, ks, vs, betas, ecums, kdecs, cdecs, amats, qkms = [], [], [], [], [], [], [], [], []
    raw = [(qkvs[b][chunks[j], h * D_HEAD:(h + 1) * D_HEAD],
            qkvs[b][chunks[j], D_GRP + h * D_HEAD:D_GRP + (h + 1) * D_HEAD]) for j, b, h in tasks]
    sumsq = [(jnp.sum(q * q, axis=-1, keepdims=True), jnp.sum(k * k, axis=-1, keepdims=True)) for q, k in raw]
    for i, (j, b, h) in enumerate(tasks):
        rs = chunks[j]
        lo = h * D_HEAD
        v = qkvs[b][rs, 2 * D_GRP + lo:2 * D_GRP + lo + D_HEAD]
        q = raw[i][0] * lax.rsqrt(sumsq[i][0] + EPS) * scale
        k = raw[i][1] * lax.rsqrt(sumsq[i][1] + EPS)
        beta = jnp.broadcast_to(beta_alls[b][rs, h:h + 1], (c, LANES))
        cum = jnp.broadcast_to(cum_cols[b, j][:, N_HEADS + h:N_HEADS + h + 1], (c, LANES))
        cum_row = cum_rows[b, j][N_HEADS + h:N_HEADS + h + 1, :]
        dmask = jnp.exp(jnp.where(tri, cum[:, :c] - cum_row, -jnp.inf))
        cum_last = jnp.broadcast_to(cum_tots[b, j][:, N_HEADS + h:N_HEADS + h + 1], (D_HEAD, LANES))
        kq = _mm_nt(jnp.concatenate([k, q], axis=0), k)
        amats.append(jnp.where(strict, beta[:, :c] * kq[:c] * dmask, 0.0))
        qkms.append(jnp.where(tri, kq[c:] * dmask, 0.0))
        qs.append(q)
        ks.append(k)
        vs.append(v)
        betas.append(beta)
        ecums.append(jnp.exp(cum))
        kdecs.append(jnp.exp(cum_last[:c] - cum))
        cdecs.append(jnp.exp(cum_last))
        if i % N_HEADS == N_HEADS - 1:
            tick()

    tinv = _unit_lower_inverses(amats, c, tick)
    sols = []
    for i in range(len(tasks)):
        rhs = jnp.concatenate([vs[i] * betas[i], ks[i] * (betas[i] * ecums[i])], axis=1)
        sols.append(_mm3(_split2(tinv[i]), _split2(rhs)))
    tick()

    s_gdn = {bh: sgdn_ref[bh[0], bh[1]] for bh in pairs}
    o_gdn = {}
    for j in range(nch):
        idx = {tasks[i][1:]: i for i in range(len(tasks)) if tasks[i][0] == j}
        lhs = {bh: jnp.concatenate([sols[i][:, D_HEAD:], qs[i] * ecums[i]], axis=0) for bh, i in idx.items()}
        both = {bh: _mm(lhs[bh], s_gdn[bh]) for bh in idx}
        tick()
        w = {bh: sols[i][:, :D_HEAD] - both[bh][:c] for bh, i in idx.items()}
        for bh, i in idx.items():
            o_gdn[bh + (j,)] = both[bh][c:] + _mm(qkms[i], w[bh])
        tick()
        upd = {bh: _mm_tn(ks[i] * kdecs[i], w[bh]) for bh, i in idx.items()}
        for bh, i in idx.items():
            s_gdn[bh] = cdecs[i] * s_gdn[bh] + upd[bh]
        tick()
    o_gdn = {bh: join([o_gdn[bh + (j,)] for j in range(nch)]) for bh in pairs}
    msq = {bh: jnp.mean(o_gdn[bh] * o_gdn[bh], axis=-1, keepdims=True) for bh in pairs}
    for b, h in pairs:
        lo = h * D_HEAD
        gate = getp(b, OFF_GG + lo, OFF_GG + lo + D_HEAD)
        o = o_gdn[b, h] * lax.rsqrt(msq[b, h] + EPS) * ngdn_ref[...]
        put_mix(b, D_GRP + lo, o * _silu(gate))
        sgdn_ref[b, h] = s_gdn[b, h]


N_MIXER_CONSTS = 11


def _mixer_kernel(p_ref, sret0_ref, sgdn0_ref, cq0_ref, cos_ref, sin_ref, *rest, bb, c, nch, n_valid):
    const_refs, (mix_ref, sret_ref, sgdn_ref, cq_ref, xs_ref) = rest[:N_MIXER_CONSTS], rest[N_MIXER_CONSTS:]
    rows = nch * c

    @pl.when(pl.program_id(1) == 0)
    def _():
        _mixer_init(sret0_ref, sgdn0_ref, cq0_ref, sret_ref, sgdn_ref, xs_ref)

    def put_mix(b, lo, value):
        mix_ref[b, :, lo:lo + D_HEAD] = value.astype(mix_ref.dtype)

    _mixer_block(lambda b, lo, hi: p_ref[b * rows:(b + 1) * rows, lo:hi], put_mix, cos_ref[...], sin_ref[...],
                 const_refs, sret_ref, sgdn_ref, cq_ref, xs_ref, bb=bb, c=c, nch=nch, n_valid=n_valid)


PROJ_PANEL = 2 * LANES


def _mixer_block_ticks(bb, c, nch):
    levels = (c // SUBLANES).bit_length() - 1
    retention = 4 + nch + 1
    gdn_prep = bb + 1 + bb * nch
    inverse = 3 + 2 * levels
    return retention + gdn_prep + inverse + 1 + 3 * nch


def _proj_mixer_kernel(x0_ref, xa_ref, xb_ref, nw_ref, w_ref, sret0_ref, sgdn0_ref, cq0_ref, cos_ref, sin_ref,
                       *rest, c, nch, steps_per_seq, n_cast):
    const_refs = rest[:N_MIXER_CONSTS]
    cast_in = rest[N_MIXER_CONSTS:N_MIXER_CONSTS + n_cast]
    mix_ref, sret_ref, sgdn_ref, cq_ref = rest[N_MIXER_CONSTS + n_cast:N_MIXER_CONSTS + n_cast + 4]
    cast_out = rest[N_MIXER_CONSTS + n_cast + 4:N_MIXER_CONSTS + 2 * n_cast + 4]
    pja_ref, pjb_ref, xs_ref = rest[N_MIXER_CONSTS + 2 * n_cast + 4:]
    step = pl.program_id(0)
    rows = nch * c

    for src_ref, dst_ref in zip(cast_in, cast_out):
        dst_ref[...] = src_ref[...].astype(dst_ref.dtype)

    def project(x_ref, dst_ref, n_ticks):
        h = _rms(x_ref[...], nw_ref[...]).astype(BF16)
        panels = [(lo, min(lo + PROJ_PANEL, IN_PAD)) for lo in range(0, IN_PAD, PROJ_PANEL)]
        n_panels = len(panels)
        calls = [0]

        def emit():
            lo, hi = panels.pop(0)
            dst_ref[:, lo:hi] = jnp.dot(h, w_ref[:, lo:hi], preferred_element_type=F32)

        def tick():
            calls[0] += 1
            due = min(n_panels, -(-calls[0] * n_panels // n_ticks))
            while n_panels - len(panels) < due:
                emit()

        def flush():
            while panels:
                emit()

        return tick, flush

    @pl.when(step == 0)
    def _():
        project(x0_ref, pja_ref, 1)[1]()

    @pl.when(lax.rem(step, steps_per_seq) == 0)
    def _():
        _mixer_init(sret0_ref, sgdn0_ref, cq0_ref, sret_ref, sgdn_ref, xs_ref)

    for half, (cur_ref, x_next_ref, nxt_ref) in enumerate(((pja_ref, xa_ref, pjb_ref), (pjb_ref, xb_ref, pja_ref))):
        r0 = half * rows
        tick, flush = project(x_next_ref, nxt_ref, _mixer_block_ticks(1, c, nch))

        def put_mix(b, lo, value, r0=r0):
            mix_ref[r0:r0 + rows, lo:lo + D_HEAD] = value.astype(mix_ref.dtype)

        _mixer_block(lambda b, lo, hi, cur_ref=cur_ref: cur_ref[:, lo:hi], put_mix,
                     cos_ref[r0:r0 + rows, :], sin_ref[r0:r0 + rows, :], const_refs,
                     sret_ref, sgdn_ref, cq_ref, xs_ref, bb=1, c=c, nch=nch, n_valid=c, tick=tick)
        flush()


def _retention_decay_tables(c, n_valid):
    f32 = np.float32
    lg = np.log1p(-np.power(f32(2.0), f32(-5.0) - np.arange(N_HEADS, dtype=f32)))[:, None].astype(f32)
    idx = np.arange(c, dtype=f32)
    diff = idx[:, None] - idx[None, :]
    dint = np.where(diff[None] >= 0, np.exp(lg[:, :, None] * np.maximum(diff[None], 0)), f32(0.0)).astype(f32)
    qdec = np.exp(lg * (idx + f32(1.0))).astype(f32)
    kdec = np.where(idx[None, :] < n_valid, np.exp(lg * np.minimum(f32(n_valid) - f32(1.0) - idx, c)), f32(0.0))
    cdec = np.exp(lg * f32(n_valid)).astype(f32)
    bc = lambda t: np.broadcast_to(t.astype(f32)[:, :, None], t.shape + (LANES,))
    return dint, bc(qdec), bc(kdec), np.broadcast_to(cdec[:, :, None], (N_HEADS, 1, LANES))


def _mixer_const_operands(c, n_valid, consts):
    cw, alog, dtb, nret, ngdn = consts
    dint, qdec, kdec, cdec = _retention_decay_tables(c, n_valid)
    idx = np.arange(c)
    tril = (idx[:, None] >= idx[None, :]).astype(np.float32)
    arrays = (jnp.asarray(tril, BF16), jnp.asarray(tril.T, BF16), jnp.asarray(dint), jnp.asarray(qdec),
              jnp.asarray(kdec), jnp.asarray(cdec), cw, alog, dtb, nret, ngdn)
    assert len(arrays) == N_MIXER_CONSTS
    return arrays, [a.shape for a in arrays]


def _mixer(proj, row0, nb, length, sret0, sgdn0, cq0, rope, rope_row0, consts, *, bb, c, nch, n_valid,
           shared_init):
    rows = nch * c
    assert nb % bb == 0 and length % rows == 0 and rope_row0 % rows == 0 and row0 % (bb * length) == 0
    assert not shared_init or bb == 1
    assert n_valid == c or nch == 1
    assert bb == 1 or length == rows
    const_arrays, const_shapes = _mixer_const_operands(c, n_valid, consts)
    nsteps = length // rows
    blk0 = row0 // (bb * rows)
    init_idx = (lambda b, i: (0, 0, 0, 0)) if shared_init else (lambda b, i: (b, 0, 0, 0))
    init_idx3 = (lambda b, i: (0, 0, 0)) if shared_init else (lambda b, i: (b, 0, 0))
    whole = lambda shape: pl.BlockSpec(shape, lambda b, i: (0,) * len(shape))
    rope_spec = pl.BlockSpec((rows, LANES), lambda b, i: (rope_row0 // rows + i, 0))
    state_shape = (bb, N_HEADS, D_HEAD, D_HEAD)
    tail = CONV_GDN - 1
    kern = functools.partial(_mixer_kernel, bb=bb, c=c, nch=nch, n_valid=n_valid)
    return pl.pallas_call(
        kern,
        out_shape=(
            jax.ShapeDtypeStruct((nb, length, D_MODEL), BF16),
            jax.ShapeDtypeStruct((nb, N_HEADS, D_HEAD, D_HEAD), F32),
            jax.ShapeDtypeStruct((nb, N_HEADS, D_HEAD, D_HEAD), F32),
            jax.ShapeDtypeStruct((nb, tail, GDN_QKV), F32),
        ),
        grid=(nb // bb, nsteps),
        in_specs=[
            pl.BlockSpec((bb * rows, IN_PAD), lambda b, i: (blk0 + b * nsteps + i, 0)),
            pl.BlockSpec(state_shape, init_idx),
            pl.BlockSpec(state_shape, init_idx),
            pl.BlockSpec((bb, tail, GDN_QKV), init_idx3),
            rope_spec,
            rope_spec,
        ] + [whole(shape) for shape in const_shapes],
        out_specs=(
            pl.BlockSpec((bb, rows, D_MODEL), lambda b, i: (b, i, 0)),
            pl.BlockSpec(state_shape, lambda b, i: (b, 0, 0, 0)),
            pl.BlockSpec(state_shape, lambda b, i: (b, 0, 0, 0)),
            pl.BlockSpec((bb, tail, GDN_QKV), lambda b, i: (b, 0, 0)),
        ),
        scratch_shapes=[pltpu.VMEM((bb, XS_TOP + rows, GDN_QKV), F32)],
        compiler_params=pltpu.CompilerParams(dimension_semantics=("arbitrary", "arbitrary"),
                                             vmem_limit_bytes=VMEM_LIMIT),
        name="mixer",
    )(proj, sret0, sgdn0, cq0, rope[0], rope[1], *const_arrays)


def _row_slab(nrows, nsteps):
    for hold in (1, 2, 4, 8):
        slabs = nsteps // hold
        if nsteps % hold == 0 and nrows % slabs == 0 and (nrows // slabs) % (2 * SUBLANES) == 0:
            return nrows // slabs, hold
    raise ValueError((nrows, nsteps))


def _proj_mixer(x, norm_w, w_bf, nseq, sret0, sgdn0, cq0, rope, consts, to_bf16, *, c, nch):
    rows = nch * c
    total = x.shape[0]
    length = total // nseq
    assert total % nseq == 0 and length % (2 * rows) == 0
    nblk = total // rows
    steps_per_seq = length // (2 * rows)
    const_arrays, const_shapes = _mixer_const_operands(c, c, consts)
    whole = lambda shape, **kw: pl.BlockSpec(shape, lambda s: (0,) * len(shape), **kw)
    rope_spec = pl.BlockSpec((2 * rows, LANES), lambda s: (lax.rem(s, steps_per_seq), 0))
    state_shape = (1, N_HEADS, D_HEAD, D_HEAD)
    tail = CONV_GDN - 1
    nsteps = nblk // 2
    slabs = [_row_slab(w.shape[0], nsteps) for w in to_bf16]
    cast_specs = [pl.BlockSpec((r, w.shape[1]), lambda s, hold=hold: (s // hold, 0))
                  for w, (r, hold) in zip(to_bf16, slabs)]
    kern = functools.partial(_proj_mixer_kernel, c=c, nch=nch, steps_per_seq=steps_per_seq, n_cast=len(to_bf16))
    return pl.pallas_call(
        kern,
        out_shape=(
            jax.ShapeDtypeStruct((total, D_MODEL), BF16),
            jax.ShapeDtypeStruct((nseq, N_HEADS, D_HEAD, D_HEAD), F32),
            jax.ShapeDtypeStruct((nseq, N_HEADS, D_HEAD, D_HEAD), F32),
            jax.ShapeDtypeStruct((nseq, tail, GDN_QKV), F32),
        ) + tuple(jax.ShapeDtypeStruct(w.shape, BF16) for w in to_bf16),
        grid=(nsteps,),
        in_specs=[
            pl.BlockSpec((rows, D_MODEL), lambda s: (0, 0), pipeline_mode=pl.Buffered(1)),
            pl.BlockSpec((rows, D_MODEL), lambda s: (2 * s + 1, 0)),
            pl.BlockSpec((rows, D_MODEL), lambda s: (jnp.minimum(2 * s + 2, nblk - 1), 0)),
            whole((1, D_MODEL)),
            whole((D_MODEL, IN_PAD), pipeline_mode=pl.Buffered(1)),
            whole(state_shape),
            whole(state_shape),
            whole((1, tail, GDN_QKV)),
            rope_spec,
            rope_spec,
        ] + [whole(shape) for shape in const_shapes] + cast_specs,
        out_specs=(
            pl.BlockSpec((2 * rows, D_MODEL), lambda s: (s, 0)),
            pl.BlockSpec(state_shape, lambda s: (s // steps_per_seq, 0, 0, 0)),
            pl.BlockSpec(state_shape, lambda s: (s // steps_per_seq, 0, 0, 0)),
            pl.BlockSpec((1, tail, GDN_QKV), lambda s: (s // steps_per_seq, 0, 0)),
        ) + tuple(cast_specs),
        scratch_shapes=[pltpu.VMEM((rows, IN_PAD), F32), pltpu.VMEM((rows, IN_PAD), F32),
                        pltpu.VMEM((1, XS_TOP + rows, GDN_QKV), F32)],
        compiler_params=pltpu.CompilerParams(dimension_semantics=("arbitrary",),
                                             vmem_limit_bytes=VMEM_LIMIT),
        name="proj_mixer",
    )(x, x, x, norm_w, w_bf, sret0, sgdn0, cq0, rope[0], rope[1], *const_arrays, *to_bf16)


FFN_COL_CHUNK = D_FF // 11


def _ffn_kernel(x_ref, mix_ref, *rest, tm, stride, prefix):
    if prefix:
        (xm_ref, mixm_ref, wout_ref, nffn_ref, wup_ref, cw_ref, wdn_ref, nfin_ref,
         y_ref, tail_ref, full_ref, lead_ref) = rest
    else:
        tail0_ref, wout_ref, nffn_ref, wup_ref, cw_ref, wdn_ref, nfin_ref, y_ref, tail_ref, full_ref = rest
    t = pl.program_id(1)
    carry = (CONV_FFN - 1) * stride
    base = _round_up(carry, SUBLANES)

    def up_project(x, mix):
        x1 = x + jnp.dot(mix, wout_ref[...], preferred_element_type=F32)
        h = _rms(x1, nffn_ref[...]).astype(BF16)
        return x1, jnp.dot(h, wup_ref[...], preferred_element_type=F32)

    if prefix:
        @pl.when((pl.program_id(0) == 0) & (t == 0))
        def _():
            um = up_project(xm_ref[...], mixm_ref[...])[1]
            lead_ref[...] = um[um.shape[0] - carry:, :]

    @pl.when(t == 0)
    def _():
        full_ref[base - carry:base, :] = lead_ref[...] if prefix else tail0_ref[0]

    x1, up = up_project(x_ref[...], mix_ref[...])
    full_ref[base:base + tm, :] = up

    def conv_cols(lo):
        acc = full_ref[base - carry:base - carry + tm, lo:lo + FFN_COL_CHUNK] * cw_ref[0:1, lo:lo + FFN_COL_CHUNK]
        for i in range(1, CONV_FFN):
            r0 = base - carry + i * stride
            acc = acc + full_ref[r0:r0 + tm, lo:lo + FFN_COL_CHUNK] * cw_ref[i:i + 1, lo:lo + FFN_COL_CHUNK]
        return acc

    x2 = x1
    for j in range(D_FF // FFN_COL_CHUNK):
        lo = j * FFN_COL_CHUNK
        act = (_silu(conv_cols(lo)) * conv_cols(D_FF + lo)).astype(BF16)
        x2 = x2 + jnp.dot(act, wdn_ref[lo:lo + FFN_COL_CHUNK, :], preferred_element_type=F32)
    y_ref[...] = _rms(x2, nfin_ref[...])

    new_tail = full_ref[base + tm - carry:base + tm, :]
    full_ref[base - carry:base, :] = new_tail
    tail_ref[0] = new_tail


def _ffn(x, mix, lead, weights, *, nseq, tm, stride):
    wout, nffn, wup, cw, wdn, nfin = weights
    rows = x.shape[0]
    assert rows % (nseq * tm) == 0
    nt = rows // (nseq * tm)
    carry = (CONV_FFN - 1) * stride
    base = _round_up(carry, SUBLANES)
    assert tm >= carry
    resident = lambda shape: pl.BlockSpec(shape, lambda b, t: (0, 0), pipeline_mode=pl.Buffered(1))
    small = lambda shape: pl.BlockSpec(shape, lambda b, t: (0, 0))
    prefix = isinstance(lead, tuple)
    if prefix:
        assert stride == 1 and all(a.shape[0] >= carry for a in lead)
        lead_specs = [small(a.shape) for a in lead]
        lead_scratch = [pltpu.VMEM((carry, 2 * D_FF), F32)]
    else:
        lead = (lead,)
        lead_specs = [pl.BlockSpec((1, carry, 2 * D_FF), lambda b, t: (b, 0, 0))]
        lead_scratch = []
    kern = functools.partial(_ffn_kernel, tm=tm, stride=stride, prefix=prefix)
    return pl.pallas_call(
        kern,
        out_shape=(
            jax.ShapeDtypeStruct((rows, D_MODEL), F32),
            jax.ShapeDtypeStruct((nseq, carry, 2 * D_FF), F32),
        ),
        grid=(nseq, nt),
        in_specs=[
            pl.BlockSpec((tm, D_MODEL), lambda b, t: (b * nt + t, 0)),
            pl.BlockSpec((tm, D_MODEL), lambda b, t: (b * nt + t, 0)),
        ] + lead_specs + [
            resident((D_MODEL, D_MODEL)),
            small((1, D_MODEL)),
            resident((D_MODEL, 2 * D_FF)),
            small((CONV_FFN, 2 * D_FF)),
            resident((D_FF, D_MODEL)),
            small((1, D_MODEL)),
        ],
        out_specs=(
            pl.BlockSpec((tm, D_MODEL), lambda b, t: (b * nt + t, 0)),
            pl.BlockSpec((1, carry, 2 * D_FF), lambda b, t: (b, 0, 0)),
        ),
        scratch_shapes=[pltpu.VMEM((base + tm, 2 * D_FF), F32)] + lead_scratch,
        compiler_params=pltpu.CompilerParams(dimension_semantics=("arbitrary", "arbitrary"),
                                             vmem_limit_bytes=VMEM_LIMIT),
        name="out_ffn",
    )(x, mix, *lead, wout, nffn, wup, cw, wdn, nfin)


def kernel(x_prompt, x_sample, state_ret, state_gdn, state_conv_qkv, state_ffn_conv, meta_tokens, norm_mix,
           w_in, conv_gdn, gdn_a_log, gdn_dt_bias, norm_ret, norm_gdn, w_out, norm_ffn, w_up, conv_ffn,
           w_down, norm_final):
    depth = w_in.shape[0]
    assert depth == 1
    nbp, seq, _ = x_prompt.shape
    nbs, dec_seq, _ = x_sample.shape
    assert dec_seq <= SAMPLE_PAD and nbs % SAMPLE_GROUP == 0
    assert seq % (2 * PROMPT_CHUNK * PROMPT_CHUNKS_PER_STEP) == 0
    layer = 0

    w_in_bf = _cast_pad_w_in(w_in[layer])
    row = lambda v: v.reshape(1, -1).astype(F32)
    pad_ba = lambda v: jnp.pad(v.astype(F32), (N_HEADS, LANES - 2 * N_HEADS)).reshape(1, LANES)
    mixer_consts = (conv_gdn[layer], pad_ba(gdn_a_log[layer]), pad_ba(gdn_dt_bias[layer]),
                    norm_ret[layer].reshape(N_HEADS, D_HEAD), row(norm_gdn[layer]))
    nmix = row(norm_mix[layer])

    assert seq % N_META == 0 and (seq + N_META) % SAMPLE_PAD == 0
    rope_meta_row0, rope_sample_row0 = seq, seq + N_META
    rope = _rope_tables(seq + N_META + SAMPLE_PAD,
                        [(0, N_META), (rope_meta_row0, 0), (rope_sample_row0, PAST_LEN)])

    xs_pad = jnp.pad(x_sample, ((0, 0), (0, SAMPLE_PAD - dec_seq), (0, 0))).reshape(nbs * SAMPLE_PAD, D_MODEL)
    small_rows = jnp.concatenate([xs_pad, meta_tokens.astype(F32)], axis=0)
    n_small = small_rows.shape[0]
    proj_small = _proj(small_rows, nmix, w_in_bf, tm=n_small // 2)
    meta_row0 = nbs * SAMPLE_PAD

    zero_state = jnp.zeros((1, N_HEADS, D_HEAD, D_HEAD), F32)
    zero_cq = jnp.zeros((1, CONV_GDN - 1, GDN_QKV), F32)
    mix_m, sret_m, sgdn_m, cq_m = _mixer(proj_small, meta_row0, 1, N_META, zero_state, zero_state, zero_cq, rope,
                                         rope_meta_row0, mixer_consts, bb=1, c=N_META, nch=1, n_valid=N_META,
                                         shared_init=True)

    xp = x_prompt.reshape(nbp * seq, D_MODEL)
    mix_p, sret_p, sgdn_p, cq_p, w_out_bf, w_up_bf, w_down_bf = _proj_mixer(
        xp, nmix, w_in_bf, nbp, sret_m, sgdn_m, cq_m, rope, mixer_consts,
        (w_out[layer], w_up[layer], w_down[layer]), c=PROMPT_CHUNK, nch=PROMPT_CHUNKS_PER_STEP)
    ffn_weights = (w_out_bf, row(norm_ffn[layer]), w_up_bf, conv_ffn[layer], w_down_bf, row(norm_final))

    meta_lead = (meta_tokens.astype(F32), mix_m.reshape(N_META, D_MODEL))
    y_p, cf_p = _ffn(xp, mix_p, meta_lead, ffn_weights, nseq=nbp, tm=512, stride=1)
    y_prompt = y_p.reshape(nbp, seq, D_MODEL)

    mix_s, sret_s, sgdn_s, cq_s = _mixer(proj_small, 0, nbs, SAMPLE_PAD, state_ret[layer], state_gdn[layer],
                                         state_conv_qkv[layer], rope, rope_sample_row0, mixer_consts,
                                         bb=SAMPLE_SEQS_PER_STEP, c=SAMPLE_PAD, nch=1, n_valid=dec_seq,
                                         shared_init=False)
    ng = nbs // SAMPLE_GROUP
    to_tmajor = lambda a: a.reshape(ng, SAMPLE_GROUP, a.shape[1], a.shape[2]).transpose(0, 2, 1, 3)
    xs_t = to_tmajor(x_sample).reshape(nbs * dec_seq, D_MODEL)
    mix_t = to_tmajor(mix_s[:, :dec_seq]).reshape(nbs * dec_seq, D_MODEL)
    cf0_t = to_tmajor(state_ffn_conv[layer]).reshape(ng, (CONV_FFN - 1) * SAMPLE_GROUP, 2 * D_FF)
    y_s_t, cf_s_t = _ffn(xs_t, mix_t, cf0_t, ffn_weights, nseq=ng, tm=dec_seq * SAMPLE_GROUP,
                         stride=SAMPLE_GROUP)
    y_sample = y_s_t.reshape(ng, dec_seq, SAMPLE_GROUP, D_MODEL).transpose(0, 2, 1, 3).reshape(nbs, dec_seq, D_MODEL)
    cf_s = cf_s_t.reshape(ng, CONV_FFN - 1, SAMPLE_GROUP, 2 * D_FF).transpose(0, 2, 1, 3).reshape(
        nbs, CONV_FFN - 1, 2 * D_FF)

    return (y_prompt, y_sample, sret_p[None], sgdn_p[None], cq_p[None], cf_p[None],
            sret_s[None], sgdn_s[None], cq_s[None], cf_s[None])
```

```python
import functools

import jax
import numpy as np
import jax.numpy as jnp
from jax import lax
from jax.experimental import pallas as pl
from jax.experimental.pallas import tpu as pltpu

F32 = jnp.float32
BF16 = jnp.bfloat16

D_MODEL = 1024
N_META = 16
PAST_LEN = 16384
N_HEADS = 4
D_HEAD = 128
D_GRP = N_HEADS * D_HEAD
GDN_QKV = 3 * D_GRP
CONV_GDN = 4
CONV_FFN = 3
D_FF = 2816
ROPE_THETA = 10000.0
EPS = 1e-6

OFF_RQ, OFF_RK, OFF_RV, OFF_RG = 0, D_GRP, 2 * D_GRP, 3 * D_GRP
OFF_QKV = 4 * D_GRP
OFF_GG = OFF_QKV + GDN_QKV
OFF_BA = OFF_GG + D_GRP
IN_WIDTH = OFF_BA + 2 * N_HEADS
LANES = 128
SUBLANES = 8
IN_PAD = OFF_BA + LANES

PROMPT_CHUNK = 128
PROMPT_CHUNKS_PER_STEP = 2
SAMPLE_PAD = 8
SAMPLE_SEQS_PER_STEP = 16
SAMPLE_GROUP = 32
VMEM_LIMIT = 56 * 1024 * 1024


def _round_up(n, m):
    return (n + m - 1) // m * m


def _mm(a, b):
    return jnp.dot(a.astype(BF16), b.astype(BF16), preferred_element_type=F32)


def _mm_nt(a, b):
    return lax.dot_general(a.astype(BF16), b.astype(BF16), (((1,), (1,)), ((), ())),
                           preferred_element_type=F32)


def _mm_tn(a, b):
    return lax.dot_general(a.astype(BF16), b.astype(BF16), (((0,), (0,)), ((), ())),
                           preferred_element_type=F32)


def _split2(x):
    hi = x.astype(BF16)
    lo = (x - hi.astype(F32)).astype(BF16)
    return hi, lo


def _split3(x):
    p0 = x.astype(BF16)
    r = x - p0.astype(F32)
    p1 = r.astype(BF16)
    p2 = (r - p1.astype(F32)).astype(BF16)
    return p0, p1, p2


def _mm3(a2, b2):
    (ah, al), (bh, bl) = a2, b2
    dot = lambda x, y: jnp.dot(x, y, preferred_element_type=F32)
    return dot(ah, bh) + (dot(ah, bl) + dot(al, bh))


def _silu(x):
    return x * jax.nn.sigmoid(x)


def _rms(x, w):
    return x * lax.rsqrt(jnp.mean(x * x, axis=-1, keepdims=True) + EPS) * w


def _rope_kernel(invf_ref, cos_ref, sin_ref, *, segments):
    shape = cos_ref.shape
    r = lax.broadcasted_iota(jnp.int32, shape, 0)
    pos = r + (segments[0][1] - segments[0][0])
    for row0, pos0 in segments[1:]:
        pos = jnp.where(r >= row0, r + (pos0 - row0), pos)
    ang = pos.astype(F32) * invf_ref[...]
    lane = lax.broadcasted_iota(jnp.int32, shape, 1)
    sin = jnp.sin(ang)
    cos_ref[...] = jnp.cos(ang)
    sin_ref[...] = jnp.where(lane < D_HEAD // 2, -sin, sin)


def _rope_tables(n, segments):
    half = D_HEAD // 2
    inv_freq = ROPE_THETA ** (-jnp.arange(half, dtype=F32) / half)
    invf2 = jnp.concatenate([inv_freq, inv_freq]).reshape(1, LANES)
    out = jax.ShapeDtypeStruct((n, LANES), F32)
    return pl.pallas_call(functools.partial(_rope_kernel, segments=tuple(segments)), out_shape=(out, out),
                          name="rope_tables")(invf2)


CAST_COLS = 3 * LANES


def _cast_pad_kernel(wt_ref, o_ref):
    first = pl.program_id(0) * CAST_COLS
    r = lax.broadcasted_iota(jnp.int32, wt_ref.shape, 0)
    blk = jnp.where(first + r < IN_WIDTH, wt_ref[...], 0.0)
    o_ref[...] = blk.T.astype(BF16)


def _cast_pad_w_in(w):
    wt = w.T
    return pl.pallas_call(
        _cast_pad_kernel,
        out_shape=jax.ShapeDtypeStruct((D_MODEL, IN_PAD), BF16),
        grid=(IN_PAD // CAST_COLS,),
        in_specs=[pl.BlockSpec((CAST_COLS, D_MODEL), lambda i: (i, 0))],
        out_specs=pl.BlockSpec((D_MODEL, CAST_COLS), lambda i: (0, i)),
        name="cast_w_in",
    )(wt)


def _proj_kernel(x_ref, nw_ref, w_ref, o_ref):
    h = _rms(x_ref[...], nw_ref[...])
    o_ref[...] = jnp.dot(h.astype(BF16), w_ref[...], preferred_element_type=F32)


def _proj(x, norm_w, w_bf, tm):
    rows = x.shape[0]
    assert rows % tm == 0
    return pl.pallas_call(
        _proj_kernel,
        out_shape=jax.ShapeDtypeStruct((rows, IN_PAD), F32),
        grid=(rows // tm,),
        in_specs=[
            pl.BlockSpec((tm, D_MODEL), lambda i: (i, 0)),
            pl.BlockSpec((1, D_MODEL), lambda i: (0, 0)),
            pl.BlockSpec((D_MODEL, IN_PAD), lambda i: (0, 0), pipeline_mode=pl.Buffered(1)),
        ],
        out_specs=pl.BlockSpec((tm, IN_PAD), lambda i: (i, 0)),
        compiler_params=pltpu.CompilerParams(dimension_semantics=("arbitrary",),
                                             vmem_limit_bytes=VMEM_LIMIT),
        name="in_proj",
    )(x, norm_w, w_bf)


def _unit_lower_inverses(mats, c, tick=lambda: None):
    ri = lax.broadcasted_iota(jnp.int32, (c, c), 0)
    ci = lax.broadcasted_iota(jnp.int32, (c, c), 1)
    eye = (ri == ci).astype(F32)
    diag_blk = (ri // SUBLANES) == (ci // SUBLANES)
    ad = [jnp.where(diag_blk, a, 0.0) for a in mats]
    a2 = [_mm(x, x) for x in ad]
    tick()
    a4 = [_mm(x, x) for x in a2]
    t = [eye - x for x in ad]
    t = [x + _mm(x, s) for x, s in zip(t, a2)]
    tick()
    t = [x + _mm(x, s) for x, s in zip(t, a4)]
    tick()
    s = SUBLANES
    while s < c:
        level = ((ri // (2 * s)) == (ci // (2 * s))) & ((ri // s) != (ci // s))
        off = [jnp.where(level, a, 0.0) for a in mats]
        lt = [_mm(o, x) for o, x in zip(off, t)]
        tick()
        t = [x - _mm(x, y) for x, y in zip(t, lt)]
        tick()
        s *= 2
    return t


XS_TOP = SUBLANES


def _mixer_init(sret0_ref, sgdn0_ref, cq0_ref, sret_ref, sgdn_ref, xs_ref):
    tail = CONV_GDN - 1
    sret_ref[...] = sret0_ref[...]
    sgdn_ref[...] = sgdn0_ref[...]
    xs_ref[:, XS_TOP - tail:XS_TOP, :] = cq0_ref[...]


def _mixer_block(getp, put_mix, cos2, sin2, const_refs, sret_ref, sgdn_ref, cq_ref, xs_ref, *,
                 bb, c, nch, n_valid, tick=lambda: None):
    (tril_ref, triu_ref, dint_ref, qdec_ref, kdec_ref, cdec_ref, cw_ref, alog_ref, dtb_ref, nret_ref,
     ngdn_ref) = const_refs
    rows = nch * c
    tail = CONV_GDN - 1
    top = XS_TOP
    ri = lax.broadcasted_iota(jnp.int32, (c, c), 0)
    cj = lax.broadcasted_iota(jnp.int32, (c, c), 1)
    tri = ri >= cj
    strict = ri > cj
    tril_bf = tril_ref[...]
    triu_bf = triu_ref[...]
    scale = D_HEAD ** -0.5
    heads = range(N_HEADS)
    seqs = range(bb)
    chunks = [slice(j * c, (j + 1) * c) for j in range(nch)]
    join = lambda parts: parts[0] if len(parts) == 1 else jnp.concatenate(parts, axis=0)

    ret = {}
    for b in seqs:
        for h in heads:
            lo = h * D_HEAD
            q = getp(b, OFF_RQ + lo, OFF_RQ + lo + D_HEAD)
            k = getp(b, OFF_RK + lo, OFF_RK + lo + D_HEAD)
            v = getp(b, OFF_RV + lo, OFF_RV + lo + D_HEAD)
            qr = q * cos2 + pltpu.roll(q, D_HEAD // 2, 1) * sin2
            kr = (k * cos2 + pltpu.roll(k, D_HEAD // 2, 1) * sin2) * scale
            ret[b, h] = (qr, kr, v)
    tick()
    rtasks = [(b, h, j) for j in range(nch) for b in seqs for h in heads]
    scores = {t: _mm_nt(ret[t[0], t[1]][0][chunks[t[2]]], ret[t[0], t[1]][1][chunks[t[2]]]) * dint_ref[t[1]]
              for t in rtasks}
    tick()
    intra = {t: _mm(scores[t], ret[t[0], t[1]][2][chunks[t[2]]]) for t in rtasks}
    tick()
    kv = {t: _mm_tn(ret[t[0], t[1]][1][chunks[t[2]]] * kdec_ref[t[1]], ret[t[0], t[1]][2][chunks[t[2]]])
          for t in rtasks}
    tick()
    pairs = [(b, h) for b in seqs for h in heads]
    s_ret = {bh: sret_ref[bh[0], bh[1]] for bh in pairs}
    o_ret = {}
    for j in range(nch):
        for b, h in pairs:
            o_ret[b, h, j] = intra[b, h, j] + _mm(ret[b, h][0][chunks[j]] * qdec_ref[h], s_ret[b, h])
        for b, h in pairs:
            s_ret[b, h] = cdec_ref[h] * s_ret[b, h] + kv[b, h, j]
        tick()
    o_ret = {bh: join([o_ret[bh[0], bh[1], j] for j in range(nch)]) for bh in pairs}
    mu = {bh: jnp.mean(o_ret[bh], axis=-1, keepdims=True) for bh in pairs}
    cen = {bh: o_ret[bh] - mu[bh] for bh in pairs}
    var = {bh: jnp.mean(cen[bh] * cen[bh], axis=-1, keepdims=True) for bh in pairs}
    for b, h in pairs:
        lo = h * D_HEAD
        gate = getp(b, OFF_RG + lo, OFF_RG + lo + D_HEAD)
        o = cen[b, h] * lax.rsqrt(var[b, h] + EPS) * nret_ref[h:h + 1, :]
        put_mix(b, lo, o * _silu(gate))
        sret_ref[b, h] = s_ret[b, h]
    tick()

    qkvs, beta_alls, cum_cols, cum_rows, cum_tots = {}, {}, {}, {}, {}
    ones_bf = jnp.ones((D_HEAD, c), BF16)
    for b in seqs:
        xs_ref[b, top:top + rows, :] = getp(b, OFF_QKV, OFF_QKV + GDN_QKV)
        conv = xs_ref[b, top - tail:top - tail + rows, :] * cw_ref[0:1, :]
        for i in range(1, CONV_GDN):
            conv = conv + xs_ref[b, top - tail + i:top - tail + i + rows, :] * cw_ref[i:i + 1, :]
        last = n_valid if nch == 1 else rows
        new_tail = xs_ref[b, top + last - tail:top + last, :]
        xs_ref[b, top - tail:top, :] = new_tail
        cq_ref[b] = new_tail
        qkvs[b] = _silu(conv)
        tick()

        ba = getp(b, OFF_BA, OFF_BA + LANES)
        beta_all = jax.nn.sigmoid(ba)
        z = ba + dtb_ref[...]
        softplus = jnp.maximum(z, 0.0) + jnp.log1p(jnp.exp(-jnp.abs(z)))
        g_all = -jnp.exp(alog_ref[...]) * softplus
        if n_valid < c:
            row = lax.broadcasted_iota(jnp.int32, (rows, LANES), 0)
            rowmask = (row < n_valid).astype(F32)
            beta_all = beta_all * rowmask
            g_all = g_all * rowmask
        beta_alls[b] = beta_all
        for j, rs in enumerate(chunks):
            parts = _split3(g_all[rs])
            cum_cols[b, j] = sum(jnp.dot(tril_bf, g, preferred_element_type=F32) for g in parts)
            cum_tots[b, j] = sum(jnp.dot(ones_bf, g, preferred_element_type=F32) for g in parts)
            cum_rows[b, j] = sum(lax.dot_general(g, triu_bf, (((0,), (0,)), ((), ())),
                                                 preferred_element_type=F32) for g in parts)
    tick()

    tasks = [(j, b, h) for j in range(nch) for b in seqs for h in heads]
    qs, ks, vs, betas, ecums, kdecs, cdecs, amats, qkms = [], [], [], [], [], [], [], [], []
    raw = [(qkvs[b][chunks[j], h * D_HEAD:(h + 1) * D_HEAD],
            qkvs[b][chunks[j], D_GRP + h * D_HEAD:D_GRP + (h + 1) * D_HEAD]) for j, b, h in tasks]
    sumsq = [(jnp.sum(q * q, axis=-1, keepdims=True), jnp.sum(k * k, axis=-1, keepdims=True)) for q, k in raw]
    for i, (j, b, h) in enumerate(tasks):
        rs = chunks[j]
        lo = h * D_HEAD
        v = qkvs[b][rs, 2 * D_GRP + lo:2 * D_GRP + lo + D_HEAD]
        q = raw[i][0] * lax.rsqrt(sumsq[i][0] + EPS) * scale
        k = raw[i][1] * lax.rsqrt(sumsq[i][1] + EPS)
        beta = jnp.broadcast_to(beta_alls[b][rs, h:h + 1], (c, LANES))
        cum = jnp.broadcast_to(cum_cols[b, j][:, N_HEADS + h:N_HEADS + h + 1], (c, LANES))
        cum_row = cum_rows[b, j][N_HEADS + h:N_HEADS + h + 1, :]
        dmask = jnp.exp(jnp.where(tri, cum[:, :c] - cum_row, -jnp.inf))
        cum_last = jnp.broadcast_to(cum_tots[b, j][:, N_HEADS + h:N_HEADS + h + 1], (D_HEAD, LANES))
        kq = _mm_nt(jnp.concatenate([k, q], axis=0), k)
        amats.append(jnp.where(strict, beta[:, :c] * kq[:c] * dmask, 0.0))
        qkms.append(jnp.where(tri, kq[c:] * dmask, 0.0))
        qs.append(q)
        ks.append(k)
        vs.append(v)
        betas.append(beta)
        ecums.append(jnp.exp(cum))
        kdecs.append(jnp.exp(cum_last[:c] - cum))
        cdecs.append(jnp.exp(cum_last))
        if i % N_HEADS == N_HEADS - 1:
            tick()

    tinv = _unit_lower_inverses(amats, c, tick)
    sols = []
    for i in range(len(tasks)):
        rhs = jnp.concatenate([vs[i] * betas[i], ks[i] * (betas[i] * ecums[i])], axis=1)
        sols.append(_mm3(_split2(tinv[i]), _split2(rhs)))
    tick()

    s_gdn = {bh: sgdn_ref[bh[0], bh[1]] for bh in pairs}
    o_gdn = {}
    for j in range(nch):
        idx = {tasks[i][1:]: i for i in range(len(tasks)) if tasks[i][0] == j}
        lhs = {bh: jnp.concatenate([sols[i][:, D_HEAD:], qs[i] * ecums[i]], axis=0) for bh, i in idx.items()}
        both = {bh: _mm(lhs[bh], s_gdn[bh]) for bh in idx}
        tick()
        w = {bh: sols[i][:, :D_HEAD] - both[bh][:c] for bh, i in idx.items()}
        for bh, i in idx.items():
            o_gdn[bh + (j,)] = both[bh][c:] + _mm(qkms[i], w[bh])
        tick()
        upd = {bh: _mm_tn(ks[i] * kdecs[i], w[bh]) for bh, i in idx.items()}
        for bh, i in idx.items():
            s_gdn[bh] = cdecs[i] * s_gdn[bh] + upd[bh]
        tick()
    o_gdn = {bh: join([o_gdn[bh + (j,)] for j in range(nch)]) for bh in pairs}
    msq = {bh: jnp.mean(o_gdn[bh] * o_gdn[bh], axis=-1, keepdims=True) for bh in pairs}
    for b, h in pairs:
        lo = h * D_HEAD
        gate = getp(b, OFF_GG + lo, OFF_GG + lo + D_HEAD)
        o = o_gdn[b, h] * lax.rsqrt(msq[b, h] + EPS) * ngdn_ref[...]
        put_mix(b, D_GRP + lo, o * _silu(gate))
        sgdn_ref[b, h] = s_gdn[b, h]


N_MIXER_CONSTS = 11


def _mixer_kernel(p_ref, sret0_ref, sgdn0_ref, cq0_ref, cos_ref, sin_ref, *rest, bb, c, nch, n_valid, stored):
    const_refs, (mix_ref, sret_ref, sgdn_ref, cq_ref, xs_ref) = rest[:N_MIXER_CONSTS], rest[N_MIXER_CONSTS:]
    rows = nch * c

    @pl.when(pl.program_id(1) == 0)
    def _():
        _mixer_init(sret0_ref, sgdn0_ref, cq0_ref, sret_ref, sgdn_ref, xs_ref)

    def put_mix(b, lo, value):
        mix_ref[b, :, lo:lo + D_HEAD] = value.astype(mix_ref.dtype)

    def getp(b, lo, hi):
        blk = p_ref[b * stored:(b + 1) * stored, lo:hi]
        if stored < rows:
            blk = jnp.concatenate([blk, jnp.zeros((rows - stored, hi - lo), F32)], axis=0)
        return blk

    _mixer_block(getp, put_mix, cos_ref[...], sin_ref[...], const_refs, sret_ref, sgdn_ref, cq_ref, xs_ref,
                 bb=bb, c=c, nch=nch, n_valid=n_valid)


PROJ_PANEL = 2 * LANES


def _mixer_block_ticks(bb, c, nch):
    levels = (c // SUBLANES).bit_length() - 1
    retention = 4 + nch + 1
    gdn_prep = bb + 1 + bb * nch
    inverse = 3 + 2 * levels
    return retention + gdn_prep + inverse + 1 + 3 * nch


def _proj_mixer_kernel(x0_ref, xa_ref, xb_ref, nw_ref, w_ref, sret0_ref, sgdn0_ref, cq0_ref, cos_ref, sin_ref,
                       *rest, c, nch, steps_per_seq, n_cast):
    const_refs = rest[:N_MIXER_CONSTS]
    cast_in = rest[N_MIXER_CONSTS:N_MIXER_CONSTS + n_cast]
    mix_ref, sret_ref, sgdn_ref, cq_ref = rest[N_MIXER_CONSTS + n_cast:N_MIXER_CONSTS + n_cast + 4]
    cast_out = rest[N_MIXER_CONSTS + n_cast + 4:N_MIXER_CONSTS + 2 * n_cast + 4]
    pja_ref, pjb_ref, xs_ref = rest[N_MIXER_CONSTS + 2 * n_cast + 4:]
    step = pl.program_id(0)
    rows = nch * c

    for src_ref, dst_ref in zip(cast_in, cast_out):
        dst_ref[...] = src_ref[...].astype(dst_ref.dtype)

    def project(x_ref, dst_ref, n_ticks):
        h = _rms(x_ref[...], nw_ref[...]).astype(BF16)
        panels = [(lo, min(lo + PROJ_PANEL, IN_PAD)) for lo in range(0, IN_PAD, PROJ_PANEL)]
        n_panels = len(panels)
        calls = [0]

        def emit():
            lo, hi = panels.pop(0)
            dst_ref[:, lo:hi] = jnp.dot(h, w_ref[:, lo:hi], preferred_element_type=F32)

        def tick():
            calls[0] += 1
            due = min(n_panels, -(-calls[0] * n_panels // n_ticks))
            while n_panels - len(panels) < due:
                emit()

        def flush():
            while panels:
                emit()

        return tick, flush

    @pl.when(step == 0)
    def _():
        project(x0_ref, pja_ref, 1)[1]()

    @pl.when(lax.rem(step, steps_per_seq) == 0)
    def _():
        _mixer_init(sret0_ref, sgdn0_ref, cq0_ref, sret_ref, sgdn_ref, xs_ref)

    for half, (cur_ref, x_next_ref, nxt_ref) in enumerate(((pja_ref, xa_ref, pjb_ref), (pjb_ref, xb_ref, pja_ref))):
        r0 = half * rows
        tick, flush = project(x_next_ref, nxt_ref, _mixer_block_ticks(1, c, nch))

        def put_mix(b, lo, value, r0=r0):
            mix_ref[r0:r0 + rows, lo:lo + D_HEAD] = value.astype(mix_ref.dtype)

        _mixer_block(lambda b, lo, hi, cur_ref=cur_ref: cur_ref[:, lo:hi], put_mix,
                     cos_ref[r0:r0 + rows, :], sin_ref[r0:r0 + rows, :], const_refs,
                     sret_ref, sgdn_ref, cq_ref, xs_ref, bb=1, c=c, nch=nch, n_valid=c, tick=tick)
        flush()


def _retention_decay_tables(c, n_valid):
    f32 = np.float32
    lg = np.log1p(-np.power(f32(2.0), f32(-5.0) - np.arange(N_HEADS, dtype=f32)))[:, None].astype(f32)
    idx = np.arange(c, dtype=f32)
    diff = idx[:, None] - idx[None, :]
    dint = np.where(diff[None] >= 0, np.exp(lg[:, :, None] * np.maximum(diff[None], 0)), f32(0.0)).astype(f32)
    qdec = np.exp(lg * (idx + f32(1.0))).astype(f32)
    kdec = np.where(idx[None, :] < n_valid, np.exp(lg * np.minimum(f32(n_valid) - f32(1.0) - idx, c)), f32(0.0))
    cdec = np.exp(lg * f32(n_valid)).astype(f32)
    bc = lambda t: np.broadcast_to(t.astype(f32)[:, :, None], t.shape + (LANES,))
    return dint, bc(qdec), bc(kdec), np.broadcast_to(cdec[:, :, None], (N_HEADS, 1, LANES))


def _mixer_const_operands(c, n_valid, consts):
    cw, alog, dtb, nret, ngdn = consts
    dint, qdec, kdec, cdec = _retention_decay_tables(c, n_valid)
    idx = np.arange(c)
    tril = (idx[:, None] >= idx[None, :]).astype(np.float32)
    arrays = (jnp.asarray(tril, BF16), jnp.asarray(tril.T, BF16), jnp.asarray(dint), jnp.asarray(qdec),
              jnp.asarray(kdec), jnp.asarray(cdec), cw, alog, dtb, nret, ngdn)
    assert len(arrays) == N_MIXER_CONSTS
    return arrays, [a.shape for a in arrays]


def _mixer(proj, row0, nb, length, sret0, sgdn0, cq0, rope, rope_row0, consts, *, bb, c, nch, n_valid,
           shared_init, stored=None):
    rows = nch * c
    assert nb % bb == 0 and length % rows == 0 and rope_row0 % rows == 0
    assert not shared_init or bb == 1
    assert n_valid == c or nch == 1
    assert bb == 1 or length == rows
    const_arrays, const_shapes = _mixer_const_operands(c, n_valid, consts)
    nsteps = length // rows
    stored = rows if stored is None else stored
    assert stored == rows or (nsteps == 1 and stored >= n_valid and (bb * stored) % SUBLANES == 0)
    assert row0 % (bb * stored) == 0
    blk0 = row0 // (bb * stored)
    init_idx = (lambda b, i: (0, 0, 0, 0)) if shared_init else (lambda b, i: (b, 0, 0, 0))
    init_idx3 = (lambda b, i: (0, 0, 0)) if shared_init else (lambda b, i: (b, 0, 0))
    whole = lambda shape: pl.BlockSpec(shape, lambda b, i: (0,) * len(shape))
    rope_spec = pl.BlockSpec((rows, LANES), lambda b, i: (rope_row0 // rows + i, 0))
    state_shape = (bb, N_HEADS, D_HEAD, D_HEAD)
    tail = CONV_GDN - 1
    kern = functools.partial(_mixer_kernel, bb=bb, c=c, nch=nch, n_valid=n_valid, stored=stored)
    return pl.pallas_call(
        kern,
        out_shape=(
            jax.ShapeDtypeStruct((nb, length, D_MODEL), BF16),
            jax.ShapeDtypeStruct((nb, N_HEADS, D_HEAD, D_HEAD), F32),
            jax.ShapeDtypeStruct((nb, N_HEADS, D_HEAD, D_HEAD), F32),
            jax.ShapeDtypeStruct((nb, tail, GDN_QKV), F32),
        ),
        grid=(nb // bb, nsteps),
        in_specs=[
            pl.BlockSpec((bb * stored, IN_PAD), lambda b, i: (blk0 + b * nsteps + i, 0)),
            pl.BlockSpec(state_shape, init_idx),
            pl.BlockSpec(state_shape, init_idx),
            pl.BlockSpec((bb, tail, GDN_QKV), init_idx3),
            rope_spec,
            rope_spec,
        ] + [whole(shape) for shape in const_shapes],
        out_specs=(
            pl.BlockSpec((bb, rows, D_MODEL), lambda b, i: (b, i, 0)),
            pl.BlockSpec(state_shape, lambda b, i: (b, 0, 0, 0)),
            pl.BlockSpec(state_shape, lambda b, i: (b, 0, 0, 0)),
            pl.BlockSpec((bb, tail, GDN_QKV), lambda b, i: (b, 0, 0)),
        ),
        scratch_shapes=[pltpu.VMEM((bb, XS_TOP + rows, GDN_QKV), F32)],
        compiler_params=pltpu.CompilerParams(dimension_semantics=("arbitrary", "arbitrary"),
                                             vmem_limit_bytes=VMEM_LIMIT),
        name="mixer",
    )(proj, sret0, sgdn0, cq0, rope[0], rope[1], *const_arrays)


def _row_slab(nrows, nsteps):
    for hold in (1, 2, 4, 8):
        slabs = nsteps // hold
        if nsteps % hold == 0 and nrows % slabs == 0 and (nrows // slabs) % (2 * SUBLANES) == 0:
            return nrows // slabs, hold
    raise ValueError((nrows, nsteps))


def _proj_mixer(x, norm_w, w_bf, nseq, sret0, sgdn0, cq0, rope, consts, to_bf16, *, c, nch):
    rows = nch * c
    total = x.shape[0]
    length = total // nseq
    assert total % nseq == 0 and length % (2 * rows) == 0
    nblk = total // rows
    steps_per_seq = length // (2 * rows)
    const_arrays, const_shapes = _mixer_const_operands(c, c, consts)
    whole = lambda shape, **kw: pl.BlockSpec(shape, lambda s: (0,) * len(shape), **kw)
    rope_spec = pl.BlockSpec((2 * rows, LANES), lambda s: (lax.rem(s, steps_per_seq), 0))
    state_shape = (1, N_HEADS, D_HEAD, D_HEAD)
    tail = CONV_GDN - 1
    nsteps = nblk // 2
    slabs = [_row_slab(w.shape[0], nsteps) for w in to_bf16]
    cast_specs = [pl.BlockSpec((r, w.shape[1]), lambda s, hold=hold: (s // hold, 0))
                  for w, (r, hold) in zip(to_bf16, slabs)]
    kern = functools.partial(_proj_mixer_kernel, c=c, nch=nch, steps_per_seq=steps_per_seq, n_cast=len(to_bf16))
    return pl.pallas_call(
        kern,
        out_shape=(
            jax.ShapeDtypeStruct((total, D_MODEL), BF16),
            jax.ShapeDtypeStruct((nseq, N_HEADS, D_HEAD, D_HEAD), F32),
            jax.ShapeDtypeStruct((nseq, N_HEADS, D_HEAD, D_HEAD), F32),
            jax.ShapeDtypeStruct((nseq, tail, GDN_QKV), F32),
        ) + tuple(jax.ShapeDtypeStruct(w.shape, BF16) for w in to_bf16),
        grid=(nsteps,),
        in_specs=[
            pl.BlockSpec((rows, D_MODEL), lambda s: (0, 0), pipeline_mode=pl.Buffered(1)),
            pl.BlockSpec((rows, D_MODEL), lambda s: (2 * s + 1, 0)),
            pl.BlockSpec((rows, D_MODEL), lambda s: (jnp.minimum(2 * s + 2, nblk - 1), 0)),
            whole((1, D_MODEL)),
            whole((D_MODEL, IN_PAD), pipeline_mode=pl.Buffered(1)),
            whole(state_shape),
            whole(state_shape),
            whole((1, tail, GDN_QKV)),
            rope_spec,
            rope_spec,
        ] + [whole(shape) for shape in const_shapes] + cast_specs,
        out_specs=(
            pl.BlockSpec((2 * rows, D_MODEL), lambda s: (s, 0)),
            pl.BlockSpec(state_shape, lambda s: (s // steps_per_seq, 0, 0, 0)),
            pl.BlockSpec(state_shape, lambda s: (s // steps_per_seq, 0, 0, 0)),
            pl.BlockSpec((1, tail, GDN_QKV), lambda s: (s // steps_per_seq, 0, 0)),
        ) + tuple(cast_specs),
        scratch_shapes=[pltpu.VMEM((rows, IN_PAD), F32), pltpu.VMEM((rows, IN_PAD), F32),
                        pltpu.VMEM((1, XS_TOP + rows, GDN_QKV), F32)],
        compiler_params=pltpu.CompilerParams(dimension_semantics=("arbitrary",),
                                             vmem_limit_bytes=VMEM_LIMIT),
        name="proj_mixer",
    )(x, x, x, norm_w, w_bf, sret0, sgdn0, cq0, rope[0], rope[1], *const_arrays, *to_bf16)


FFN_COL_CHUNK = D_FF // 11


def _ffn_kernel(x_ref, mix_ref, *rest, tm, stride, prefix):
    if prefix:
        (xm_ref, mixm_ref, wout_ref, nffn_ref, wup_ref, cw_ref, wdn_ref, nfin_ref,
         y_ref, tail_ref, full_ref, lead_ref) = rest
    else:
        tail0_ref, wout_ref, nffn_ref, wup_ref, cw_ref, wdn_ref, nfin_ref, y_ref, tail_ref, full_ref = rest
    t = pl.program_id(1)
    carry = (CONV_FFN - 1) * stride
    base = _round_up(carry, SUBLANES)

    def up_project(x, mix):
        x1 = x + jnp.dot(mix, wout_ref[...], preferred_element_type=F32)
        h = _rms(x1, nffn_ref[...]).astype(BF16)
        return x1, jnp.dot(h, wup_ref[...], preferred_element_type=F32)

    if prefix:
        @pl.when((pl.program_id(0) == 0) & (t == 0))
        def _():
            um = up_project(xm_ref[...], mixm_ref[...])[1]
            lead_ref[...] = um[um.shape[0] - carry:, :]

    @pl.when(t == 0)
    def _():
        full_ref[base - carry:base, :] = lead_ref[...] if prefix else tail0_ref[0]

    x1, up = up_project(x_ref[...], mix_ref[...])
    full_ref[base:base + tm, :] = up

    def conv_cols(lo):
        acc = full_ref[base - carry:base - carry + tm, lo:lo + FFN_COL_CHUNK] * cw_ref[0:1, lo:lo + FFN_COL_CHUNK]
        for i in range(1, CONV_FFN):
            r0 = base - carry + i * stride
            acc = acc + full_ref[r0:r0 + tm, lo:lo + FFN_COL_CHUNK] * cw_ref[i:i + 1, lo:lo + FFN_COL_CHUNK]
        return acc

    x2 = x1
    for j in range(D_FF // FFN_COL_CHUNK):
        lo = j * FFN_COL_CHUNK
        act = (_silu(conv_cols(lo)) * conv_cols(D_FF + lo)).astype(BF16)
        x2 = x2 + jnp.dot(act, wdn_ref[lo:lo + FFN_COL_CHUNK, :], preferred_element_type=F32)
    y_ref[...] = _rms(x2, nfin_ref[...])

    new_tail = full_ref[base + tm - carry:base + tm, :]
    full_ref[base - carry:base, :] = new_tail
    tail_ref[0] = new_tail


def _ffn(x, mix, lead, weights, *, nseq, tm, stride):
    wout, nffn, wup, cw, wdn, nfin = weights
    rows = x.shape[0]
    assert rows % (nseq * tm) == 0
    nt = rows // (nseq * tm)
    carry = (CONV_FFN - 1) * stride
    base = _round_up(carry, SUBLANES)
    assert tm >= carry
    resident = lambda shape: pl.BlockSpec(shape, lambda b, t: (0, 0), pipeline_mode=pl.Buffered(1))
    small = lambda shape: pl.BlockSpec(shape, lambda b, t: (0, 0))
    prefix = isinstance(lead, tuple)
    if prefix:
        assert stride == 1 and all(a.shape[0] >= carry for a in lead)
        lead_specs = [small(a.shape) for a in lead]
        lead_scratch = [pltpu.VMEM((carry, 2 * D_FF), F32)]
    else:
        lead = (lead,)
        lead_specs = [pl.BlockSpec((1, carry, 2 * D_FF), lambda b, t: (b, 0, 0))]
        lead_scratch = []
    kern = functools.partial(_ffn_kernel, tm=tm, stride=stride, prefix=prefix)
    return pl.pallas_call(
        kern,
        out_shape=(
            jax.ShapeDtypeStruct((rows, D_MODEL), F32),
            jax.ShapeDtypeStruct((nseq, carry, 2 * D_FF), F32),
        ),
        grid=(nseq, nt),
        in_specs=[
            pl.BlockSpec((tm, D_MODEL), lambda b, t: (b * nt + t, 0)),
            pl.BlockSpec((tm, D_MODEL), lambda b, t: (b * nt + t, 0)),
        ] + lead_specs + [
            resident((D_MODEL, D_MODEL)),
            small((1, D_MODEL)),
            resident((D_MODEL, 2 * D_FF)),
            small((CONV_FFN, 2 * D_FF)),
            resident((D_FF, D_MODEL)),
            small((1, D_MODEL)),
        ],
        out_specs=(
            pl.BlockSpec((tm, D_MODEL), lambda b, t: (b * nt + t, 0)),
            pl.BlockSpec((1, carry, 2 * D_FF), lambda b, t: (b, 0, 0)),
        ),
        scratch_shapes=[pltpu.VMEM((base + tm, 2 * D_FF), F32)] + lead_scratch,
        compiler_params=pltpu.CompilerParams(dimension_semantics=("arbitrary", "arbitrary"),
                                             vmem_limit_bytes=VMEM_LIMIT),
        name="out_ffn",
    )(x, mix, *lead, wout, nffn, wup, cw, wdn, nfin)


def kernel(x_prompt, x_sample, state_ret, state_gdn, state_conv_qkv, state_ffn_conv, meta_tokens, norm_mix,
           w_in, conv_gdn, gdn_a_log, gdn_dt_bias, norm_ret, norm_gdn, w_out, norm_ffn, w_up, conv_ffn,
           w_down, norm_final):
    depth = w_in.shape[0]
    assert depth == 1
    nbp, seq, _ = x_prompt.shape
    nbs, dec_seq, _ = x_sample.shape
    assert dec_seq <= SAMPLE_PAD and nbs % SAMPLE_GROUP == 0
    assert seq % (2 * PROMPT_CHUNK * PROMPT_CHUNKS_PER_STEP) == 0
    layer = 0

    w_in_bf = _cast_pad_w_in(w_in[layer])
    row = lambda v: v.reshape(1, -1).astype(F32)
    pad_ba = lambda v: jnp.pad(v.astype(F32), (N_HEADS, LANES - 2 * N_HEADS)).reshape(1, LANES)
    mixer_consts = (conv_gdn[layer], pad_ba(gdn_a_log[layer]), pad_ba(gdn_dt_bias[layer]),
                    norm_ret[layer].reshape(N_HEADS, D_HEAD), row(norm_gdn[layer]))
    nmix = row(norm_mix[layer])

    assert seq % N_META == 0 and (seq + N_META) % SAMPLE_PAD == 0
    rope_meta_row0, rope_sample_row0 = seq, seq + N_META
    rope = _rope_tables(seq + N_META + SAMPLE_PAD,
                        [(0, N_META), (rope_meta_row0, 0), (rope_sample_row0, PAST_LEN)])

    small_rows = jnp.concatenate([x_sample.reshape(nbs * dec_seq, D_MODEL), meta_tokens.astype(F32)], axis=0)
    proj_small = _proj(small_rows, nmix, w_in_bf, tm=small_rows.shape[0])
    meta_row0 = nbs * dec_seq

    zero_state = jnp.zeros((1, N_HEADS, D_HEAD, D_HEAD), F32)
    zero_cq = jnp.zeros((1, CONV_GDN - 1, GDN_QKV), F32)
    mix_m, sret_m, sgdn_m, cq_m = _mixer(proj_small, meta_row0, 1, N_META, zero_state, zero_state, zero_cq, rope,
                                         rope_meta_row0, mixer_consts, bb=1, c=N_META, nch=1, n_valid=N_META,
                                         shared_init=True)

    xp = x_prompt.reshape(nbp * seq, D_MODEL)
    mix_p, sret_p, sgdn_p, cq_p, w_out_bf, w_up_bf, w_down_bf = _proj_mixer(
        xp, nmix, w_in_bf, nbp, sret_m, sgdn_m, cq_m, rope, mixer_consts,
        (w_out[layer], w_up[layer], w_down[layer]), c=PROMPT_CHUNK, nch=PROMPT_CHUNKS_PER_STEP)
    ffn_weights = (w_out_bf, row(norm_ffn[layer]), w_up_bf, conv_ffn[layer], w_down_bf, row(norm_final))

    meta_lead = (meta_tokens.astype(F32), mix_m.reshape(N_META, D_MODEL))
    y_p, cf_p = _ffn(xp, mix_p, meta_lead, ffn_weights, nseq=nbp, tm=512, stride=1)
    y_prompt = y_p.reshape(nbp, seq, D_MODEL)

    mix_s, sret_s, sgdn_s, cq_s = _mixer(proj_small, 0, nbs, SAMPLE_PAD, state_ret[layer], state_gdn[layer],
                                         state_conv_qkv[layer], rope, rope_sample_row0, mixer_consts,
                                         bb=SAMPLE_SEQS_PER_STEP, c=SAMPLE_PAD, nch=1, n_valid=dec_seq,
                                         shared_init=False, stored=dec_seq)
    ng = nbs // SAMPLE_GROUP
    to_tmajor = lambda a: a.reshape(ng, SAMPLE_GROUP, a.shape[1], a.shape[2]).transpose(0, 2, 1, 3)
    xs_t = to_tmajor(x_sample).reshape(nbs * dec_seq, D_MODEL)
    mix_t = to_tmajor(mix_s[:, :dec_seq]).reshape(nbs * dec_seq, D_MODEL)
    cf0_t = to_tmajor(state_ffn_conv[layer]).reshape(ng, (CONV_FFN - 1) * SAMPLE_GROUP, 2 * D_FF)
    y_s_t, cf_s_t = _ffn(xs_t, mix_t, cf0_t, ffn_weights, nseq=ng, tm=dec_seq * SAMPLE_GROUP,
                         stride=SAMPLE_GROUP)
    y_sample = y_s_t.reshape(ng, dec_seq, SAMPLE_GROUP, D_MODEL).transpose(0, 2, 1, 3).reshape(nbs, dec_seq, D_MODEL)
    cf_s = cf_s_t.reshape(ng, CONV_FFN - 1, SAMPLE_GROUP, 2 * D_FF).transpose(0, 2, 1, 3).reshape(
        nbs, CONV_FFN - 1, 2 * D_FF)

    return (y_prompt, y_sample, sret_p[None], sgdn_p[None], cq_p[None], cf_p[None],
            sret_s[None], sgdn_s[None], cq_s[None], cf_s[None])
```

```python
import functools

import jax
import numpy as np
import jax.numpy as jnp
from jax import lax
from jax.experimental import pallas as pl
from jax.experimental.pallas import tpu as pltpu

F32 = jnp.float32
BF16 = jnp.bfloat16

D_MODEL = 1024
N_META = 16
PAST_LEN = 16384
N_HEADS = 4
D_HEAD = 128
D_GRP = N_HEADS * D_HEAD
GDN_QKV = 3 * D_GRP
CONV_GDN = 4
CONV_FFN = 3
D_FF = 2816
ROPE_THETA = 10000.0
EPS = 1e-6

OFF_RQ, OFF_RK, OFF_RV, OFF_RG = 0, D_GRP, 2 * D_GRP, 3 * D_GRP
OFF_QKV = 4 * D_GRP
OFF_GG = OFF_QKV + GDN_QKV
OFF_BA = OFF_GG + D_GRP
IN_WIDTH = OFF_BA + 2 * N_HEADS
LANES = 128
SUBLANES = 8
IN_PAD = OFF_BA + LANES

PROMPT_CHUNK = 128
PROMPT_CHUNKS_PER_STEP = 2
SAMPLE_PAD = 8
SAMPLE_SEQS_PER_STEP = 16
SAMPLE_GROUP = 32
VMEM_LIMIT = 56 * 1024 * 1024


def _round_up(n, m):
    return (n + m - 1) // m * m


def _mm(a, b):
    return jnp.dot(a.astype(BF16), b.astype(BF16), preferred_element_type=F32)


def _mm_nt(a, b):
    return lax.dot_general(a.astype(BF16), b.astype(BF16), (((1,), (1,)), ((), ())),
                           preferred_element_type=F32)


def _mm_tn(a, b):
    return lax.dot_general(a.astype(BF16), b.astype(BF16), (((0,), (0,)), ((), ())),
                           preferred_element_type=F32)


def _split2(x):
    hi = x.astype(BF16)
    lo = (x - hi.astype(F32)).astype(BF16)
    return hi, lo


def _split3(x):
    p0 = x.astype(BF16)
    r = x - p0.astype(F32)
    p1 = r.astype(BF16)
    p2 = (r - p1.astype(F32)).astype(BF16)
    return p0, p1, p2


def _mm3(a2, b2):
    (ah, al), (bh, bl) = a2, b2
    dot = lambda x, y: jnp.dot(x, y, preferred_element_type=F32)
    return dot(ah, bh) + (dot(ah, bl) + dot(al, bh))


def _silu(x):
    return x * jax.nn.sigmoid(x)


def _rms(x, w):
    return x * lax.rsqrt(jnp.mean(x * x, axis=-1, keepdims=True) + EPS) * w


def _rope_kernel(invf_ref, cos_ref, sin_ref, *, segments):
    shape = cos_ref.shape
    r = lax.broadcasted_iota(jnp.int32, shape, 0)
    pos = r + (segments[0][1] - segments[0][0])
    for row0, pos0 in segments[1:]:
        pos = jnp.where(r >= row0, r + (pos0 - row0), pos)
    ang = pos.astype(F32) * invf_ref[...]
    lane = lax.broadcasted_iota(jnp.int32, shape, 1)
    sin = jnp.sin(ang)
    cos_ref[...] = jnp.cos(ang)
    sin_ref[...] = jnp.where(lane < D_HEAD // 2, -sin, sin)


def _rope_tables(n, segments):
    half = D_HEAD // 2
    inv_freq = ROPE_THETA ** (-jnp.arange(half, dtype=F32) / half)
    invf2 = jnp.concatenate([inv_freq, inv_freq]).reshape(1, LANES)
    out = jax.ShapeDtypeStruct((n, LANES), F32)
    return pl.pallas_call(functools.partial(_rope_kernel, segments=tuple(segments)), out_shape=(out, out),
                          name="rope_tables")(invf2)


CAST_COLS = 11 * LANES


def _cast_pad_kernel(wt_ref, o_ref):
    first = pl.program_id(0) * CAST_COLS
    r = lax.broadcasted_iota(jnp.int32, wt_ref.shape, 0)
    blk = jnp.where(first + r < IN_WIDTH, wt_ref[...], 0.0)
    o_ref[...] = blk.T.astype(BF16)


def _cast_pad_w_in(w):
    wt = w.T
    return pl.pallas_call(
        _cast_pad_kernel,
        out_shape=jax.ShapeDtypeStruct((D_MODEL, IN_PAD), BF16),
        grid=(IN_PAD // CAST_COLS,),
        in_specs=[pl.BlockSpec((CAST_COLS, D_MODEL), lambda i: (i, 0))],
        out_specs=pl.BlockSpec((D_MODEL, CAST_COLS), lambda i: (0, i)),
        name="cast_w_in",
    )(wt)


def _proj_kernel(x_ref, nw_ref, w_ref, o_ref):
    h = _rms(x_ref[...], nw_ref[...])
    o_ref[...] = jnp.dot(h.astype(BF16), w_ref[...], preferred_element_type=F32)


def _proj(x, norm_w, w_bf, tm):
    rows = x.shape[0]
    assert rows % tm == 0
    return pl.pallas_call(
        _proj_kernel,
        out_shape=jax.ShapeDtypeStruct((rows, IN_PAD), F32),
        grid=(rows // tm,),
        in_specs=[
            pl.BlockSpec((tm, D_MODEL), lambda i: (i, 0)),
            pl.BlockSpec((1, D_MODEL), lambda i: (0, 0)),
            pl.BlockSpec((D_MODEL, IN_PAD), lambda i: (0, 0), pipeline_mode=pl.Buffered(1)),
        ],
        out_specs=pl.BlockSpec((tm, IN_PAD), lambda i: (i, 0)),
        compiler_params=pltpu.CompilerParams(dimension_semantics=("arbitrary",),
                                             vmem_limit_bytes=VMEM_LIMIT),
        name="in_proj",
    )(x, norm_w, w_bf)


def _unit_lower_inverses(mats, c, tick=lambda: None, nilpotent=SUBLANES):
    ri = lax.broadcasted_iota(jnp.int32, (c, c), 0)
    ci = lax.broadcasted_iota(jnp.int32, (c, c), 1)
    eye = (ri == ci).astype(F32)
    diag_blk = (ri // SUBLANES) == (ci // SUBLANES)
    ad = [jnp.where(diag_blk, a, 0.0) for a in mats]
    t = [eye - x for x in ad]
    power = ad
    for _ in range(max(0, (nilpotent - 1).bit_length() - 1)):
        power = [_mm(x, x) for x in power]
        t = [x + _mm(x, s) for x, s in zip(t, power)]
        tick()
    for _ in range(3 - max(0, (nilpotent - 1).bit_length() - 1)):
        tick()
    s = SUBLANES
    while s < c:
        level = ((ri // (2 * s)) == (ci // (2 * s))) & ((ri // s) != (ci // s))
        off = [jnp.where(level, a, 0.0) for a in mats]
        lt = [_mm(o, x) for o, x in zip(off, t)]
        tick()
        t = [x - _mm(x, y) for x, y in zip(t, lt)]
        tick()
        s *= 2
    return t


XS_TOP = SUBLANES


def _mixer_init(sret0_ref, sgdn0_ref, cq0_ref, sret_ref, sgdn_ref, xs_ref):
    tail = CONV_GDN - 1
    sret_ref[...] = sret0_ref[...]
    sgdn_ref[...] = sgdn0_ref[...]
    xs_ref[:, XS_TOP - tail:XS_TOP, :] = cq0_ref[...]


def _mixer_block(getp, put_mix, cos2, sin2, const_refs, sret_ref, sgdn_ref, cq_ref, xs_ref, *,
                 bb, c, nch, n_valid, tick=lambda: None):
    (tril_ref, triu_ref, dint_ref, qdec_ref, kdec_ref, cdec_ref, cw_ref, alog_ref, dtb_ref, nret_ref,
     ngdn_ref) = const_refs
    rows = nch * c
    tail = CONV_GDN - 1
    top = XS_TOP
    ri = lax.broadcasted_iota(jnp.int32, (c, c), 0)
    cj = lax.broadcasted_iota(jnp.int32, (c, c), 1)
    tri = ri >= cj
    strict = ri > cj
    tril_bf = tril_ref[...]
    triu_bf = triu_ref[...]
    scale = D_HEAD ** -0.5
    heads = range(N_HEADS)
    seqs = range(bb)
    chunks = [slice(j * c, (j + 1) * c) for j in range(nch)]
    join = lambda parts: parts[0] if len(parts) == 1 else jnp.concatenate(parts, axis=0)

    ret = {}
    for b in seqs:
        for h in heads:
            lo = h * D_HEAD
            q = getp(b, OFF_RQ + lo, OFF_RQ + lo + D_HEAD)
            k = getp(b, OFF_RK + lo, OFF_RK + lo + D_HEAD)
            v = getp(b, OFF_RV + lo, OFF_RV + lo + D_HEAD)
            qr = q * cos2 + pltpu.roll(q, D_HEAD // 2, 1) * sin2
            kr = (k * cos2 + pltpu.roll(k, D_HEAD // 2, 1) * sin2) * scale
            ret[b, h] = (qr, kr, v)
    tick()
    rtasks = [(b, h, j) for j in range(nch) for b in seqs for h in heads]
    scores = {t: _mm_nt(ret[t[0], t[1]][0][chunks[t[2]]], ret[t[0], t[1]][1][chunks[t[2]]]) * dint_ref[t[1]]
              for t in rtasks}
    tick()
    intra = {t: _mm(scores[t], ret[t[0], t[1]][2][chunks[t[2]]]) for t in rtasks}
    tick()
    kv = {t: _mm_tn(ret[t[0], t[1]][1][chunks[t[2]]] * kdec_ref[t[1]], ret[t[0], t[1]][2][chunks[t[2]]])
          for t in rtasks}
    tick()
    pairs = [(b, h) for b in seqs for h in heads]
    s_ret = {bh: sret_ref[bh[0], bh[1]] for bh in pairs}
    o_ret = {}
    for j in range(nch):
        for b, h in pairs:
            o_ret[b, h, j] = intra[b, h, j] + _mm(ret[b, h][0][chunks[j]] * qdec_ref[h], s_ret[b, h])
        for b, h in pairs:
            s_ret[b, h] = cdec_ref[h] * s_ret[b, h] + kv[b, h, j]
        tick()
    o_ret = {bh: join([o_ret[bh[0], bh[1], j] for j in range(nch)]) for bh in pairs}
    mu = {bh: jnp.mean(o_ret[bh], axis=-1, keepdims=True) for bh in pairs}
    cen = {bh: o_ret[bh] - mu[bh] for bh in pairs}
    var = {bh: jnp.mean(cen[bh] * cen[bh], axis=-1, keepdims=True) for bh in pairs}
    for b, h in pairs:
        lo = h * D_HEAD
        gate = getp(b, OFF_RG + lo, OFF_RG + lo + D_HEAD)
        o = cen[b, h] * lax.rsqrt(var[b, h] + EPS) * nret_ref[h:h + 1, :]
        put_mix(b, lo, o * _silu(gate))
        sret_ref[b, h] = s_ret[b, h]
    tick()

    qkvs, beta_alls, cum_cols, cum_rows, cum_tots = {}, {}, {}, {}, {}
    ones_bf = jnp.ones((D_HEAD, c), BF16)
    for b in seqs:
        xs_ref[b, top:top + rows, :] = getp(b, OFF_QKV, OFF_QKV + GDN_QKV)
        conv = xs_ref[b, top - tail:top - tail + rows, :] * cw_ref[0:1, :]
        for i in range(1, CONV_GDN):
            conv = conv + xs_ref[b, top - tail + i:top - tail + i + rows, :] * cw_ref[i:i + 1, :]
        last = n_valid if nch == 1 else rows
        new_tail = xs_ref[b, top + last - tail:top + last, :]
        xs_ref[b, top - tail:top, :] = new_tail
        cq_ref[b] = new_tail
        qkvs[b] = _silu(conv)
        tick()

        ba = getp(b, OFF_BA, OFF_BA + LANES)
        beta_all = jax.nn.sigmoid(ba)
        z = ba + dtb_ref[...]
        softplus = jnp.maximum(z, 0.0) + jnp.log1p(jnp.exp(-jnp.abs(z)))
        g_all = -jnp.exp(alog_ref[...]) * softplus
        if n_valid < c:
            row = lax.broadcasted_iota(jnp.int32, (rows, LANES), 0)
            rowmask = (row < n_valid).astype(F32)
            beta_all = beta_all * rowmask
            g_all = g_all * rowmask
        beta_alls[b] = beta_all
        for j, rs in enumerate(chunks):
            parts = _split3(g_all[rs])
            cum_cols[b, j] = sum(jnp.dot(tril_bf, g, preferred_element_type=F32) for g in parts)
            cum_tots[b, j] = sum(jnp.dot(ones_bf, g, preferred_element_type=F32) for g in parts)
            cum_rows[b, j] = sum(lax.dot_general(g, triu_bf, (((0,), (0,)), ((), ())),
                                                 preferred_element_type=F32) for g in parts)
    tick()

    tasks = [(j, b, h) for j in range(nch) for b in seqs for h in heads]
    qs, ks, vs, betas, ecums, kdecs, cdecs, amats, qkms = [], [], [], [], [], [], [], [], []
    raw = [(qkvs[b][chunks[j], h * D_HEAD:(h + 1) * D_HEAD],
            qkvs[b][chunks[j], D_GRP + h * D_HEAD:D_GRP + (h + 1) * D_HEAD]) for j, b, h in tasks]
    sumsq = [(jnp.sum(q * q, axis=-1, keepdims=True), jnp.sum(k * k, axis=-1, keepdims=True)) for q, k in raw]
    for i, (j, b, h) in enumerate(tasks):
        rs = chunks[j]
        lo = h * D_HEAD
        v = qkvs[b][rs, 2 * D_GRP + lo:2 * D_GRP + lo + D_HEAD]
        q = raw[i][0] * lax.rsqrt(sumsq[i][0] + EPS) * scale
        k = raw[i][1] * lax.rsqrt(sumsq[i][1] + EPS)
        beta = jnp.broadcast_to(beta_alls[b][rs, h:h + 1], (c, LANES))
        cum = jnp.broadcast_to(cum_cols[b, j][:, N_HEADS + h:N_HEADS + h + 1], (c, LANES))
        cum_row = cum_rows[b, j][N_HEADS + h:N_HEADS + h + 1, :]
        dmask = jnp.exp(jnp.where(tri, cum[:, :c] - cum_row, -jnp.inf))
        cum_last = jnp.broadcast_to(cum_tots[b, j][:, N_HEADS + h:N_HEADS + h + 1], (D_HEAD, LANES))
        kq = _mm_nt(jnp.concatenate([k, q], axis=0), k)
        amats.append(jnp.where(strict, beta[:, :c] * kq[:c] * dmask, 0.0))
        qkms.append(jnp.where(tri, kq[c:] * dmask, 0.0))
        qs.append(q)
        ks.append(k)
        vs.append(v)
        betas.append(beta)
        ecums.append(jnp.exp(cum))
        kdecs.append(jnp.exp(cum_last[:c] - cum))
        cdecs.append(jnp.exp(cum_last))
        if i % N_HEADS == N_HEADS - 1:
            tick()

    tinv = _unit_lower_inverses(amats, c, tick, nilpotent=min(SUBLANES, n_valid))
    eye = (ri == cj).astype(F32)
    sols = []
    for i in range(len(tasks)):
        rhs = jnp.concatenate([vs[i] * betas[i], ks[i] * (betas[i] * ecums[i])], axis=1)
        sols.append(rhs + _mm(tinv[i] - eye, rhs))
    tick()

    s_gdn = {bh: sgdn_ref[bh[0], bh[1]] for bh in pairs}
    o_gdn = {}
    for j in range(nch):
        idx = {tasks[i][1:]: i for i in range(len(tasks)) if tasks[i][0] == j}
        lhs = {bh: jnp.concatenate([sols[i][:, D_HEAD:], qs[i] * ecums[i]], axis=0) for bh, i in idx.items()}
        both = {bh: _mm(lhs[bh], s_gdn[bh]) for bh in idx}
        tick()
        w = {bh: sols[i][:, :D_HEAD] - both[bh][:c] for bh, i in idx.items()}
        for bh, i in idx.items():
            o_gdn[bh + (j,)] = both[bh][c:] + _mm(qkms[i], w[bh])
        tick()
        upd = {bh: _mm_tn(ks[i] * kdecs[i], w[bh]) for bh, i in idx.items()}
        for bh, i in idx.items():
            s_gdn[bh] = cdecs[i] * s_gdn[bh] + upd[bh]
        tick()
    o_gdn = {bh: join([o_gdn[bh + (j,)] for j in range(nch)]) for bh in pairs}
    msq = {bh: jnp.mean(o_gdn[bh] * o_gdn[bh], axis=-1, keepdims=True) for bh in pairs}
    for b, h in pairs:
        lo = h * D_HEAD
        gate = getp(b, OFF_GG + lo, OFF_GG + lo + D_HEAD)
        o = o_gdn[b, h] * lax.rsqrt(msq[b, h] + EPS) * ngdn_ref[...]
        put_mix(b, D_GRP + lo, o * _silu(gate))
        sgdn_ref[b, h] = s_gdn[b, h]


N_MIXER_CONSTS = 11


def _mixer_kernel(p_ref, sret0_ref, sgdn0_ref, cq0_ref, cos_ref, sin_ref, *rest, bb, c, nch, n_valid, stored):
    const_refs, (mix_ref, sret_ref, sgdn_ref, cq_ref, xs_ref) = rest[:N_MIXER_CONSTS], rest[N_MIXER_CONSTS:]
    rows = nch * c

    @pl.when(pl.program_id(1) == 0)
    def _():
        _mixer_init(sret0_ref, sgdn0_ref, cq0_ref, sret_ref, sgdn_ref, xs_ref)

    def put_mix(b, lo, value):
        mix_ref[b, :, lo:lo + D_HEAD] = value.astype(mix_ref.dtype)

    def getp(b, lo, hi):
        blk = p_ref[b * stored:(b + 1) * stored, lo:hi]
        if stored < rows:
            blk = jnp.concatenate([blk, jnp.zeros((rows - stored, hi - lo), F32)], axis=0)
        return blk

    _mixer_block(getp, put_mix, cos_ref[...], sin_ref[...], const_refs, sret_ref, sgdn_ref, cq_ref, xs_ref,
                 bb=bb, c=c, nch=nch, n_valid=n_valid)


PROJ_PANEL = 2 * LANES


def _mixer_block_ticks(bb, c, nch):
    levels = (c // SUBLANES).bit_length() - 1
    retention = 4 + nch + 1
    gdn_prep = bb + 1 + bb * nch
    inverse = 3 + 2 * levels
    return retention + gdn_prep + inverse + 1 + 3 * nch


def _proj_mixer_kernel(x0_ref, xa_ref, xb_ref, nw_ref, w_ref, sret0_ref, sgdn0_ref, cq0_ref, cos_ref, sin_ref,
                       *rest, c, nch, steps_per_seq, n_cast):
    const_refs = rest[:N_MIXER_CONSTS]
    cast_in = rest[N_MIXER_CONSTS:N_MIXER_CONSTS + n_cast]
    mix_ref, sret_ref, sgdn_ref, cq_ref = rest[N_MIXER_CONSTS + n_cast:N_MIXER_CONSTS + n_cast + 4]
    cast_out = rest[N_MIXER_CONSTS + n_cast + 4:N_MIXER_CONSTS + 2 * n_cast + 4]
    pja_ref, pjb_ref, xs_ref = rest[N_MIXER_CONSTS + 2 * n_cast + 4:]
    step = pl.program_id(0)
    rows = nch * c

    for src_ref, dst_ref in zip(cast_in, cast_out):
        dst_ref[...] = src_ref[...].astype(dst_ref.dtype)

    def project(x_ref, dst_ref, n_ticks):
        h = _rms(x_ref[...], nw_ref[...]).astype(BF16)
        panels = [(lo, min(lo + PROJ_PANEL, IN_PAD)) for lo in range(0, IN_PAD, PROJ_PANEL)]
        n_panels = len(panels)
        calls = [0]

        def emit():
            lo, hi = panels.pop(0)
            dst_ref[:, lo:hi] = jnp.dot(h, w_ref[:, lo:hi], preferred_element_type=F32)

        def tick():
            calls[0] += 1
            due = min(n_panels, -(-calls[0] * n_panels // n_ticks))
            while n_panels - len(panels) < due:
                emit()

        def flush():
            while panels:
                emit()

        return tick, flush

    @pl.when(step == 0)
    def _():
        project(x0_ref, pja_ref, 1)[1]()

    @pl.when(lax.rem(step, steps_per_seq) == 0)
    def _():
        _mixer_init(sret0_ref, sgdn0_ref, cq0_ref, sret_ref, sgdn_ref, xs_ref)

    for half, (cur_ref, x_next_ref, nxt_ref) in enumerate(((pja_ref, xa_ref, pjb_ref), (pjb_ref, xb_ref, pja_ref))):
        r0 = half * rows
        tick, flush = project(x_next_ref, nxt_ref, _mixer_block_ticks(1, c, nch))

        def put_mix(b, lo, value, r0=r0):
            mix_ref[r0:r0 + rows, lo:lo + D_HEAD] = value.astype(mix_ref.dtype)

        _mixer_block(lambda b, lo, hi, cur_ref=cur_ref: cur_ref[:, lo:hi], put_mix,
                     cos_ref[r0:r0 + rows, :], sin_ref[r0:r0 + rows, :], const_refs,
                     sret_ref, sgdn_ref, cq_ref, xs_ref, bb=1, c=c, nch=nch, n_valid=c, tick=tick)
        flush()


def _retention_decay_tables(c, n_valid):
    f32 = np.float32
    lg = np.log1p(-np.power(f32(2.0), f32(-5.0) - np.arange(N_HEADS, dtype=f32)))[:, None].astype(f32)
    idx = np.arange(c, dtype=f32)
    diff = idx[:, None] - idx[None, :]
    dint = np.where(diff[None] >= 0, np.exp(lg[:, :, None] * np.maximum(diff[None], 0)), f32(0.0)).astype(f32)
    qdec = np.exp(lg * (idx + f32(1.0))).astype(f32)
    kdec = np.where(idx[None, :] < n_valid, np.exp(lg * np.minimum(f32(n_valid) - f32(1.0) - idx, c)), f32(0.0))
    cdec = np.exp(lg * f32(n_valid)).astype(f32)
    bc = lambda t: np.broadcast_to(t.astype(f32)[:, :, None], t.shape + (LANES,))
    return dint, bc(qdec), bc(kdec), np.broadcast_to(cdec[:, :, None], (N_HEADS, 1, LANES))


def _mixer_const_operands(c, n_valid, consts):
    cw, alog, dtb, nret, ngdn = consts
    dint, qdec, kdec, cdec = _retention_decay_tables(c, n_valid)
    idx = np.arange(c)
    tril = (idx[:, None] >= idx[None, :]).astype(np.float32)
    arrays = (jnp.asarray(tril, BF16), jnp.asarray(tril.T, BF16), jnp.asarray(dint), jnp.asarray(qdec),
              jnp.asarray(kdec), jnp.asarray(cdec), cw, alog, dtb, nret, ngdn)
    assert len(arrays) == N_MIXER_CONSTS
    return arrays, [a.shape for a in arrays]


def _mixer(proj, row0, nb, length, sret0, sgdn0, cq0, rope, rope_row0, consts, *, bb, c, nch, n_valid,
           shared_init, stored=None):
    rows = nch * c
    assert nb % bb == 0 and length % rows == 0 and rope_row0 % rows == 0
    assert not shared_init or bb == 1
    assert n_valid == c or nch == 1
    assert bb == 1 or length == rows
    const_arrays, const_shapes = _mixer_const_operands(c, n_valid, consts)
    nsteps = length // rows
    stored = rows if stored is None else stored
    assert stored == rows or (nsteps == 1 and stored >= n_valid and (bb * stored) % SUBLANES == 0)
    assert row0 % (bb * stored) == 0
    blk0 = row0 // (bb * stored)
    init_idx = (lambda b, i: (0, 0, 0, 0)) if shared_init else (lambda b, i: (b, 0, 0, 0))
    init_idx3 = (lambda b, i: (0, 0, 0)) if shared_init else (lambda b, i: (b, 0, 0))
    whole = lambda shape: pl.BlockSpec(shape, lambda b, i: (0,) * len(shape))
    rope_spec = pl.BlockSpec((rows, LANES), lambda b, i: (rope_row0 // rows + i, 0))
    state_shape = (bb, N_HEADS, D_HEAD, D_HEAD)
    tail = CONV_GDN - 1
    kern = functools.partial(_mixer_kernel, bb=bb, c=c, nch=nch, n_valid=n_valid, stored=stored)
    return pl.pallas_call(
        kern,
        out_shape=(
            jax.ShapeDtypeStruct((nb, length, D_MODEL), BF16),
            jax.ShapeDtypeStruct((nb, N_HEADS, D_HEAD, D_HEAD), F32),
            jax.ShapeDtypeStruct((nb, N_HEADS, D_HEAD, D_HEAD), F32),
            jax.ShapeDtypeStruct((nb, tail, GDN_QKV), F32),
        ),
        grid=(nb // bb, nsteps),
        in_specs=[
            pl.BlockSpec((bb * stored, IN_PAD), lambda b, i: (blk0 + b * nsteps + i, 0)),
            pl.BlockSpec(state_shape, init_idx),
            pl.BlockSpec(state_shape, init_idx),
            pl.BlockSpec((bb, tail, GDN_QKV), init_idx3),
            rope_spec,
            rope_spec,
        ] + [whole(shape) for shape in const_shapes],
        out_specs=(
            pl.BlockSpec((bb, rows, D_MODEL), lambda b, i: (b, i, 0)),
            pl.BlockSpec(state_shape, lambda b, i: (b, 0, 0, 0)),
            pl.BlockSpec(state_shape, lambda b, i: (b, 0, 0, 0)),
            pl.BlockSpec((bb, tail, GDN_QKV), lambda b, i: (b, 0, 0)),
        ),
        scratch_shapes=[pltpu.VMEM((bb, XS_TOP + rows, GDN_QKV), F32)],
        compiler_params=pltpu.CompilerParams(dimension_semantics=("arbitrary", "arbitrary"),
                                             vmem_limit_bytes=VMEM_LIMIT),
        name="mixer",
    )(proj, sret0, sgdn0, cq0, rope[0], rope[1], *const_arrays)


def _row_slab(nrows, nsteps):
    for hold in (1, 2, 4, 8):
        slabs = nsteps // hold
        if nsteps % hold == 0 and nrows % slabs == 0 and (nrows // slabs) % (2 * SUBLANES) == 0:
            return nrows // slabs, hold
    raise ValueError((nrows, nsteps))


def _proj_mixer(x, norm_w, w_bf, nseq, sret0, sgdn0, cq0, rope, consts, to_bf16, *, c, nch):
    rows = nch * c
    total = x.shape[0]
    length = total // nseq
    assert total % nseq == 0 and length % (2 * rows) == 0
    nblk = total // rows
    steps_per_seq = length // (2 * rows)
    const_arrays, const_shapes = _mixer_const_operands(c, c, consts)
    whole = lambda shape, **kw: pl.BlockSpec(shape, lambda s: (0,) * len(shape), **kw)
    rope_spec = pl.BlockSpec((2 * rows, LANES), lambda s: (lax.rem(s, steps_per_seq), 0))
    state_shape = (1, N_HEADS, D_HEAD, D_HEAD)
    tail = CONV_GDN - 1
    nsteps = nblk // 2
    slabs = [_row_slab(w.shape[0], nsteps) for w in to_bf16]
    cast_specs = [pl.BlockSpec((r, w.shape[1]), lambda s, hold=hold: (s // hold, 0))
                  for w, (r, hold) in zip(to_bf16, slabs)]
    kern = functools.partial(_proj_mixer_kernel, c=c, nch=nch, steps_per_seq=steps_per_seq, n_cast=len(to_bf16))
    return pl.pallas_call(
        kern,
        out_shape=(
            jax.ShapeDtypeStruct((total, D_MODEL), BF16),
            jax.ShapeDtypeStruct((nseq, N_HEADS, D_HEAD, D_HEAD), F32),
            jax.ShapeDtypeStruct((nseq, N_HEADS, D_HEAD, D_HEAD), F32),
            jax.ShapeDtypeStruct((nseq, tail, GDN_QKV), F32),
        ) + tuple(jax.ShapeDtypeStruct(w.shape, BF16) for w in to_bf16),
        grid=(nsteps,),
        in_specs=[
            pl.BlockSpec((rows, D_MODEL), lambda s: (0, 0), pipeline_mode=pl.Buffered(1)),
            pl.BlockSpec((rows, D_MODEL), lambda s: (2 * s + 1, 0)),
            pl.BlockSpec((rows, D_MODEL), lambda s: (jnp.minimum(2 * s + 2, nblk - 1), 0)),
            whole((1, D_MODEL)),
            whole((D_MODEL, IN_PAD), pipeline_mode=pl.Buffered(1)),
            whole(state_shape),
            whole(state_shape),
            whole((1, tail, GDN_QKV)),
            rope_spec,
            rope_spec,
        ] + [whole(shape) for shape in const_shapes] + cast_specs,
        out_specs=(
            pl.BlockSpec((2 * rows, D_MODEL), lambda s: (s, 0)),
            pl.BlockSpec(state_shape, lambda s: (s // steps_per_seq, 0, 0, 0)),
            pl.BlockSpec(state_shape, lambda s: (s // steps_per_seq, 0, 0, 0)),
            pl.BlockSpec((1, tail, GDN_QKV), lambda s: (s // steps_per_seq, 0, 0)),
        ) + tuple(cast_specs),
        scratch_shapes=[pltpu.VMEM((rows, IN_PAD), F32), pltpu.VMEM((rows, IN_PAD), F32),
                        pltpu.VMEM((1, XS_TOP + rows, GDN_QKV), F32)],
        compiler_params=pltpu.CompilerParams(dimension_semantics=("arbitrary",),
                                             vmem_limit_bytes=VMEM_LIMIT),
        name="proj_mixer",
    )(x, x, x, norm_w, w_bf, sret0, sgdn0, cq0, rope[0], rope[1], *const_arrays, *to_bf16)


FFN_COL_CHUNK = D_FF // 11


def _ffn_kernel(x_ref, mix_ref, *rest, tm, stride, prefix):
    if prefix:
        (xm_ref, mixm_ref, wout_ref, nffn_ref, wup_ref, cw_ref, wdn_ref, nfin_ref,
         y_ref, tail_ref, full_ref, lead_ref) = rest
    else:
        tail0_ref, wout_ref, nffn_ref, wup_ref, cw_ref, wdn_ref, nfin_ref, y_ref, tail_ref, full_ref = rest
    t = pl.program_id(1)
    carry = (CONV_FFN - 1) * stride
    base = _round_up(carry, SUBLANES)

    def up_project(x, mix):
        x1 = x + jnp.dot(mix, wout_ref[...], preferred_element_type=F32)
        h = _rms(x1, nffn_ref[...]).astype(BF16)
        return x1, jnp.dot(h, wup_ref[...], preferred_element_type=F32)

    if prefix:
        @pl.when((pl.program_id(0) == 0) & (t == 0))
        def _():
            um = up_project(xm_ref[...], mixm_ref[...])[1]
            lead_ref[...] = um[um.shape[0] - carry:, :]

    @pl.when(t == 0)
    def _():
        full_ref[base - carry:base, :] = lead_ref[...] if prefix else tail0_ref[0]

    x1, up = up_project(x_ref[...], mix_ref[...])
    full_ref[base:base + tm, :] = up

    def conv_cols(lo):
        acc = full_ref[base - carry:base - carry + tm, lo:lo + FFN_COL_CHUNK] * cw_ref[0:1, lo:lo + FFN_COL_CHUNK]
        for i in range(1, CONV_FFN):
            r0 = base - carry + i * stride
            acc = acc + full_ref[r0:r0 + tm, lo:lo + FFN_COL_CHUNK] * cw_ref[i:i + 1, lo:lo + FFN_COL_CHUNK]
        return acc

    x2 = x1
    for j in range(D_FF // FFN_COL_CHUNK):
        lo = j * FFN_COL_CHUNK
        act = (_silu(conv_cols(lo)) * conv_cols(D_FF + lo)).astype(BF16)
        x2 = x2 + jnp.dot(act, wdn_ref[lo:lo + FFN_COL_CHUNK, :], preferred_element_type=F32)
    y_ref[...] = _rms(x2, nfin_ref[...])

    new_tail = full_ref[base + tm - carry:base + tm, :]
    full_ref[base - carry:base, :] = new_tail
    tail_ref[0] = new_tail


def _ffn(x, mix, lead, weights, *, nseq, tm, stride):
    wout, nffn, wup, cw, wdn, nfin = weights
    rows = x.shape[0]
    assert rows % (nseq * tm) == 0
    nt = rows // (nseq * tm)
    carry = (CONV_FFN - 1) * stride
    base = _round_up(carry, SUBLANES)
    assert tm >= carry
    resident = lambda shape: pl.BlockSpec(shape, lambda b, t: (0, 0), pipeline_mode=pl.Buffered(1))
    small = lambda shape: pl.BlockSpec(shape, lambda b, t: (0, 0))
    prefix = isinstance(lead, tuple)
    if prefix:
        assert stride == 1 and all(a.shape[0] >= carry for a in lead)
        lead_specs = [small(a.shape) for a in lead]
        lead_scratch = [pltpu.VMEM((carry, 2 * D_FF), F32)]
    else:
        lead = (lead,)
        lead_specs = [pl.BlockSpec((1, carry, 2 * D_FF), lambda b, t: (b, 0, 0))]
        lead_scratch = []
    kern = functools.partial(_ffn_kernel, tm=tm, stride=stride, prefix=prefix)
    return pl.pallas_call(
        kern,
        out_shape=(
            jax.ShapeDtypeStruct((rows, D_MODEL), F32),
            jax.ShapeDtypeStruct((nseq, carry, 2 * D_FF), F32),
        ),
        grid=(nseq, nt),
        in_specs=[
            pl.BlockSpec((tm, D_MODEL), lambda b, t: (b * nt + t, 0)),
            pl.BlockSpec((tm, D_MODEL), lambda b, t: (b * nt + t, 0)),
        ] + lead_specs + [
            resident((D_MODEL, D_MODEL)),
            small((1, D_MODEL)),
            resident((D_MODEL, 2 * D_FF)),
            small((CONV_FFN, 2 * D_FF)),
            resident((D_FF, D_MODEL)),
            small((1, D_MODEL)),
        ],
        out_specs=(
            pl.BlockSpec((tm, D_MODEL), lambda b, t: (b * nt + t, 0)),
            pl.BlockSpec((1, carry, 2 * D_FF), lambda b, t: (b, 0, 0)),
        ),
        scratch_shapes=[pltpu.VMEM((base + tm, 2 * D_FF), F32)] + lead_scratch,
        compiler_params=pltpu.CompilerParams(dimension_semantics=("arbitrary", "arbitrary"),
                                             vmem_limit_bytes=VMEM_LIMIT),
        name="out_ffn",
    )(x, mix, *lead, wout, nffn, wup, cw, wdn, nfin)


def kernel(x_prompt, x_sample, state_ret, state_gdn, state_conv_qkv, state_ffn_conv, meta_tokens, norm_mix,
           w_in, conv_gdn, gdn_a_log, gdn_dt_bias, norm_ret, norm_gdn, w_out, norm_ffn, w_up, conv_ffn,
           w_down, norm_final):
    depth = w_in.shape[0]
    assert depth == 1
    nbp, seq, _ = x_prompt.shape
    nbs, dec_seq, _ = x_sample.shape
    assert dec_seq <= SAMPLE_PAD and nbs % SAMPLE_GROUP == 0
    assert seq % (2 * PROMPT_CHUNK * PROMPT_CHUNKS_PER_STEP) == 0
    layer = 0

    w_in_bf = _cast_pad_w_in(w_in[layer])
    row = lambda v: v.reshape(1, -1).astype(F32)
    pad_ba = lambda v: jnp.pad(v.astype(F32), (N_HEADS, LANES - 2 * N_HEADS)).reshape(1, LANES)
    mixer_consts = (conv_gdn[layer], pad_ba(gdn_a_log[layer]), pad_ba(gdn_dt_bias[layer]),
                    norm_ret[layer].reshape(N_HEADS, D_HEAD), row(norm_gdn[layer]))
    nmix = row(norm_mix[layer])

    assert seq % N_META == 0 and (seq + N_META) % SAMPLE_PAD == 0
    rope_meta_row0, rope_sample_row0 = seq, seq + N_META
    rope = _rope_tables(seq + N_META + SAMPLE_PAD,
                        [(0, N_META), (rope_meta_row0, 0), (rope_sample_row0, PAST_LEN)])

    small_rows = jnp.concatenate([x_sample.reshape(nbs * dec_seq, D_MODEL), meta_tokens.astype(F32)], axis=0)
    proj_small = _proj(small_rows, nmix, w_in_bf, tm=small_rows.shape[0])
    meta_row0 = nbs * dec_seq

    zero_state = jnp.zeros((1, N_HEADS, D_HEAD, D_HEAD), F32)
    zero_cq = jnp.zeros((1, CONV_GDN - 1, GDN_QKV), F32)
    mix_m, sret_m, sgdn_m, cq_m = _mixer(proj_small, meta_row0, 1, N_META, zero_state, zero_state, zero_cq, rope,
                                         rope_meta_row0, mixer_consts, bb=1, c=N_META, nch=1, n_valid=N_META,
                                         shared_init=True)

    xp = x_prompt.reshape(nbp * seq, D_MODEL)
    mix_p, sret_p, sgdn_p, cq_p, w_out_bf, w_up_bf, w_down_bf = _proj_mixer(
        xp, nmix, w_in_bf, nbp, sret_m, sgdn_m, cq_m, rope, mixer_consts,
        (w_out[layer], w_up[layer], w_down[layer]), c=PROMPT_CHUNK, nch=PROMPT_CHUNKS_PER_STEP)
    ffn_weights = (w_out_bf, row(norm_ffn[layer]), w_up_bf, conv_ffn[layer], w_down_bf, row(norm_final))

    meta_lead = (meta_tokens.astype(F32), mix_m.reshape(N_META, D_MODEL))
    y_p, cf_p = _ffn(xp, mix_p, meta_lead, ffn_weights, nseq=nbp, tm=512, stride=1)
    y_prompt = y_p.reshape(nbp, seq, D_MODEL)

    mix_s, sret_s, sgdn_s, cq_s = _mixer(proj_small, 0, nbs, SAMPLE_PAD, state_ret[layer], state_gdn[layer],
                                         state_conv_qkv[layer], rope, rope_sample_row0, mixer_consts,
                                         bb=SAMPLE_SEQS_PER_STEP, c=SAMPLE_PAD, nch=1, n_valid=dec_seq,
                                         shared_init=False, stored=dec_seq)
    ng = nbs // SAMPLE_GROUP
    to_tmajor = lambda a: a.reshape(ng, SAMPLE_GROUP, a.shape[1], a.shape[2]).transpose(0, 2, 1, 3)
    xs_t = to_tmajor(x_sample).reshape(nbs * dec_seq, D_MODEL)
    mix_t = to_tmajor(mix_s[:, :dec_seq]).reshape(nbs * dec_seq, D_MODEL)
    cf0_t = to_tmajor(state_ffn_conv[layer]).reshape(ng, (CONV_FFN - 1) * SAMPLE_GROUP, 2 * D_FF)
    y_s_t, cf_s_t = _ffn(xs_t, mix_t, cf0_t, ffn_weights, nseq=ng, tm=dec_seq * SAMPLE_GROUP,
                         stride=SAMPLE_GROUP)
    y_sample = y_s_t.reshape(ng, dec_seq, SAMPLE_GROUP, D_MODEL).transpose(0, 2, 1, 3).reshape(nbs, dec_seq, D_MODEL)
    cf_s = cf_s_t.reshape(ng, CONV_FFN - 1, SAMPLE_GROUP, 2 * D_FF).transpose(0, 2, 1, 3).reshape(
        nbs, CONV_FFN - 1, 2 * D_FF)

    return (y_prompt, y_sample, sret_p[None], sgdn_p[None], cq_p[None], cf_p[None],
            sret_s[None], sgdn_s[None], cq_s[None], cf_s[None])
```

```python
import functools

import jax
import numpy as np
import jax.numpy as jnp
from jax import lax
from jax.experimental import pallas as pl
from jax.experimental.pallas import tpu as pltpu

F32 = jnp.float32
BF16 = jnp.bfloat16

D_MODEL = 1024
N_META = 16
PAST_LEN = 16384
N_HEADS = 4
D_HEAD = 128
D_GRP = N_HEADS * D_HEAD
GDN_QKV = 3 * D_GRP
CONV_GDN = 4
CONV_FFN = 3
D_FF = 2816
ROPE_THETA = 10000.0
EPS = 1e-6

OFF_RQ, OFF_RK, OFF_RV, OFF_RG = 0, D_GRP, 2 * D_GRP, 3 * D_GRP
OFF_QKV = 4 * D_GRP
OFF_GG = OFF_QKV + GDN_QKV
OFF_BA = OFF_GG + D_GRP
IN_WIDTH = OFF_BA + 2 * N_HEADS
LANES = 128
SUBLANES = 8
MXU_WIDTH = 256
V7X_VMEM_BYTES = 64 * 1024 * 1024
IN_PAD = OFF_BA + LANES

PROMPT_CHUNK = 128
PROMPT_CHUNKS_PER_STEP = 2
SAMPLE_PAD = 8
SAMPLE_SEQS_PER_STEP = 16
SAMPLE_GROUP = 64
VMEM_LIMIT = V7X_VMEM_BYTES * 7 // 8


def _round_up(n, m):
    return (n + m - 1) // m * m


def _mm(a, b):
    return jnp.dot(a.astype(BF16), b.astype(BF16), preferred_element_type=F32)


def _mm_nt(a, b):
    return lax.dot_general(a.astype(BF16), b.astype(BF16), (((1,), (1,)), ((), ())),
                           preferred_element_type=F32)


def _mm_tn(a, b):
    return lax.dot_general(a.astype(BF16), b.astype(BF16), (((0,), (0,)), ((), ())),
                           preferred_element_type=F32)


def _split3(x):
    p0 = x.astype(BF16)
    r = x - p0.astype(F32)
    p1 = r.astype(BF16)
    p2 = (r - p1.astype(F32)).astype(BF16)
    return p0, p1, p2


def _silu(x):
    return x * jax.nn.sigmoid(x)


def _rms(x, w):
    return x * lax.rsqrt(jnp.mean(x * x, axis=-1, keepdims=True) + EPS) * w


def _rope_kernel(invf_ref, cos_ref, sin_ref, *, segments):
    shape = cos_ref.shape
    r = lax.broadcasted_iota(jnp.int32, shape, 0)
    pos = r + (segments[0][1] - segments[0][0])
    for row0, pos0 in segments[1:]:
        pos = jnp.where(r >= row0, r + (pos0 - row0), pos)
    ang = pos.astype(F32) * invf_ref[...]
    lane = lax.broadcasted_iota(jnp.int32, shape, 1)
    sin = jnp.sin(ang)
    cos_ref[...] = jnp.cos(ang)
    sin_ref[...] = jnp.where(lane < D_HEAD // 2, -sin, sin)


def _rope_tables(n, segments):
    half = D_HEAD // 2
    inv_freq = ROPE_THETA ** (-jnp.arange(half, dtype=F32) / half)
    invf2 = jnp.concatenate([inv_freq, inv_freq]).reshape(1, LANES)
    out = jax.ShapeDtypeStruct((n, LANES), F32)
    return pl.pallas_call(functools.partial(_rope_kernel, segments=tuple(segments)), out_shape=(out, out),
                          name="rope_tables")(invf2)


CAST_COLS = 11 * LANES


def _cast_pad_kernel(wt_ref, o_ref):
    first = pl.program_id(0) * CAST_COLS
    r = lax.broadcasted_iota(jnp.int32, wt_ref.shape, 0)
    blk = jnp.where(first + r < IN_WIDTH, wt_ref[...], 0.0)
    o_ref[...] = blk.T.astype(BF16)


def _cast_pad_w_in(w):
    wt = w.T
    return pl.pallas_call(
        _cast_pad_kernel,
        out_shape=jax.ShapeDtypeStruct((D_MODEL, IN_PAD), BF16),
        grid=(IN_PAD // CAST_COLS,),
        in_specs=[pl.BlockSpec((CAST_COLS, D_MODEL), lambda i: (i, 0))],
        out_specs=pl.BlockSpec((D_MODEL, CAST_COLS), lambda i: (0, i)),
        name="cast_w_in",
    )(wt)


def _proj_kernel(x_ref, nw_ref, w_ref, o_ref):
    h = _rms(x_ref[...], nw_ref[...])
    o_ref[...] = jnp.dot(h.astype(BF16), w_ref[...], preferred_element_type=F32)


def _proj(x, norm_w, w_bf, tm):
    rows = x.shape[0]
    assert rows % tm == 0
    return pl.pallas_call(
        _proj_kernel,
        out_shape=jax.ShapeDtypeStruct((rows, IN_PAD), F32),
        grid=(rows // tm,),
        in_specs=[
            pl.BlockSpec((tm, D_MODEL), lambda i: (i, 0)),
            pl.BlockSpec((1, D_MODEL), lambda i: (0, 0)),
            pl.BlockSpec((D_MODEL, IN_PAD), lambda i: (0, 0), pipeline_mode=pl.Buffered(1)),
        ],
        out_specs=pl.BlockSpec((tm, IN_PAD), lambda i: (i, 0)),
        compiler_params=pltpu.CompilerParams(dimension_semantics=("arbitrary",),
                                             vmem_limit_bytes=VMEM_LIMIT),
        name="in_proj",
    )(x, norm_w, w_bf)


def _unit_lower_inverses(mats, c, tick=lambda: None, nilpotent=SUBLANES):
    ri = lax.broadcasted_iota(jnp.int32, (c, c), 0)
    ci = lax.broadcasted_iota(jnp.int32, (c, c), 1)
    eye = (ri == ci).astype(F32)
    diag_blk = (ri // SUBLANES) == (ci // SUBLANES)
    ad = [jnp.where(diag_blk, a, 0.0) for a in mats]
    assert 1 <= nilpotent <= SUBLANES
    n_factors = max(0, (nilpotent - 1).bit_length() - 1)
    t = [eye - x for x in ad]
    power = ad
    for i in range(2):
        if i < n_factors:
            power = [_mm(x, x) for x in power]
            t = [x + _mm(x, s) for x, s in zip(t, power)]
        tick()
    tick()
    s = SUBLANES
    while s < c:
        level = ((ri // (2 * s)) == (ci // (2 * s))) & ((ri // s) != (ci // s))
        off = [jnp.where(level, a, 0.0) for a in mats]
        lt = [_mm(o, x) for o, x in zip(off, t)]
        tick()
        t = [x - _mm(x, y) for x, y in zip(t, lt)]
        tick()
        s *= 2
    return t


XS_TOP = SUBLANES


def _mixer_init(sret0_ref, sgdn0_ref, cq0_ref, sret_ref, sgdn_ref, xs_ref):
    tail = CONV_GDN - 1
    sret_ref[...] = sret0_ref[...]
    sgdn_ref[...] = sgdn0_ref[...]
    xs_ref[:, XS_TOP - tail:XS_TOP, :] = cq0_ref[...]


def _mixer_block(getp, put_mix, cos2, sin2, const_refs, sret_ref, sgdn_ref, cq_ref, xs_ref, *,
                 bb, c, nch, n_valid, tick=lambda: None):
    (tril_ref, triu_ref, dint_ref, qdec_ref, kdec_ref, cdec_ref, cw_ref, alog_ref, dtb_ref, nret_ref,
     ngdn_ref) = const_refs
    rows = nch * c
    tail = CONV_GDN - 1
    top = XS_TOP
    ri = lax.broadcasted_iota(jnp.int32, (c, c), 0)
    cj = lax.broadcasted_iota(jnp.int32, (c, c), 1)
    tri = ri >= cj
    strict = ri > cj
    tril_bf = tril_ref[...]
    triu_bf = triu_ref[...]
    scale = D_HEAD ** -0.5
    heads = range(N_HEADS)
    seqs = range(bb)
    chunks = [slice(j * c, (j + 1) * c) for j in range(nch)]
    join = lambda parts: parts[0] if len(parts) == 1 else jnp.concatenate(parts, axis=0)

    ret = {}
    for b in seqs:
        for h in heads:
            lo = h * D_HEAD
            q = getp(b, OFF_RQ + lo, OFF_RQ + lo + D_HEAD)
            k = getp(b, OFF_RK + lo, OFF_RK + lo + D_HEAD)
            v = getp(b, OFF_RV + lo, OFF_RV + lo + D_HEAD)
            qr = q * cos2 + pltpu.roll(q, D_HEAD // 2, 1) * sin2
            kr = (k * cos2 + pltpu.roll(k, D_HEAD // 2, 1) * sin2) * scale
            ret[b, h] = (qr, kr, v)
    tick()
    rtasks = [(b, h, j) for j in range(nch) for b in seqs for h in heads]
    scores = {t: _mm_nt(ret[t[0], t[1]][0][chunks[t[2]]], ret[t[0], t[1]][1][chunks[t[2]]]) * dint_ref[t[1]]
              for t in rtasks}
    tick()
    intra = {t: _mm(scores[t], ret[t[0], t[1]][2][chunks[t[2]]]) for t in rtasks}
    tick()
    kv = {t: _mm_tn(ret[t[0], t[1]][1][chunks[t[2]]] * kdec_ref[t[1]], ret[t[0], t[1]][2][chunks[t[2]]])
          for t in rtasks}
    tick()
    pairs = [(b, h) for b in seqs for h in heads]
    s_ret = {bh: sret_ref[bh[0], bh[1]] for bh in pairs}
    o_ret = {}
    for j in range(nch):
        for b, h in pairs:
            o_ret[b, h, j] = intra[b, h, j] + _mm(ret[b, h][0][chunks[j]] * qdec_ref[h], s_ret[b, h])
        for b, h in pairs:
            s_ret[b, h] = cdec_ref[h] * s_ret[b, h] + kv[b, h, j]
        tick()
    o_ret = {bh: join([o_ret[bh[0], bh[1], j] for j in range(nch)]) for bh in pairs}
    mu = {bh: jnp.mean(o_ret[bh], axis=-1, keepdims=True) for bh in pairs}
    cen = {bh: o_ret[bh] - mu[bh] for bh in pairs}
    var = {bh: jnp.mean(cen[bh] * cen[bh], axis=-1, keepdims=True) for bh in pairs}
    for b, h in pairs:
        lo = h * D_HEAD
        gate = getp(b, OFF_RG + lo, OFF_RG + lo + D_HEAD)
        o = cen[b, h] * lax.rsqrt(var[b, h] + EPS) * nret_ref[h:h + 1, :]
        put_mix(b, lo, o * _silu(gate))
        sret_ref[b, h] = s_ret[b, h]
    tick()

    qkvs, beta_alls, cum_cols, cum_rows, cum_tots = {}, {}, {}, {}, {}
    ones_bf = jnp.ones((D_HEAD, c), BF16)
    for b in seqs:
        xs_ref[b, top:top + rows, :] = getp(b, OFF_QKV, OFF_QKV + GDN_QKV)
        conv = xs_ref[b, top - tail:top - tail + rows, :] * cw_ref[0:1, :]
        for i in range(1, CONV_GDN):
            conv = conv + xs_ref[b, top - tail + i:top - tail + i + rows, :] * cw_ref[i:i + 1, :]
        last = n_valid if nch == 1 else rows
        new_tail = xs_ref[b, top + last - tail:top + last, :]
        xs_ref[b, top - tail:top, :] = new_tail
        cq_ref[b] = new_tail
        qkvs[b] = _silu(conv)
        tick()

        ba = getp(b, OFF_BA, OFF_BA + LANES)
        beta_all = jax.nn.sigmoid(ba)
        z = ba + dtb_ref[...]
        softplus = jnp.maximum(z, 0.0) + jnp.log1p(jnp.exp(-jnp.abs(z)))
        g_all = -jnp.exp(alog_ref[...]) * softplus
        if n_valid < c:
            row = lax.broadcasted_iota(jnp.int32, (rows, LANES), 0)
            rowmask = (row < n_valid).astype(F32)
            beta_all = beta_all * rowmask
            g_all = g_all * rowmask
        beta_alls[b] = beta_all
        for j, rs in enumerate(chunks):
            parts = _split3(g_all[rs])
            cum_cols[b, j] = sum(jnp.dot(tril_bf, g, preferred_element_type=F32) for g in parts)
            cum_tots[b, j] = sum(jnp.dot(ones_bf, g, preferred_element_type=F32) for g in parts)
            cum_rows[b, j] = sum(lax.dot_general(g, triu_bf, (((0,), (0,)), ((), ())),
                                                 preferred_element_type=F32) for g in parts)
    tick()

    tasks = [(j, b, h) for j in range(nch) for b in seqs for h in heads]
    qs, ks, vs, betas, ecums, kdecs, cdecs, amats, qkms = [], [], [], [], [], [], [], [], []
    raw = [(qkvs[b][chunks[j], h * D_HEAD:(h + 1) * D_HEAD],
            qkvs[b][chunks[j], D_GRP + h * D_HEAD:D_GRP + (h + 1) * D_HEAD]) for j, b, h in tasks]
    sumsq = [(jnp.sum(q * q, axis=-1, keepdims=True), jnp.sum(k * k, axis=-1, keepdims=True)) for q, k in raw]
    for i, (j, b, h) in enumerate(tasks):
        rs = chunks[j]
        lo = h * D_HEAD
        v = qkvs[b][rs, 2 * D_GRP + lo:2 * D_GRP + lo + D_HEAD]
        q = raw[i][0] * lax.rsqrt(sumsq[i][0] + EPS) * scale
        k = raw[i][1] * lax.rsqrt(sumsq[i][1] + EPS)
        beta = jnp.broadcast_to(beta_alls[b][rs, h:h + 1], (c, LANES))
        cum = jnp.broadcast_to(cum_cols[b, j][:, N_HEADS + h:N_HEADS + h + 1], (c, LANES))
        cum_row = cum_rows[b, j][N_HEADS + h:N_HEADS + h + 1, :]
        dmask = jnp.exp(jnp.where(tri, cum[:, :c] - cum_row, -jnp.inf))
        cum_last = jnp.broadcast_to(cum_tots[b, j][:, N_HEADS + h:N_HEADS + h + 1], (D_HEAD, LANES))
        kq = _mm_nt(jnp.concatenate([k, q], axis=0), k)
        amats.append(jnp.where(strict, beta[:, :c] * kq[:c] * dmask, 0.0))
        qkms.append(jnp.where(tri, kq[c:] * dmask, 0.0))
        qs.append(q)
        ks.append(k)
        vs.append(v)
        betas.append(beta)
        ecums.append(jnp.exp(cum))
        kdecs.append(jnp.exp(cum_last[:c] - cum))
        cdecs.append(jnp.exp(cum_last))
        if i % N_HEADS == N_HEADS - 1:
            tick()

    tinv = _unit_lower_inverses(amats, c, tick, nilpotent=min(SUBLANES, n_valid))
    eye = (ri == cj).astype(F32)
    sols = []
    for i in range(len(tasks)):
        rhs = jnp.concatenate([vs[i] * betas[i], ks[i] * (betas[i] * ecums[i])], axis=1)
        sols.append(rhs + _mm(tinv[i] - eye, rhs))
    tick()

    s_gdn = {bh: sgdn_ref[bh[0], bh[1]] for bh in pairs}
    o_gdn = {}
    for j in range(nch):
        idx = {tasks[i][1:]: i for i in range(len(tasks)) if tasks[i][0] == j}
        lhs = {bh: jnp.concatenate([sols[i][:, D_HEAD:], qs[i] * ecums[i]], axis=0) for bh, i in idx.items()}
        both = {bh: _mm(lhs[bh], s_gdn[bh]) for bh in idx}
        tick()
        w = {bh: sols[i][:, :D_HEAD] - both[bh][:c] for bh, i in idx.items()}
        for bh, i in idx.items():
            o_gdn[bh + (j,)] = both[bh][c:] + _mm(qkms[i], w[bh])
        tick()
        upd = {bh: _mm_tn(ks[i] * kdecs[i], w[bh]) for bh, i in idx.items()}
        for bh, i in idx.items():
            s_gdn[bh] = cdecs[i] * s_gdn[bh] + upd[bh]
        tick()
    o_gdn = {bh: join([o_gdn[bh + (j,)] for j in range(nch)]) for bh in pairs}
    msq = {bh: jnp.mean(o_gdn[bh] * o_gdn[bh], axis=-1, keepdims=True) for bh in pairs}
    for b, h in pairs:
        lo = h * D_HEAD
        gate = getp(b, OFF_GG + lo, OFF_GG + lo + D_HEAD)
        o = o_gdn[b, h] * lax.rsqrt(msq[b, h] + EPS) * ngdn_ref[...]
        put_mix(b, D_GRP + lo, o * _silu(gate))
        sgdn_ref[b, h] = s_gdn[b, h]


N_MIXER_CONSTS = 11


def _mixer_kernel(p_ref, sret0_ref, sgdn0_ref, cq0_ref, cos_ref, sin_ref, *rest, bb, c, nch, n_valid, stored):
    const_refs, (mix_ref, sret_ref, sgdn_ref, cq_ref, xs_ref) = rest[:N_MIXER_CONSTS], rest[N_MIXER_CONSTS:]
    rows = nch * c

    @pl.when(pl.program_id(1) == 0)
    def _():
        _mixer_init(sret0_ref, sgdn0_ref, cq0_ref, sret_ref, sgdn_ref, xs_ref)

    def put_mix(b, lo, value):
        mix_ref[b, :, lo:lo + D_HEAD] = value.astype(mix_ref.dtype)

    def getp(b, lo, hi):
        blk = p_ref[b * stored:(b + 1) * stored, lo:hi]
        if stored < rows:
            blk = jnp.concatenate([blk, jnp.zeros((rows - stored, hi - lo), F32)], axis=0)
        return blk

    _mixer_block(getp, put_mix, cos_ref[...], sin_ref[...], const_refs, sret_ref, sgdn_ref, cq_ref, xs_ref,
                 bb=bb, c=c, nch=nch, n_valid=n_valid)


PROJ_PANEL = MXU_WIDTH


def _mixer_block_ticks(bb, c, nch):
    levels = (c // SUBLANES).bit_length() - 1
    retention = 4 + nch + 1
    gdn_prep = bb + 1 + bb * nch
    inverse = 3 + 2 * levels
    return retention + gdn_prep + inverse + 1 + 3 * nch


def _proj_mixer_kernel(x0_ref, xa_ref, xb_ref, nw_ref, w_ref, sret0_ref, sgdn0_ref, cq0_ref, cos_ref, sin_ref,
                       *rest, c, nch, steps_per_seq, n_cast):
    const_refs = rest[:N_MIXER_CONSTS]
    cast_in = rest[N_MIXER_CONSTS:N_MIXER_CONSTS + n_cast]
    mix_ref, sret_ref, sgdn_ref, cq_ref = rest[N_MIXER_CONSTS + n_cast:N_MIXER_CONSTS + n_cast + 4]
    cast_out = rest[N_MIXER_CONSTS + n_cast + 4:N_MIXER_CONSTS + 2 * n_cast + 4]
    pja_ref, pjb_ref, xs_ref = rest[N_MIXER_CONSTS + 2 * n_cast + 4:]
    step = pl.program_id(0)
    rows = nch * c

    for src_ref, dst_ref in zip(cast_in, cast_out):
        dst_ref[...] = src_ref[...].astype(dst_ref.dtype)

    def project(x_ref, dst_ref, n_ticks):
        h = _rms(x_ref[...], nw_ref[...]).astype(BF16)
        panels = [(lo, min(lo + PROJ_PANEL, IN_PAD)) for lo in range(0, IN_PAD, PROJ_PANEL)]
        n_panels = len(panels)
        calls = [0]

        def emit():
            lo, hi = panels.pop(0)
            dst_ref[:, lo:hi] = jnp.dot(h, w_ref[:, lo:hi], preferred_element_type=F32)

        def tick():
            calls[0] += 1
            due = min(n_panels, -(-calls[0] * n_panels // n_ticks))
            while n_panels - len(panels) < due:
                emit()

        def flush():
            assert n_ticks == 1 or calls[0] == n_ticks, (calls[0], n_ticks)
            while panels:
                emit()

        return tick, flush

    @pl.when(step == 0)
    def _():
        project(x0_ref, pja_ref, 1)[1]()

    @pl.when(lax.rem(step, steps_per_seq) == 0)
    def _():
        _mixer_init(sret0_ref, sgdn0_ref, cq0_ref, sret_ref, sgdn_ref, xs_ref)

    for half, (cur_ref, x_next_ref, nxt_ref) in enumerate(((pja_ref, xa_ref, pjb_ref), (pjb_ref, xb_ref, pja_ref))):
        r0 = half * rows
        tick, flush = project(x_next_ref, nxt_ref, _mixer_block_ticks(1, c, nch))

        def put_mix(b, lo, value, r0=r0):
            mix_ref[r0:r0 + rows, lo:lo + D_HEAD] = value.astype(mix_ref.dtype)

        _mixer_block(lambda b, lo, hi, cur_ref=cur_ref: cur_ref[:, lo:hi], put_mix,
                     cos_ref[r0:r0 + rows, :], sin_ref[r0:r0 + rows, :], const_refs,
                     sret_ref, sgdn_ref, cq_ref, xs_ref, bb=1, c=c, nch=nch, n_valid=c, tick=tick)
        flush()


def _retention_decay_tables(c, n_valid):
    f32 = np.float32
    lg = np.log1p(-np.power(f32(2.0), f32(-5.0) - np.arange(N_HEADS, dtype=f32)))[:, None].astype(f32)
    idx = np.arange(c, dtype=f32)
    diff = idx[:, None] - idx[None, :]
    dint = np.where(diff[None] >= 0, np.exp(lg[:, :, None] * np.maximum(diff[None], 0)), f32(0.0)).astype(f32)
    qdec = np.exp(lg * (idx + f32(1.0))).astype(f32)
    kdec = np.where(idx[None, :] < n_valid, np.exp(lg * np.minimum(f32(n_valid) - f32(1.0) - idx, c)), f32(0.0))
    cdec = np.exp(lg * f32(n_valid)).astype(f32)
    bc = lambda t: np.broadcast_to(t.astype(f32)[:, :, None], t.shape + (LANES,))
    return dint, bc(qdec), bc(kdec), np.broadcast_to(cdec[:, :, None], (N_HEADS, 1, LANES))


def _mixer_const_operands(c, n_valid, consts):
    cw, alog, dtb, nret, ngdn = consts
    dint, qdec, kdec, cdec = _retention_decay_tables(c, n_valid)
    idx = np.arange(c)
    tril = (idx[:, None] >= idx[None, :]).astype(np.float32)
    arrays = (jnp.asarray(tril, BF16), jnp.asarray(tril.T, BF16), jnp.asarray(dint), jnp.asarray(qdec),
              jnp.asarray(kdec), jnp.asarray(cdec), cw, alog, dtb, nret, ngdn)
    assert len(arrays) == N_MIXER_CONSTS
    return arrays, [a.shape for a in arrays]


def _mixer(proj, row0, nb, length, sret0, sgdn0, cq0, rope, rope_row0, consts, *, bb, c, nch, n_valid,
           shared_init, stored=None):
    rows = nch * c
    assert nb % bb == 0 and length % rows == 0 and rope_row0 % rows == 0
    assert not shared_init or bb == 1
    assert n_valid == c or nch == 1
    assert bb == 1 or length == rows
    const_arrays, const_shapes = _mixer_const_operands(c, n_valid, consts)
    nsteps = length // rows
    stored = rows if stored is None else stored
    assert stored == rows or (nsteps == 1 and stored >= n_valid and (bb * stored) % SUBLANES == 0)
    assert row0 % (bb * stored) == 0
    blk0 = row0 // (bb * stored)
    init_idx = (lambda b, i: (0, 0, 0, 0)) if shared_init else (lambda b, i: (b, 0, 0, 0))
    init_idx3 = (lambda b, i: (0, 0, 0)) if shared_init else (lambda b, i: (b, 0, 0))
    whole = lambda shape: pl.BlockSpec(shape, lambda b, i: (0,) * len(shape))
    rope_spec = pl.BlockSpec((rows, LANES), lambda b, i: (rope_row0 // rows + i, 0))
    state_shape = (bb, N_HEADS, D_HEAD, D_HEAD)
    tail = CONV_GDN - 1
    kern = functools.partial(_mixer_kernel, bb=bb, c=c, nch=nch, n_valid=n_valid, stored=stored)
    return pl.pallas_call(
        kern,
        out_shape=(
            jax.ShapeDtypeStruct((nb, length, D_MODEL), BF16),
            jax.ShapeDtypeStruct((nb, N_HEADS, D_HEAD, D_HEAD), F32),
            jax.ShapeDtypeStruct((nb, N_HEADS, D_HEAD, D_HEAD), F32),
            jax.ShapeDtypeStruct((nb, tail, GDN_QKV), F32),
        ),
        grid=(nb // bb, nsteps),
        in_specs=[
            pl.BlockSpec((bb * stored, IN_PAD), lambda b, i: (blk0 + b * nsteps + i, 0)),
            pl.BlockSpec(state_shape, init_idx),
            pl.BlockSpec(state_shape, init_idx),
            pl.BlockSpec((bb, tail, GDN_QKV), init_idx3),
            rope_spec,
            rope_spec,
        ] + [whole(shape) for shape in const_shapes],
        out_specs=(
            pl.BlockSpec((bb, rows, D_MODEL), lambda b, i: (b, i, 0)),
            pl.BlockSpec(state_shape, lambda b, i: (b, 0, 0, 0)),
            pl.BlockSpec(state_shape, lambda b, i: (b, 0, 0, 0)),
            pl.BlockSpec((bb, tail, GDN_QKV), lambda b, i: (b, 0, 0)),
        ),
        scratch_shapes=[pltpu.VMEM((bb, XS_TOP + rows, GDN_QKV), F32)],
        compiler_params=pltpu.CompilerParams(dimension_semantics=("arbitrary", "arbitrary"),
                                             vmem_limit_bytes=VMEM_LIMIT),
        name="mixer",
    )(proj, sret0, sgdn0, cq0, rope[0], rope[1], *const_arrays)


def _row_slab(nrows, nsteps):
    for hold in (1, 2, 4, 8):
        slabs = nsteps // hold
        if nsteps % hold == 0 and nrows % slabs == 0 and (nrows // slabs) % (2 * SUBLANES) == 0:
            return nrows // slabs, hold
    raise ValueError((nrows, nsteps))


def _proj_mixer(x, norm_w, w_bf, nseq, sret0, sgdn0, cq0, rope, consts, to_bf16, *, c, nch):
    rows = nch * c
    total = x.shape[0]
    length = total // nseq
    assert total % nseq == 0 and length % (2 * rows) == 0
    nblk = total // rows
    steps_per_seq = length // (2 * rows)
    const_arrays, const_shapes = _mixer_const_operands(c, c, consts)
    whole = lambda shape, **kw: pl.BlockSpec(shape, lambda s: (0,) * len(shape), **kw)
    rope_spec = pl.BlockSpec((2 * rows, LANES), lambda s: (lax.rem(s, steps_per_seq), 0))
    state_shape = (1, N_HEADS, D_HEAD, D_HEAD)
    tail = CONV_GDN - 1
    nsteps = nblk // 2
    slabs = [_row_slab(w.shape[0], nsteps) for w in to_bf16]
    cast_specs = [pl.BlockSpec((r, w.shape[1]), lambda s, hold=hold: (s // hold, 0))
                  for w, (r, hold) in zip(to_bf16, slabs)]
    kern = functools.partial(_proj_mixer_kernel, c=c, nch=nch, steps_per_seq=steps_per_seq, n_cast=len(to_bf16))
    return pl.pallas_call(
        kern,
        out_shape=(
            jax.ShapeDtypeStruct((total, D_MODEL), BF16),
            jax.ShapeDtypeStruct((nseq, N_HEADS, D_HEAD, D_HEAD), F32),
            jax.ShapeDtypeStruct((nseq, N_HEADS, D_HEAD, D_HEAD), F32),
            jax.ShapeDtypeStruct((nseq, tail, GDN_QKV), F32),
        ) + tuple(jax.ShapeDtypeStruct(w.shape, BF16) for w in to_bf16),
        grid=(nsteps,),
        in_specs=[
            pl.BlockSpec((rows, D_MODEL), lambda s: (0, 0), pipeline_mode=pl.Buffered(1)),
            pl.BlockSpec((rows, D_MODEL), lambda s: (2 * s + 1, 0)),
            pl.BlockSpec((rows, D_MODEL), lambda s: (jnp.minimum(2 * s + 2, nblk - 1), 0)),
            whole((1, D_MODEL)),
            whole((D_MODEL, IN_PAD), pipeline_mode=pl.Buffered(1)),
            whole(state_shape),
            whole(state_shape),
            whole((1, tail, GDN_QKV)),
            rope_spec,
            rope_spec,
        ] + [whole(shape) for shape in const_shapes] + cast_specs,
        out_specs=(
            pl.BlockSpec((2 * rows, D_MODEL), lambda s: (s, 0)),
            pl.BlockSpec(state_shape, lambda s: (s // steps_per_seq, 0, 0, 0)),
            pl.BlockSpec(state_shape, lambda s: (s // steps_per_seq, 0, 0, 0)),
            pl.BlockSpec((1, tail, GDN_QKV), lambda s: (s // steps_per_seq, 0, 0)),
        ) + tuple(cast_specs),
        scratch_shapes=[pltpu.VMEM((rows, IN_PAD), F32), pltpu.VMEM((rows, IN_PAD), F32),
                        pltpu.VMEM((1, XS_TOP + rows, GDN_QKV), F32)],
        compiler_params=pltpu.CompilerParams(dimension_semantics=("arbitrary",),
                                             vmem_limit_bytes=VMEM_LIMIT),
        name="proj_mixer",
    )(x, x, x, norm_w, w_bf, sret0, sgdn0, cq0, rope[0], rope[1], *const_arrays, *to_bf16)


FFN_COL_CHUNK = D_FF // 11


def _ffn_kernel(x_ref, mix_ref, *rest, tm, stride, prefix):
    if prefix:
        (xm_ref, mixm_ref, wout_ref, nffn_ref, wup_ref, cw_ref, wdn_ref, nfin_ref,
         y_ref, tail_ref, full_ref, lead_ref) = rest
    else:
        tail0_ref, wout_ref, nffn_ref, wup_ref, cw_ref, wdn_ref, nfin_ref, y_ref, tail_ref, full_ref = rest
    t = pl.program_id(1)
    carry = (CONV_FFN - 1) * stride
    base = _round_up(carry, SUBLANES)

    def up_project(x, mix):
        x1 = x + jnp.dot(mix, wout_ref[...], preferred_element_type=F32)
        h = _rms(x1, nffn_ref[...]).astype(BF16)
        return x1, jnp.dot(h, wup_ref[...], preferred_element_type=F32)

    if prefix:
        @pl.when((pl.program_id(0) == 0) & (t == 0))
        def _():
            um = up_project(xm_ref[...], mixm_ref[...])[1]
            lead_ref[...] = um[um.shape[0] - carry:, :]

    @pl.when(t == 0)
    def _():
        full_ref[base - carry:base, :] = lead_ref[...] if prefix else tail0_ref[0]

    x1, up = up_project(x_ref[...], mix_ref[...])
    full_ref[base:base + tm, :] = up

    def conv_cols(lo):
        acc = full_ref[base - carry:base - carry + tm, lo:lo + FFN_COL_CHUNK] * cw_ref[0:1, lo:lo + FFN_COL_CHUNK]
        for i in range(1, CONV_FFN):
            r0 = base - carry + i * stride
            acc = acc + full_ref[r0:r0 + tm, lo:lo + FFN_COL_CHUNK] * cw_ref[i:i + 1, lo:lo + FFN_COL_CHUNK]
        return acc

    x2 = x1
    for j in range(D_FF // FFN_COL_CHUNK):
        lo = j * FFN_COL_CHUNK
        act = (_silu(conv_cols(lo)) * conv_cols(D_FF + lo)).astype(BF16)
        x2 = x2 + jnp.dot(act, wdn_ref[lo:lo + FFN_COL_CHUNK, :], preferred_element_type=F32)
    y_ref[...] = _rms(x2, nfin_ref[...])

    new_tail = full_ref[base + tm - carry:base + tm, :]
    full_ref[base - carry:base, :] = new_tail
    tail_ref[0] = new_tail


def _ffn(x, mix, lead, weights, *, nseq, tm, stride):
    wout, nffn, wup, cw, wdn, nfin = weights
    rows = x.shape[0]
    assert rows % (nseq * tm) == 0
    nt = rows // (nseq * tm)
    carry = (CONV_FFN - 1) * stride
    base = _round_up(carry, SUBLANES)
    assert tm >= carry
    resident = lambda shape: pl.BlockSpec(shape, lambda b, t: (0, 0), pipeline_mode=pl.Buffered(1))
    small = lambda shape: pl.BlockSpec(shape, lambda b, t: (0, 0))
    prefix = isinstance(lead, tuple)
    if prefix:
        assert stride == 1 and all(a.shape[0] >= carry for a in lead)
        lead_specs = [small(a.shape) for a in lead]
        lead_scratch = [pltpu.VMEM((carry, 2 * D_FF), F32)]
    else:
        lead = (lead,)
        lead_specs = [pl.BlockSpec((1, carry, 2 * D_FF), lambda b, t: (b, 0, 0))]
        lead_scratch = []
    kern = functools.partial(_ffn_kernel, tm=tm, stride=stride, prefix=prefix)
    return pl.pallas_call(
        kern,
        out_shape=(
            jax.ShapeDtypeStruct((rows, D_MODEL), F32),
            jax.ShapeDtypeStruct((nseq, carry, 2 * D_FF), F32),
        ),
        grid=(nseq, nt),
        in_specs=[
            pl.BlockSpec((tm, D_MODEL), lambda b, t: (b * nt + t, 0)),
            pl.BlockSpec((tm, D_MODEL), lambda b, t: (b * nt + t, 0)),
        ] + lead_specs + [
            resident((D_MODEL, D_MODEL)),
            small((1, D_MODEL)),
            resident((D_MODEL, 2 * D_FF)),
            small((CONV_FFN, 2 * D_FF)),
            resident((D_FF, D_MODEL)),
            small((1, D_MODEL)),
        ],
        out_specs=(
            pl.BlockSpec((tm, D_MODEL), lambda b, t: (b * nt + t, 0)),
            pl.BlockSpec((1, carry, 2 * D_FF), lambda b, t: (b, 0, 0)),
        ),
        scratch_shapes=[pltpu.VMEM((base + tm, 2 * D_FF), F32)] + lead_scratch,
        compiler_params=pltpu.CompilerParams(dimension_semantics=("arbitrary", "arbitrary"),
                                             vmem_limit_bytes=VMEM_LIMIT),
        name="out_ffn",
    )(x, mix, *lead, wout, nffn, wup, cw, wdn, nfin)


def kernel(x_prompt, x_sample, state_ret, state_gdn, state_conv_qkv, state_ffn_conv, meta_tokens, norm_mix,
           w_in, conv_gdn, gdn_a_log, gdn_dt_bias, norm_ret, norm_gdn, w_out, norm_ffn, w_up, conv_ffn,
           w_down, norm_final):
    depth = w_in.shape[0]
    assert depth == 1
    nbp, seq, _ = x_prompt.shape
    nbs, dec_seq, _ = x_sample.shape
    assert dec_seq <= SAMPLE_PAD and nbs % SAMPLE_GROUP == 0
    assert seq % (2 * PROMPT_CHUNK * PROMPT_CHUNKS_PER_STEP) == 0
    layer = 0

    w_in_bf = _cast_pad_w_in(w_in[layer])
    row = lambda v: v.reshape(1, -1).astype(F32)
    pad_ba = lambda v: jnp.pad(v.astype(F32), (N_HEADS, LANES - 2 * N_HEADS)).reshape(1, LANES)
    mixer_consts = (conv_gdn[layer], pad_ba(gdn_a_log[layer]), pad_ba(gdn_dt_bias[layer]),
                    norm_ret[layer].reshape(N_HEADS, D_HEAD), row(norm_gdn[layer]))
    nmix = row(norm_mix[layer])

    assert seq % N_META == 0 and (seq + N_META) % SAMPLE_PAD == 0
    rope_meta_row0, rope_sample_row0 = seq, seq + N_META
    rope = _rope_tables(seq + N_META + SAMPLE_PAD,
                        [(0, N_META), (rope_meta_row0, 0), (rope_sample_row0, PAST_LEN)])

    small_rows = jnp.concatenate([x_sample.reshape(nbs * dec_seq, D_MODEL), meta_tokens.astype(F32)], axis=0)
    proj_small = _proj(small_rows, nmix, w_in_bf, tm=small_rows.shape[0])
    meta_row0 = nbs * dec_seq

    zero_state = jnp.zeros((1, N_HEADS, D_HEAD, D_HEAD), F32)
    zero_cq = jnp.zeros((1, CONV_GDN - 1, GDN_QKV), F32)
    mix_m, sret_m, sgdn_m, cq_m = _mixer(proj_small, meta_row0, 1, N_META, zero_state, zero_state, zero_cq, rope,
                                         rope_meta_row0, mixer_consts, bb=1, c=N_META, nch=1, n_valid=N_META,
                                         shared_init=True)

    xp = x_prompt.reshape(nbp * seq, D_MODEL)
    mix_p, sret_p, sgdn_p, cq_p, w_out_bf, w_up_bf, w_down_bf = _proj_mixer(
        xp, nmix, w_in_bf, nbp, sret_m, sgdn_m, cq_m, rope, mixer_consts,
        (w_out[layer], w_up[layer], w_down[layer]), c=PROMPT_CHUNK, nch=PROMPT_CHUNKS_PER_STEP)
    ffn_weights = (w_out_bf, row(norm_ffn[layer]), w_up_bf, conv_ffn[layer], w_down_bf, row(norm_final))

    meta_lead = (meta_tokens.astype(F32), mix_m.reshape(N_META, D_MODEL))
    y_p, cf_p = _ffn(xp, mix_p, meta_lead, ffn_weights, nseq=nbp, tm=512, stride=1)
    y_prompt = y_p.reshape(nbp, seq, D_MODEL)

    mix_s, sret_s, sgdn_s, cq_s = _mixer(proj_small, 0, nbs, SAMPLE_PAD, state_ret[layer], state_gdn[layer],
                                         state_conv_qkv[layer], rope, rope_sample_row0, mixer_consts,
                                         bb=SAMPLE_SEQS_PER_STEP, c=SAMPLE_PAD, nch=1, n_valid=dec_seq,
                                         shared_init=False, stored=dec_seq)
    ng = nbs // SAMPLE_GROUP
    to_tmajor = lambda a: a.reshape(ng, SAMPLE_GROUP, a.shape[1], a.shape[2]).transpose(0, 2, 1, 3)
    xs_t = to_tmajor(x_sample).reshape(nbs * dec_seq, D_MODEL)
    mix_t = to_tmajor(mix_s[:, :dec_seq]).reshape(nbs * dec_seq, D_MODEL)
    cf0_t = to_tmajor(state_ffn_conv[layer]).reshape(ng, (CONV_FFN - 1) * SAMPLE_GROUP, 2 * D_FF)
    y_s_t, cf_s_t = _ffn(xs_t, mix_t, cf0_t, ffn_weights, nseq=ng, tm=dec_seq * SAMPLE_GROUP,
                         stride=SAMPLE_GROUP)
    y_sample = y_s_t.reshape(ng, dec_seq, SAMPLE_GROUP, D_MODEL).transpose(0, 2, 1, 3).reshape(nbs, dec_seq, D_MODEL)
    cf_s = cf_s_t.reshape(ng, CONV_FFN - 1, SAMPLE_GROUP, 2 * D_FF).transpose(0, 2, 1, 3).reshape(
        nbs, CONV_FFN - 1, 2 * D_FF)

    return (y_prompt, y_sample, sret_p[None], sgdn_p[None], cq_p[None], cf_p[None],
            sret_s[None], sgdn_s[None], cq_s[None], cf_s[None])
```

```python
import functools

import jax
import numpy as np
import jax.numpy as jnp
from jax import lax
from jax.experimental import pallas as pl
from jax.experimental.pallas import tpu as pltpu

F32 = jnp.float32
BF16 = jnp.bfloat16

D_MODEL = 1024
N_META = 16
PAST_LEN = 16384
N_HEADS = 4
D_HEAD = 128
D_GRP = N_HEADS * D_HEAD
GDN_QKV = 3 * D_GRP
CONV_GDN = 4
CONV_FFN = 3
D_FF = 2816
ROPE_THETA = 10000.0
EPS = 1e-6

OFF_RQ, OFF_RK, OFF_RV, OFF_RG = 0, D_GRP, 2 * D_GRP, 3 * D_GRP
OFF_QKV = 4 * D_GRP
OFF_GG = OFF_QKV + GDN_QKV
OFF_BA = OFF_GG + D_GRP
IN_WIDTH = OFF_BA + 2 * N_HEADS
LANES = 128
SUBLANES = 8
MXU_WIDTH = 256
V7X_VMEM_BYTES = 64 * 1024 * 1024
IN_PAD = OFF_BA + LANES

PROMPT_CHUNK = 128
PROMPT_CHUNKS_PER_STEP = 2
SAMPLE_PAD = 8
SAMPLE_SEQS_PER_STEP = 16
SAMPLE_GROUP = 64
VMEM_LIMIT = V7X_VMEM_BYTES * 7 // 8


def _round_up(n, m):
    return (n + m - 1) // m * m


def _mm(a, b):
    return jnp.dot(a.astype(BF16), b.astype(BF16), preferred_element_type=F32)


def _mm_nt(a, b):
    return lax.dot_general(a.astype(BF16), b.astype(BF16), (((1,), (1,)), ((), ())),
                           preferred_element_type=F32)


def _mm_tn(a, b):
    return lax.dot_general(a.astype(BF16), b.astype(BF16), (((0,), (0,)), ((), ())),
                           preferred_element_type=F32)


def _split3(x):
    p0 = x.astype(BF16)
    r = x - p0.astype(F32)
    p1 = r.astype(BF16)
    p2 = (r - p1.astype(F32)).astype(BF16)
    return p0, p1, p2


def _silu(x):
    return x * jax.nn.sigmoid(x)


def _rms(x, w):
    return x * lax.rsqrt(jnp.mean(x * x, axis=-1, keepdims=True) + EPS) * w


def _rope_kernel(invf_ref, cos_ref, sin_ref, *, segments):
    shape = cos_ref.shape
    r = lax.broadcasted_iota(jnp.int32, shape, 0)
    pos = r + (segments[0][1] - segments[0][0])
    for row0, pos0 in segments[1:]:
        pos = jnp.where(r >= row0, r + (pos0 - row0), pos)
    ang = pos.astype(F32) * invf_ref[...]
    lane = lax.broadcasted_iota(jnp.int32, shape, 1)
    sin = jnp.sin(ang)
    cos_ref[...] = jnp.cos(ang)
    sin_ref[...] = jnp.where(lane < D_HEAD // 2, -sin, sin)


def _rope_tables(n, segments):
    half = D_HEAD // 2
    inv_freq = ROPE_THETA ** (-jnp.arange(half, dtype=F32) / half)
    invf2 = jnp.concatenate([inv_freq, inv_freq]).reshape(1, LANES)
    out = jax.ShapeDtypeStruct((n, LANES), F32)
    return pl.pallas_call(functools.partial(_rope_kernel, segments=tuple(segments)), out_shape=(out, out),
                          name="rope_tables")(invf2)


CAST_COLS = 11 * LANES


def _cast_pad_kernel(wt_ref, o_ref):
    first = pl.program_id(0) * CAST_COLS
    r = lax.broadcasted_iota(jnp.int32, wt_ref.shape, 0)
    blk = jnp.where(first + r < IN_WIDTH, wt_ref[...], 0.0)
    o_ref[...] = blk.T.astype(BF16)


def _cast_pad_w_in(w):
    wt = w.T
    return pl.pallas_call(
        _cast_pad_kernel,
        out_shape=jax.ShapeDtypeStruct((D_MODEL, IN_PAD), BF16),
        grid=(IN_PAD // CAST_COLS,),
        in_specs=[pl.BlockSpec((CAST_COLS, D_MODEL), lambda i: (i, 0))],
        out_specs=pl.BlockSpec((D_MODEL, CAST_COLS), lambda i: (0, i)),
        name="cast_w_in",
    )(wt)


def _proj_kernel(x_ref, nw_ref, w_ref, o_ref, h_ref):
    @pl.when(pl.program_id(0) == 0)
    def _():
        h_ref[...] = _rms(x_ref[...], nw_ref[...]).astype(BF16)

    o_ref[...] = jnp.dot(h_ref[...], w_ref[...], preferred_element_type=F32)


def _proj(x, norm_w, w_bf):
    rows = x.shape[0]
    return pl.pallas_call(
        _proj_kernel,
        out_shape=jax.ShapeDtypeStruct((rows, IN_PAD), F32),
        grid=(IN_PAD // CAST_COLS,),
        in_specs=[
            pl.BlockSpec((rows, D_MODEL), lambda j: (0, 0)),
            pl.BlockSpec((1, D_MODEL), lambda j: (0, 0)),
            pl.BlockSpec((D_MODEL, CAST_COLS), lambda j: (0, j)),
        ],
        out_specs=pl.BlockSpec((rows, CAST_COLS), lambda j: (0, j)),
        scratch_shapes=[pltpu.VMEM((rows, D_MODEL), BF16)],
        compiler_params=pltpu.CompilerParams(dimension_semantics=("arbitrary",),
                                             vmem_limit_bytes=VMEM_LIMIT),
        name="in_proj",
    )(x, norm_w, w_bf)


def _unit_lower_inverses(mats, c, tick=lambda: None, nilpotent=SUBLANES):
    ri = lax.broadcasted_iota(jnp.int32, (c, c), 0)
    ci = lax.broadcasted_iota(jnp.int32, (c, c), 1)
    eye = (ri == ci).astype(F32)
    diag_blk = (ri // SUBLANES) == (ci // SUBLANES)
    ad = [jnp.where(diag_blk, a, 0.0) for a in mats]
    assert 1 <= nilpotent <= SUBLANES
    n_factors = max(0, (nilpotent - 1).bit_length() - 1)
    t = [eye - x for x in ad]
    power = ad
    for i in range(2):
        if i < n_factors:
            power = [_mm(x, x) for x in power]
            t = [x + _mm(x, s) for x, s in zip(t, power)]
        tick()
    tick()
    s = SUBLANES
    while s < c:
        level = ((ri // (2 * s)) == (ci // (2 * s))) & ((ri // s) != (ci // s))
        off = [jnp.where(level, a, 0.0) for a in mats]
        lt = [_mm(o, x) for o, x in zip(off, t)]
        tick()
        t = [x - _mm(x, y) for x, y in zip(t, lt)]
        tick()
        s *= 2
    return t


XS_TOP = SUBLANES


def _mixer_init(sret0_ref, sgdn0_ref, cq0_ref, sret_ref, sgdn_ref, xs_ref, batch_minor=False):
    tail = CONV_GDN - 1
    sret_ref[...] = sret0_ref[...]
    sgdn_ref[...] = sgdn0_ref[...]
    if batch_minor:
        for b in range(xs_ref.shape[0]):
            xs_ref[b, XS_TOP - tail:XS_TOP, :] = cq0_ref[:, b, :]
    else:
        xs_ref[:, XS_TOP - tail:XS_TOP, :] = cq0_ref[...]


def _mixer_block(getp, put_mix, cos2, sin2, const_refs, sret_ref, sgdn_ref, cq_ref, xs_ref, *,
                 bb, c, nch, n_valid, tick=lambda: None, batch_minor=False):
    (tril_ref, triu_ref, dint_ref, qdec_ref, kdec_ref, cdec_ref, cw_ref, alog_ref, dtb_ref, nret_ref,
     ngdn_ref) = const_refs
    rows = nch * c
    tail = CONV_GDN - 1
    top = XS_TOP
    ri = lax.broadcasted_iota(jnp.int32, (c, c), 0)
    cj = lax.broadcasted_iota(jnp.int32, (c, c), 1)
    tri = ri >= cj
    strict = ri > cj
    tril_bf = tril_ref[...]
    triu_bf = triu_ref[...]
    scale = D_HEAD ** -0.5
    heads = range(N_HEADS)
    seqs = range(bb)
    chunks = [slice(j * c, (j + 1) * c) for j in range(nch)]
    join = lambda parts: parts[0] if len(parts) == 1 else jnp.concatenate(parts, axis=0)

    ret = {}
    for b in seqs:
        for h in heads:
            lo = h * D_HEAD
            q = getp(b, OFF_RQ + lo, OFF_RQ + lo + D_HEAD)
            k = getp(b, OFF_RK + lo, OFF_RK + lo + D_HEAD)
            v = getp(b, OFF_RV + lo, OFF_RV + lo + D_HEAD)
            qr = q * cos2 + pltpu.roll(q, D_HEAD // 2, 1) * sin2
            kr = (k * cos2 + pltpu.roll(k, D_HEAD // 2, 1) * sin2) * scale
            ret[b, h] = (qr, kr, v)
    tick()
    rtasks = [(b, h, j) for j in range(nch) for b in seqs for h in heads]
    scores = {t: _mm_nt(ret[t[0], t[1]][0][chunks[t[2]]], ret[t[0], t[1]][1][chunks[t[2]]]) * dint_ref[t[1]]
              for t in rtasks}
    tick()
    intra = {t: _mm(scores[t], ret[t[0], t[1]][2][chunks[t[2]]]) for t in rtasks}
    tick()
    kv = {t: _mm_tn(ret[t[0], t[1]][1][chunks[t[2]]] * kdec_ref[t[1]], ret[t[0], t[1]][2][chunks[t[2]]])
          for t in rtasks}
    tick()
    pairs = [(b, h) for b in seqs for h in heads]
    s_ret = {bh: sret_ref[bh[0], bh[1]] for bh in pairs}
    o_ret = {}
    for j in range(nch):
        for b, h in pairs:
            o_ret[b, h, j] = intra[b, h, j] + _mm(ret[b, h][0][chunks[j]] * qdec_ref[h], s_ret[b, h])
        for b, h in pairs:
            s_ret[b, h] = cdec_ref[h] * s_ret[b, h] + kv[b, h, j]
        tick()
    o_ret = {bh: join([o_ret[bh[0], bh[1], j] for j in range(nch)]) for bh in pairs}
    mu = {bh: jnp.mean(o_ret[bh], axis=-1, keepdims=True) for bh in pairs}
    cen = {bh: o_ret[bh] - mu[bh] for bh in pairs}
    var = {bh: jnp.mean(cen[bh] * cen[bh], axis=-1, keepdims=True) for bh in pairs}
    for b, h in pairs:
        lo = h * D_HEAD
        gate = getp(b, OFF_RG + lo, OFF_RG + lo + D_HEAD)
        o = cen[b, h] * lax.rsqrt(var[b, h] + EPS) * nret_ref[h:h + 1, :]
        put_mix(b, lo, o * _silu(gate))
        sret_ref[b, h] = s_ret[b, h]
    tick()

    qkvs, beta_alls, cum_cols, cum_rows, cum_tots = {}, {}, {}, {}, {}
    ones_bf = jnp.ones((D_HEAD, c), BF16)
    for b in seqs:
        xs_ref[b, top:top + rows, :] = getp(b, OFF_QKV, OFF_QKV + GDN_QKV)
        conv = xs_ref[b, top - tail:top - tail + rows, :] * cw_ref[0:1, :]
        for i in range(1, CONV_GDN):
            conv = conv + xs_ref[b, top - tail + i:top - tail + i + rows, :] * cw_ref[i:i + 1, :]
        last = n_valid if nch == 1 else rows
        new_tail = xs_ref[b, top + last - tail:top + last, :]
        xs_ref[b, top - tail:top, :] = new_tail
        if batch_minor:
            cq_ref[:, b, :] = new_tail
        else:
            cq_ref[b] = new_tail
        qkvs[b] = _silu(conv)
        tick()

        ba = getp(b, OFF_BA, OFF_BA + LANES)
        beta_all = jax.nn.sigmoid(ba)
        z = ba + dtb_ref[...]
        softplus = jnp.maximum(z, 0.0) + jnp.log1p(jnp.exp(-jnp.abs(z)))
        g_all = -jnp.exp(alog_ref[...]) * softplus
        if n_valid < c:
            row = lax.broadcasted_iota(jnp.int32, (rows, LANES), 0)
            rowmask = (row < n_valid).astype(F32)
            beta_all = beta_all * rowmask
            g_all = g_all * rowmask
        beta_alls[b] = beta_all
        for j, rs in enumerate(chunks):
            parts = _split3(g_all[rs])
            cum_cols[b, j] = sum(jnp.dot(tril_bf, g, preferred_element_type=F32) for g in parts)
            cum_tots[b, j] = sum(jnp.dot(ones_bf, g, preferred_element_type=F32) for g in parts)
            cum_rows[b, j] = sum(lax.dot_general(g, triu_bf, (((0,), (0,)), ((), ())),
                                                 preferred_element_type=F32) for g in parts)
    tick()

    tasks = [(j, b, h) for j in range(nch) for b in seqs for h in heads]
    qs, ks, vs, betas, ecums, kdecs, cdecs, amats, qkms = [], [], [], [], [], [], [], [], []
    raw = [(qkvs[b][chunks[j], h * D_HEAD:(h + 1) * D_HEAD],
            qkvs[b][chunks[j], D_GRP + h * D_HEAD:D_GRP + (h + 1) * D_HEAD]) for j, b, h in tasks]
    sumsq = [(jnp.sum(q * q, axis=-1, keepdims=True), jnp.sum(k * k, axis=-1, keepdims=True)) for q, k in raw]
    for i, (j, b, h) in enumerate(tasks):
        rs = chunks[j]
        lo = h * D_HEAD
        v = qkvs[b][rs, 2 * D_GRP + lo:2 * D_GRP + lo + D_HEAD]
        q = raw[i][0] * lax.rsqrt(sumsq[i][0] + EPS) * scale
        k = raw[i][1] * lax.rsqrt(sumsq[i][1] + EPS)
        beta = jnp.broadcast_to(beta_alls[b][rs, h:h + 1], (c, LANES))
        cum = jnp.broadcast_to(cum_cols[b, j][:, N_HEADS + h:N_HEADS + h + 1], (c, LANES))
        cum_row = cum_rows[b, j][N_HEADS + h:N_HEADS + h + 1, :]
        dmask = jnp.exp(jnp.where(tri, cum[:, :c] - cum_row, -jnp.inf))
        cum_last = jnp.broadcast_to(cum_tots[b, j][:, N_HEADS + h:N_HEADS + h + 1], (D_HEAD, LANES))
        kq = _mm_nt(jnp.concatenate([k, q], axis=0), k)
        amats.append(jnp.where(strict, beta[:, :c] * kq[:c] * dmask, 0.0))
        qkms.append(jnp.where(tri, kq[c:] * dmask, 0.0))
        qs.append(q)
        ks.append(k)
        vs.append(v)
        betas.append(beta)
        ecums.append(jnp.exp(cum))
        kdecs.append(jnp.exp(cum_last[:c] - cum))
        cdecs.append(jnp.exp(cum_last))
        if i % N_HEADS == N_HEADS - 1:
            tick()

    tinv = _unit_lower_inverses(amats, c, tick, nilpotent=min(SUBLANES, n_valid))
    eye = (ri == cj).astype(F32)
    sols = []
    for i in range(len(tasks)):
        rhs = jnp.concatenate([vs[i] * betas[i], ks[i] * (betas[i] * ecums[i])], axis=1)
        sols.append(rhs + _mm(tinv[i] - eye, rhs))
    tick()

    s_gdn = {bh: sgdn_ref[bh[0], bh[1]] for bh in pairs}
    o_gdn = {}
    for j in range(nch):
        idx = {tasks[i][1:]: i for i in range(len(tasks)) if tasks[i][0] == j}
        lhs = {bh: jnp.concatenate([sols[i][:, D_HEAD:], qs[i] * ecums[i]], axis=0) for bh, i in idx.items()}
        both = {bh: _mm(lhs[bh], s_gdn[bh]) for bh in idx}
        tick()
        w = {bh: sols[i][:, :D_HEAD] - both[bh][:c] for bh, i in idx.items()}
        for bh, i in idx.items():
            o_gdn[bh + (j,)] = both[bh][c:] + _mm(qkms[i], w[bh])
        tick()
        upd = {bh: _mm_tn(ks[i] * kdecs[i], w[bh]) for bh, i in idx.items()}
        for bh, i in idx.items():
            s_gdn[bh] = cdecs[i] * s_gdn[bh] + upd[bh]
        tick()
    o_gdn = {bh: join([o_gdn[bh + (j,)] for j in range(nch)]) for bh in pairs}
    msq = {bh: jnp.mean(o_gdn[bh] * o_gdn[bh], axis=-1, keepdims=True) for bh in pairs}
    for b, h in pairs:
        lo = h * D_HEAD
        gate = getp(b, OFF_GG + lo, OFF_GG + lo + D_HEAD)
        o = o_gdn[b, h] * lax.rsqrt(msq[b, h] + EPS) * ngdn_ref[...]
        put_mix(b, D_GRP + lo, o * _silu(gate))
        sgdn_ref[b, h] = s_gdn[b, h]


N_MIXER_CONSTS = 11


def _mixer_kernel(p_ref, sret0_ref, sgdn0_ref, cq0_ref, cos_ref, sin_ref, *rest, bb, c, nch, n_valid, stored,
                  batch_minor):
    const_refs, (mix_ref, sret_ref, sgdn_ref, cq_ref, xs_ref) = rest[:N_MIXER_CONSTS], rest[N_MIXER_CONSTS:]
    rows = nch * c

    @pl.when(pl.program_id(1) == 0)
    def _():
        _mixer_init(sret0_ref, sgdn0_ref, cq0_ref, sret_ref, sgdn_ref, xs_ref, batch_minor)

    def put_mix(b, lo, value):
        mix_ref[b, :, lo:lo + D_HEAD] = value.astype(mix_ref.dtype)

    def getp(b, lo, hi):
        blk = p_ref[b * stored:(b + 1) * stored, lo:hi]
        if stored < rows:
            blk = jnp.concatenate([blk, jnp.zeros((rows - stored, hi - lo), F32)], axis=0)
        return blk

    _mixer_block(getp, put_mix, cos_ref[...], sin_ref[...], const_refs, sret_ref, sgdn_ref, cq_ref, xs_ref,
                 bb=bb, c=c, nch=nch, n_valid=n_valid, batch_minor=batch_minor)


PROJ_PANEL = MXU_WIDTH


def _mixer_block_ticks(bb, c, nch):
    levels = (c // SUBLANES).bit_length() - 1
    retention = 4 + nch + 1
    gdn_prep = bb + 1 + bb * nch
    inverse = 3 + 2 * levels
    return retention + gdn_prep + inverse + 1 + 3 * nch


def _proj_mixer_kernel(x0_ref, xa_ref, xb_ref, nw_ref, w_ref, sret0_ref, sgdn0_ref, cq0_ref, cos_ref, sin_ref,
                       *rest, c, nch, steps_per_seq, n_cast):
    const_refs = rest[:N_MIXER_CONSTS]
    cast_in = rest[N_MIXER_CONSTS:N_MIXER_CONSTS + n_cast]
    mix_ref, sret_ref, sgdn_ref, cq_ref = rest[N_MIXER_CONSTS + n_cast:N_MIXER_CONSTS + n_cast + 4]
    cast_out = rest[N_MIXER_CONSTS + n_cast + 4:N_MIXER_CONSTS + 2 * n_cast + 4]
    pja_ref, pjb_ref, xs_ref = rest[N_MIXER_CONSTS + 2 * n_cast + 4:]
    step = pl.program_id(0)
    rows = nch * c

    for src_ref, dst_ref in zip(cast_in, cast_out):
        dst_ref[...] = src_ref[...].astype(dst_ref.dtype)

    def project(x_ref, dst_ref, n_ticks):
        h = _rms(x_ref[...], nw_ref[...]).astype(BF16)
        panels = [(lo, min(lo + PROJ_PANEL, IN_PAD)) for lo in range(0, IN_PAD, PROJ_PANEL)]
        n_panels = len(panels)
        calls = [0]

        def emit():
            lo, hi = panels.pop(0)
            dst_ref[:, lo:hi] = jnp.dot(h, w_ref[:, lo:hi], preferred_element_type=F32)

        def tick():
            calls[0] += 1
            due = min(n_panels, -(-calls[0] * n_panels // n_ticks))
            while n_panels - len(panels) < due:
                emit()

        def flush():
            assert n_ticks == 1 or calls[0] == n_ticks, (calls[0], n_ticks)
            while panels:
                emit()

        return tick, flush

    @pl.when(step == 0)
    def _():
        project(x0_ref, pja_ref, 1)[1]()

    @pl.when(lax.rem(step, steps_per_seq) == 0)
    def _():
        _mixer_init(sret0_ref, sgdn0_ref, cq0_ref, sret_ref, sgdn_ref, xs_ref)

    for half, (cur_ref, x_next_ref, nxt_ref) in enumerate(((pja_ref, xa_ref, pjb_ref), (pjb_ref, xb_ref, pja_ref))):
        r0 = half * rows
        tick, flush = project(x_next_ref, nxt_ref, _mixer_block_ticks(1, c, nch))

        def put_mix(b, lo, value, r0=r0):
            mix_ref[r0:r0 + rows, lo:lo + D_HEAD] = value.astype(mix_ref.dtype)

        _mixer_block(lambda b, lo, hi, cur_ref=cur_ref: cur_ref[:, lo:hi], put_mix,
                     cos_ref[r0:r0 + rows, :], sin_ref[r0:r0 + rows, :], const_refs,
                     sret_ref, sgdn_ref, cq_ref, xs_ref, bb=1, c=c, nch=nch, n_valid=c, tick=tick)
        flush()


def _retention_decay_tables(c, n_valid):
    f32 = np.float32
    lg = np.log1p(-np.power(f32(2.0), f32(-5.0) - np.arange(N_HEADS, dtype=f32)))[:, None].astype(f32)
    idx = np.arange(c, dtype=f32)
    diff = idx[:, None] - idx[None, :]
    dint = np.where(diff[None] >= 0, np.exp(lg[:, :, None] * np.maximum(diff[None], 0)), f32(0.0)).astype(f32)
    qdec = np.exp(lg * (idx + f32(1.0))).astype(f32)
    kdec = np.where(idx[None, :] < n_valid, np.exp(lg * np.minimum(f32(n_valid) - f32(1.0) - idx, c)), f32(0.0))
    cdec = np.exp(lg * f32(n_valid)).astype(f32)
    bc = lambda t: np.broadcast_to(t.astype(f32)[:, :, None], t.shape + (LANES,))
    return dint, bc(qdec), bc(kdec), np.broadcast_to(cdec[:, :, None], (N_HEADS, 1, LANES))


def _mixer_const_operands(c, n_valid, consts):
    cw, alog, dtb, nret, ngdn = consts
    dint, qdec, kdec, cdec = _retention_decay_tables(c, n_valid)
    idx = np.arange(c)
    tril = (idx[:, None] >= idx[None, :]).astype(np.float32)
    arrays = (jnp.asarray(tril, BF16), jnp.asarray(tril.T, BF16), jnp.asarray(dint), jnp.asarray(qdec),
              jnp.asarray(kdec), jnp.asarray(cdec), cw, alog, dtb, nret, ngdn)
    assert len(arrays) == N_MIXER_CONSTS
    return arrays, [a.shape for a in arrays]


def _mixer(proj, row0, nb, length, sret0, sgdn0, cq0, rope, rope_row0, consts, *, bb, c, nch, n_valid,
           shared_init, stored=None, batch_minor=False):
    rows = nch * c
    assert nb % bb == 0 and length % rows == 0 and rope_row0 % rows == 0
    assert not shared_init or bb == 1
    assert n_valid == c or nch == 1
    assert bb == 1 or length == rows
    const_arrays, const_shapes = _mixer_const_operands(c, n_valid, consts)
    nsteps = length // rows
    stored = rows if stored is None else stored
    assert stored == rows or (nsteps == 1 and stored >= n_valid and (bb * stored) % SUBLANES == 0)
    assert row0 % (bb * stored) == 0
    blk0 = row0 // (bb * stored)
    init_idx = (lambda b, i: (0, 0, 0, 0)) if shared_init else (lambda b, i: (b, 0, 0, 0))
    init_idx3 = (lambda b, i: (0, 0, 0)) if shared_init else (lambda b, i: (b, 0, 0))
    whole = lambda shape: pl.BlockSpec(shape, lambda b, i: (0,) * len(shape))
    rope_spec = pl.BlockSpec((rows, LANES), lambda b, i: (rope_row0 // rows + i, 0))
    state_shape = (bb, N_HEADS, D_HEAD, D_HEAD)
    tail = CONV_GDN - 1
    assert not (batch_minor and shared_init)
    cq_shape, cq_block = (nb, tail, GDN_QKV), (bb, tail, GDN_QKV)
    cq_idx = lambda b, i: (b, 0, 0)
    if batch_minor:
        cq_shape, cq_block = (tail, nb, GDN_QKV), (tail, bb, GDN_QKV)
        cq_idx = init_idx3 = lambda b, i: (0, b, 0)
    kern = functools.partial(_mixer_kernel, bb=bb, c=c, nch=nch, n_valid=n_valid, stored=stored,
                             batch_minor=batch_minor)
    return pl.pallas_call(
        kern,
        out_shape=(
            jax.ShapeDtypeStruct((nb, length, D_MODEL), BF16),
            jax.ShapeDtypeStruct((nb, N_HEADS, D_HEAD, D_HEAD), F32),
            jax.ShapeDtypeStruct((nb, N_HEADS, D_HEAD, D_HEAD), F32),
            jax.ShapeDtypeStruct(cq_shape, F32),
        ),
        grid=(nb // bb, nsteps),
        in_specs=[
            pl.BlockSpec((bb * stored, IN_PAD), lambda b, i: (blk0 + b * nsteps + i, 0)),
            pl.BlockSpec(state_shape, init_idx),
            pl.BlockSpec(state_shape, init_idx),
            pl.BlockSpec(cq_block, init_idx3),
            rope_spec,
            rope_spec,
        ] + [whole(shape) for shape in const_shapes],
        out_specs=(
            pl.BlockSpec((bb, rows, D_MODEL), lambda b, i: (b, i, 0)),
            pl.BlockSpec(state_shape, lambda b, i: (b, 0, 0, 0)),
            pl.BlockSpec(state_shape, lambda b, i: (b, 0, 0, 0)),
            pl.BlockSpec(cq_block, cq_idx),
        ),
        scratch_shapes=[pltpu.VMEM((bb, XS_TOP + rows, GDN_QKV), F32)],
        compiler_params=pltpu.CompilerParams(dimension_semantics=("arbitrary", "arbitrary"),
                                             vmem_limit_bytes=VMEM_LIMIT),
        name="mixer",
    )(proj, sret0, sgdn0, cq0, rope[0], rope[1], *const_arrays)


def _row_slab(nrows, nsteps):
    for hold in (1, 2, 4, 8):
        slabs = nsteps // hold
        if nsteps % hold == 0 and nrows % slabs == 0 and (nrows // slabs) % (2 * SUBLANES) == 0:
            return nrows // slabs, hold
    raise ValueError((nrows, nsteps))


def _proj_mixer(x, norm_w, w_bf, nseq, sret0, sgdn0, cq0, rope, consts, to_bf16, *, c, nch):
    rows = nch * c
    total = x.shape[0]
    length = total // nseq
    assert total % nseq == 0 and length % (2 * rows) == 0
    nblk = total // rows
    steps_per_seq = length // (2 * rows)
    const_arrays, const_shapes = _mixer_const_operands(c, c, consts)
    whole = lambda shape, **kw: pl.BlockSpec(shape, lambda s: (0,) * len(shape), **kw)
    rope_spec = pl.BlockSpec((2 * rows, LANES), lambda s: (lax.rem(s, steps_per_seq), 0))
    state_shape = (1, N_HEADS, D_HEAD, D_HEAD)
    tail = CONV_GDN - 1
    nsteps = nblk // 2
    slabs = [_row_slab(w.shape[0], nsteps) for w in to_bf16]
    cast_specs = [pl.BlockSpec((r, w.shape[1]), lambda s, hold=hold: (s // hold, 0))
                  for w, (r, hold) in zip(to_bf16, slabs)]
    kern = functools.partial(_proj_mixer_kernel, c=c, nch=nch, steps_per_seq=steps_per_seq, n_cast=len(to_bf16))
    return pl.pallas_call(
        kern,
        out_shape=(
            jax.ShapeDtypeStruct((total, D_MODEL), BF16),
            jax.ShapeDtypeStruct((nseq, N_HEADS, D_HEAD, D_HEAD), F32),
            jax.ShapeDtypeStruct((nseq, N_HEADS, D_HEAD, D_HEAD), F32),
            jax.ShapeDtypeStruct((nseq, tail, GDN_QKV), F32),
        ) + tuple(jax.ShapeDtypeStruct(w.shape, BF16) for w in to_bf16),
        grid=(nsteps,),
        in_specs=[
            pl.BlockSpec((rows, D_MODEL), lambda s: (0, 0), pipeline_mode=pl.Buffered(1)),
            pl.BlockSpec((rows, D_MODEL), lambda s: (2 * s + 1, 0)),
            pl.BlockSpec((rows, D_MODEL), lambda s: (jnp.minimum(2 * s + 2, nblk - 1), 0)),
            whole((1, D_MODEL)),
            whole((D_MODEL, IN_PAD), pipeline_mode=pl.Buffered(1)),
            whole(state_shape),
            whole(state_shape),
            whole((1, tail, GDN_QKV)),
            rope_spec,
            rope_spec,
        ] + [whole(shape) for shape in const_shapes] + cast_specs,
        out_specs=(
            pl.BlockSpec((2 * rows, D_MODEL), lambda s: (s, 0)),
            pl.BlockSpec(state_shape, lambda s: (s // steps_per_seq, 0, 0, 0)),
            pl.BlockSpec(state_shape, lambda s: (s // steps_per_seq, 0, 0, 0)),
            pl.BlockSpec((1, tail, GDN_QKV), lambda s: (s // steps_per_seq, 0, 0)),
        ) + tuple(cast_specs),
        scratch_shapes=[pltpu.VMEM((rows, IN_PAD), F32), pltpu.VMEM((rows, IN_PAD), F32),
                        pltpu.VMEM((1, XS_TOP + rows, GDN_QKV), F32)],
        compiler_params=pltpu.CompilerParams(dimension_semantics=("arbitrary",),
                                             vmem_limit_bytes=VMEM_LIMIT),
        name="proj_mixer",
    )(x, x, x, norm_w, w_bf, sret0, sgdn0, cq0, rope[0], rope[1], *const_arrays, *to_bf16)


FFN_COL_CHUNK = D_FF // 11


def _ffn_kernel(x_ref, mix_ref, *rest, tm, stride, prefix):
    if prefix:
        (xm_ref, mixm_ref, wout_ref, nffn_ref, wup_ref, cw_ref, wdn_ref, nfin_ref,
         y_ref, tail_ref, full_ref, lead_ref) = rest
    else:
        tail0_ref, wout_ref, nffn_ref, wup_ref, cw_ref, wdn_ref, nfin_ref, y_ref, tail_ref, full_ref = rest
    t = pl.program_id(1)
    carry = (CONV_FFN - 1) * stride
    base = _round_up(carry, SUBLANES)

    def up_project(x, mix):
        x1 = x + jnp.dot(mix, wout_ref[...], preferred_element_type=F32)
        h = _rms(x1, nffn_ref[...]).astype(BF16)
        return x1, jnp.dot(h, wup_ref[...], preferred_element_type=F32)

    if prefix:
        @pl.when((pl.program_id(0) == 0) & (t == 0))
        def _():
            um = up_project(xm_ref[...], mixm_ref[...])[1]
            lead_ref[...] = um[um.shape[0] - carry:, :]

    @pl.when(t == 0)
    def _():
        full_ref[base - carry:base, :] = lead_ref[...] if prefix else tail0_ref[0]

    x1, up = up_project(x_ref[...], mix_ref[...])
    full_ref[base:base + tm, :] = up

    def conv_cols(lo):
        acc = full_ref[base - carry:base - carry + tm, lo:lo + FFN_COL_CHUNK] * cw_ref[0:1, lo:lo + FFN_COL_CHUNK]
        for i in range(1, CONV_FFN):
            r0 = base - carry + i * stride
            acc = acc + full_ref[r0:r0 + tm, lo:lo + FFN_COL_CHUNK] * cw_ref[i:i + 1, lo:lo + FFN_COL_CHUNK]
        return acc

    x2 = x1
    for j in range(D_FF // FFN_COL_CHUNK):
        lo = j * FFN_COL_CHUNK
        act = (_silu(conv_cols(lo)) * conv_cols(D_FF + lo)).astype(BF16)
        x2 = x2 + jnp.dot(act, wdn_ref[lo:lo + FFN_COL_CHUNK, :], preferred_element_type=F32)
    y_ref[...] = _rms(x2, nfin_ref[...])

    new_tail = full_ref[base + tm - carry:base + tm, :]
    full_ref[base - carry:base, :] = new_tail
    tail_ref[0] = new_tail


def _ffn(x, mix, lead, weights, *, nseq, tm, stride):
    wout, nffn, wup, cw, wdn, nfin = weights
    rows = x.shape[0]
    assert rows % (nseq * tm) == 0
    nt = rows // (nseq * tm)
    carry = (CONV_FFN - 1) * stride
    base = _round_up(carry, SUBLANES)
    assert tm >= carry
    resident = lambda shape: pl.BlockSpec(shape, lambda b, t: (0, 0), pipeline_mode=pl.Buffered(1))
    small = lambda shape: pl.BlockSpec(shape, lambda b, t: (0, 0))
    prefix = isinstance(lead, tuple)
    if prefix:
        assert stride == 1 and all(a.shape[0] >= carry for a in lead)
        lead_specs = [small(a.shape) for a in lead]
        lead_scratch = [pltpu.VMEM((carry, 2 * D_FF), F32)]
    else:
        lead = (lead,)
        lead_specs = [pl.BlockSpec((1, carry, 2 * D_FF), lambda b, t: (b, 0, 0))]
        lead_scratch = []
    kern = functools.partial(_ffn_kernel, tm=tm, stride=stride, prefix=prefix)
    return pl.pallas_call(
        kern,
        out_shape=(
            jax.ShapeDtypeStruct((rows, D_MODEL), F32),
            jax.ShapeDtypeStruct((nseq, carry, 2 * D_FF), F32),
        ),
        grid=(nseq, nt),
        in_specs=[
            pl.BlockSpec((tm, D_MODEL), lambda b, t: (b * nt + t, 0)),
            pl.BlockSpec((tm, D_MODEL), lambda b, t: (b * nt + t, 0)),
        ] + lead_specs + [
            resident((D_MODEL, D_MODEL)),
            small((1, D_MODEL)),
            resident((D_MODEL, 2 * D_FF)),
            small((CONV_FFN, 2 * D_FF)),
            resident((D_FF, D_MODEL)),
            small((1, D_MODEL)),
        ],
        out_specs=(
            pl.BlockSpec((tm, D_MODEL), lambda b, t: (b * nt + t, 0)),
            pl.BlockSpec((1, carry, 2 * D_FF), lambda b, t: (b, 0, 0)),
        ),
        scratch_shapes=[pltpu.VMEM((base + tm, 2 * D_FF), F32)] + lead_scratch,
        compiler_params=pltpu.CompilerParams(dimension_semantics=("arbitrary", "arbitrary"),
                                             vmem_limit_bytes=VMEM_LIMIT),
        name="out_ffn",
    )(x, mix, *lead, wout, nffn, wup, cw, wdn, nfin)


def kernel(x_prompt, x_sample, state_ret, state_gdn, state_conv_qkv, state_ffn_conv, meta_tokens, norm_mix,
           w_in, conv_gdn, gdn_a_log, gdn_dt_bias, norm_ret, norm_gdn, w_out, norm_ffn, w_up, conv_ffn,
           w_down, norm_final):
    depth = w_in.shape[0]
    assert depth == 1
    nbp, seq, _ = x_prompt.shape
    nbs, dec_seq, _ = x_sample.shape
    assert dec_seq <= SAMPLE_PAD and nbs % SAMPLE_GROUP == 0
    assert seq % (2 * PROMPT_CHUNK * PROMPT_CHUNKS_PER_STEP) == 0
    layer = 0

    w_in_bf = _cast_pad_w_in(w_in[layer])
    row = lambda v: v.reshape(1, -1).astype(F32)
    pad_ba = lambda v: jnp.pad(v.astype(F32), (N_HEADS, LANES - 2 * N_HEADS)).reshape(1, LANES)
    mixer_consts = (conv_gdn[layer], pad_ba(gdn_a_log[layer]), pad_ba(gdn_dt_bias[layer]),
                    norm_ret[layer].reshape(N_HEADS, D_HEAD), row(norm_gdn[layer]))
    nmix = row(norm_mix[layer])

    assert seq % N_META == 0 and (seq + N_META) % SAMPLE_PAD == 0
    rope_meta_row0, rope_sample_row0 = seq, seq + N_META
    rope = _rope_tables(seq + N_META + SAMPLE_PAD,
                        [(0, N_META), (rope_meta_row0, 0), (rope_sample_row0, PAST_LEN)])

    small_rows = jnp.concatenate([x_sample.reshape(nbs * dec_seq, D_MODEL), meta_tokens.astype(F32)], axis=0)
    proj_small = _proj(small_rows, nmix, w_in_bf)
    meta_row0 = nbs * dec_seq

    zero_state = jnp.zeros((1, N_HEADS, D_HEAD, D_HEAD), F32)
    zero_cq = jnp.zeros((1, CONV_GDN - 1, GDN_QKV), F32)
    mix_m, sret_m, sgdn_m, cq_m = _mixer(proj_small, meta_row0, 1, N_META, zero_state, zero_state, zero_cq, rope,
                                         rope_meta_row0, mixer_consts, bb=1, c=N_META, nch=1, n_valid=N_META,
                                         shared_init=True)

    xp = x_prompt.reshape(nbp * seq, D_MODEL)
    mix_p, sret_p, sgdn_p, cq_p, w_out_bf, w_up_bf, w_down_bf = _proj_mixer(
        xp, nmix, w_in_bf, nbp, sret_m, sgdn_m, cq_m, rope, mixer_consts,
        (w_out[layer], w_up[layer], w_down[layer]), c=PROMPT_CHUNK, nch=PROMPT_CHUNKS_PER_STEP)
    ffn_weights = (w_out_bf, row(norm_ffn[layer]), w_up_bf, conv_ffn[layer], w_down_bf, row(norm_final))

    meta_lead = (meta_tokens.astype(F32), mix_m.reshape(N_META, D_MODEL))
    y_p, cf_p = _ffn(xp, mix_p, meta_lead, ffn_weights, nseq=nbp, tm=512, stride=1)
    y_prompt = y_p.reshape(nbp, seq, D_MODEL)

    mix_s, sret_s, sgdn_s, cq_s = _mixer(proj_small, 0, nbs, SAMPLE_PAD, state_ret[layer], state_gdn[layer],
                                         jnp.swapaxes(state_conv_qkv[layer], 0, 1), rope, rope_sample_row0,
                                         mixer_consts, bb=SAMPLE_SEQS_PER_STEP, c=SAMPLE_PAD, nch=1,
                                         n_valid=dec_seq, shared_init=False, stored=dec_seq, batch_minor=True)
    cq_s = jnp.swapaxes(cq_s, 0, 1)
    ng = nbs // SAMPLE_GROUP
    to_tmajor = lambda a: a.reshape(ng, SAMPLE_GROUP, a.shape[1], a.shape[2]).transpose(0, 2, 1, 3)
    xs_t = to_tmajor(x_sample).reshape(nbs * dec_seq, D_MODEL)
    mix_t = to_tmajor(mix_s[:, :dec_seq]).reshape(nbs * dec_seq, D_MODEL)
    cf0_t = to_tmajor(state_ffn_conv[layer]).reshape(ng, (CONV_FFN - 1) * SAMPLE_GROUP, 2 * D_FF)
    y_s_t, cf_s_t = _ffn(xs_t, mix_t, cf0_t, ffn_weights, nseq=ng, tm=dec_seq * SAMPLE_GROUP,
                         stride=SAMPLE_GROUP)
    y_sample = y_s_t.reshape(ng, dec_seq, SAMPLE_GROUP, D_MODEL).transpose(0, 2, 1, 3).reshape(nbs, dec_seq, D_MODEL)
    cf_s = cf_s_t.reshape(ng, CONV_FFN - 1, SAMPLE_GROUP, 2 * D_FF).transpose(0, 2, 1, 3).reshape(
        nbs, CONV_FFN - 1, 2 * D_FF)

    return (y_prompt, y_sample, sret_p[None], sgdn_p[None], cq_p[None], cf_p[None],
            sret_s[None], sgdn_s[None], cq_s[None], cf_s[None])
```

```python
import functools

import jax
import numpy as np
import jax.numpy as jnp
from jax import lax
from jax.experimental import pallas as pl
from jax.experimental.pallas import tpu as pltpu

F32 = jnp.float32
BF16 = jnp.bfloat16

D_MODEL = 1024
N_META = 16
PAST_LEN = 16384
N_HEADS = 4
D_HEAD = 128
D_GRP = N_HEADS * D_HEAD
GDN_QKV = 3 * D_GRP
CONV_GDN = 4
CONV_FFN = 3
D_FF = 2816
ROPE_THETA = 10000.0
EPS = 1e-6

OFF_RQ, OFF_RK, OFF_RV, OFF_RG = 0, D_GRP, 2 * D_GRP, 3 * D_GRP
OFF_QKV = 4 * D_GRP
OFF_GG = OFF_QKV + GDN_QKV
OFF_BA = OFF_GG + D_GRP
IN_WIDTH = OFF_BA + 2 * N_HEADS
LANES = 128
SUBLANES = 8
MXU_WIDTH = 256
V7X_VMEM_BYTES = 64 * 1024 * 1024
IN_PAD = OFF_BA + LANES

PROMPT_CHUNK = 128
PROMPT_CHUNKS_PER_STEP = 2
SAMPLE_PAD = 8
SAMPLE_SEQS_PER_STEP = 16
SAMPLE_GROUP = 64
VMEM_LIMIT = V7X_VMEM_BYTES * 7 // 8


def _round_up(n, m):
    return (n + m - 1) // m * m


def _mm(a, b):
    return jnp.dot(a.astype(BF16), b.astype(BF16), preferred_element_type=F32)


def _mm_nt(a, b):
    return lax.dot_general(a.astype(BF16), b.astype(BF16), (((1,), (1,)), ((), ())),
                           preferred_element_type=F32)


def _mm_tn(a, b):
    return lax.dot_general(a.astype(BF16), b.astype(BF16), (((0,), (0,)), ((), ())),
                           preferred_element_type=F32)


def _split3(x):
    p0 = x.astype(BF16)
    r = x - p0.astype(F32)
    p1 = r.astype(BF16)
    p2 = (r - p1.astype(F32)).astype(BF16)
    return p0, p1, p2


def _shift_rows(x, lead):
    k = lead.shape[0]
    row = lax.broadcasted_iota(jnp.int32, x.shape, 0)
    out = pltpu.roll(x, k, 0)
    for i in range(k):
        out = jnp.where(row == i, lead[i:i + 1, :], out)
    return out


def _silu(x):
    return x * jax.nn.sigmoid(x)


def _rms(x, w):
    return x * lax.rsqrt(jnp.mean(x * x, axis=-1, keepdims=True) + EPS) * w


def _rope_kernel(invf_ref, cos_ref, sin_ref, *, segments):
    shape = cos_ref.shape
    r = lax.broadcasted_iota(jnp.int32, shape, 0)
    pos = r + (segments[0][1] - segments[0][0])
    for row0, pos0 in segments[1:]:
        pos = jnp.where(r >= row0, r + (pos0 - row0), pos)
    ang = pos.astype(F32) * invf_ref[...]
    lane = lax.broadcasted_iota(jnp.int32, shape, 1)
    sin = jnp.sin(ang)
    cos_ref[...] = jnp.cos(ang)
    sin_ref[...] = jnp.where(lane < D_HEAD // 2, -sin, sin)


def _rope_tables(n, segments):
    half = D_HEAD // 2
    inv_freq = ROPE_THETA ** (-jnp.arange(half, dtype=F32) / half)
    invf2 = jnp.concatenate([inv_freq, inv_freq]).reshape(1, LANES)
    out = jax.ShapeDtypeStruct((n, LANES), F32)
    return pl.pallas_call(functools.partial(_rope_kernel, segments=tuple(segments)), out_shape=(out, out),
                          name="rope_tables")(invf2)


CAST_COLS = 11 * LANES


def _cast_pad_kernel(wt_ref, o_ref):
    first = pl.program_id(0) * CAST_COLS
    r = lax.broadcasted_iota(jnp.int32, wt_ref.shape, 0)
    blk = jnp.where(first + r < IN_WIDTH, wt_ref[...], 0.0)
    o_ref[...] = blk.T.astype(BF16)


def _cast_pad_w_in(w):
    wt = w.T
    return pl.pallas_call(
        _cast_pad_kernel,
        out_shape=jax.ShapeDtypeStruct((D_MODEL, IN_PAD), BF16),
        grid=(IN_PAD // CAST_COLS,),
        in_specs=[pl.BlockSpec((CAST_COLS, D_MODEL), lambda i: (i, 0))],
        out_specs=pl.BlockSpec((D_MODEL, CAST_COLS), lambda i: (0, i)),
        name="cast_w_in",
    )(wt)


def _proj_kernel(x_ref, nw_ref, w_ref, o_ref, h_ref):
    @pl.when(pl.program_id(0) == 0)
    def _():
        h_ref[...] = _rms(x_ref[...], nw_ref[...]).astype(BF16)

    o_ref[...] = jnp.dot(h_ref[...], w_ref[...], preferred_element_type=F32)


def _proj(x, norm_w, w_bf):
    rows = x.shape[0]
    return pl.pallas_call(
        _proj_kernel,
        out_shape=jax.ShapeDtypeStruct((rows, IN_PAD), F32),
        grid=(IN_PAD // CAST_COLS,),
        in_specs=[
            pl.BlockSpec((rows, D_MODEL), lambda j: (0, 0)),
            pl.BlockSpec((1, D_MODEL), lambda j: (0, 0)),
            pl.BlockSpec((D_MODEL, CAST_COLS), lambda j: (0, j)),
        ],
        out_specs=pl.BlockSpec((rows, CAST_COLS), lambda j: (0, j)),
        scratch_shapes=[pltpu.VMEM((rows, D_MODEL), BF16)],
        compiler_params=pltpu.CompilerParams(dimension_semantics=("arbitrary",),
                                             vmem_limit_bytes=VMEM_LIMIT),
        name="in_proj",
    )(x, norm_w, w_bf)


def _unit_lower_inverses(mats, c, tick=lambda: None, nilpotent=SUBLANES):
    ri = lax.broadcasted_iota(jnp.int32, (c, c), 0)
    ci = lax.broadcasted_iota(jnp.int32, (c, c), 1)
    eye = (ri == ci).astype(F32)
    diag_blk = (ri // SUBLANES) == (ci // SUBLANES)
    ad = [jnp.where(diag_blk, a, 0.0) for a in mats]
    assert 1 <= nilpotent <= SUBLANES
    n_factors = max(0, (nilpotent - 1).bit_length() - 1)
    t = [eye - x for x in ad]
    power = ad
    for i in range(2):
        if i < n_factors:
            power = [_mm(x, x) for x in power]
            t = [x + _mm(x, s) for x, s in zip(t, power)]
        tick()
    tick()
    s = SUBLANES
    while s < c:
        level = ((ri // (2 * s)) == (ci // (2 * s))) & ((ri // s) != (ci // s))
        off = [jnp.where(level, a, 0.0) for a in mats]
        lt = [_mm(o, x) for o, x in zip(off, t)]
        tick()
        t = [x - _mm(x, y) for x, y in zip(t, lt)]
        tick()
        s *= 2
    return t


XS_TOP = SUBLANES


def _mixer_init(sret0_ref, sgdn0_ref, cq0_ref, sret_ref, sgdn_ref, xs_ref, batch_minor=False):
    tail = CONV_GDN - 1
    sret_ref[...] = sret0_ref[...]
    sgdn_ref[...] = sgdn0_ref[...]
    if batch_minor:
        for b in range(xs_ref.shape[0]):
            xs_ref[b, XS_TOP - tail:XS_TOP, :] = cq0_ref[:, b, :]
    else:
        xs_ref[:, XS_TOP - tail:XS_TOP, :] = cq0_ref[...]


def _mixer_block(getp, put_mix, cos2, sin2, const_refs, sret_ref, sgdn_ref, cq_ref, xs_ref, *,
                 bb, c, nch, n_valid, tick=lambda: None, batch_minor=False):
    (tril_ref, triu_ref, dint_ref, qdec_ref, kdec_ref, cdec_ref, cw_ref, alog_ref, dtb_ref, nret_ref,
     ngdn_ref) = const_refs
    rows = nch * c
    tail = CONV_GDN - 1
    top = XS_TOP
    ri = lax.broadcasted_iota(jnp.int32, (c, c), 0)
    cj = lax.broadcasted_iota(jnp.int32, (c, c), 1)
    tri = ri >= cj
    strict = ri > cj
    tril_bf = tril_ref[...]
    triu_bf = triu_ref[...]
    scale = D_HEAD ** -0.5
    heads = range(N_HEADS)
    seqs = range(bb)
    chunks = [slice(j * c, (j + 1) * c) for j in range(nch)]
    join = lambda parts: parts[0] if len(parts) == 1 else jnp.concatenate(parts, axis=0)

    ret = {}
    for b in seqs:
        for h in heads:
            lo = h * D_HEAD
            q = getp(b, OFF_RQ + lo, OFF_RQ + lo + D_HEAD)
            k = getp(b, OFF_RK + lo, OFF_RK + lo + D_HEAD)
            v = getp(b, OFF_RV + lo, OFF_RV + lo + D_HEAD)
            qr = q * cos2 + pltpu.roll(q, D_HEAD // 2, 1) * sin2
            kr = (k * cos2 + pltpu.roll(k, D_HEAD // 2, 1) * sin2) * scale
            ret[b, h] = (qr, kr, v)
    tick()
    rtasks = [(b, h, j) for j in range(nch) for b in seqs for h in heads]
    scores = {t: _mm_nt(ret[t[0], t[1]][0][chunks[t[2]]], ret[t[0], t[1]][1][chunks[t[2]]]) * dint_ref[t[1]]
              for t in rtasks}
    tick()
    intra = {t: _mm(scores[t], ret[t[0], t[1]][2][chunks[t[2]]]) for t in rtasks}
    tick()
    kv = {t: _mm_tn(ret[t[0], t[1]][1][chunks[t[2]]] * kdec_ref[t[1]], ret[t[0], t[1]][2][chunks[t[2]]])
          for t in rtasks}
    tick()
    pairs = [(b, h) for b in seqs for h in heads]
    s_ret = {bh: sret_ref[bh[0], bh[1]] for bh in pairs}
    o_ret = {}
    for j in range(nch):
        for b, h in pairs:
            o_ret[b, h, j] = intra[b, h, j] + _mm(ret[b, h][0][chunks[j]] * qdec_ref[h], s_ret[b, h])
        for b, h in pairs:
            s_ret[b, h] = cdec_ref[h] * s_ret[b, h] + kv[b, h, j]
        tick()
    o_ret = {bh: join([o_ret[bh[0], bh[1], j] for j in range(nch)]) for bh in pairs}
    mu = {bh: jnp.mean(o_ret[bh], axis=-1, keepdims=True) for bh in pairs}
    cen = {bh: o_ret[bh] - mu[bh] for bh in pairs}
    var = {bh: jnp.mean(cen[bh] * cen[bh], axis=-1, keepdims=True) for bh in pairs}
    for b, h in pairs:
        lo = h * D_HEAD
        gate = getp(b, OFF_RG + lo, OFF_RG + lo + D_HEAD)
        o = cen[b, h] * lax.rsqrt(var[b, h] + EPS) * nret_ref[h:h + 1, :]
        put_mix(b, lo, o * _silu(gate))
        sret_ref[b, h] = s_ret[b, h]
    tick()

    qkvs, beta_alls, cum_cols, cum_rows, cum_tots = {}, {}, {}, {}, {}
    ones_bf = jnp.ones((D_HEAD, c), BF16)
    for b in seqs:
        assert CONV_GDN == 4
        w0, w1, w2, w3 = (cw_ref[i:i + 1, :] for i in range(CONV_GDN))
        u0 = getp(b, OFF_QKV, OFF_QKV + GDN_QKV)
        xs_ref[b, top:top + rows, :] = u0
        u1 = _shift_rows(u0, xs_ref[b, top - 1:top, :])
        p = u0 * w1 + u1 * w0
        p_lead = xs_ref[b, top - 2:top, :] * w1 + xs_ref[b, top - 3:top - 1, :] * w0
        conv = u0 * w3 + u1 * w2 + _shift_rows(p, p_lead)
        last = n_valid if nch == 1 else rows
        new_tail = xs_ref[b, top + last - tail:top + last, :]
        xs_ref[b, top - tail:top, :] = new_tail
        if batch_minor:
            cq_ref[:, b, :] = new_tail
        else:
            cq_ref[b] = new_tail
        qkvs[b] = _silu(conv)
        tick()

        ba = getp(b, OFF_BA, OFF_BA + LANES)
        beta_all = jax.nn.sigmoid(ba)
        z = ba + dtb_ref[...]
        softplus = jnp.maximum(z, 0.0) + jnp.log1p(jnp.exp(-jnp.abs(z)))
        g_all = -jnp.exp(alog_ref[...]) * softplus
        if n_valid < c:
            row = lax.broadcasted_iota(jnp.int32, (rows, LANES), 0)
            rowmask = (row < n_valid).astype(F32)
            beta_all = beta_all * rowmask
            g_all = g_all * rowmask
        beta_alls[b] = beta_all
        for j, rs in enumerate(chunks):
            parts = _split3(g_all[rs])
            cum_cols[b, j] = sum(jnp.dot(tril_bf, g, preferred_element_type=F32) for g in parts)
            cum_tots[b, j] = sum(jnp.dot(ones_bf, g, preferred_element_type=F32) for g in parts)
            cum_rows[b, j] = sum(lax.dot_general(g, triu_bf, (((0,), (0,)), ((), ())),
                                                 preferred_element_type=F32) for g in parts)
    tick()

    tasks = [(j, b, h) for j in range(nch) for b in seqs for h in heads]
    qs, ks, vs, betas, ecums, kdecs, cdecs, amats, qkms = [], [], [], [], [], [], [], [], []
    raw = [(qkvs[b][chunks[j], h * D_HEAD:(h + 1) * D_HEAD],
            qkvs[b][chunks[j], D_GRP + h * D_HEAD:D_GRP + (h + 1) * D_HEAD]) for j, b, h in tasks]
    sumsq = [(jnp.sum(q * q, axis=-1, keepdims=True), jnp.sum(k * k, axis=-1, keepdims=True)) for q, k in raw]
    for i, (j, b, h) in enumerate(tasks):
        rs = chunks[j]
        lo = h * D_HEAD
        v = qkvs[b][rs, 2 * D_GRP + lo:2 * D_GRP + lo + D_HEAD]
        q = raw[i][0] * lax.rsqrt(sumsq[i][0] + EPS) * scale
        k = raw[i][1] * lax.rsqrt(sumsq[i][1] + EPS)
        beta = jnp.broadcast_to(beta_alls[b][rs, h:h + 1], (c, LANES))
        cum = jnp.broadcast_to(cum_cols[b, j][:, N_HEADS + h:N_HEADS + h + 1], (c, LANES))
        cum_row = cum_rows[b, j][N_HEADS + h:N_HEADS + h + 1, :]
        dmask = jnp.exp(jnp.where(tri, cum[:, :c] - cum_row, -jnp.inf))
        cum_last = jnp.broadcast_to(cum_tots[b, j][:, N_HEADS + h:N_HEADS + h + 1], (D_HEAD, LANES))
        kq = _mm_nt(jnp.concatenate([k, q], axis=0), k)
        amats.append(jnp.where(strict, beta[:, :c] * kq[:c] * dmask, 0.0))
        qkms.append(jnp.where(tri, kq[c:] * dmask, 0.0))
        qs.append(q)
        ks.append(k)
        vs.append(v)
        betas.append(beta)
        ecums.append(jnp.exp(cum))
        kdecs.append(jnp.exp(cum_last[:c] - cum))
        cdecs.append(jnp.exp(cum_last))
        if i % N_HEADS == N_HEADS - 1:
            tick()

    tinv = _unit_lower_inverses(amats, c, tick, nilpotent=min(SUBLANES, n_valid))
    eye = (ri == cj).astype(F32)
    sols = []
    for i in range(len(tasks)):
        rhs = jnp.concatenate([vs[i] * betas[i], ks[i] * (betas[i] * ecums[i])], axis=1)
        sols.append(rhs + _mm(tinv[i] - eye, rhs))
    tick()

    s_gdn = {bh: sgdn_ref[bh[0], bh[1]] for bh in pairs}
    o_gdn = {}
    for j in range(nch):
        idx = {tasks[i][1:]: i for i in range(len(tasks)) if tasks[i][0] == j}
        lhs = {bh: jnp.concatenate([sols[i][:, D_HEAD:], qs[i] * ecums[i]], axis=0) for bh, i in idx.items()}
        both = {bh: _mm(lhs[bh], s_gdn[bh]) for bh in idx}
        tick()
        w = {bh: sols[i][:, :D_HEAD] - both[bh][:c] for bh, i in idx.items()}
        for bh, i in idx.items():
            o_gdn[bh + (j,)] = both[bh][c:] + _mm(qkms[i], w[bh])
        tick()
        upd = {bh: _mm_tn(ks[i] * kdecs[i], w[bh]) for bh, i in idx.items()}
        for bh, i in idx.items():
            s_gdn[bh] = cdecs[i] * s_gdn[bh] + upd[bh]
        tick()
    o_gdn = {bh: join([o_gdn[bh + (j,)] for j in range(nch)]) for bh in pairs}
    msq = {bh: jnp.mean(o_gdn[bh] * o_gdn[bh], axis=-1, keepdims=True) for bh in pairs}
    for b, h in pairs:
        lo = h * D_HEAD
        gate = getp(b, OFF_GG + lo, OFF_GG + lo + D_HEAD)
        o = o_gdn[b, h] * lax.rsqrt(msq[b, h] + EPS) * ngdn_ref[...]
        put_mix(b, D_GRP + lo, o * _silu(gate))
        sgdn_ref[b, h] = s_gdn[b, h]


N_MIXER_CONSTS = 11


def _mixer_kernel(p_ref, sret0_ref, sgdn0_ref, cq0_ref, cos_ref, sin_ref, *rest, bb, c, nch, n_valid, stored,
                  batch_minor):
    const_refs, (mix_ref, sret_ref, sgdn_ref, cq_ref, xs_ref) = rest[:N_MIXER_CONSTS], rest[N_MIXER_CONSTS:]
    rows = nch * c

    @pl.when(pl.program_id(1) == 0)
    def _():
        _mixer_init(sret0_ref, sgdn0_ref, cq0_ref, sret_ref, sgdn_ref, xs_ref, batch_minor)

    def put_mix(b, lo, value):
        mix_ref[b, :, lo:lo + D_HEAD] = value.astype(mix_ref.dtype)

    def getp(b, lo, hi):
        blk = p_ref[b * stored:(b + 1) * stored, lo:hi]
        if stored < rows:
            blk = jnp.concatenate([blk, jnp.zeros((rows - stored, hi - lo), F32)], axis=0)
        return blk

    _mixer_block(getp, put_mix, cos_ref[...], sin_ref[...], const_refs, sret_ref, sgdn_ref, cq_ref, xs_ref,
                 bb=bb, c=c, nch=nch, n_valid=n_valid, batch_minor=batch_minor)


PROJ_PANEL = MXU_WIDTH


def _mixer_block_ticks(bb, c, nch):
    levels = (c // SUBLANES).bit_length() - 1
    retention = 4 + nch + 1
    gdn_prep = bb + 1 + bb * nch
    inverse = 3 + 2 * levels
    return retention + gdn_prep + inverse + 1 + 3 * nch


def _proj_mixer_kernel(x0_ref, xa_ref, xb_ref, nw_ref, w_ref, sret0_ref, sgdn0_ref, cq0_ref, cos_ref, sin_ref,
                       *rest, c, nch, steps_per_seq, n_cast):
    const_refs = rest[:N_MIXER_CONSTS]
    cast_in = rest[N_MIXER_CONSTS:N_MIXER_CONSTS + n_cast]
    mix_ref, sret_ref, sgdn_ref, cq_ref = rest[N_MIXER_CONSTS + n_cast:N_MIXER_CONSTS + n_cast + 4]
    cast_out = rest[N_MIXER_CONSTS + n_cast + 4:N_MIXER_CONSTS + 2 * n_cast + 4]
    pja_ref, pjb_ref, xs_ref = rest[N_MIXER_CONSTS + 2 * n_cast + 4:]
    step = pl.program_id(0)
    rows = nch * c

    for src_ref, dst_ref in zip(cast_in, cast_out):
        dst_ref[...] = src_ref[...].astype(dst_ref.dtype)

    def project(x_ref, dst_ref, n_ticks):
        h = _rms(x_ref[...], nw_ref[...]).astype(BF16)
        panels = [(lo, min(lo + PROJ_PANEL, IN_PAD)) for lo in range(0, IN_PAD, PROJ_PANEL)]
        n_panels = len(panels)
        calls = [0]

        def emit():
            lo, hi = panels.pop(0)
            dst_ref[:, lo:hi] = jnp.dot(h, w_ref[:, lo:hi], preferred_element_type=F32)

        def tick():
            calls[0] += 1
            due = min(n_panels, -(-calls[0] * n_panels // n_ticks))
            while n_panels - len(panels) < due:
                emit()

        def flush():
            assert n_ticks == 1 or calls[0] == n_ticks, (calls[0], n_ticks)
            while panels:
                emit()

        return tick, flush

    @pl.when(step == 0)
    def _():
        project(x0_ref, pja_ref, 1)[1]()

    @pl.when(lax.rem(step, steps_per_seq) == 0)
    def _():
        _mixer_init(sret0_ref, sgdn0_ref, cq0_ref, sret_ref, sgdn_ref, xs_ref)

    for half, (cur_ref, x_next_ref, nxt_ref) in enumerate(((pja_ref, xa_ref, pjb_ref), (pjb_ref, xb_ref, pja_ref))):
        r0 = half * rows
        tick, flush = project(x_next_ref, nxt_ref, _mixer_block_ticks(1, c, nch))

        def put_mix(b, lo, value, r0=r0):
            mix_ref[r0:r0 + rows, lo:lo + D_HEAD] = value.astype(mix_ref.dtype)

        _mixer_block(lambda b, lo, hi, cur_ref=cur_ref: cur_ref[:, lo:hi], put_mix,
                     cos_ref[r0:r0 + rows, :], sin_ref[r0:r0 + rows, :], const_refs,
                     sret_ref, sgdn_ref, cq_ref, xs_ref, bb=1, c=c, nch=nch, n_valid=c, tick=tick)
        flush()


def _retention_decay_tables(c, n_valid):
    f32 = np.float32
    lg = np.log1p(-np.power(f32(2.0), f32(-5.0) - np.arange(N_HEADS, dtype=f32)))[:, None].astype(f32)
    idx = np.arange(c, dtype=f32)
    diff = idx[:, None] - idx[None, :]
    dint = np.where(diff[None] >= 0, np.exp(lg[:, :, None] * np.maximum(diff[None], 0)), f32(0.0)).astype(f32)
    qdec = np.exp(lg * (idx + f32(1.0))).astype(f32)
    kdec = np.where(idx[None, :] < n_valid, np.exp(lg * np.minimum(f32(n_valid) - f32(1.0) - idx, c)), f32(0.0))
    cdec = np.exp(lg * f32(n_valid)).astype(f32)
    bc = lambda t: np.broadcast_to(t.astype(f32)[:, :, None], t.shape + (LANES,))
    return dint, bc(qdec), bc(kdec), np.broadcast_to(cdec[:, :, None], (N_HEADS, 1, LANES))


def _mixer_const_operands(c, n_valid, consts):
    cw, alog, dtb, nret, ngdn = consts
    dint, qdec, kdec, cdec = _retention_decay_tables(c, n_valid)
    idx = np.arange(c)
    tril = (idx[:, None] >= idx[None, :]).astype(np.float32)
    arrays = (jnp.asarray(tril, BF16), jnp.asarray(tril.T, BF16), jnp.asarray(dint), jnp.asarray(qdec),
              jnp.asarray(kdec), jnp.asarray(cdec), cw, alog, dtb, nret, ngdn)
    assert len(arrays) == N_MIXER_CONSTS
    return arrays, [a.shape for a in arrays]


def _mixer(proj, row0, nb, length, sret0, sgdn0, cq0, rope, rope_row0, consts, *, bb, c, nch, n_valid,
           shared_init, stored=None, batch_minor=False):
    rows = nch * c
    assert nb % bb == 0 and length % rows == 0 and rope_row0 % rows == 0
    assert not shared_init or bb == 1
    assert n_valid == c or nch == 1
    assert bb == 1 or length == rows
    const_arrays, const_shapes = _mixer_const_operands(c, n_valid, consts)
    nsteps = length // rows
    stored = rows if stored is None else stored
    assert stored == rows or (nsteps == 1 and stored >= n_valid and (bb * stored) % SUBLANES == 0)
    assert row0 % (bb * stored) == 0
    blk0 = row0 // (bb * stored)
    init_idx = (lambda b, i: (0, 0, 0, 0)) if shared_init else (lambda b, i: (b, 0, 0, 0))
    init_idx3 = (lambda b, i: (0, 0, 0)) if shared_init else (lambda b, i: (b, 0, 0))
    whole = lambda shape: pl.BlockSpec(shape, lambda b, i: (0,) * len(shape))
    rope_spec = pl.BlockSpec((rows, LANES), lambda b, i: (rope_row0 // rows + i, 0))
    state_shape = (bb, N_HEADS, D_HEAD, D_HEAD)
    tail = CONV_GDN - 1
    assert not (batch_minor and shared_init)
    cq_shape, cq_block = (nb, tail, GDN_QKV), (bb, tail, GDN_QKV)
    cq_idx = lambda b, i: (b, 0, 0)
    if batch_minor:
        cq_shape, cq_block = (tail, nb, GDN_QKV), (tail, bb, GDN_QKV)
        cq_idx = init_idx3 = lambda b, i: (0, b, 0)
    kern = functools.partial(_mixer_kernel, bb=bb, c=c, nch=nch, n_valid=n_valid, stored=stored,
                             batch_minor=batch_minor)
    return pl.pallas_call(
        kern,
        out_shape=(
            jax.ShapeDtypeStruct((nb, length, D_MODEL), BF16),
            jax.ShapeDtypeStruct((nb, N_HEADS, D_HEAD, D_HEAD), F32),
            jax.ShapeDtypeStruct((nb, N_HEADS, D_HEAD, D_HEAD), F32),
            jax.ShapeDtypeStruct(cq_shape, F32),
        ),
        grid=(nb // bb, nsteps),
        in_specs=[
            pl.BlockSpec((bb * stored, IN_PAD), lambda b, i: (blk0 + b * nsteps + i, 0)),
            pl.BlockSpec(state_shape, init_idx),
            pl.BlockSpec(state_shape, init_idx),
            pl.BlockSpec(cq_block, init_idx3),
            rope_spec,
            rope_spec,
        ] + [whole(shape) for shape in const_shapes],
        out_specs=(
            pl.BlockSpec((bb, rows, D_MODEL), lambda b, i: (b, i, 0)),
            pl.BlockSpec(state_shape, lambda b, i: (b, 0, 0, 0)),
            pl.BlockSpec(state_shape, lambda b, i: (b, 0, 0, 0)),
            pl.BlockSpec(cq_block, cq_idx),
        ),
        scratch_shapes=[pltpu.VMEM((bb, XS_TOP + rows, GDN_QKV), F32)],
        compiler_params=pltpu.CompilerParams(dimension_semantics=("arbitrary", "arbitrary"),
                                             vmem_limit_bytes=VMEM_LIMIT),
        name="mixer",
    )(proj, sret0, sgdn0, cq0, rope[0], rope[1], *const_arrays)


def _row_slab(nrows, nsteps):
    for hold in (1, 2, 4, 8):
        slabs = nsteps // hold
        if nsteps % hold == 0 and nrows % slabs == 0 and (nrows // slabs) % (2 * SUBLANES) == 0:
            return nrows // slabs, hold
    raise ValueError((nrows, nsteps))


def _proj_mixer(x, norm_w, w_bf, nseq, sret0, sgdn0, cq0, rope, consts, to_bf16, *, c, nch):
    rows = nch * c
    total = x.shape[0]
    length = total // nseq
    assert total % nseq == 0 and length % (2 * rows) == 0
    nblk = total // rows
    steps_per_seq = length // (2 * rows)
    const_arrays, const_shapes = _mixer_const_operands(c, c, consts)
    whole = lambda shape, **kw: pl.BlockSpec(shape, lambda s: (0,) * len(shape), **kw)
    rope_spec = pl.BlockSpec((2 * rows, LANES), lambda s: (lax.rem(s, steps_per_seq), 0))
    state_shape = (1, N_HEADS, D_HEAD, D_HEAD)
    tail = CONV_GDN - 1
    nsteps = nblk // 2
    slabs = [_row_slab(w.shape[0], nsteps) for w in to_bf16]
    cast_specs = [pl.BlockSpec((r, w.shape[1]), lambda s, hold=hold: (s // hold, 0))
                  for w, (r, hold) in zip(to_bf16, slabs)]
    kern = functools.partial(_proj_mixer_kernel, c=c, nch=nch, steps_per_seq=steps_per_seq, n_cast=len(to_bf16))
    return pl.pallas_call(
        kern,
        out_shape=(
            jax.ShapeDtypeStruct((total, D_MODEL), BF16),
            jax.ShapeDtypeStruct((nseq, N_HEADS, D_HEAD, D_HEAD), F32),
            jax.ShapeDtypeStruct((nseq, N_HEADS, D_HEAD, D_HEAD), F32),
            jax.ShapeDtypeStruct((nseq, tail, GDN_QKV), F32),
        ) + tuple(jax.ShapeDtypeStruct(w.shape, BF16) for w in to_bf16),
        grid=(nsteps,),
        in_specs=[
            pl.BlockSpec((rows, D_MODEL), lambda s: (0, 0), pipeline_mode=pl.Buffered(1)),
            pl.BlockSpec((rows, D_MODEL), lambda s: (2 * s + 1, 0)),
            pl.BlockSpec((rows, D_MODEL), lambda s: (jnp.minimum(2 * s + 2, nblk - 1), 0)),
            whole((1, D_MODEL)),
            whole((D_MODEL, IN_PAD), pipeline_mode=pl.Buffered(1)),
            whole(state_shape),
            whole(state_shape),
            whole((1, tail, GDN_QKV)),
            rope_spec,
            rope_spec,
        ] + [whole(shape) for shape in const_shapes] + cast_specs,
        out_specs=(
            pl.BlockSpec((2 * rows, D_MODEL), lambda s: (s, 0)),
            pl.BlockSpec(state_shape, lambda s: (s // steps_per_seq, 0, 0, 0)),
            pl.BlockSpec(state_shape, lambda s: (s // steps_per_seq, 0, 0, 0)),
            pl.BlockSpec((1, tail, GDN_QKV), lambda s: (s // steps_per_seq, 0, 0)),
        ) + tuple(cast_specs),
        scratch_shapes=[pltpu.VMEM((rows, IN_PAD), F32), pltpu.VMEM((rows, IN_PAD), F32),
                        pltpu.VMEM((1, XS_TOP + rows, GDN_QKV), F32)],
        compiler_params=pltpu.CompilerParams(dimension_semantics=("arbitrary",),
                                             vmem_limit_bytes=VMEM_LIMIT),
        name="proj_mixer",
    )(x, x, x, norm_w, w_bf, sret0, sgdn0, cq0, rope[0], rope[1], *const_arrays, *to_bf16)


FFN_COL_CHUNK = D_FF // 11


def _ffn_kernel(x_ref, mix_ref, *rest, tm, stride, prefix):
    if prefix:
        (xm_ref, mixm_ref, wout_ref, nffn_ref, wup_ref, cw_ref, wdn_ref, nfin_ref,
         y_ref, tail_ref, full_ref, lead_ref) = rest
    else:
        tail0_ref, wout_ref, nffn_ref, wup_ref, cw_ref, wdn_ref, nfin_ref, y_ref, tail_ref, full_ref = rest
    t = pl.program_id(1)
    carry = (CONV_FFN - 1) * stride
    base = _round_up(carry, SUBLANES)

    def up_project(x, mix):
        x1 = x + jnp.dot(mix, wout_ref[...], preferred_element_type=F32)
        h = _rms(x1, nffn_ref[...]).astype(BF16)
        return x1, jnp.dot(h, wup_ref[...], preferred_element_type=F32)

    if prefix:
        @pl.when((pl.program_id(0) == 0) & (t == 0))
        def _():
            um = up_project(xm_ref[...], mixm_ref[...])[1]
            lead_ref[...] = um[um.shape[0] - carry:, :]

    @pl.when(t == 0)
    def _():
        full_ref[base - carry:base, :] = lead_ref[...] if prefix else tail0_ref[0]

    x1, up = up_project(x_ref[...], mix_ref[...])
    full_ref[base:base + tm, :] = up

    def conv_cols(lo):
        acc = full_ref[base - carry:base - carry + tm, lo:lo + FFN_COL_CHUNK] * cw_ref[0:1, lo:lo + FFN_COL_CHUNK]
        for i in range(1, CONV_FFN):
            r0 = base - carry + i * stride
            acc = acc + full_ref[r0:r0 + tm, lo:lo + FFN_COL_CHUNK] * cw_ref[i:i + 1, lo:lo + FFN_COL_CHUNK]
        return acc

    x2 = x1
    for j in range(D_FF // FFN_COL_CHUNK):
        lo = j * FFN_COL_CHUNK
        act = (_silu(conv_cols(lo)) * conv_cols(D_FF + lo)).astype(BF16)
        x2 = x2 + jnp.dot(act, wdn_ref[lo:lo + FFN_COL_CHUNK, :], preferred_element_type=F32)
    y_ref[...] = _rms(x2, nfin_ref[...])

    new_tail = full_ref[base + tm - carry:base + tm, :]
    full_ref[base - carry:base, :] = new_tail
    tail_ref[0] = new_tail


def _ffn(x, mix, lead, weights, *, nseq, tm, stride):
    wout, nffn, wup, cw, wdn, nfin = weights
    rows = x.shape[0]
    assert rows % (nseq * tm) == 0
    nt = rows // (nseq * tm)
    carry = (CONV_FFN - 1) * stride
    base = _round_up(carry, SUBLANES)
    assert tm >= carry
    resident = lambda shape: pl.BlockSpec(shape, lambda b, t: (0, 0), pipeline_mode=pl.Buffered(1))
    small = lambda shape: pl.BlockSpec(shape, lambda b, t: (0, 0))
    prefix = isinstance(lead, tuple)
    if prefix:
        assert stride == 1 and all(a.shape[0] >= carry for a in lead)
        lead_specs = [small(a.shape) for a in lead]
        lead_scratch = [pltpu.VMEM((carry, 2 * D_FF), F32)]
    else:
        lead = (lead,)
        lead_specs = [pl.BlockSpec((1, carry, 2 * D_FF), lambda b, t: (b, 0, 0))]
        lead_scratch = []
    kern = functools.partial(_ffn_kernel, tm=tm, stride=stride, prefix=prefix)
    return pl.pallas_call(
        kern,
        out_shape=(
            jax.ShapeDtypeStruct((rows, D_MODEL), F32),
            jax.ShapeDtypeStruct((nseq, carry, 2 * D_FF), F32),
        ),
        grid=(nseq, nt),
        in_specs=[
            pl.BlockSpec((tm, D_MODEL), lambda b, t: (b * nt + t, 0)),
            pl.BlockSpec((tm, D_MODEL), lambda b, t: (b * nt + t, 0)),
        ] + lead_specs + [
            resident((D_MODEL, D_MODEL)),
            small((1, D_MODEL)),
            resident((D_MODEL, 2 * D_FF)),
            small((CONV_FFN, 2 * D_FF)),
            resident((D_FF, D_MODEL)),
            small((1, D_MODEL)),
        ],
        out_specs=(
            pl.BlockSpec((tm, D_MODEL), lambda b, t: (b * nt + t, 0)),
            pl.BlockSpec((1, carry, 2 * D_FF), lambda b, t: (b, 0, 0)),
        ),
        scratch_shapes=[pltpu.VMEM((base + tm, 2 * D_FF), F32)] + lead_scratch,
        compiler_params=pltpu.CompilerParams(dimension_semantics=("arbitrary", "arbitrary"),
                                             vmem_limit_bytes=VMEM_LIMIT),
        name="out_ffn",
    )(x, mix, *lead, wout, nffn, wup, cw, wdn, nfin)


def kernel(x_prompt, x_sample, state_ret, state_gdn, state_conv_qkv, state_ffn_conv, meta_tokens, norm_mix,
           w_in, conv_gdn, gdn_a_log, gdn_dt_bias, norm_ret, norm_gdn, w_out, norm_ffn, w_up, conv_ffn,
           w_down, norm_final):
    depth = w_in.shape[0]
    assert depth == 1
    nbp, seq, _ = x_prompt.shape
    nbs, dec_seq, _ = x_sample.shape
    assert dec_seq <= SAMPLE_PAD and nbs % SAMPLE_GROUP == 0
    assert seq % (2 * PROMPT_CHUNK * PROMPT_CHUNKS_PER_STEP) == 0
    layer = 0

    w_in_bf = _cast_pad_w_in(w_in[layer])
    row = lambda v: v.reshape(1, -1).astype(F32)
    pad_ba = lambda v: jnp.pad(v.astype(F32), (N_HEADS, LANES - 2 * N_HEADS)).reshape(1, LANES)
    mixer_consts = (conv_gdn[layer], pad_ba(gdn_a_log[layer]), pad_ba(gdn_dt_bias[layer]),
                    norm_ret[layer].reshape(N_HEADS, D_HEAD), row(norm_gdn[layer]))
    nmix = row(norm_mix[layer])

    assert seq % N_META == 0 and (seq + N_META) % SAMPLE_PAD == 0
    rope_meta_row0, rope_sample_row0 = seq, seq + N_META
    rope = _rope_tables(seq + N_META + SAMPLE_PAD,
                        [(0, N_META), (rope_meta_row0, 0), (rope_sample_row0, PAST_LEN)])

    small_rows = jnp.concatenate([x_sample.reshape(nbs * dec_seq, D_MODEL), meta_tokens.astype(F32)], axis=0)
    proj_small = _proj(small_rows, nmix, w_in_bf)
    meta_row0 = nbs * dec_seq

    zero_state = jnp.zeros((1, N_HEADS, D_HEAD, D_HEAD), F32)
    zero_cq = jnp.zeros((1, CONV_GDN - 1, GDN_QKV), F32)
    mix_m, sret_m, sgdn_m, cq_m = _mixer(proj_small, meta_row0, 1, N_META, zero_state, zero_state, zero_cq, rope,
                                         rope_meta_row0, mixer_consts, bb=1, c=N_META, nch=1, n_valid=N_META,
                                         shared_init=True)

    xp = x_prompt.reshape(nbp * seq, D_MODEL)
    mix_p, sret_p, sgdn_p, cq_p, w_out_bf, w_up_bf, w_down_bf = _proj_mixer(
        xp, nmix, w_in_bf, nbp, sret_m, sgdn_m, cq_m, rope, mixer_consts,
        (w_out[layer], w_up[layer], w_down[layer]), c=PROMPT_CHUNK, nch=PROMPT_CHUNKS_PER_STEP)
    ffn_weights = (w_out_bf, row(norm_ffn[layer]), w_up_bf, conv_ffn[layer], w_down_bf, row(norm_final))

    meta_lead = (meta_tokens.astype(F32), mix_m.reshape(N_META, D_MODEL))
    y_p, cf_p = _ffn(xp, mix_p, meta_lead, ffn_weights, nseq=nbp, tm=512, stride=1)
    y_prompt = y_p.reshape(nbp, seq, D_MODEL)

    mix_s, sret_s, sgdn_s, cq_s = _mixer(proj_small, 0, nbs, SAMPLE_PAD, state_ret[layer], state_gdn[layer],
                                         jnp.swapaxes(state_conv_qkv[layer], 0, 1), rope, rope_sample_row0,
                                         mixer_consts, bb=SAMPLE_SEQS_PER_STEP, c=SAMPLE_PAD, nch=1,
                                         n_valid=dec_seq, shared_init=False, stored=dec_seq, batch_minor=True)
    cq_s = jnp.swapaxes(cq_s, 0, 1)
    ng = nbs // SAMPLE_GROUP
    to_tmajor = lambda a: a.reshape(ng, SAMPLE_GROUP, a.shape[1], a.shape[2]).transpose(0, 2, 1, 3)
    xs_t = to_tmajor(x_sample).reshape(nbs * dec_seq, D_MODEL)
    mix_t = to_tmajor(mix_s[:, :dec_seq]).reshape(nbs * dec_seq, D_MODEL)
    cf0_t = to_tmajor(state_ffn_conv[layer]).reshape(ng, (CONV_FFN - 1) * SAMPLE_GROUP, 2 * D_FF)
    y_s_t, cf_s_t = _ffn(xs_t, mix_t, cf0_t, ffn_weights, nseq=ng, tm=dec_seq * SAMPLE_GROUP,
                         stride=SAMPLE_GROUP)
    y_sample = y_s_t.reshape(ng, dec_seq, SAMPLE_GROUP, D_MODEL).transpose(0, 2, 1, 3).reshape(nbs, dec_seq, D_MODEL)
    cf_s = cf_s_t.reshape(ng, CONV_FFN - 1, SAMPLE_GROUP, 2 * D_FF).transpose(0, 2, 1, 3).reshape(
        nbs, CONV_FFN - 1, 2 * D_FF)

    return (y_prompt, y_sample, sret_p[None], sgdn_p[None], cq_p[None], cf_p[None],
            sret_s[None], sgdn_s[None], cq_s[None], cf_s[None])
```

```python
import functools

import jax
import numpy as np
import jax.numpy as jnp
from jax import lax
from jax.experimental import pallas as pl
from jax.experimental.pallas import tpu as pltpu

F32 = jnp.float32
BF16 = jnp.bfloat16

D_MODEL = 1024
N_META = 16
PAST_LEN = 16384
N_HEADS = 4
D_HEAD = 128
D_GRP = N_HEADS * D_HEAD
GDN_QKV = 3 * D_GRP
CONV_GDN = 4
CONV_FFN = 3
D_FF = 2816
ROPE_THETA = 10000.0
EPS = 1e-6

OFF_RQ, OFF_RK, OFF_RV, OFF_RG = 0, D_GRP, 2 * D_GRP, 3 * D_GRP
OFF_QKV = 4 * D_GRP
OFF_GG = OFF_QKV + GDN_QKV
OFF_BA = OFF_GG + D_GRP
IN_WIDTH = OFF_BA + 2 * N_HEADS
LANES = 128
SUBLANES = 8
MXU_WIDTH = 256
V7X_VMEM_BYTES = 64 * 1024 * 1024
IN_PAD = OFF_BA + LANES

PROMPT_CHUNK = 128
PROMPT_CHUNKS_PER_STEP = 2
SAMPLE_PAD = 8
SAMPLE_SEQS_PER_STEP = 16
SAMPLE_GROUP = 64
VMEM_LIMIT = V7X_VMEM_BYTES * 7 // 8


def _round_up(n, m):
    return (n + m - 1) // m * m


def _mm(a, b):
    return jnp.dot(a.astype(BF16), b.astype(BF16), preferred_element_type=F32)


def _mm_nt(a, b):
    return lax.dot_general(a.astype(BF16), b.astype(BF16), (((1,), (1,)), ((), ())),
                           preferred_element_type=F32)


def _mm_tn(a, b):
    return lax.dot_general(a.astype(BF16), b.astype(BF16), (((0,), (0,)), ((), ())),
                           preferred_element_type=F32)


def _split3(x):
    p0 = x.astype(BF16)
    r = x - p0.astype(F32)
    p1 = r.astype(BF16)
    p2 = (r - p1.astype(F32)).astype(BF16)
    return p0, p1, p2


def _shift_rows(x, lead):
    k = lead.shape[0]
    row = lax.broadcasted_iota(jnp.int32, x.shape, 0)
    out = pltpu.roll(x, k, 0)
    for i in range(k):
        out = jnp.where(row == i, lead[i:i + 1, :], out)
    return out


def _silu(x):
    return x * jax.nn.sigmoid(x)


def _rms(x, w):
    return x * lax.rsqrt(jnp.mean(x * x, axis=-1, keepdims=True) + EPS) * w


def _rope_kernel(invf_ref, cos_ref, sin_ref, *, segments):
    shape = cos_ref.shape
    r = lax.broadcasted_iota(jnp.int32, shape, 0)
    pos = r + (segments[0][1] - segments[0][0])
    for row0, pos0 in segments[1:]:
        pos = jnp.where(r >= row0, r + (pos0 - row0), pos)
    ang = pos.astype(F32) * invf_ref[...]
    lane = lax.broadcasted_iota(jnp.int32, shape, 1)
    sin = jnp.sin(ang)
    cos_ref[...] = jnp.cos(ang)
    sin_ref[...] = jnp.where(lane < D_HEAD // 2, -sin, sin)


def _rope_tables(n, segments):
    half = D_HEAD // 2
    inv_freq = ROPE_THETA ** (-jnp.arange(half, dtype=F32) / half)
    invf2 = jnp.concatenate([inv_freq, inv_freq]).reshape(1, LANES)
    out = jax.ShapeDtypeStruct((n, LANES), F32)
    return pl.pallas_call(functools.partial(_rope_kernel, segments=tuple(segments)), out_shape=(out, out),
                          name="rope_tables")(invf2)


CAST_COLS = 11 * LANES


def _cast_pad_kernel(wt_ref, o_ref):
    first = pl.program_id(0) * CAST_COLS
    r = lax.broadcasted_iota(jnp.int32, wt_ref.shape, 0)
    blk = jnp.where(first + r < IN_WIDTH, wt_ref[...], 0.0)
    o_ref[...] = blk.T.astype(BF16)


def _cast_pad_w_in(w):
    wt = w.T
    return pl.pallas_call(
        _cast_pad_kernel,
        out_shape=jax.ShapeDtypeStruct((D_MODEL, IN_PAD), BF16),
        grid=(IN_PAD // CAST_COLS,),
        in_specs=[pl.BlockSpec((CAST_COLS, D_MODEL), lambda i: (i, 0))],
        out_specs=pl.BlockSpec((D_MODEL, CAST_COLS), lambda i: (0, i)),
        name="cast_w_in",
    )(wt)


def _proj_kernel(x_ref, nw_ref, w_ref, o_ref, h_ref):
    @pl.when(pl.program_id(0) == 0)
    def _():
        h_ref[...] = _rms(x_ref[...], nw_ref[...]).astype(BF16)

    o_ref[...] = jnp.dot(h_ref[...], w_ref[...], preferred_element_type=F32)


def _proj(x, norm_w, w_bf):
    rows = x.shape[0]
    return pl.pallas_call(
        _proj_kernel,
        out_shape=jax.ShapeDtypeStruct((rows, IN_PAD), F32),
        grid=(IN_PAD // CAST_COLS,),
        in_specs=[
            pl.BlockSpec((rows, D_MODEL), lambda j: (0, 0)),
            pl.BlockSpec((1, D_MODEL), lambda j: (0, 0)),
            pl.BlockSpec((D_MODEL, CAST_COLS), lambda j: (0, j)),
        ],
        out_specs=pl.BlockSpec((rows, CAST_COLS), lambda j: (0, j)),
        scratch_shapes=[pltpu.VMEM((rows, D_MODEL), BF16)],
        compiler_params=pltpu.CompilerParams(dimension_semantics=("arbitrary",),
                                             vmem_limit_bytes=VMEM_LIMIT),
        name="in_proj",
    )(x, norm_w, w_bf)


def _unit_lower_inverses(mats, c, tick=lambda: None, nilpotent=SUBLANES):
    ri = lax.broadcasted_iota(jnp.int32, (c, c), 0)
    ci = lax.broadcasted_iota(jnp.int32, (c, c), 1)
    eye = (ri == ci).astype(F32)
    diag_blk = (ri // SUBLANES) == (ci // SUBLANES)
    ad = [jnp.where(diag_blk, a, 0.0) for a in mats]
    assert 1 <= nilpotent <= SUBLANES
    n_factors = max(0, (nilpotent - 1).bit_length() - 1)
    t = [eye - x for x in ad]
    power = ad
    for i in range(2):
        if i < n_factors:
            power = [_mm(x, x) for x in power]
            t = [x + _mm(x, s) for x, s in zip(t, power)]
        tick()
    tick()
    s = SUBLANES
    while s < c:
        level = ((ri // (2 * s)) == (ci // (2 * s))) & ((ri // s) != (ci // s))
        off = [jnp.where(level, a, 0.0) for a in mats]
        lt = [_mm(o, x) for o, x in zip(off, t)]
        tick()
        t = [x - _mm(x, y) for x, y in zip(t, lt)]
        tick()
        s *= 2
    return t


XS_TOP = SUBLANES


def _mixer_init(sret0_ref, sgdn0_ref, cq0_ref, sret_ref, sgdn_ref, xs_ref, batch_minor=False):
    tail = CONV_GDN - 1
    sret_ref[...] = sret0_ref[...]
    sgdn_ref[...] = sgdn0_ref[...]
    if batch_minor:
        for b in range(xs_ref.shape[0]):
            xs_ref[b, XS_TOP - tail:XS_TOP, :] = cq0_ref[:, b, :]
    else:
        xs_ref[:, XS_TOP - tail:XS_TOP, :] = cq0_ref[...]


def _mixer_block(getp, put_mix, cos2, sin2, const_refs, sret_ref, sgdn_ref, cq_ref, xs_ref, *,
                 bb, c, nch, n_valid, tick=lambda: None, batch_minor=False):
    (tril_ref, triu_ref, dint_ref, qdec_ref, kdec_ref, cdec_ref, cw_ref, alog_ref, dtb_ref, nret_ref,
     ngdn_ref) = const_refs
    rows = nch * c
    tail = CONV_GDN - 1
    top = XS_TOP
    ri = lax.broadcasted_iota(jnp.int32, (c, c), 0)
    cj = lax.broadcasted_iota(jnp.int32, (c, c), 1)
    tri = ri >= cj
    strict = ri > cj
    tril_bf = tril_ref[...]
    triu_bf = triu_ref[...]
    scale = D_HEAD ** -0.5
    heads = range(N_HEADS)
    seqs = range(bb)
    chunks = [slice(j * c, (j + 1) * c) for j in range(nch)]
    join = lambda parts: parts[0] if len(parts) == 1 else jnp.concatenate(parts, axis=0)

    ret = {}
    for b in seqs:
        for h in heads:
            lo = h * D_HEAD
            q = getp(b, OFF_RQ + lo, OFF_RQ + lo + D_HEAD)
            k = getp(b, OFF_RK + lo, OFF_RK + lo + D_HEAD)
            v = getp(b, OFF_RV + lo, OFF_RV + lo + D_HEAD)
            qr = q * cos2 + pltpu.roll(q, D_HEAD // 2, 1) * sin2
            kr = (k * cos2 + pltpu.roll(k, D_HEAD // 2, 1) * sin2) * scale
            ret[b, h] = (qr, kr, v)
    tick()
    rtasks = [(b, h, j) for j in range(nch) for b in seqs for h in heads]
    scores = {t: _mm_nt(ret[t[0], t[1]][0][chunks[t[2]]], ret[t[0], t[1]][1][chunks[t[2]]]) * dint_ref[t[1]]
              for t in rtasks}
    tick()
    intra = {t: _mm(scores[t], ret[t[0], t[1]][2][chunks[t[2]]]) for t in rtasks}
    tick()
    kv = {t: _mm_tn(ret[t[0], t[1]][1][chunks[t[2]]] * kdec_ref[t[1]], ret[t[0], t[1]][2][chunks[t[2]]])
          for t in rtasks}
    tick()
    pairs = [(b, h) for b in seqs for h in heads]
    s_ret = {bh: sret_ref[bh[0], bh[1]] for bh in pairs}
    o_ret = {}
    for j in range(nch):
        for b, h in pairs:
            o_ret[b, h, j] = intra[b, h, j] + _mm(ret[b, h][0][chunks[j]] * qdec_ref[h], s_ret[b, h])
        for b, h in pairs:
            s_ret[b, h] = cdec_ref[h] * s_ret[b, h] + kv[b, h, j]
        tick()
    o_ret = {bh: join([o_ret[bh[0], bh[1], j] for j in range(nch)]) for bh in pairs}
    mu = {bh: jnp.mean(o_ret[bh], axis=-1, keepdims=True) for bh in pairs}
    cen = {bh: o_ret[bh] - mu[bh] for bh in pairs}
    var = {bh: jnp.mean(cen[bh] * cen[bh], axis=-1, keepdims=True) for bh in pairs}
    for b, h in pairs:
        lo = h * D_HEAD
        gate = getp(b, OFF_RG + lo, OFF_RG + lo + D_HEAD)
        o = cen[b, h] * lax.rsqrt(var[b, h] + EPS) * nret_ref[h:h + 1, :]
        put_mix(b, lo, o * _silu(gate))
        sret_ref[b, h] = s_ret[b, h]
    tick()

    qkvs, beta_alls, cum_cols, cum_rows, cum_tots = {}, {}, {}, {}, {}
    ones_bf = jnp.ones((D_HEAD, c), BF16)
    for b in seqs:
        assert CONV_GDN == 4
        w0, w1, w2, w3 = (cw_ref[i:i + 1, :] for i in range(CONV_GDN))
        u0 = getp(b, OFF_QKV, OFF_QKV + GDN_QKV)
        xs_ref[b, top:top + rows, :] = u0
        u1 = _shift_rows(u0, xs_ref[b, top - 1:top, :])
        p = u0 * w1 + u1 * w0
        p_lead = xs_ref[b, top - 2:top, :] * w1 + xs_ref[b, top - 3:top - 1, :] * w0
        conv = u0 * w3 + u1 * w2 + _shift_rows(p, p_lead)
        last = n_valid if nch == 1 else rows
        new_tail = xs_ref[b, top + last - tail:top + last, :]
        xs_ref[b, top - tail:top, :] = new_tail
        if batch_minor:
            cq_ref[:, b, :] = new_tail
        else:
            cq_ref[b] = new_tail
        qkvs[b] = _silu(conv)
        tick()

        ba = getp(b, OFF_BA, OFF_BA + LANES)
        beta_all = jax.nn.sigmoid(ba)
        z = ba + dtb_ref[...]
        softplus = jnp.maximum(z, 0.0) + jnp.log1p(jnp.exp(-jnp.abs(z)))
        g_all = -jnp.exp(alog_ref[...]) * softplus
        if n_valid < c:
            row = lax.broadcasted_iota(jnp.int32, (rows, LANES), 0)
            rowmask = (row < n_valid).astype(F32)
            beta_all = beta_all * rowmask
            g_all = g_all * rowmask
        beta_alls[b] = beta_all
        for j, rs in enumerate(chunks):
            parts = _split3(g_all[rs])
            cum_cols[b, j] = sum(jnp.dot(tril_bf, g, preferred_element_type=F32) for g in parts)
            cum_tots[b, j] = sum(jnp.dot(ones_bf, g, preferred_element_type=F32) for g in parts)
            cum_rows[b, j] = sum(lax.dot_general(g, triu_bf, (((0,), (0,)), ((), ())),
                                                 preferred_element_type=F32) for g in parts)
    tick()

    tasks = [(j, b, h) for j in range(nch) for b in seqs for h in heads]
    qs, ks, vs, betas, ecums, kdecs, cdecs, amats, qkms = [], [], [], [], [], [], [], [], []
    raw = [(qkvs[b][chunks[j], h * D_HEAD:(h + 1) * D_HEAD],
            qkvs[b][chunks[j], D_GRP + h * D_HEAD:D_GRP + (h + 1) * D_HEAD]) for j, b, h in tasks]
    sumsq = [(jnp.sum(q * q, axis=-1, keepdims=True), jnp.sum(k * k, axis=-1, keepdims=True)) for q, k in raw]
    for i, (j, b, h) in enumerate(tasks):
        rs = chunks[j]
        lo = h * D_HEAD
        v = qkvs[b][rs, 2 * D_GRP + lo:2 * D_GRP + lo + D_HEAD]
        q = raw[i][0] * lax.rsqrt(sumsq[i][0] + EPS) * scale
        k = raw[i][1] * lax.rsqrt(sumsq[i][1] + EPS)
        beta = jnp.broadcast_to(beta_alls[b][rs, h:h + 1], (c, LANES))
        cum = jnp.broadcast_to(cum_cols[b, j][:, N_HEADS + h:N_HEADS + h + 1], (c, LANES))
        cum_row = cum_rows[b, j][N_HEADS + h:N_HEADS + h + 1, :]
        dmask = jnp.exp(jnp.where(tri, cum[:, :c] - cum_row, -jnp.inf))
        cum_last = jnp.broadcast_to(cum_tots[b, j][:, N_HEADS + h:N_HEADS + h + 1], (D_HEAD, LANES))
        kq = _mm_nt(jnp.concatenate([k, q], axis=0), k)
        amats.append(jnp.where(strict, beta[:, :c] * kq[:c] * dmask, 0.0))
        qkms.append(jnp.where(tri, kq[c:] * dmask, 0.0))
        qs.append(q)
        ks.append(k)
        vs.append(v)
        betas.append(beta)
        ecums.append(jnp.exp(cum))
        kdecs.append(jnp.exp(cum_last[:c] - cum))
        cdecs.append(jnp.exp(cum_last))
        if i % N_HEADS == N_HEADS - 1:
            tick()

    tinv = _unit_lower_inverses(amats, c, tick, nilpotent=min(SUBLANES, n_valid))
    eye = (ri == cj).astype(F32)
    sols = []
    for i in range(len(tasks)):
        rhs = jnp.concatenate([vs[i] * betas[i], ks[i] * (betas[i] * ecums[i])], axis=1)
        sols.append(rhs + _mm(tinv[i] - eye, rhs))
    tick()

    s_gdn = {bh: sgdn_ref[bh[0], bh[1]] for bh in pairs}
    o_gdn = {}
    for j in range(nch):
        idx = {tasks[i][1:]: i for i in range(len(tasks)) if tasks[i][0] == j}
        lhs = {bh: jnp.concatenate([sols[i][:, D_HEAD:], qs[i] * ecums[i]], axis=0) for bh, i in idx.items()}
        both = {bh: _mm(lhs[bh], s_gdn[bh]) for bh in idx}
        tick()
        w = {bh: sols[i][:, :D_HEAD] - both[bh][:c] for bh, i in idx.items()}
        for bh, i in idx.items():
            o_gdn[bh + (j,)] = both[bh][c:] + _mm(qkms[i], w[bh])
        tick()
        upd = {bh: _mm_tn(ks[i] * kdecs[i], w[bh]) for bh, i in idx.items()}
        for bh, i in idx.items():
            s_gdn[bh] = cdecs[i] * s_gdn[bh] + upd[bh]
        tick()
    o_gdn = {bh: join([o_gdn[bh + (j,)] for j in range(nch)]) for bh in pairs}
    msq = {bh: jnp.mean(o_gdn[bh] * o_gdn[bh], axis=-1, keepdims=True) for bh in pairs}
    for b, h in pairs:
        lo = h * D_HEAD
        gate = getp(b, OFF_GG + lo, OFF_GG + lo + D_HEAD)
        o = o_gdn[b, h] * lax.rsqrt(msq[b, h] + EPS) * ngdn_ref[...]
        put_mix(b, D_GRP + lo, o * _silu(gate))
        sgdn_ref[b, h] = s_gdn[b, h]


N_MIXER_CONSTS = 11


def _mixer_kernel(p_ref, sret0_ref, sgdn0_ref, cq0_ref, cos_ref, sin_ref, *rest, bb, c, nch, n_valid, stored,
                  batch_minor):
    const_refs, (mix_ref, sret_ref, sgdn_ref, cq_ref, xs_ref) = rest[:N_MIXER_CONSTS], rest[N_MIXER_CONSTS:]
    rows = nch * c

    @pl.when(pl.program_id(1) == 0)
    def _():
        _mixer_init(sret0_ref, sgdn0_ref, cq0_ref, sret_ref, sgdn_ref, xs_ref, batch_minor)

    def put_mix(b, lo, value):
        mix_ref[b, :, lo:lo + D_HEAD] = value.astype(mix_ref.dtype)

    def getp(b, lo, hi):
        blk = p_ref[b * stored:(b + 1) * stored, lo:hi]
        if stored < rows:
            blk = jnp.concatenate([blk, jnp.zeros((rows - stored, hi - lo), F32)], axis=0)
        return blk

    _mixer_block(getp, put_mix, cos_ref[...], sin_ref[...], const_refs, sret_ref, sgdn_ref, cq_ref, xs_ref,
                 bb=bb, c=c, nch=nch, n_valid=n_valid, batch_minor=batch_minor)


PROJ_PANEL = MXU_WIDTH


def _mixer_block_ticks(bb, c, nch):
    levels = (c // SUBLANES).bit_length() - 1
    retention = 4 + nch + 1
    gdn_prep = bb + 1 + bb * nch
    inverse = 3 + 2 * levels
    return retention + gdn_prep + inverse + 1 + 3 * nch


def _proj_mixer_kernel(x0_ref, xa_ref, xb_ref, nw_ref, w_ref, sret0_ref, sgdn0_ref, cq0_ref, cos_ref, sin_ref,
                       *rest, c, nch, steps_per_seq, n_cast):
    const_refs = rest[:N_MIXER_CONSTS]
    cast_in = rest[N_MIXER_CONSTS:N_MIXER_CONSTS + n_cast]
    mix_ref, sret_ref, sgdn_ref, cq_ref = rest[N_MIXER_CONSTS + n_cast:N_MIXER_CONSTS + n_cast + 4]
    cast_out = rest[N_MIXER_CONSTS + n_cast + 4:N_MIXER_CONSTS + 2 * n_cast + 4]
    pja_ref, pjb_ref, xs_ref = rest[N_MIXER_CONSTS + 2 * n_cast + 4:]
    step = pl.program_id(0)
    rows = nch * c

    for src_ref, dst_ref in zip(cast_in, cast_out):
        dst_ref[...] = src_ref[...].astype(dst_ref.dtype)

    def project(x_ref, dst_ref, n_ticks):
        h = _rms(x_ref[...], nw_ref[...]).astype(BF16)
        panels = [(lo, min(lo + PROJ_PANEL, IN_PAD)) for lo in range(0, IN_PAD, PROJ_PANEL)]
        n_panels = len(panels)
        calls = [0]

        def emit():
            lo, hi = panels.pop(0)
            dst_ref[:, lo:hi] = jnp.dot(h, w_ref[:, lo:hi], preferred_element_type=F32)

        def tick():
            calls[0] += 1
            due = min(n_panels, -(-calls[0] * n_panels // n_ticks))
            while n_panels - len(panels) < due:
                emit()

        def flush():
            assert n_ticks == 1 or calls[0] == n_ticks, (calls[0], n_ticks)
            while panels:
                emit()

        return tick, flush

    @pl.when(step == 0)
    def _():
        project(x0_ref, pja_ref, 1)[1]()

    @pl.when(lax.rem(step, steps_per_seq) == 0)
    def _():
        _mixer_init(sret0_ref, sgdn0_ref, cq0_ref, sret_ref, sgdn_ref, xs_ref)

    for half, (cur_ref, x_next_ref, nxt_ref) in enumerate(((pja_ref, xa_ref, pjb_ref), (pjb_ref, xb_ref, pja_ref))):
        r0 = half * rows
        tick, flush = project(x_next_ref, nxt_ref, _mixer_block_ticks(1, c, nch))

        def put_mix(b, lo, value, r0=r0):
            mix_ref[r0:r0 + rows, lo:lo + D_HEAD] = value.astype(mix_ref.dtype)

        _mixer_block(lambda b, lo, hi, cur_ref=cur_ref: cur_ref[:, lo:hi], put_mix,
                     cos_ref[r0:r0 + rows, :], sin_ref[r0:r0 + rows, :], const_refs,
                     sret_ref, sgdn_ref, cq_ref, xs_ref, bb=1, c=c, nch=nch, n_valid=c, tick=tick)
        flush()


def _retention_decay_tables(c, n_valid):
    f32 = np.float32
    lg = np.log1p(-np.power(f32(2.0), f32(-5.0) - np.arange(N_HEADS, dtype=f32)))[:, None].astype(f32)
    idx = np.arange(c, dtype=f32)
    diff = idx[:, None] - idx[None, :]
    dint = np.where(diff[None] >= 0, np.exp(lg[:, :, None] * np.maximum(diff[None], 0)), f32(0.0)).astype(f32)
    qdec = np.exp(lg * (idx + f32(1.0))).astype(f32)
    kdec = np.where(idx[None, :] < n_valid, np.exp(lg * np.minimum(f32(n_valid) - f32(1.0) - idx, c)), f32(0.0))
    cdec = np.exp(lg * f32(n_valid)).astype(f32)
    bc = lambda t: np.broadcast_to(t.astype(f32)[:, :, None], t.shape + (LANES,))
    return dint, bc(qdec), bc(kdec), np.broadcast_to(cdec[:, :, None], (N_HEADS, 1, LANES))


def _mixer_const_operands(c, n_valid, consts):
    cw, alog, dtb, nret, ngdn = consts
    dint, qdec, kdec, cdec = _retention_decay_tables(c, n_valid)
    idx = np.arange(c)
    tril = (idx[:, None] >= idx[None, :]).astype(np.float32)
    arrays = (jnp.asarray(tril, BF16), jnp.asarray(tril.T, BF16), jnp.asarray(dint), jnp.asarray(qdec),
              jnp.asarray(kdec), jnp.asarray(cdec), cw, alog, dtb, nret, ngdn)
    assert len(arrays) == N_MIXER_CONSTS
    return arrays, [a.shape for a in arrays]


def _mixer(proj, row0, nb, length, sret0, sgdn0, cq0, rope, rope_row0, consts, *, bb, c, nch, n_valid,
           shared_init, stored=None, batch_minor=False):
    rows = nch * c
    assert nb % bb == 0 and length % rows == 0 and rope_row0 % rows == 0
    assert not shared_init or bb == 1
    assert n_valid == c or nch == 1
    assert bb == 1 or length == rows
    const_arrays, const_shapes = _mixer_const_operands(c, n_valid, consts)
    nsteps = length // rows
    stored = rows if stored is None else stored
    assert stored == rows or (nsteps == 1 and stored >= n_valid and (bb * stored) % SUBLANES == 0)
    assert row0 % (bb * stored) == 0
    blk0 = row0 // (bb * stored)
    init_idx = (lambda b, i: (0, 0, 0, 0)) if shared_init else (lambda b, i: (b, 0, 0, 0))
    init_idx3 = (lambda b, i: (0, 0, 0)) if shared_init else (lambda b, i: (b, 0, 0))
    whole = lambda shape: pl.BlockSpec(shape, lambda b, i: (0,) * len(shape))
    rope_spec = pl.BlockSpec((rows, LANES), lambda b, i: (rope_row0 // rows + i, 0))
    state_shape = (bb, N_HEADS, D_HEAD, D_HEAD)
    tail = CONV_GDN - 1
    assert not (batch_minor and shared_init)
    cq_shape, cq_block = (nb, tail, GDN_QKV), (bb, tail, GDN_QKV)
    cq_idx = lambda b, i: (b, 0, 0)
    if batch_minor:
        cq_shape, cq_block = (tail, nb, GDN_QKV), (tail, bb, GDN_QKV)
        cq_idx = init_idx3 = lambda b, i: (0, b, 0)
    kern = functools.partial(_mixer_kernel, bb=bb, c=c, nch=nch, n_valid=n_valid, stored=stored,
                             batch_minor=batch_minor)
    return pl.pallas_call(
        kern,
        out_shape=(
            jax.ShapeDtypeStruct((nb, length, D_MODEL), BF16),
            jax.ShapeDtypeStruct((nb, N_HEADS, D_HEAD, D_HEAD), F32),
            jax.ShapeDtypeStruct((nb, N_HEADS, D_HEAD, D_HEAD), F32),
            jax.ShapeDtypeStruct(cq_shape, F32),
        ),
        grid=(nb // bb, nsteps),
        in_specs=[
            pl.BlockSpec((bb * stored, IN_PAD), lambda b, i: (blk0 + b * nsteps + i, 0)),
            pl.BlockSpec(state_shape, init_idx),
            pl.BlockSpec(state_shape, init_idx),
            pl.BlockSpec(cq_block, init_idx3),
            rope_spec,
            rope_spec,
        ] + [whole(shape) for shape in const_shapes],
        out_specs=(
            pl.BlockSpec((bb, rows, D_MODEL), lambda b, i: (b, i, 0)),
            pl.BlockSpec(state_shape, lambda b, i: (b, 0, 0, 0)),
            pl.BlockSpec(state_shape, lambda b, i: (b, 0, 0, 0)),
            pl.BlockSpec(cq_block, cq_idx),
        ),
        scratch_shapes=[pltpu.VMEM((bb, XS_TOP + rows, GDN_QKV), F32)],
        compiler_params=pltpu.CompilerParams(dimension_semantics=("arbitrary", "arbitrary"),
                                             vmem_limit_bytes=VMEM_LIMIT),
        name="mixer",
    )(proj, sret0, sgdn0, cq0, rope[0], rope[1], *const_arrays)


def _row_slab(nrows, nsteps):
    for hold in (1, 2, 4, 8):
        slabs = nsteps // hold
        if nsteps % hold == 0 and nrows % slabs == 0 and (nrows // slabs) % (2 * SUBLANES) == 0:
            return nrows // slabs, hold
    raise ValueError((nrows, nsteps))


def _proj_mixer(x, norm_w, w_bf, nseq, sret0, sgdn0, cq0, rope, consts, to_bf16, *, c, nch):
    rows = nch * c
    total = x.shape[0]
    length = total // nseq
    assert total % nseq == 0 and length % (2 * rows) == 0
    nblk = total // rows
    steps_per_seq = length // (2 * rows)
    const_arrays, const_shapes = _mixer_const_operands(c, c, consts)
    whole = lambda shape, **kw: pl.BlockSpec(shape, lambda s: (0,) * len(shape), **kw)
    rope_spec = pl.BlockSpec((2 * rows, LANES), lambda s: (lax.rem(s, steps_per_seq), 0))
    state_shape = (1, N_HEADS, D_HEAD, D_HEAD)
    tail = CONV_GDN - 1
    nsteps = nblk // 2
    slabs = [_row_slab(w.shape[0], nsteps) for w in to_bf16]
    cast_specs = [pl.BlockSpec((r, w.shape[1]), lambda s, hold=hold: (s // hold, 0))
                  for w, (r, hold) in zip(to_bf16, slabs)]
    kern = functools.partial(_proj_mixer_kernel, c=c, nch=nch, steps_per_seq=steps_per_seq, n_cast=len(to_bf16))
    return pl.pallas_call(
        kern,
        out_shape=(
            jax.ShapeDtypeStruct((total, D_MODEL), BF16),
            jax.ShapeDtypeStruct((nseq, N_HEADS, D_HEAD, D_HEAD), F32),
            jax.ShapeDtypeStruct((nseq, N_HEADS, D_HEAD, D_HEAD), F32),
            jax.ShapeDtypeStruct((nseq, tail, GDN_QKV), F32),
        ) + tuple(jax.ShapeDtypeStruct(w.shape, BF16) for w in to_bf16),
        grid=(nsteps,),
        in_specs=[
            pl.BlockSpec((rows, D_MODEL), lambda s: (0, 0), pipeline_mode=pl.Buffered(1)),
            pl.BlockSpec((rows, D_MODEL), lambda s: (2 * s + 1, 0)),
            pl.BlockSpec((rows, D_MODEL), lambda s: (jnp.minimum(2 * s + 2, nblk - 1), 0)),
            whole((1, D_MODEL)),
            whole((D_MODEL, IN_PAD), pipeline_mode=pl.Buffered(1)),
            whole(state_shape),
            whole(state_shape),
            whole((1, tail, GDN_QKV)),
            rope_spec,
            rope_spec,
        ] + [whole(shape) for shape in const_shapes] + cast_specs,
        out_specs=(
            pl.BlockSpec((2 * rows, D_MODEL), lambda s: (s, 0)),
            pl.BlockSpec(state_shape, lambda s: (s // steps_per_seq, 0, 0, 0)),
            pl.BlockSpec(state_shape, lambda s: (s // steps_per_seq, 0, 0, 0)),
            pl.BlockSpec((1, tail, GDN_QKV), lambda s: (s // steps_per_seq, 0, 0)),
        ) + tuple(cast_specs),
        scratch_shapes=[pltpu.VMEM((rows, IN_PAD), F32), pltpu.VMEM((rows, IN_PAD), F32),
                        pltpu.VMEM((1, XS_TOP + rows, GDN_QKV), F32)],
        compiler_params=pltpu.CompilerParams(dimension_semantics=("arbitrary",),
                                             vmem_limit_bytes=VMEM_LIMIT),
        name="proj_mixer",
    )(x, x, x, norm_w, w_bf, sret0, sgdn0, cq0, rope[0], rope[1], *const_arrays, *to_bf16)


FFN_COL_CHUNK = D_FF // 11


def _ffn_kernel(x_ref, mix_ref, *rest, tm, stride, prefix):
    if prefix:
        (xm_ref, mixm_ref, wout_ref, nffn_ref, wup_ref, cw_ref, wdn_ref, nfin_ref,
         y_ref, tail_ref, full_ref, lead_ref) = rest
    else:
        tail0_ref, wout_ref, nffn_ref, wup_ref, cw_ref, wdn_ref, nfin_ref, y_ref, tail_ref, full_ref = rest
    t = pl.program_id(1)
    carry = (CONV_FFN - 1) * stride
    base = _round_up(carry, SUBLANES)

    def up_project(x, mix):
        x1 = x + jnp.dot(mix, wout_ref[...], preferred_element_type=F32)
        h = _rms(x1, nffn_ref[...]).astype(BF16)
        return x1, jnp.dot(h, wup_ref[...], preferred_element_type=F32)

    if prefix:
        @pl.when((pl.program_id(0) == 0) & (t == 0))
        def _():
            um = up_project(xm_ref[...], mixm_ref[...])[1]
            lead_ref[...] = um[um.shape[0] - carry:, :]

    @pl.when(t == 0)
    def _():
        full_ref[base - carry:base, :] = lead_ref[...] if prefix else tail0_ref[0]

    x1, up = up_project(x_ref[...], mix_ref[...])
    full_ref[base:base + tm, :] = up

    def conv_cols(lo):
        cols = slice(lo, lo + FFN_COL_CHUNK)
        if stride % SUBLANES == 0:
            taps = [full_ref[base - carry + i * stride:base - carry + i * stride + tm, cols] for i in range(CONV_FFN)]
        else:
            u = full_ref[base:base + tm, cols]
            taps = [_shift_rows(u, full_ref[base - (CONV_FFN - 1 - i) * stride:base, cols])
                    for i in range(CONV_FFN - 1)] + [u]
        acc = taps[0] * cw_ref[0:1, cols]
        for i in range(1, CONV_FFN):
            acc = acc + taps[i] * cw_ref[i:i + 1, cols]
        return acc

    x2 = x1
    for j in range(D_FF // FFN_COL_CHUNK):
        lo = j * FFN_COL_CHUNK
        act = (_silu(conv_cols(lo)) * conv_cols(D_FF + lo)).astype(BF16)
        x2 = x2 + jnp.dot(act, wdn_ref[lo:lo + FFN_COL_CHUNK, :], preferred_element_type=F32)
    y_ref[...] = _rms(x2, nfin_ref[...])

    new_tail = full_ref[base + tm - carry:base + tm, :]
    full_ref[base - carry:base, :] = new_tail
    tail_ref[0] = new_tail


def _ffn(x, mix, lead, weights, *, nseq, tm, stride):
    wout, nffn, wup, cw, wdn, nfin = weights
    rows = x.shape[0]
    assert rows % (nseq * tm) == 0
    nt = rows // (nseq * tm)
    carry = (CONV_FFN - 1) * stride
    base = _round_up(carry, SUBLANES)
    assert tm >= carry
    resident = lambda shape: pl.BlockSpec(shape, lambda b, t: (0, 0), pipeline_mode=pl.Buffered(1))
    small = lambda shape: pl.BlockSpec(shape, lambda b, t: (0, 0))
    prefix = isinstance(lead, tuple)
    if prefix:
        assert stride == 1 and all(a.shape[0] >= carry for a in lead)
        lead_specs = [small(a.shape) for a in lead]
        lead_scratch = [pltpu.VMEM((carry, 2 * D_FF), F32)]
    else:
        lead = (lead,)
        lead_specs = [pl.BlockSpec((1, carry, 2 * D_FF), lambda b, t: (b, 0, 0))]
        lead_scratch = []
    kern = functools.partial(_ffn_kernel, tm=tm, stride=stride, prefix=prefix)
    return pl.pallas_call(
        kern,
        out_shape=(
            jax.ShapeDtypeStruct((rows, D_MODEL), F32),
            jax.ShapeDtypeStruct((nseq, carry, 2 * D_FF), F32),
        ),
        grid=(nseq, nt),
        in_specs=[
            pl.BlockSpec((tm, D_MODEL), lambda b, t: (b * nt + t, 0)),
            pl.BlockSpec((tm, D_MODEL), lambda b, t: (b * nt + t, 0)),
        ] + lead_specs + [
            resident((D_MODEL, D_MODEL)),
            small((1, D_MODEL)),
            resident((D_MODEL, 2 * D_FF)),
            small((CONV_FFN, 2 * D_FF)),
            resident((D_FF, D_MODEL)),
            small((1, D_MODEL)),
        ],
        out_specs=(
            pl.BlockSpec((tm, D_MODEL), lambda b, t: (b * nt + t, 0)),
            pl.BlockSpec((1, carry, 2 * D_FF), lambda b, t: (b, 0, 0)),
        ),
        scratch_shapes=[pltpu.VMEM((base + tm, 2 * D_FF), F32)] + lead_scratch,
        compiler_params=pltpu.CompilerParams(dimension_semantics=("arbitrary", "arbitrary"),
                                             vmem_limit_bytes=VMEM_LIMIT),
        name="out_ffn",
    )(x, mix, *lead, wout, nffn, wup, cw, wdn, nfin)


def kernel(x_prompt, x_sample, state_ret, state_gdn, state_conv_qkv, state_ffn_conv, meta_tokens, norm_mix,
           w_in, conv_gdn, gdn_a_log, gdn_dt_bias, norm_ret, norm_gdn, w_out, norm_ffn, w_up, conv_ffn,
           w_down, norm_final):
    depth = w_in.shape[0]
    assert depth == 1
    nbp, seq, _ = x_prompt.shape
    nbs, dec_seq, _ = x_sample.shape
    assert dec_seq <= SAMPLE_PAD and nbs % SAMPLE_GROUP == 0
    assert seq % (2 * PROMPT_CHUNK * PROMPT_CHUNKS_PER_STEP) == 0
    layer = 0

    w_in_bf = _cast_pad_w_in(w_in[layer])
    row = lambda v: v.reshape(1, -1).astype(F32)
    pad_ba = lambda v: jnp.pad(v.astype(F32), (N_HEADS, LANES - 2 * N_HEADS)).reshape(1, LANES)
    mixer_consts = (conv_gdn[layer], pad_ba(gdn_a_log[layer]), pad_ba(gdn_dt_bias[layer]),
                    norm_ret[layer].reshape(N_HEADS, D_HEAD), row(norm_gdn[layer]))
    nmix = row(norm_mix[layer])

    assert seq % N_META == 0 and (seq + N_META) % SAMPLE_PAD == 0
    rope_meta_row0, rope_sample_row0 = seq, seq + N_META
    rope = _rope_tables(seq + N_META + SAMPLE_PAD,
                        [(0, N_META), (rope_meta_row0, 0), (rope_sample_row0, PAST_LEN)])

    small_rows = jnp.concatenate([x_sample.reshape(nbs * dec_seq, D_MODEL), meta_tokens.astype(F32)], axis=0)
    proj_small = _proj(small_rows, nmix, w_in_bf)
    meta_row0 = nbs * dec_seq

    zero_state = jnp.zeros((1, N_HEADS, D_HEAD, D_HEAD), F32)
    zero_cq = jnp.zeros((1, CONV_GDN - 1, GDN_QKV), F32)
    mix_m, sret_m, sgdn_m, cq_m = _mixer(proj_small, meta_row0, 1, N_META, zero_state, zero_state, zero_cq, rope,
                                         rope_meta_row0, mixer_consts, bb=1, c=N_META, nch=1, n_valid=N_META,
                                         shared_init=True)

    xp = x_prompt.reshape(nbp * seq, D_MODEL)
    mix_p, sret_p, sgdn_p, cq_p, w_out_bf, w_up_bf, w_down_bf = _proj_mixer(
        xp, nmix, w_in_bf, nbp, sret_m, sgdn_m, cq_m, rope, mixer_consts,
        (w_out[layer], w_up[layer], w_down[layer]), c=PROMPT_CHUNK, nch=PROMPT_CHUNKS_PER_STEP)
    ffn_weights = (w_out_bf, row(norm_ffn[layer]), w_up_bf, conv_ffn[layer], w_down_bf, row(norm_final))

    meta_lead = (meta_tokens.astype(F32), mix_m.reshape(N_META, D_MODEL))
    y_p, cf_p = _ffn(xp, mix_p, meta_lead, ffn_weights, nseq=nbp, tm=512, stride=1)
    y_prompt = y_p.reshape(nbp, seq, D_MODEL)

    mix_s, sret_s, sgdn_s, cq_s = _mixer(proj_small, 0, nbs, SAMPLE_PAD, state_ret[layer], state_gdn[layer],
                                         jnp.swapaxes(state_conv_qkv[layer], 0, 1), rope, rope_sample_row0,
                                         mixer_consts, bb=SAMPLE_SEQS_PER_STEP, c=SAMPLE_PAD, nch=1,
                                         n_valid=dec_seq, shared_init=False, stored=dec_seq, batch_minor=True)
    cq_s = jnp.swapaxes(cq_s, 0, 1)
    ng = nbs // SAMPLE_GROUP
    to_tmajor = lambda a: a.reshape(ng, SAMPLE_GROUP, a.shape[1], a.shape[2]).transpose(0, 2, 1, 3)
    xs_t = to_tmajor(x_sample).reshape(nbs * dec_seq, D_MODEL)
    mix_t = to_tmajor(mix_s[:, :dec_seq]).reshape(nbs * dec_seq, D_MODEL)
    cf0_t = to_tmajor(state_ffn_conv[layer]).reshape(ng, (CONV_FFN - 1) * SAMPLE_GROUP, 2 * D_FF)
    y_s_t, cf_s_t = _ffn(xs_t, mix_t, cf0_t, ffn_weights, nseq=ng, tm=dec_seq * SAMPLE_GROUP,
                         stride=SAMPLE_GROUP)
    y_sample = y_s_t.reshape(ng, dec_seq, SAMPLE_GROUP, D_MODEL).transpose(0, 2, 1, 3).reshape(nbs, dec_seq, D_MODEL)
    cf_s = cf_s_t.reshape(ng, CONV_FFN - 1, SAMPLE_GROUP, 2 * D_FF).transpose(0, 2, 1, 3).reshape(
        nbs, CONV_FFN - 1, 2 * D_FF)

    return (y_prompt, y_sample, sret_p[None], sgdn_p[None], cq_p[None], cf_p[None],
            sret_s[None], sgdn_s[None], cq_s[None], cf_s[None])
```

```python
import functools

import jax
import numpy as np
import jax.numpy as jnp
from jax import lax
from jax.experimental import pallas as pl
from jax.experimental.pallas import tpu as pltpu

F32 = jnp.float32
BF16 = jnp.bfloat16

D_MODEL = 1024
N_META = 16
PAST_LEN = 16384
N_HEADS = 4
D_HEAD = 128
D_GRP = N_HEADS * D_HEAD
GDN_QKV = 3 * D_GRP
CONV_GDN = 4
CONV_FFN = 3
D_FF = 2816
ROPE_THETA = 10000.0
EPS = 1e-6

OFF_RQ, OFF_RK, OFF_RV, OFF_RG = 0, D_GRP, 2 * D_GRP, 3 * D_GRP
OFF_QKV = 4 * D_GRP
OFF_GG = OFF_QKV + GDN_QKV
OFF_BA = OFF_GG + D_GRP
IN_WIDTH = OFF_BA + 2 * N_HEADS
LANES = 128
SUBLANES = 8
MXU_WIDTH = 256
V7X_VMEM_BYTES = 64 * 1024 * 1024
IN_PAD = OFF_BA + LANES

PROMPT_CHUNK = 128
PROMPT_CHUNKS_PER_STEP = 2
SAMPLE_PAD = 8
SAMPLE_SEQS_PER_STEP = 16
SAMPLE_GROUP = 64
VMEM_LIMIT = V7X_VMEM_BYTES * 7 // 8


def _round_up(n, m):
    return (n + m - 1) // m * m


def _mm(a, b):
    return jnp.dot(a.astype(BF16), b.astype(BF16), preferred_element_type=F32)


def _mm_nt(a, b):
    return lax.dot_general(a.astype(BF16), b.astype(BF16), (((1,), (1,)), ((), ())),
                           preferred_element_type=F32)


def _mm_tn(a, b):
    return lax.dot_general(a.astype(BF16), b.astype(BF16), (((0,), (0,)), ((), ())),
                           preferred_element_type=F32)


def _split3(x):
    p0 = x.astype(BF16)
    r = x - p0.astype(F32)
    p1 = r.astype(BF16)
    p2 = (r - p1.astype(F32)).astype(BF16)
    return p0, p1, p2


def _shift_rows(x, lead):
    k = lead.shape[0]
    row = lax.broadcasted_iota(jnp.int32, x.shape, 0)
    out = pltpu.roll(x, k, 0)
    for i in range(k):
        out = jnp.where(row == i, lead[i:i + 1, :], out)
    return out


def _silu(x):
    return x * jax.nn.sigmoid(x)


def _rms(x, w):
    return x * lax.rsqrt(jnp.mean(x * x, axis=-1, keepdims=True) + EPS) * w


def _rope_kernel(invf_ref, cos_ref, sin_ref, *, segments):
    shape = cos_ref.shape
    r = lax.broadcasted_iota(jnp.int32, shape, 0)
    pos = r + (segments[0][1] - segments[0][0])
    for row0, pos0 in segments[1:]:
        pos = jnp.where(r >= row0, r + (pos0 - row0), pos)
    ang = pos.astype(F32) * invf_ref[...]
    lane = lax.broadcasted_iota(jnp.int32, shape, 1)
    sin = jnp.sin(ang)
    cos_ref[...] = jnp.cos(ang)
    sin_ref[...] = jnp.where(lane < D_HEAD // 2, -sin, sin)


def _rope_tables(n, segments):
    half = D_HEAD // 2
    inv_freq = ROPE_THETA ** (-jnp.arange(half, dtype=F32) / half)
    invf2 = jnp.concatenate([inv_freq, inv_freq]).reshape(1, LANES)
    out = jax.ShapeDtypeStruct((n, LANES), F32)
    return pl.pallas_call(functools.partial(_rope_kernel, segments=tuple(segments)), out_shape=(out, out),
                          name="rope_tables")(invf2)


CAST_COLS = 11 * LANES


def _cast_pad_kernel(wt_ref, o_ref):
    first = pl.program_id(0) * CAST_COLS
    r = lax.broadcasted_iota(jnp.int32, wt_ref.shape, 0)
    blk = jnp.where(first + r < IN_WIDTH, wt_ref[...], 0.0)
    o_ref[...] = blk.T.astype(BF16)


def _cast_pad_w_in(w):
    wt = w.T
    return pl.pallas_call(
        _cast_pad_kernel,
        out_shape=jax.ShapeDtypeStruct((D_MODEL, IN_PAD), BF16),
        grid=(IN_PAD // CAST_COLS,),
        in_specs=[pl.BlockSpec((CAST_COLS, D_MODEL), lambda i: (i, 0))],
        out_specs=pl.BlockSpec((D_MODEL, CAST_COLS), lambda i: (0, i)),
        name="cast_w_in",
    )(wt)


def _proj_kernel(x_ref, nw_ref, w_ref, o_ref, h_ref):
    @pl.when(pl.program_id(0) == 0)
    def _():
        h_ref[...] = _rms(x_ref[...], nw_ref[...]).astype(BF16)

    o_ref[...] = jnp.dot(h_ref[...], w_ref[...], preferred_element_type=F32)


def _proj(x, norm_w, w_bf):
    rows = x.shape[0]
    return pl.pallas_call(
        _proj_kernel,
        out_shape=jax.ShapeDtypeStruct((rows, IN_PAD), F32),
        grid=(IN_PAD // CAST_COLS,),
        in_specs=[
            pl.BlockSpec((rows, D_MODEL), lambda j: (0, 0)),
            pl.BlockSpec((1, D_MODEL), lambda j: (0, 0)),
            pl.BlockSpec((D_MODEL, CAST_COLS), lambda j: (0, j)),
        ],
        out_specs=pl.BlockSpec((rows, CAST_COLS), lambda j: (0, j)),
        scratch_shapes=[pltpu.VMEM((rows, D_MODEL), BF16)],
        compiler_params=pltpu.CompilerParams(dimension_semantics=("arbitrary",),
                                             vmem_limit_bytes=VMEM_LIMIT),
        name="in_proj",
    )(x, norm_w, w_bf)


def _unit_lower_inverses(mats, c, tick=lambda: None, nilpotent=SUBLANES):
    ri = lax.broadcasted_iota(jnp.int32, (c, c), 0)
    ci = lax.broadcasted_iota(jnp.int32, (c, c), 1)
    eye = (ri == ci).astype(F32)
    diag_blk = (ri // SUBLANES) == (ci // SUBLANES)
    ad = [jnp.where(diag_blk, a, 0.0) for a in mats]
    assert 1 <= nilpotent <= SUBLANES
    n_factors = max(0, (nilpotent - 1).bit_length() - 1)
    t = [eye - x for x in ad]
    power = ad
    for i in range(2):
        if i < n_factors:
            power = [_mm(x, x) for x in power]
            t = [x + _mm(x, s) for x, s in zip(t, power)]
        tick()
    tick()
    s = SUBLANES
    while s < c:
        level = ((ri // (2 * s)) == (ci // (2 * s))) & ((ri // s) != (ci // s))
        off = [jnp.where(level, a, 0.0) for a in mats]
        lt = [_mm(o, x) for o, x in zip(off, t)]
        tick()
        t = [x - _mm(x, y) for x, y in zip(t, lt)]
        tick()
        s *= 2
    return t


XS_TOP = SUBLANES


def _mixer_init(sret0_ref, sgdn0_ref, cq0_ref, sret_ref, sgdn_ref, xs_ref, batch_minor=False):
    tail = CONV_GDN - 1
    sret_ref[...] = sret0_ref[...]
    sgdn_ref[...] = sgdn0_ref[...]
    if batch_minor:
        for b in range(xs_ref.shape[0]):
            xs_ref[b, XS_TOP - tail:XS_TOP, :] = cq0_ref[:, b, :]
    else:
        xs_ref[:, XS_TOP - tail:XS_TOP, :] = cq0_ref[...]


def _mixer_block(getp, put_mix, cos2, sin2, const_refs, sret_ref, sgdn_ref, cq_ref, xs_ref, *,
                 bb, c, nch, n_valid, tick=lambda: None, batch_minor=False):
    (tril_ref, dint_ref, qdec_ref, kdec_ref, cdec_ref, cw_ref, alog_ref, dtb_ref, nret_ref, ngdn_ref) = const_refs
    rows = nch * c
    tail = CONV_GDN - 1
    top = XS_TOP
    ri = lax.broadcasted_iota(jnp.int32, (c, c), 0)
    cj = lax.broadcasted_iota(jnp.int32, (c, c), 1)
    tri = ri >= cj
    strict = ri > cj
    tril_bf = tril_ref[...]
    scale = D_HEAD ** -0.5
    heads = range(N_HEADS)
    seqs = range(bb)
    chunks = [slice(j * c, (j + 1) * c) for j in range(nch)]
    join = lambda parts: parts[0] if len(parts) == 1 else jnp.concatenate(parts, axis=0)

    ret = {}
    for b in seqs:
        for h in heads:
            lo = h * D_HEAD
            q = getp(b, OFF_RQ + lo, OFF_RQ + lo + D_HEAD)
            k = getp(b, OFF_RK + lo, OFF_RK + lo + D_HEAD)
            v = getp(b, OFF_RV + lo, OFF_RV + lo + D_HEAD)
            qr = q * cos2 + pltpu.roll(q, D_HEAD // 2, 1) * sin2
            kr = (k * cos2 + pltpu.roll(k, D_HEAD // 2, 1) * sin2) * scale
            ret[b, h] = (qr, kr, v)
    tick()
    rtasks = [(b, h, j) for j in range(nch) for b in seqs for h in heads]
    scores = {t: _mm_nt(ret[t[0], t[1]][0][chunks[t[2]]], ret[t[0], t[1]][1][chunks[t[2]]]) * dint_ref[t[1]]
              for t in rtasks}
    tick()
    intra = {t: _mm(scores[t], ret[t[0], t[1]][2][chunks[t[2]]]) for t in rtasks}
    tick()
    kv = {t: _mm_tn(ret[t[0], t[1]][1][chunks[t[2]]] * kdec_ref[t[1]], ret[t[0], t[1]][2][chunks[t[2]]])
          for t in rtasks}
    tick()
    pairs = [(b, h) for b in seqs for h in heads]
    s_ret = {bh: sret_ref[bh[0], bh[1]] for bh in pairs}
    o_ret = {}
    for j in range(nch):
        for b, h in pairs:
            o_ret[b, h, j] = intra[b, h, j] + _mm(ret[b, h][0][chunks[j]] * qdec_ref[h], s_ret[b, h])
        for b, h in pairs:
            s_ret[b, h] = cdec_ref[h] * s_ret[b, h] + kv[b, h, j]
        tick()
    o_ret = {bh: join([o_ret[bh[0], bh[1], j] for j in range(nch)]) for bh in pairs}
    mu = {bh: jnp.mean(o_ret[bh], axis=-1, keepdims=True) for bh in pairs}
    cen = {bh: o_ret[bh] - mu[bh] for bh in pairs}
    var = {bh: jnp.mean(cen[bh] * cen[bh], axis=-1, keepdims=True) for bh in pairs}
    for b, h in pairs:
        lo = h * D_HEAD
        gate = getp(b, OFF_RG + lo, OFF_RG + lo + D_HEAD)
        o = cen[b, h] * lax.rsqrt(var[b, h] + EPS) * nret_ref[h:h + 1, :]
        put_mix(b, lo, o * _silu(gate))
        sret_ref[b, h] = s_ret[b, h]
    tick()

    qkvs, beta_alls, cum_cols, cum_rows, cum_tots = {}, {}, {}, {}, {}
    for b in seqs:
        assert CONV_GDN == 4
        w0, w1, w2, w3 = (cw_ref[i:i + 1, :] for i in range(CONV_GDN))
        u0 = getp(b, OFF_QKV, OFF_QKV + GDN_QKV)
        xs_ref[b, top:top + rows, :] = u0
        u1 = _shift_rows(u0, xs_ref[b, top - 1:top, :])
        p = u0 * w1 + u1 * w0
        p_lead = xs_ref[b, top - 2:top, :] * w1 + xs_ref[b, top - 3:top - 1, :] * w0
        conv = u0 * w3 + u1 * w2 + _shift_rows(p, p_lead)
        last = n_valid if nch == 1 else rows
        new_tail = xs_ref[b, top + last - tail:top + last, :]
        xs_ref[b, top - tail:top, :] = new_tail
        if batch_minor:
            cq_ref[:, b, :] = new_tail
        else:
            cq_ref[b] = new_tail
        qkvs[b] = _silu(conv)
        tick()

        ba = getp(b, OFF_BA, OFF_BA + LANES)
        beta_all = jax.nn.sigmoid(ba)
        z = ba + dtb_ref[...]
        softplus = jnp.maximum(z, 0.0) + jnp.log1p(jnp.exp(-jnp.abs(z)))
        g_all = -jnp.exp(alog_ref[...]) * softplus
        if n_valid < c:
            row = lax.broadcasted_iota(jnp.int32, (rows, LANES), 0)
            rowmask = (row < n_valid).astype(F32)
            beta_all = beta_all * rowmask
            g_all = g_all * rowmask
        beta_alls[b] = beta_all
        for j, rs in enumerate(chunks):
            parts = _split3(g_all[rs])
            cum_cols[b, j] = sum(jnp.dot(tril_bf, g, preferred_element_type=F32) for g in parts)
            cum_tots[b, j] = jnp.broadcast_to(cum_cols[b, j][c - 1:c, :], (D_HEAD, LANES))
            cum_rows[b, j] = cum_cols[b, j].T
    tick()

    tasks = [(j, b, h) for j in range(nch) for b in seqs for h in heads]
    qs, ks, vs, betas, ecums, kdecs, cdecs, amats, qkms = [], [], [], [], [], [], [], [], []
    raw = [(qkvs[b][chunks[j], h * D_HEAD:(h + 1) * D_HEAD],
            qkvs[b][chunks[j], D_GRP + h * D_HEAD:D_GRP + (h + 1) * D_HEAD]) for j, b, h in tasks]
    sumsq = [(jnp.sum(q * q, axis=-1, keepdims=True), jnp.sum(k * k, axis=-1, keepdims=True)) for q, k in raw]
    for i, (j, b, h) in enumerate(tasks):
        rs = chunks[j]
        lo = h * D_HEAD
        v = qkvs[b][rs, 2 * D_GRP + lo:2 * D_GRP + lo + D_HEAD]
        q = raw[i][0] * lax.rsqrt(sumsq[i][0] + EPS) * scale
        k = raw[i][1] * lax.rsqrt(sumsq[i][1] + EPS)
        beta = jnp.broadcast_to(beta_alls[b][rs, h:h + 1], (c, LANES))
        cum = jnp.broadcast_to(cum_cols[b, j][:, N_HEADS + h:N_HEADS + h + 1], (c, LANES))
        cum_row = cum_rows[b, j][N_HEADS + h:N_HEADS + h + 1, :]
        dmask = jnp.exp(jnp.where(tri, cum[:, :c] - cum_row, -jnp.inf))
        cum_last = jnp.broadcast_to(cum_tots[b, j][:, N_HEADS + h:N_HEADS + h + 1], (D_HEAD, LANES))
        kq = _mm_nt(jnp.concatenate([k, q], axis=0), k)
        amats.append(jnp.where(strict, beta[:, :c] * kq[:c] * dmask, 0.0))
        qkms.append(jnp.where(tri, kq[c:] * dmask, 0.0))
        qs.append(q)
        ks.append(k)
        vs.append(v)
        betas.append(beta)
        ecums.append(jnp.exp(cum))
        kdecs.append(jnp.exp(cum_last[:c] - cum))
        cdecs.append(jnp.exp(cum_last))
        if i % N_HEADS == N_HEADS - 1:
            tick()

    tinv = _unit_lower_inverses(amats, c, tick, nilpotent=min(SUBLANES, n_valid))
    eye = (ri == cj).astype(F32)
    sols = []
    for i in range(len(tasks)):
        rhs = jnp.concatenate([vs[i] * betas[i], ks[i] * (betas[i] * ecums[i])], axis=1)
        sols.append(rhs + _mm(tinv[i] - eye, rhs))
    tick()

    s_gdn = {bh: sgdn_ref[bh[0], bh[1]] for bh in pairs}
    o_gdn = {}
    for j in range(nch):
        idx = {tasks[i][1:]: i for i in range(len(tasks)) if tasks[i][0] == j}
        lhs = {bh: jnp.concatenate([sols[i][:, D_HEAD:], qs[i] * ecums[i]], axis=0) for bh, i in idx.items()}
        both = {bh: _mm(lhs[bh], s_gdn[bh]) for bh in idx}
        tick()
        w = {bh: sols[i][:, :D_HEAD] - both[bh][:c] for bh, i in idx.items()}
        for bh, i in idx.items():
            o_gdn[bh + (j,)] = both[bh][c:] + _mm(qkms[i], w[bh])
        tick()
        upd = {bh: _mm_tn(ks[i] * kdecs[i], w[bh]) for bh, i in idx.items()}
        for bh, i in idx.items():
            s_gdn[bh] = cdecs[i] * s_gdn[bh] + upd[bh]
        tick()
    o_gdn = {bh: join([o_gdn[bh + (j,)] for j in range(nch)]) for bh in pairs}
    msq = {bh: jnp.mean(o_gdn[bh] * o_gdn[bh], axis=-1, keepdims=True) for bh in pairs}
    for b, h in pairs:
        lo = h * D_HEAD
        gate = getp(b, OFF_GG + lo, OFF_GG + lo + D_HEAD)
        o = o_gdn[b, h] * lax.rsqrt(msq[b, h] + EPS) * ngdn_ref[...]
        put_mix(b, D_GRP + lo, o * _silu(gate))
        sgdn_ref[b, h] = s_gdn[b, h]


N_MIXER_CONSTS = 10


def _mixer_kernel(p_ref, sret0_ref, sgdn0_ref, cq0_ref, cos_ref, sin_ref, *rest, bb, c, nch, n_valid, stored,
                  batch_minor):
    const_refs, (mix_ref, sret_ref, sgdn_ref, cq_ref, xs_ref) = rest[:N_MIXER_CONSTS], rest[N_MIXER_CONSTS:]
    rows = nch * c

    @pl.when(pl.program_id(1) == 0)
    def _():
        _mixer_init(sret0_ref, sgdn0_ref, cq0_ref, sret_ref, sgdn_ref, xs_ref, batch_minor)

    def put_mix(b, lo, value):
        mix_ref[b, :, lo:lo + D_HEAD] = value.astype(mix_ref.dtype)

    def getp(b, lo, hi):
        blk = p_ref[b * stored:(b + 1) * stored, lo:hi]
        if stored < rows:
            blk = jnp.concatenate([blk, jnp.zeros((rows - stored, hi - lo), F32)], axis=0)
        return blk

    _mixer_block(getp, put_mix, cos_ref[...], sin_ref[...], const_refs, sret_ref, sgdn_ref, cq_ref, xs_ref,
                 bb=bb, c=c, nch=nch, n_valid=n_valid, batch_minor=batch_minor)


PROJ_PANEL = MXU_WIDTH


def _mixer_block_ticks(bb, c, nch):
    levels = (c // SUBLANES).bit_length() - 1
    retention = 4 + nch + 1
    gdn_prep = bb + 1 + bb * nch
    inverse = 3 + 2 * levels
    return retention + gdn_prep + inverse + 1 + 3 * nch


def _proj_mixer_kernel(x0_ref, xa_ref, xb_ref, nw_ref, w_ref, sret0_ref, sgdn0_ref, cq0_ref, cos_ref, sin_ref,
                       *rest, c, nch, steps_per_seq, n_cast):
    const_refs = rest[:N_MIXER_CONSTS]
    cast_in = rest[N_MIXER_CONSTS:N_MIXER_CONSTS + n_cast]
    mix_ref, sret_ref, sgdn_ref, cq_ref = rest[N_MIXER_CONSTS + n_cast:N_MIXER_CONSTS + n_cast + 4]
    cast_out = rest[N_MIXER_CONSTS + n_cast + 4:N_MIXER_CONSTS + 2 * n_cast + 4]
    pja_ref, pjb_ref, xs_ref = rest[N_MIXER_CONSTS + 2 * n_cast + 4:]
    step = pl.program_id(0)
    rows = nch * c

    for src_ref, dst_ref in zip(cast_in, cast_out):
        dst_ref[...] = src_ref[...].astype(dst_ref.dtype)

    def project(x_ref, dst_ref, n_ticks):
        h = _rms(x_ref[...], nw_ref[...]).astype(BF16)
        panels = [(lo, min(lo + PROJ_PANEL, IN_PAD)) for lo in range(0, IN_PAD, PROJ_PANEL)]
        n_panels = len(panels)
        calls = [0]

        def emit():
            lo, hi = panels.pop(0)
            dst_ref[:, lo:hi] = jnp.dot(h, w_ref[:, lo:hi], preferred_element_type=F32)

        def tick():
            calls[0] += 1
            due = min(n_panels, -(-calls[0] * n_panels // n_ticks))
            while n_panels - len(panels) < due:
                emit()

        def flush():
            assert n_ticks == 1 or calls[0] == n_ticks, (calls[0], n_ticks)
            while panels:
                emit()

        return tick, flush

    @pl.when(step == 0)
    def _():
        project(x0_ref, pja_ref, 1)[1]()

    @pl.when(lax.rem(step, steps_per_seq) == 0)
    def _():
        _mixer_init(sret0_ref, sgdn0_ref, cq0_ref, sret_ref, sgdn_ref, xs_ref)

    for half, (cur_ref, x_next_ref, nxt_ref) in enumerate(((pja_ref, xa_ref, pjb_ref), (pjb_ref, xb_ref, pja_ref))):
        r0 = half * rows
        tick, flush = project(x_next_ref, nxt_ref, _mixer_block_ticks(1, c, nch))

        def put_mix(b, lo, value, r0=r0):
            mix_ref[r0:r0 + rows, lo:lo + D_HEAD] = value.astype(mix_ref.dtype)

        _mixer_block(lambda b, lo, hi, cur_ref=cur_ref: cur_ref[:, lo:hi], put_mix,
                     cos_ref[r0:r0 + rows, :], sin_ref[r0:r0 + rows, :], const_refs,
                     sret_ref, sgdn_ref, cq_ref, xs_ref, bb=1, c=c, nch=nch, n_valid=c, tick=tick)
        flush()


def _retention_decay_tables(c, n_valid):
    f32 = np.float32
    lg = np.log1p(-np.power(f32(2.0), f32(-5.0) - np.arange(N_HEADS, dtype=f32)))[:, None].astype(f32)
    idx = np.arange(c, dtype=f32)
    diff = idx[:, None] - idx[None, :]
    dint = np.where(diff[None] >= 0, np.exp(lg[:, :, None] * np.maximum(diff[None], 0)), f32(0.0)).astype(f32)
    qdec = np.exp(lg * (idx + f32(1.0))).astype(f32)
    kdec = np.where(idx[None, :] < n_valid, np.exp(lg * np.minimum(f32(n_valid) - f32(1.0) - idx, c)), f32(0.0))
    cdec = np.exp(lg * f32(n_valid)).astype(f32)
    bc = lambda t: np.broadcast_to(t.astype(f32)[:, :, None], t.shape + (LANES,))
    return dint, bc(qdec), bc(kdec), np.broadcast_to(cdec[:, :, None], (N_HEADS, 1, LANES))


def _mixer_const_operands(c, n_valid, consts):
    cw, alog, dtb, nret, ngdn = consts
    dint, qdec, kdec, cdec = _retention_decay_tables(c, n_valid)
    idx = np.arange(c)
    tril = (idx[:, None] >= idx[None, :]).astype(np.float32)
    arrays = (jnp.asarray(tril, BF16), jnp.asarray(dint), jnp.asarray(qdec), jnp.asarray(kdec), jnp.asarray(cdec),
              cw, alog, dtb, nret, ngdn)
    assert len(arrays) == N_MIXER_CONSTS
    return arrays, [a.shape for a in arrays]


def _mixer(proj, row0, nb, length, sret0, sgdn0, cq0, rope, rope_row0, consts, *, bb, c, nch, n_valid,
           shared_init, stored=None, batch_minor=False):
    rows = nch * c
    assert nb % bb == 0 and length % rows == 0 and rope_row0 % rows == 0
    assert not shared_init or bb == 1
    assert n_valid == c or nch == 1
    assert bb == 1 or length == rows
    const_arrays, const_shapes = _mixer_const_operands(c, n_valid, consts)
    nsteps = length // rows
    stored = rows if stored is None else stored
    assert stored == rows or (nsteps == 1 and stored >= n_valid and (bb * stored) % SUBLANES == 0)
    assert row0 % (bb * stored) == 0
    blk0 = row0 // (bb * stored)
    init_idx = (lambda b, i: (0, 0, 0, 0)) if shared_init else (lambda b, i: (b, 0, 0, 0))
    init_idx3 = (lambda b, i: (0, 0, 0)) if shared_init else (lambda b, i: (b, 0, 0))
    whole = lambda shape: pl.BlockSpec(shape, lambda b, i: (0,) * len(shape))
    rope_spec = pl.BlockSpec((rows, LANES), lambda b, i: (rope_row0 // rows + i, 0))
    state_shape = (bb, N_HEADS, D_HEAD, D_HEAD)
    tail = CONV_GDN - 1
    assert not (batch_minor and shared_init)
    cq_shape, cq_block = (nb, tail, GDN_QKV), (bb, tail, GDN_QKV)
    cq_idx = lambda b, i: (b, 0, 0)
    if batch_minor:
        cq_shape, cq_block = (tail, nb, GDN_QKV), (tail, bb, GDN_QKV)
        cq_idx = init_idx3 = lambda b, i: (0, b, 0)
    kern = functools.partial(_mixer_kernel, bb=bb, c=c, nch=nch, n_valid=n_valid, stored=stored,
                             batch_minor=batch_minor)
    return pl.pallas_call(
        kern,
        out_shape=(
            jax.ShapeDtypeStruct((nb, length, D_MODEL), BF16),
            jax.ShapeDtypeStruct((nb, N_HEADS, D_HEAD, D_HEAD), F32),
            jax.ShapeDtypeStruct((nb, N_HEADS, D_HEAD, D_HEAD), F32),
            jax.ShapeDtypeStruct(cq_shape, F32),
        ),
        grid=(nb // bb, nsteps),
        in_specs=[
            pl.BlockSpec((bb * stored, IN_PAD), lambda b, i: (blk0 + b * nsteps + i, 0)),
            pl.BlockSpec(state_shape, init_idx),
            pl.BlockSpec(state_shape, init_idx),
            pl.BlockSpec(cq_block, init_idx3),
            rope_spec,
            rope_spec,
        ] + [whole(shape) for shape in const_shapes],
        out_specs=(
            pl.BlockSpec((bb, rows, D_MODEL), lambda b, i: (b, i, 0)),
            pl.BlockSpec(state_shape, lambda b, i: (b, 0, 0, 0)),
            pl.BlockSpec(state_shape, lambda b, i: (b, 0, 0, 0)),
            pl.BlockSpec(cq_block, cq_idx),
        ),
        scratch_shapes=[pltpu.VMEM((bb, XS_TOP + rows, GDN_QKV), F32)],
        compiler_params=pltpu.CompilerParams(dimension_semantics=("arbitrary", "arbitrary"),
                                             vmem_limit_bytes=VMEM_LIMIT),
        name="mixer",
    )(proj, sret0, sgdn0, cq0, rope[0], rope[1], *const_arrays)


def _row_slab(nrows, nsteps):
    for hold in (1, 2, 4, 8):
        slabs = nsteps // hold
        if nsteps % hold == 0 and nrows % slabs == 0 and (nrows // slabs) % (2 * SUBLANES) == 0:
            return nrows // slabs, hold
    raise ValueError((nrows, nsteps))


def _proj_mixer(x, norm_w, w_bf, nseq, sret0, sgdn0, cq0, rope, consts, to_bf16, *, c, nch):
    rows = nch * c
    total = x.shape[0]
    length = total // nseq
    assert total % nseq == 0 and length % (2 * rows) == 0
    nblk = total // rows
    steps_per_seq = length // (2 * rows)
    const_arrays, const_shapes = _mixer_const_operands(c, c, consts)
    whole = lambda shape, **kw: pl.BlockSpec(shape, lambda s: (0,) * len(shape), **kw)
    rope_spec = pl.BlockSpec((2 * rows, LANES), lambda s: (lax.rem(s, steps_per_seq), 0))
    state_shape = (1, N_HEADS, D_HEAD, D_HEAD)
    tail = CONV_GDN - 1
    nsteps = nblk // 2
    slabs = [_row_slab(w.shape[0], nsteps) for w in to_bf16]
    cast_specs = [pl.BlockSpec((r, w.shape[1]), lambda s, hold=hold: (s // hold, 0))
                  for w, (r, hold) in zip(to_bf16, slabs)]
    kern = functools.partial(_proj_mixer_kernel, c=c, nch=nch, steps_per_seq=steps_per_seq, n_cast=len(to_bf16))
    return pl.pallas_call(
        kern,
        out_shape=(
            jax.ShapeDtypeStruct((total, D_MODEL), BF16),
            jax.ShapeDtypeStruct((nseq, N_HEADS, D_HEAD, D_HEAD), F32),
            jax.ShapeDtypeStruct((nseq, N_HEADS, D_HEAD, D_HEAD), F32),
            jax.ShapeDtypeStruct((nseq, tail, GDN_QKV), F32),
        ) + tuple(jax.ShapeDtypeStruct(w.shape, BF16) for w in to_bf16),
        grid=(nsteps,),
        in_specs=[
            pl.BlockSpec((rows, D_MODEL), lambda s: (0, 0), pipeline_mode=pl.Buffered(1)),
            pl.BlockSpec((rows, D_MODEL), lambda s: (2 * s + 1, 0)),
            pl.BlockSpec((rows, D_MODEL), lambda s: (jnp.minimum(2 * s + 2, nblk - 1), 0)),
            whole((1, D_MODEL)),
            whole((D_MODEL, IN_PAD), pipeline_mode=pl.Buffered(1)),
            whole(state_shape),
            whole(state_shape),
            whole((1, tail, GDN_QKV)),
            rope_spec,
            rope_spec,
        ] + [whole(shape) for shape in const_shapes] + cast_specs,
        out_specs=(
            pl.BlockSpec((2 * rows, D_MODEL), lambda s: (s, 0)),
            pl.BlockSpec(state_shape, lambda s: (s // steps_per_seq, 0, 0, 0)),
            pl.BlockSpec(state_shape, lambda s: (s // steps_per_seq, 0, 0, 0)),
            pl.BlockSpec((1, tail, GDN_QKV), lambda s: (s // steps_per_seq, 0, 0)),
        ) + tuple(cast_specs),
        scratch_shapes=[pltpu.VMEM((rows, IN_PAD), F32), pltpu.VMEM((rows, IN_PAD), F32),
                        pltpu.VMEM((1, XS_TOP + rows, GDN_QKV), F32)],
        compiler_params=pltpu.CompilerParams(dimension_semantics=("arbitrary",),
                                             vmem_limit_bytes=VMEM_LIMIT),
        name="proj_mixer",
    )(x, x, x, norm_w, w_bf, sret0, sgdn0, cq0, rope[0], rope[1], *const_arrays, *to_bf16)


FFN_COL_CHUNK = D_FF // 11


def _ffn_kernel(x_ref, mix_ref, *rest, tm, stride, prefix):
    if prefix:
        (xm_ref, mixm_ref, wout_ref, nffn_ref, wup_ref, cw_ref, wdn_ref, nfin_ref,
         y_ref, tail_ref, full_ref, lead_ref) = rest
    else:
        tail0_ref, wout_ref, nffn_ref, wup_ref, cw_ref, wdn_ref, nfin_ref, y_ref, tail_ref, full_ref = rest
    t = pl.program_id(1)
    carry = (CONV_FFN - 1) * stride
    base = _round_up(carry, SUBLANES)

    def up_project(x, mix):
        x1 = x + jnp.dot(mix, wout_ref[...], preferred_element_type=F32)
        h = _rms(x1, nffn_ref[...]).astype(BF16)
        return x1, jnp.dot(h, wup_ref[...], preferred_element_type=F32)

    if prefix:
        @pl.when((pl.program_id(0) == 0) & (t == 0))
        def _():
            um = up_project(xm_ref[...], mixm_ref[...])[1]
            lead_ref[...] = um[um.shape[0] - carry:, :]

    @pl.when(t == 0)
    def _():
        full_ref[base - carry:base, :] = lead_ref[...] if prefix else tail0_ref[0]

    x1, up = up_project(x_ref[...], mix_ref[...])
    full_ref[base:base + tm, :] = up

    def conv_cols(lo):
        acc = full_ref[base - carry:base - carry + tm, lo:lo + FFN_COL_CHUNK] * cw_ref[0:1, lo:lo + FFN_COL_CHUNK]
        for i in range(1, CONV_FFN):
            r0 = base - carry + i * stride
            acc = acc + full_ref[r0:r0 + tm, lo:lo + FFN_COL_CHUNK] * cw_ref[i:i + 1, lo:lo + FFN_COL_CHUNK]
        return acc

    x2 = x1
    for j in range(D_FF // FFN_COL_CHUNK):
        lo = j * FFN_COL_CHUNK
        act = (_silu(conv_cols(lo)) * conv_cols(D_FF + lo)).astype(BF16)
        x2 = x2 + jnp.dot(act, wdn_ref[lo:lo + FFN_COL_CHUNK, :], preferred_element_type=F32)
    y_ref[...] = _rms(x2, nfin_ref[...])

    new_tail = full_ref[base + tm - carry:base + tm, :]
    full_ref[base - carry:base, :] = new_tail
    tail_ref[0] = new_tail


def _ffn(x, mix, lead, weights, *, nseq, tm, stride):
    wout, nffn, wup, cw, wdn, nfin = weights
    rows = x.shape[0]
    assert rows % (nseq * tm) == 0
    nt = rows // (nseq * tm)
    carry = (CONV_FFN - 1) * stride
    base = _round_up(carry, SUBLANES)
    assert tm >= carry
    resident = lambda shape: pl.BlockSpec(shape, lambda b, t: (0, 0), pipeline_mode=pl.Buffered(1))
    small = lambda shape: pl.BlockSpec(shape, lambda b, t: (0, 0))
    prefix = isinstance(lead, tuple)
    if prefix:
        assert stride == 1 and all(a.shape[0] >= carry for a in lead)
        lead_specs = [small(a.shape) for a in lead]
        lead_scratch = [pltpu.VMEM((carry, 2 * D_FF), F32)]
    else:
        lead = (lead,)
        lead_specs = [pl.BlockSpec((1, carry, 2 * D_FF), lambda b, t: (b, 0, 0))]
        lead_scratch = []
    kern = functools.partial(_ffn_kernel, tm=tm, stride=stride, prefix=prefix)
    return pl.pallas_call(
        kern,
        out_shape=(
            jax.ShapeDtypeStruct((rows, D_MODEL), F32),
            jax.ShapeDtypeStruct((nseq, carry, 2 * D_FF), F32),
        ),
        grid=(nseq, nt),
        in_specs=[
            pl.BlockSpec((tm, D_MODEL), lambda b, t: (b * nt + t, 0)),
            pl.BlockSpec((tm, D_MODEL), lambda b, t: (b * nt + t, 0)),
        ] + lead_specs + [
            resident((D_MODEL, D_MODEL)),
            small((1, D_MODEL)),
            resident((D_MODEL, 2 * D_FF)),
            small((CONV_FFN, 2 * D_FF)),
            resident((D_FF, D_MODEL)),
            small((1, D_MODEL)),
        ],
        out_specs=(
            pl.BlockSpec((tm, D_MODEL), lambda b, t: (b * nt + t, 0)),
            pl.BlockSpec((1, carry, 2 * D_FF), lambda b, t: (b, 0, 0)),
        ),
        scratch_shapes=[pltpu.VMEM((base + tm, 2 * D_FF), F32)] + lead_scratch,
        compiler_params=pltpu.CompilerParams(dimension_semantics=("arbitrary", "arbitrary"),
                                             vmem_limit_bytes=VMEM_LIMIT),
        name="out_ffn",
    )(x, mix, *lead, wout, nffn, wup, cw, wdn, nfin)


def kernel(x_prompt, x_sample, state_ret, state_gdn, state_conv_qkv, state_ffn_conv, meta_tokens, norm_mix,
           w_in, conv_gdn, gdn_a_log, gdn_dt_bias, norm_ret, norm_gdn, w_out, norm_ffn, w_up, conv_ffn,
           w_down, norm_final):
    depth = w_in.shape[0]
    assert depth == 1
    nbp, seq, _ = x_prompt.shape
    nbs, dec_seq, _ = x_sample.shape
    assert dec_seq <= SAMPLE_PAD and nbs % SAMPLE_GROUP == 0
    assert seq % (2 * PROMPT_CHUNK * PROMPT_CHUNKS_PER_STEP) == 0
    layer = 0

    w_in_bf = _cast_pad_w_in(w_in[layer])
    row = lambda v: v.reshape(1, -1).astype(F32)
    pad_ba = lambda v: jnp.pad(v.astype(F32), (N_HEADS, LANES - 2 * N_HEADS)).reshape(1, LANES)
    mixer_consts = (conv_gdn[layer], pad_ba(gdn_a_log[layer]), pad_ba(gdn_dt_bias[layer]),
                    norm_ret[layer].reshape(N_HEADS, D_HEAD), row(norm_gdn[layer]))
    nmix = row(norm_mix[layer])

    assert seq % N_META == 0 and (seq + N_META) % SAMPLE_PAD == 0
    rope_meta_row0, rope_sample_row0 = seq, seq + N_META
    rope = _rope_tables(seq + N_META + SAMPLE_PAD,
                        [(0, N_META), (rope_meta_row0, 0), (rope_sample_row0, PAST_LEN)])

    small_rows = jnp.concatenate([x_sample.reshape(nbs * dec_seq, D_MODEL), meta_tokens.astype(F32)], axis=0)
    proj_small = _proj(small_rows, nmix, w_in_bf)
    meta_row0 = nbs * dec_seq

    zero_state = jnp.zeros((1, N_HEADS, D_HEAD, D_HEAD), F32)
    zero_cq = jnp.zeros((1, CONV_GDN - 1, GDN_QKV), F32)
    mix_m, sret_m, sgdn_m, cq_m = _mixer(proj_small, meta_row0, 1, N_META, zero_state, zero_state, zero_cq, rope,
                                         rope_meta_row0, mixer_consts, bb=1, c=N_META, nch=1, n_valid=N_META,
                                         shared_init=True)

    xp = x_prompt.reshape(nbp * seq, D_MODEL)
    mix_p, sret_p, sgdn_p, cq_p, w_out_bf, w_up_bf, w_down_bf = _proj_mixer(
        xp, nmix, w_in_bf, nbp, sret_m, sgdn_m, cq_m, rope, mixer_consts,
        (w_out[layer], w_up[layer], w_down[layer]), c=PROMPT_CHUNK, nch=PROMPT_CHUNKS_PER_STEP)
    ffn_weights = (w_out_bf, row(norm_ffn[layer]), w_up_bf, conv_ffn[layer], w_down_bf, row(norm_final))

    meta_lead = (meta_tokens.astype(F32), mix_m.reshape(N_META, D_MODEL))
    y_p, cf_p = _ffn(xp, mix_p, meta_lead, ffn_weights, nseq=nbp, tm=512, stride=1)
    y_prompt = y_p.reshape(nbp, seq, D_MODEL)

    mix_s, sret_s, sgdn_s, cq_s = _mixer(proj_small, 0, nbs, SAMPLE_PAD, state_ret[layer], state_gdn[layer],
                                         jnp.swapaxes(state_conv_qkv[layer], 0, 1), rope, rope_sample_row0,
                                         mixer_consts, bb=SAMPLE_SEQS_PER_STEP, c=SAMPLE_PAD, nch=1,
                                         n_valid=dec_seq, shared_init=False, stored=dec_seq, batch_minor=True)
    cq_s = jnp.swapaxes(cq_s, 0, 1)
    ng = nbs // SAMPLE_GROUP
    to_tmajor = lambda a: a.reshape(ng, SAMPLE_GROUP, a.shape[1], a.shape[2]).transpose(0, 2, 1, 3)
    xs_t = to_tmajor(x_sample).reshape(nbs * dec_seq, D_MODEL)
    mix_t = to_tmajor(mix_s[:, :dec_seq]).reshape(nbs * dec_seq, D_MODEL)
    cf0_t = to_tmajor(state_ffn_conv[layer]).reshape(ng, (CONV_FFN - 1) * SAMPLE_GROUP, 2 * D_FF)
    y_s_t, cf_s_t = _ffn(xs_t, mix_t, cf0_t, ffn_weights, nseq=ng, tm=dec_seq * SAMPLE_GROUP,
                         stride=SAMPLE_GROUP)
    y_sample = y_s_t.reshape(ng, dec_seq, SAMPLE_GROUP, D_MODEL).transpose(0, 2, 1, 3).reshape(nbs, dec_seq, D_MODEL)
    cf_s = cf_s_t.reshape(ng, CONV_FFN - 1, SAMPLE_GROUP, 2 * D_FF).transpose(0, 2, 1, 3).reshape(
        nbs, CONV_FFN - 1, 2 * D_FF)

    return (y_prompt, y_sample, sret_p[None], sgdn_p[None], cq_p[None], cf_p[None],
            sret_s[None], sgdn_s[None], cq_s[None], cf_s[None])
```

```python
import functools

import jax
import numpy as np
import jax.numpy as jnp
from jax import lax
from jax.experimental import pallas as pl
from jax.experimental.pallas import tpu as pltpu

F32 = jnp.float32
BF16 = jnp.bfloat16

D_MODEL = 1024
N_META = 16
PAST_LEN = 16384
N_HEADS = 4
D_HEAD = 128
D_GRP = N_HEADS * D_HEAD
GDN_QKV = 3 * D_GRP
CONV_GDN = 4
CONV_FFN = 3
D_FF = 2816
ROPE_THETA = 10000.0
EPS = 1e-6

OFF_RQ, OFF_RK, OFF_RV, OFF_RG = 0, D_GRP, 2 * D_GRP, 3 * D_GRP
OFF_QKV = 4 * D_GRP
OFF_GG = OFF_QKV + GDN_QKV
OFF_BA = OFF_GG + D_GRP
IN_WIDTH = OFF_BA + 2 * N_HEADS
LANES = 128
SUBLANES = 8
MXU_WIDTH = 256
V7X_VMEM_BYTES = 64 * 1024 * 1024
IN_PAD = OFF_BA + LANES

PROMPT_CHUNK = 128
PROMPT_CHUNKS_PER_STEP = 2
SAMPLE_PAD = 8
SAMPLE_SEQS_PER_STEP = 16
SAMPLE_GROUP = 64
VMEM_LIMIT = V7X_VMEM_BYTES * 7 // 8


def _round_up(n, m):
    return (n + m - 1) // m * m


def _mm(a, b):
    return jnp.dot(a.astype(BF16), b.astype(BF16), preferred_element_type=F32)


def _mm_nt(a, b):
    return lax.dot_general(a.astype(BF16), b.astype(BF16), (((1,), (1,)), ((), ())),
                           preferred_element_type=F32)


def _mm_tn(a, b):
    return lax.dot_general(a.astype(BF16), b.astype(BF16), (((0,), (0,)), ((), ())),
                           preferred_element_type=F32)


def _split3(x):
    p0 = x.astype(BF16)
    r = x - p0.astype(F32)
    p1 = r.astype(BF16)
    p2 = (r - p1.astype(F32)).astype(BF16)
    return p0, p1, p2


def _shift_rows(x, lead):
    k = lead.shape[0]
    row = lax.broadcasted_iota(jnp.int32, x.shape, 0)
    out = pltpu.roll(x, k, 0)
    for i in range(k):
        out = jnp.where(row == i, lead[i:i + 1, :], out)
    return out


def _silu(x):
    return x * jax.nn.sigmoid(x)


def _rms(x, w):
    return x * lax.rsqrt(jnp.mean(x * x, axis=-1, keepdims=True) + EPS) * w


def _rope_kernel(invf_ref, cos_ref, sin_ref, *, segments):
    shape = cos_ref.shape
    r = lax.broadcasted_iota(jnp.int32, shape, 0)
    pos = r + (segments[0][1] - segments[0][0])
    for row0, pos0 in segments[1:]:
        pos = jnp.where(r >= row0, r + (pos0 - row0), pos)
    ang = pos.astype(F32) * invf_ref[...]
    lane = lax.broadcasted_iota(jnp.int32, shape, 1)
    sin = jnp.sin(ang)
    cos_ref[...] = jnp.cos(ang)
    sin_ref[...] = jnp.where(lane < D_HEAD // 2, -sin, sin)


def _rope_tables(n, segments):
    half = D_HEAD // 2
    inv_freq = ROPE_THETA ** (-jnp.arange(half, dtype=F32) / half)
    invf2 = jnp.concatenate([inv_freq, inv_freq]).reshape(1, LANES)
    out = jax.ShapeDtypeStruct((n, LANES), F32)
    return pl.pallas_call(functools.partial(_rope_kernel, segments=tuple(segments)), out_shape=(out, out),
                          name="rope_tables")(invf2)


W_IN_COLS = 11 * LANES


def _proj_kernel(x_ref, nw_ref, wt_ref, o_ref, wbf_ref, h_ref):
    first = pl.program_id(0) * W_IN_COLS

    @pl.when(first == 0)
    def _():
        h_ref[...] = _rms(x_ref[...], nw_ref[...]).astype(BF16)

    r = lax.broadcasted_iota(jnp.int32, wt_ref.shape, 0)
    w_blk = jnp.where(first + r < IN_WIDTH, wt_ref[...], 0.0).T.astype(BF16)
    wbf_ref[...] = w_blk
    o_ref[...] = jnp.dot(h_ref[...], w_blk, preferred_element_type=F32)


def _proj(x, norm_w, w):
    rows = x.shape[0]
    return pl.pallas_call(
        _proj_kernel,
        out_shape=(jax.ShapeDtypeStruct((rows, IN_PAD), F32), jax.ShapeDtypeStruct((D_MODEL, IN_PAD), BF16)),
        grid=(IN_PAD // W_IN_COLS,),
        in_specs=[
            pl.BlockSpec((rows, D_MODEL), lambda j: (0, 0)),
            pl.BlockSpec((1, D_MODEL), lambda j: (0, 0)),
            pl.BlockSpec((W_IN_COLS, D_MODEL), lambda j: (j, 0)),
        ],
        out_specs=(pl.BlockSpec((rows, W_IN_COLS), lambda j: (0, j)),
                   pl.BlockSpec((D_MODEL, W_IN_COLS), lambda j: (0, j))),
        scratch_shapes=[pltpu.VMEM((rows, D_MODEL), BF16)],
        compiler_params=pltpu.CompilerParams(dimension_semantics=("arbitrary",),
                                             vmem_limit_bytes=VMEM_LIMIT),
        name="in_proj",
    )(x, norm_w, w.T)


def _unit_lower_inverses(mats, c, tick=lambda: None, nilpotent=SUBLANES):
    ri = lax.broadcasted_iota(jnp.int32, (c, c), 0)
    ci = lax.broadcasted_iota(jnp.int32, (c, c), 1)
    eye = (ri == ci).astype(F32)
    diag_blk = (ri // SUBLANES) == (ci // SUBLANES)
    ad = [jnp.where(diag_blk, a, 0.0) for a in mats]
    assert 1 <= nilpotent <= SUBLANES
    n_factors = max(0, (nilpotent - 1).bit_length() - 1)
    t = [eye - x for x in ad]
    power = ad
    for i in range(2):
        if i < n_factors:
            power = [_mm(x, x) for x in power]
            t = [x + _mm(x, s) for x, s in zip(t, power)]
        tick()
    tick()
    s = SUBLANES
    while s < c:
        level = ((ri // (2 * s)) == (ci // (2 * s))) & ((ri // s) != (ci // s))
        off = [jnp.where(level, a, 0.0) for a in mats]
        lt = [_mm(o, x) for o, x in zip(off, t)]
        tick()
        t = [x - _mm(x, y) for x, y in zip(t, lt)]
        tick()
        s *= 2
    return t


XS_TOP = SUBLANES


def _mixer_init(sret0_ref, sgdn0_ref, cq0_ref, sret_ref, sgdn_ref, xs_ref, batch_minor=False):
    tail = CONV_GDN - 1
    sret_ref[...] = sret0_ref[...]
    sgdn_ref[...] = sgdn0_ref[...]
    if batch_minor:
        for b in range(xs_ref.shape[0]):
            xs_ref[b, XS_TOP - tail:XS_TOP, :] = cq0_ref[:, b, :]
    else:
        xs_ref[:, XS_TOP - tail:XS_TOP, :] = cq0_ref[...]


def _mixer_block(getp, put_mix, cos2, sin2, const_refs, sret_ref, sgdn_ref, cq_ref, xs_ref, *,
                 bb, c, nch, n_valid, tick=lambda: None, batch_minor=False):
    (tril_ref, dint_ref, qdec_ref, kdec_ref, cdec_ref, cw_ref, alog_ref, dtb_ref, nret_ref, ngdn_ref) = const_refs
    rows = nch * c
    tail = CONV_GDN - 1
    top = XS_TOP
    ri = lax.broadcasted_iota(jnp.int32, (c, c), 0)
    cj = lax.broadcasted_iota(jnp.int32, (c, c), 1)
    tri = ri >= cj
    strict = ri > cj
    tril_bf = tril_ref[...]
    scale = D_HEAD ** -0.5
    heads = range(N_HEADS)
    seqs = range(bb)
    chunks = [slice(j * c, (j + 1) * c) for j in range(nch)]
    join = lambda parts: parts[0] if len(parts) == 1 else jnp.concatenate(parts, axis=0)

    ret = {}
    for b in seqs:
        for h in heads:
            lo = h * D_HEAD
            q = getp(b, OFF_RQ + lo, OFF_RQ + lo + D_HEAD)
            k = getp(b, OFF_RK + lo, OFF_RK + lo + D_HEAD)
            v = getp(b, OFF_RV + lo, OFF_RV + lo + D_HEAD)
            qr = q * cos2 + pltpu.roll(q, D_HEAD // 2, 1) * sin2
            kr = (k * cos2 + pltpu.roll(k, D_HEAD // 2, 1) * sin2) * scale
            ret[b, h] = (qr, kr, v)
    tick()
    rtasks = [(b, h, j) for j in range(nch) for b in seqs for h in heads]
    scores = {t: _mm_nt(ret[t[0], t[1]][0][chunks[t[2]]], ret[t[0], t[1]][1][chunks[t[2]]]) * dint_ref[t[1]]
              for t in rtasks}
    tick()
    intra = {t: _mm(scores[t], ret[t[0], t[1]][2][chunks[t[2]]]) for t in rtasks}
    tick()
    kv = {t: _mm_tn(ret[t[0], t[1]][1][chunks[t[2]]] * kdec_ref[t[1]], ret[t[0], t[1]][2][chunks[t[2]]])
          for t in rtasks}
    tick()
    pairs = [(b, h) for b in seqs for h in heads]
    s_ret = {bh: sret_ref[bh[0], bh[1]] for bh in pairs}
    o_ret = {}
    for j in range(nch):
        for b, h in pairs:
            o_ret[b, h, j] = intra[b, h, j] + _mm(ret[b, h][0][chunks[j]] * qdec_ref[h], s_ret[b, h])
        for b, h in pairs:
            s_ret[b, h] = cdec_ref[h] * s_ret[b, h] + kv[b, h, j]
        tick()
    o_ret = {bh: join([o_ret[bh[0], bh[1], j] for j in range(nch)]) for bh in pairs}
    mu = {bh: jnp.mean(o_ret[bh], axis=-1, keepdims=True) for bh in pairs}
    cen = {bh: o_ret[bh] - mu[bh] for bh in pairs}
    var = {bh: jnp.mean(cen[bh] * cen[bh], axis=-1, keepdims=True) for bh in pairs}
    for b, h in pairs:
        lo = h * D_HEAD
        gate = getp(b, OFF_RG + lo, OFF_RG + lo + D_HEAD)
        o = cen[b, h] * lax.rsqrt(var[b, h] + EPS) * nret_ref[h:h + 1, :]
        put_mix(b, lo, o * _silu(gate))
        sret_ref[b, h] = s_ret[b, h]
    tick()

    qkvs, beta_alls, cum_cols, cum_rows, cum_tots = {}, {}, {}, {}, {}
    for b in seqs:
        assert CONV_GDN == 4
        w0, w1, w2, w3 = (cw_ref[i:i + 1, :] for i in range(CONV_GDN))
        u0 = getp(b, OFF_QKV, OFF_QKV + GDN_QKV)
        xs_ref[b, top:top + rows, :] = u0
        u1 = _shift_rows(u0, xs_ref[b, top - 1:top, :])
        p = u0 * w1 + u1 * w0
        p_lead = xs_ref[b, top - 2:top, :] * w1 + xs_ref[b, top - 3:top - 1, :] * w0
        conv = u0 * w3 + u1 * w2 + _shift_rows(p, p_lead)
        last = n_valid if nch == 1 else rows
        new_tail = xs_ref[b, top + last - tail:top + last, :]
        xs_ref[b, top - tail:top, :] = new_tail
        if batch_minor:
            cq_ref[:, b, :] = new_tail
        else:
            cq_ref[b] = new_tail
        qkvs[b] = _silu(conv)
        tick()

        ba = getp(b, OFF_BA, OFF_BA + LANES)
        beta_all = jax.nn.sigmoid(ba)
        z = ba + dtb_ref[...]
        softplus = jnp.maximum(z, 0.0) + jnp.log1p(jnp.exp(-jnp.abs(z)))
        g_all = -jnp.exp(alog_ref[...]) * softplus
        if n_valid < c:
            row = lax.broadcasted_iota(jnp.int32, (rows, LANES), 0)
            rowmask = (row < n_valid).astype(F32)
            beta_all = beta_all * rowmask
            g_all = g_all * rowmask
        beta_alls[b] = beta_all
        for j, rs in enumerate(chunks):
            parts = _split3(g_all[rs])
            cum_cols[b, j] = sum(jnp.dot(tril_bf, g, preferred_element_type=F32) for g in parts)
            cum_tots[b, j] = jnp.broadcast_to(cum_cols[b, j][c - 1:c, :], (D_HEAD, LANES))
            cum_rows[b, j] = cum_cols[b, j].T
    tick()

    tasks = [(j, b, h) for j in range(nch) for b in seqs for h in heads]
    qs, ks, vs, betas, ecums, kdecs, cdecs, amats, qkms = [], [], [], [], [], [], [], [], []
    raw = [(qkvs[b][chunks[j], h * D_HEAD:(h + 1) * D_HEAD],
            qkvs[b][chunks[j], D_GRP + h * D_HEAD:D_GRP + (h + 1) * D_HEAD]) for j, b, h in tasks]
    sumsq = [(jnp.sum(q * q, axis=-1, keepdims=True), jnp.sum(k * k, axis=-1, keepdims=True)) for q, k in raw]
    for i, (j, b, h) in enumerate(tasks):
        rs = chunks[j]
        lo = h * D_HEAD
        v = qkvs[b][rs, 2 * D_GRP + lo:2 * D_GRP + lo + D_HEAD]
        q = raw[i][0] * lax.rsqrt(sumsq[i][0] + EPS) * scale
        k = raw[i][1] * lax.rsqrt(sumsq[i][1] + EPS)
        beta = jnp.broadcast_to(beta_alls[b][rs, h:h + 1], (c, LANES))
        cum = jnp.broadcast_to(cum_cols[b, j][:, N_HEADS + h:N_HEADS + h + 1], (c, LANES))
        cum_row = cum_rows[b, j][N_HEADS + h:N_HEADS + h + 1, :]
        dmask = jnp.exp(jnp.where(tri, cum[:, :c] - cum_row, -jnp.inf))
        cum_last = jnp.broadcast_to(cum_tots[b, j][:, N_HEADS + h:N_HEADS + h + 1], (D_HEAD, LANES))
        kq = _mm_nt(jnp.concatenate([k, q], axis=0), k)
        amats.append(jnp.where(strict, beta[:, :c] * kq[:c] * dmask, 0.0))
        qkms.append(jnp.where(tri, kq[c:] * dmask, 0.0))
        qs.append(q)
        ks.append(k)
        vs.append(v)
        betas.append(beta)
        ecums.append(jnp.exp(cum))
        kdecs.append(jnp.exp(cum_last[:c] - cum))
        cdecs.append(jnp.exp(cum_last))
        if i % N_HEADS == N_HEADS - 1:
            tick()

    tinv = _unit_lower_inverses(amats, c, tick, nilpotent=min(SUBLANES, n_valid))
    eye = (ri == cj).astype(F32)
    sols = []
    for i in range(len(tasks)):
        rhs = jnp.concatenate([vs[i] * betas[i], ks[i] * (betas[i] * ecums[i])], axis=1)
        sols.append(rhs + _mm(tinv[i] - eye, rhs))
    tick()

    s_gdn = {bh: sgdn_ref[bh[0], bh[1]] for bh in pairs}
    o_gdn = {}
    for j in range(nch):
        idx = {tasks[i][1:]: i for i in range(len(tasks)) if tasks[i][0] == j}
        lhs = {bh: jnp.concatenate([sols[i][:, D_HEAD:], qs[i] * ecums[i]], axis=0) for bh, i in idx.items()}
        both = {bh: _mm(lhs[bh], s_gdn[bh]) for bh in idx}
        tick()
        w = {bh: sols[i][:, :D_HEAD] - both[bh][:c] for bh, i in idx.items()}
        for bh, i in idx.items():
            o_gdn[bh + (j,)] = both[bh][c:] + _mm(qkms[i], w[bh])
        tick()
        upd = {bh: _mm_tn(ks[i] * kdecs[i], w[bh]) for bh, i in idx.items()}
        for bh, i in idx.items():
            s_gdn[bh] = cdecs[i] * s_gdn[bh] + upd[bh]
        tick()
    o_gdn = {bh: join([o_gdn[bh + (j,)] for j in range(nch)]) for bh in pairs}
    msq = {bh: jnp.mean(o_gdn[bh] * o_gdn[bh], axis=-1, keepdims=True) for bh in pairs}
    for b, h in pairs:
        lo = h * D_HEAD
        gate = getp(b, OFF_GG + lo, OFF_GG + lo + D_HEAD)
        o = o_gdn[b, h] * lax.rsqrt(msq[b, h] + EPS) * ngdn_ref[...]
        put_mix(b, D_GRP + lo, o * _silu(gate))
        sgdn_ref[b, h] = s_gdn[b, h]


N_MIXER_CONSTS = 10


def _mixer_kernel(p_ref, sret0_ref, sgdn0_ref, cq0_ref, cos_ref, sin_ref, *rest, bb, c, nch, n_valid, stored,
                  batch_minor):
    const_refs, (mix_ref, sret_ref, sgdn_ref, cq_ref, xs_ref) = rest[:N_MIXER_CONSTS], rest[N_MIXER_CONSTS:]
    rows = nch * c

    @pl.when(pl.program_id(1) == 0)
    def _():
        _mixer_init(sret0_ref, sgdn0_ref, cq0_ref, sret_ref, sgdn_ref, xs_ref, batch_minor)

    def put_mix(b, lo, value):
        mix_ref[b, :, lo:lo + D_HEAD] = value.astype(mix_ref.dtype)

    def getp(b, lo, hi):
        blk = p_ref[b * stored:(b + 1) * stored, lo:hi]
        if stored < rows:
            blk = jnp.concatenate([blk, jnp.zeros((rows - stored, hi - lo), F32)], axis=0)
        return blk

    _mixer_block(getp, put_mix, cos_ref[...], sin_ref[...], const_refs, sret_ref, sgdn_ref, cq_ref, xs_ref,
                 bb=bb, c=c, nch=nch, n_valid=n_valid, batch_minor=batch_minor)


PROJ_PANEL = MXU_WIDTH


def _mixer_block_ticks(bb, c, nch):
    levels = (c // SUBLANES).bit_length() - 1
    retention = 4 + nch + 1
    gdn_prep = bb + 1 + bb * nch
    inverse = 3 + 2 * levels
    return retention + gdn_prep + inverse + 1 + 3 * nch


def _proj_mixer_kernel(x0_ref, xa_ref, xb_ref, nw_ref, w_ref, sret0_ref, sgdn0_ref, cq0_ref, cos_ref, sin_ref,
                       *rest, c, nch, steps_per_seq, n_cast):
    const_refs = rest[:N_MIXER_CONSTS]
    cast_in = rest[N_MIXER_CONSTS:N_MIXER_CONSTS + n_cast]
    mix_ref, sret_ref, sgdn_ref, cq_ref = rest[N_MIXER_CONSTS + n_cast:N_MIXER_CONSTS + n_cast + 4]
    cast_out = rest[N_MIXER_CONSTS + n_cast + 4:N_MIXER_CONSTS + 2 * n_cast + 4]
    pja_ref, pjb_ref, xs_ref = rest[N_MIXER_CONSTS + 2 * n_cast + 4:]
    step = pl.program_id(0)
    rows = nch * c

    for src_ref, dst_ref in zip(cast_in, cast_out):
        dst_ref[...] = src_ref[...].astype(dst_ref.dtype)

    def project(x_ref, dst_ref, n_ticks):
        h = _rms(x_ref[...], nw_ref[...]).astype(BF16)
        panels = [(lo, min(lo + PROJ_PANEL, IN_PAD)) for lo in range(0, IN_PAD, PROJ_PANEL)]
        n_panels = len(panels)
        calls = [0]

        def emit():
            lo, hi = panels.pop(0)
            dst_ref[:, lo:hi] = jnp.dot(h, w_ref[:, lo:hi], preferred_element_type=F32)

        def tick():
            calls[0] += 1
            due = min(n_panels, -(-calls[0] * n_panels // n_ticks))
            while n_panels - len(panels) < due:
                emit()

        def flush():
            assert n_ticks == 1 or calls[0] == n_ticks, (calls[0], n_ticks)
            while panels:
                emit()

        return tick, flush

    @pl.when(step == 0)
    def _():
        project(x0_ref, pja_ref, 1)[1]()

    @pl.when(lax.rem(step, steps_per_seq) == 0)
    def _():
        _mixer_init(sret0_ref, sgdn0_ref, cq0_ref, sret_ref, sgdn_ref, xs_ref)

    for half, (cur_ref, x_next_ref, nxt_ref) in enumerate(((pja_ref, xa_ref, pjb_ref), (pjb_ref, xb_ref, pja_ref))):
        r0 = half * rows
        tick, flush = project(x_next_ref, nxt_ref, _mixer_block_ticks(1, c, nch))

        def put_mix(b, lo, value, r0=r0):
            mix_ref[r0:r0 + rows, lo:lo + D_HEAD] = value.astype(mix_ref.dtype)

        _mixer_block(lambda b, lo, hi, cur_ref=cur_ref: cur_ref[:, lo:hi], put_mix,
                     cos_ref[r0:r0 + rows, :], sin_ref[r0:r0 + rows, :], const_refs,
                     sret_ref, sgdn_ref, cq_ref, xs_ref, bb=1, c=c, nch=nch, n_valid=c, tick=tick)
        flush()


def _retention_decay_tables(c, n_valid):
    f32 = np.float32
    lg = np.log1p(-np.power(f32(2.0), f32(-5.0) - np.arange(N_HEADS, dtype=f32)))[:, None].astype(f32)
    idx = np.arange(c, dtype=f32)
    diff = idx[:, None] - idx[None, :]
    dint = np.where(diff[None] >= 0, np.exp(lg[:, :, None] * np.maximum(diff[None], 0)), f32(0.0)).astype(f32)
    qdec = np.exp(lg * (idx + f32(1.0))).astype(f32)
    kdec = np.where(idx[None, :] < n_valid, np.exp(lg * np.minimum(f32(n_valid) - f32(1.0) - idx, c)), f32(0.0))
    cdec = np.exp(lg * f32(n_valid)).astype(f32)
    bc = lambda t: np.broadcast_to(t.astype(f32)[:, :, None], t.shape + (LANES,))
    return dint, bc(qdec), bc(kdec), np.broadcast_to(cdec[:, :, None], (N_HEADS, 1, LANES))


def _mixer_const_operands(c, n_valid, consts):
    cw, alog, dtb, nret, ngdn = consts
    dint, qdec, kdec, cdec = _retention_decay_tables(c, n_valid)
    idx = np.arange(c)
    tril = (idx[:, None] >= idx[None, :]).astype(np.float32)
    arrays = (jnp.asarray(tril, BF16), jnp.asarray(dint), jnp.asarray(qdec), jnp.asarray(kdec), jnp.asarray(cdec),
              cw, alog, dtb, nret, ngdn)
    assert len(arrays) == N_MIXER_CONSTS
    return arrays, [a.shape for a in arrays]


def _mixer(proj, row0, nb, length, sret0, sgdn0, cq0, rope, rope_row0, consts, *, bb, c, nch, n_valid,
           shared_init, stored=None, batch_minor=False):
    rows = nch * c
    assert nb % bb == 0 and length % rows == 0 and rope_row0 % rows == 0
    assert not shared_init or bb == 1
    assert n_valid == c or nch == 1
    assert bb == 1 or length == rows
    const_arrays, const_shapes = _mixer_const_operands(c, n_valid, consts)
    nsteps = length // rows
    stored = rows if stored is None else stored
    assert stored == rows or (nsteps == 1 and stored >= n_valid and (bb * stored) % SUBLANES == 0)
    assert row0 % (bb * stored) == 0
    blk0 = row0 // (bb * stored)
    init_idx = (lambda b, i: (0, 0, 0, 0)) if shared_init else (lambda b, i: (b, 0, 0, 0))
    init_idx3 = (lambda b, i: (0, 0, 0)) if shared_init else (lambda b, i: (b, 0, 0))
    whole = lambda shape: pl.BlockSpec(shape, lambda b, i: (0,) * len(shape))
    rope_spec = pl.BlockSpec((rows, LANES), lambda b, i: (rope_row0 // rows + i, 0))
    state_shape = (bb, N_HEADS, D_HEAD, D_HEAD)
    tail = CONV_GDN - 1
    assert not (batch_minor and shared_init)
    cq_shape, cq_block = (nb, tail, GDN_QKV), (bb, tail, GDN_QKV)
    cq_idx = lambda b, i: (b, 0, 0)
    if batch_minor:
        cq_shape, cq_block = (tail, nb, GDN_QKV), (tail, bb, GDN_QKV)
        cq_idx = init_idx3 = lambda b, i: (0, b, 0)
    kern = functools.partial(_mixer_kernel, bb=bb, c=c, nch=nch, n_valid=n_valid, stored=stored,
                             batch_minor=batch_minor)
    return pl.pallas_call(
        kern,
        out_shape=(
            jax.ShapeDtypeStruct((nb, length, D_MODEL), BF16),
            jax.ShapeDtypeStruct((nb, N_HEADS, D_HEAD, D_HEAD), F32),
            jax.ShapeDtypeStruct((nb, N_HEADS, D_HEAD, D_HEAD), F32),
            jax.ShapeDtypeStruct(cq_shape, F32),
        ),
        grid=(nb // bb, nsteps),
        in_specs=[
            pl.BlockSpec((bb * stored, IN_PAD), lambda b, i: (blk0 + b * nsteps + i, 0)),
            pl.BlockSpec(state_shape, init_idx),
            pl.BlockSpec(state_shape, init_idx),
            pl.BlockSpec(cq_block, init_idx3),
            rope_spec,
            rope_spec,
        ] + [whole(shape) for shape in const_shapes],
        out_specs=(
            pl.BlockSpec((bb, rows, D_MODEL), lambda b, i: (b, i, 0)),
            pl.BlockSpec(state_shape, lambda b, i: (b, 0, 0, 0)),
            pl.BlockSpec(state_shape, lambda b, i: (b, 0, 0, 0)),
            pl.BlockSpec(cq_block, cq_idx),
        ),
        scratch_shapes=[pltpu.VMEM((bb, XS_TOP + rows, GDN_QKV), F32)],
        compiler_params=pltpu.CompilerParams(dimension_semantics=("arbitrary", "arbitrary"),
                                             vmem_limit_bytes=VMEM_LIMIT),
        name="mixer",
    )(proj, sret0, sgdn0, cq0, rope[0], rope[1], *const_arrays)


def _row_slab(nrows, nsteps):
    for hold in (1, 2, 4, 8):
        slabs = nsteps // hold
        if nsteps % hold == 0 and nrows % slabs == 0 and (nrows // slabs) % (2 * SUBLANES) == 0:
            return nrows // slabs, hold
    raise ValueError((nrows, nsteps))


def _proj_mixer(x, norm_w, w_bf, nseq, sret0, sgdn0, cq0, rope, consts, to_bf16, *, c, nch):
    rows = nch * c
    total = x.shape[0]
    length = total // nseq
    assert total % nseq == 0 and length % (2 * rows) == 0
    nblk = total // rows
    steps_per_seq = length // (2 * rows)
    const_arrays, const_shapes = _mixer_const_operands(c, c, consts)
    whole = lambda shape, **kw: pl.BlockSpec(shape, lambda s: (0,) * len(shape), **kw)
    rope_spec = pl.BlockSpec((2 * rows, LANES), lambda s: (lax.rem(s, steps_per_seq), 0))
    state_shape = (1, N_HEADS, D_HEAD, D_HEAD)
    tail = CONV_GDN - 1
    nsteps = nblk // 2
    slabs = [_row_slab(w.shape[0], nsteps) for w in to_bf16]
    cast_specs = [pl.BlockSpec((r, w.shape[1]), lambda s, hold=hold: (s // hold, 0))
                  for w, (r, hold) in zip(to_bf16, slabs)]
    kern = functools.partial(_proj_mixer_kernel, c=c, nch=nch, steps_per_seq=steps_per_seq, n_cast=len(to_bf16))
    return pl.pallas_call(
        kern,
        out_shape=(
            jax.ShapeDtypeStruct((total, D_MODEL), BF16),
            jax.ShapeDtypeStruct((nseq, N_HEADS, D_HEAD, D_HEAD), F32),
            jax.ShapeDtypeStruct((nseq, N_HEADS, D_HEAD, D_HEAD), F32),
            jax.ShapeDtypeStruct((nseq, tail, GDN_QKV), F32),
        ) + tuple(jax.ShapeDtypeStruct(w.shape, BF16) for w in to_bf16),
        grid=(nsteps,),
        in_specs=[
            pl.BlockSpec((rows, D_MODEL), lambda s: (0, 0), pipeline_mode=pl.Buffered(1)),
            pl.BlockSpec((rows, D_MODEL), lambda s: (2 * s + 1, 0)),
            pl.BlockSpec((rows, D_MODEL), lambda s: (jnp.minimum(2 * s + 2, nblk - 1), 0)),
            whole((1, D_MODEL)),
            whole((D_MODEL, IN_PAD), pipeline_mode=pl.Buffered(1)),
            whole(state_shape),
            whole(state_shape),
            whole((1, tail, GDN_QKV)),
            rope_spec,
            rope_spec,
        ] + [whole(shape) for shape in const_shapes] + cast_specs,
        out_specs=(
            pl.BlockSpec((2 * rows, D_MODEL), lambda s: (s, 0)),
            pl.BlockSpec(state_shape, lambda s: (s // steps_per_seq, 0, 0, 0)),
            pl.BlockSpec(state_shape, lambda s: (s // steps_per_seq, 0, 0, 0)),
            pl.BlockSpec((1, tail, GDN_QKV), lambda s: (s // steps_per_seq, 0, 0)),
        ) + tuple(cast_specs),
        scratch_shapes=[pltpu.VMEM((rows, IN_PAD), F32), pltpu.VMEM((rows, IN_PAD), F32),
                        pltpu.VMEM((1, XS_TOP + rows, GDN_QKV), F32)],
        compiler_params=pltpu.CompilerParams(dimension_semantics=("arbitrary",),
                                             vmem_limit_bytes=VMEM_LIMIT),
        name="proj_mixer",
    )(x, x, x, norm_w, w_bf, sret0, sgdn0, cq0, rope[0], rope[1], *const_arrays, *to_bf16)


FFN_COL_CHUNK = D_FF // 11


def _ffn_kernel(x_ref, mix_ref, *rest, tm, stride, prefix):
    if prefix:
        (xm_ref, mixm_ref, wout_ref, nffn_ref, wup_ref, cw_ref, wdn_ref, nfin_ref,
         y_ref, tail_ref, full_ref, lead_ref) = rest
    else:
        tail0_ref, wout_ref, nffn_ref, wup_ref, cw_ref, wdn_ref, nfin_ref, y_ref, tail_ref, full_ref = rest
    t = pl.program_id(1)
    carry = (CONV_FFN - 1) * stride
    base = _round_up(carry, SUBLANES)

    def up_project(x, mix):
        x1 = x + jnp.dot(mix, wout_ref[...], preferred_element_type=F32)
        h = _rms(x1, nffn_ref[...]).astype(BF16)
        return x1, jnp.dot(h, wup_ref[...], preferred_element_type=F32)

    if prefix:
        @pl.when((pl.program_id(0) == 0) & (t == 0))
        def _():
            um = up_project(xm_ref[...], mixm_ref[...])[1]
            lead_ref[...] = um[um.shape[0] - carry:, :]

    @pl.when(t == 0)
    def _():
        full_ref[base - carry:base, :] = lead_ref[...] if prefix else tail0_ref[0]

    x1, up = up_project(x_ref[...], mix_ref[...])
    full_ref[base:base + tm, :] = up

    def conv_cols(lo):
        acc = full_ref[base - carry:base - carry + tm, lo:lo + FFN_COL_CHUNK] * cw_ref[0:1, lo:lo + FFN_COL_CHUNK]
        for i in range(1, CONV_FFN):
            r0 = base - carry + i * stride
            acc = acc + full_ref[r0:r0 + tm, lo:lo + FFN_COL_CHUNK] * cw_ref[i:i + 1, lo:lo + FFN_COL_CHUNK]
        return acc

    x2 = x1
    for j in range(D_FF // FFN_COL_CHUNK):
        lo = j * FFN_COL_CHUNK
        act = (_silu(conv_cols(lo)) * conv_cols(D_FF + lo)).astype(BF16)
        x2 = x2 + jnp.dot(act, wdn_ref[lo:lo + FFN_COL_CHUNK, :], preferred_element_type=F32)
    y_ref[...] = _rms(x2, nfin_ref[...])

    new_tail = full_ref[base + tm - carry:base + tm, :]
    full_ref[base - carry:base, :] = new_tail
    tail_ref[0] = new_tail


def _ffn(x, mix, lead, weights, *, nseq, tm, stride):
    wout, nffn, wup, cw, wdn, nfin = weights
    rows = x.shape[0]
    assert rows % (nseq * tm) == 0
    nt = rows // (nseq * tm)
    carry = (CONV_FFN - 1) * stride
    base = _round_up(carry, SUBLANES)
    assert tm >= carry
    resident = lambda shape: pl.BlockSpec(shape, lambda b, t: (0, 0), pipeline_mode=pl.Buffered(1))
    small = lambda shape: pl.BlockSpec(shape, lambda b, t: (0, 0))
    prefix = isinstance(lead, tuple)
    if prefix:
        assert stride == 1 and all(a.shape[0] >= carry for a in lead)
        lead_specs = [small(a.shape) for a in lead]
        lead_scratch = [pltpu.VMEM((carry, 2 * D_FF), F32)]
    else:
        lead = (lead,)
        lead_specs = [pl.BlockSpec((1, carry, 2 * D_FF), lambda b, t: (b, 0, 0))]
        lead_scratch = []
    kern = functools.partial(_ffn_kernel, tm=tm, stride=stride, prefix=prefix)
    return pl.pallas_call(
        kern,
        out_shape=(
            jax.ShapeDtypeStruct((rows, D_MODEL), F32),
            jax.ShapeDtypeStruct((nseq, carry, 2 * D_FF), F32),
        ),
        grid=(nseq, nt),
        in_specs=[
            pl.BlockSpec((tm, D_MODEL), lambda b, t: (b * nt + t, 0)),
            pl.BlockSpec((tm, D_MODEL), lambda b, t: (b * nt + t, 0)),
        ] + lead_specs + [
            resident((D_MODEL, D_MODEL)),
            small((1, D_MODEL)),
            resident((D_MODEL, 2 * D_FF)),
            small((CONV_FFN, 2 * D_FF)),
            resident((D_FF, D_MODEL)),
            small((1, D_MODEL)),
        ],
        out_specs=(
            pl.BlockSpec((tm, D_MODEL), lambda b, t: (b * nt + t, 0)),
            pl.BlockSpec((1, carry, 2 * D_FF), lambda b, t: (b, 0, 0)),
        ),
        scratch_shapes=[pltpu.VMEM((base + tm, 2 * D_FF), F32)] + lead_scratch,
        compiler_params=pltpu.CompilerParams(dimension_semantics=("arbitrary", "arbitrary"),
                                             vmem_limit_bytes=VMEM_LIMIT),
        name="out_ffn",
    )(x, mix, *lead, wout, nffn, wup, cw, wdn, nfin)


def kernel(x_prompt, x_sample, state_ret, state_gdn, state_conv_qkv, state_ffn_conv, meta_tokens, norm_mix,
           w_in, conv_gdn, gdn_a_log, gdn_dt_bias, norm_ret, norm_gdn, w_out, norm_ffn, w_up, conv_ffn,
           w_down, norm_final):
    depth = w_in.shape[0]
    assert depth == 1
    nbp, seq, _ = x_prompt.shape
    nbs, dec_seq, _ = x_sample.shape
    assert dec_seq <= SAMPLE_PAD and nbs % SAMPLE_GROUP == 0
    assert seq % (2 * PROMPT_CHUNK * PROMPT_CHUNKS_PER_STEP) == 0
    layer = 0

    row = lambda v: v.reshape(1, -1).astype(F32)
    pad_ba = lambda v: jnp.pad(v.astype(F32), (N_HEADS, LANES - 2 * N_HEADS)).reshape(1, LANES)
    mixer_consts = (conv_gdn[layer], pad_ba(gdn_a_log[layer]), pad_ba(gdn_dt_bias[layer]),
                    norm_ret[layer].reshape(N_HEADS, D_HEAD), row(norm_gdn[layer]))
    nmix = row(norm_mix[layer])

    assert seq % N_META == 0 and (seq + N_META) % SAMPLE_PAD == 0
    rope_meta_row0, rope_sample_row0 = seq, seq + N_META
    rope = _rope_tables(seq + N_META + SAMPLE_PAD,
                        [(0, N_META), (rope_meta_row0, 0), (rope_sample_row0, PAST_LEN)])

    small_rows = jnp.concatenate([x_sample.reshape(nbs * dec_seq, D_MODEL), meta_tokens.astype(F32)], axis=0)
    proj_small, w_in_bf = _proj(small_rows, nmix, w_in[layer])
    meta_row0 = nbs * dec_seq

    zero_state = jnp.zeros((1, N_HEADS, D_HEAD, D_HEAD), F32)
    zero_cq = jnp.zeros((1, CONV_GDN - 1, GDN_QKV), F32)
    mix_m, sret_m, sgdn_m, cq_m = _mixer(proj_small, meta_row0, 1, N_META, zero_state, zero_state, zero_cq, rope,
                                         rope_meta_row0, mixer_consts, bb=1, c=N_META, nch=1, n_valid=N_META,
                                         shared_init=True)

    xp = x_prompt.reshape(nbp * seq, D_MODEL)
    mix_p, sret_p, sgdn_p, cq_p, w_out_bf, w_up_bf, w_down_bf = _proj_mixer(
        xp, nmix, w_in_bf, nbp, sret_m, sgdn_m, cq_m, rope, mixer_consts,
        (w_out[layer], w_up[layer], w_down[layer]), c=PROMPT_CHUNK, nch=PROMPT_CHUNKS_PER_STEP)
    ffn_weights = (w_out_bf, row(norm_ffn[layer]), w_up_bf, conv_ffn[layer], w_down_bf, row(norm_final))

    meta_lead = (meta_tokens.astype(F32), mix_m.reshape(N_META, D_MODEL))
    y_p, cf_p = _ffn(xp, mix_p, meta_lead, ffn_weights, nseq=nbp, tm=512, stride=1)
    y_prompt = y_p.reshape(nbp, seq, D_MODEL)

    mix_s, sret_s, sgdn_s, cq_s = _mixer(proj_small, 0, nbs, SAMPLE_PAD, state_ret[layer], state_gdn[layer],
                                         jnp.swapaxes(state_conv_qkv[layer], 0, 1), rope, rope_sample_row0,
                                         mixer_consts, bb=SAMPLE_SEQS_PER_STEP, c=SAMPLE_PAD, nch=1,
                                         n_valid=dec_seq, shared_init=False, stored=dec_seq, batch_minor=True)
    cq_s = jnp.swapaxes(cq_s, 0, 1)
    ng = nbs // SAMPLE_GROUP
    to_tmajor = lambda a: a.reshape(ng, SAMPLE_GROUP, a.shape[1], a.shape[2]).transpose(0, 2, 1, 3)
    xs_t = to_tmajor(x_sample).reshape(nbs * dec_seq, D_MODEL)
    mix_t = to_tmajor(mix_s[:, :dec_seq]).reshape(nbs * dec_seq, D_MODEL)
    cf0_t = to_tmajor(state_ffn_conv[layer]).reshape(ng, (CONV_FFN - 1) * SAMPLE_GROUP, 2 * D_FF)
    y_s_t, cf_s_t = _ffn(xs_t, mix_t, cf0_t, ffn_weights, nseq=ng, tm=dec_seq * SAMPLE_GROUP,
                         stride=SAMPLE_GROUP)
    y_sample = y_s_t.reshape(ng, dec_seq, SAMPLE_GROUP, D_MODEL).transpose(0, 2, 1, 3).reshape(nbs, dec_seq, D_MODEL)
    cf_s = cf_s_t.reshape(ng, CONV_FFN - 1, SAMPLE_GROUP, 2 * D_FF).transpose(0, 2, 1, 3).reshape(
        nbs, CONV_FFN - 1, 2 * D_FF)

    return (y_prompt, y_sample, sret_p[None], sgdn_p[None], cq_p[None], cf_p[None],
            sret_s[None], sgdn_s[None], cq_s[None], cf_s[None])
```

```python
import functools

import jax
import numpy as np
import jax.numpy as jnp
from jax import lax
from jax.experimental import pallas as pl
from jax.experimental.pallas import tpu as pltpu

F32 = jnp.float32
BF16 = jnp.bfloat16

D_MODEL = 1024
N_META = 16
PAST_LEN = 16384
N_HEADS = 4
D_HEAD = 128
D_GRP = N_HEADS * D_HEAD
GDN_QKV = 3 * D_GRP
CONV_GDN = 4
CONV_FFN = 3
D_FF = 2816
ROPE_THETA = 10000.0
EPS = 1e-6

OFF_RQ, OFF_RK, OFF_RV, OFF_RG = 0, D_GRP, 2 * D_GRP, 3 * D_GRP
OFF_QKV = 4 * D_GRP
OFF_GG = OFF_QKV + GDN_QKV
OFF_BA = OFF_GG + D_GRP
IN_WIDTH = OFF_BA + 2 * N_HEADS
LANES = 128
SUBLANES = 8
MXU_WIDTH = 256
V7X_VMEM_BYTES = 64 * 1024 * 1024
IN_PAD = OFF_BA + LANES

PROMPT_CHUNK = 128
PROMPT_CHUNKS_PER_STEP = 2
SAMPLE_PAD = 8
SAMPLE_SEQS_PER_STEP = 16
SAMPLE_GROUP = 64
VMEM_LIMIT = V7X_VMEM_BYTES * 7 // 8


def _round_up(n, m):
    return (n + m - 1) // m * m


def _mm(a, b):
    return jnp.dot(a.astype(BF16), b.astype(BF16), preferred_element_type=F32)


def _mm_nt(a, b):
    return lax.dot_general(a.astype(BF16), b.astype(BF16), (((1,), (1,)), ((), ())),
                           preferred_element_type=F32)


def _mm_tn(a, b):
    return lax.dot_general(a.astype(BF16), b.astype(BF16), (((0,), (0,)), ((), ())),
                           preferred_element_type=F32)


def _split3(x):
    p0 = x.astype(BF16)
    r = x - p0.astype(F32)
    p1 = r.astype(BF16)
    p2 = (r - p1.astype(F32)).astype(BF16)
    return p0, p1, p2


def _shift_rows(x, lead):
    k = lead.shape[0]
    row = lax.broadcasted_iota(jnp.int32, x.shape, 0)
    out = pltpu.roll(x, k, 0)
    for i in range(k):
        out = jnp.where(row == i, lead[i:i + 1, :], out)
    return out


def _silu(x):
    return x * jax.nn.sigmoid(x)


def _rms(x, w):
    return x * lax.rsqrt(jnp.mean(x * x, axis=-1, keepdims=True) + EPS) * w


def _rope_kernel(invf_ref, cos_ref, sin_ref, *, segments):
    shape = cos_ref.shape
    r = lax.broadcasted_iota(jnp.int32, shape, 0)
    pos = r + (segments[0][1] - segments[0][0])
    for row0, pos0 in segments[1:]:
        pos = jnp.where(r >= row0, r + (pos0 - row0), pos)
    ang = pos.astype(F32) * invf_ref[...]
    lane = lax.broadcasted_iota(jnp.int32, shape, 1)
    sin = jnp.sin(ang)
    cos_ref[...] = jnp.cos(ang)
    sin_ref[...] = jnp.where(lane < D_HEAD // 2, -sin, sin)


W_IN_COLS = 11 * LANES


def _proj_kernel(x_ref, nw_ref, wt_ref, invf_ref, o_ref, wbf_ref, cos_ref, sin_ref, h_ref, *, segments):
    first = pl.program_id(0) * W_IN_COLS

    @pl.when(first == 0)
    def _():
        h_ref[...] = _rms(x_ref[...], nw_ref[...]).astype(BF16)
        _rope_kernel(invf_ref, cos_ref, sin_ref, segments=segments)

    r = lax.broadcasted_iota(jnp.int32, wt_ref.shape, 0)
    w_blk = jnp.where(first + r < IN_WIDTH, wt_ref[...], 0.0).T.astype(BF16)
    wbf_ref[...] = w_blk
    o_ref[...] = jnp.dot(h_ref[...], w_blk, preferred_element_type=F32)


def _proj(x, norm_w, w, rope_rows, rope_segments):
    rows = x.shape[0]
    half = D_HEAD // 2
    inv_freq = ROPE_THETA ** (-jnp.arange(half, dtype=F32) / half)
    invf2 = jnp.concatenate([inv_freq, inv_freq]).reshape(1, LANES)
    table = jax.ShapeDtypeStruct((rope_rows, LANES), F32)
    table_spec = pl.BlockSpec((rope_rows, LANES), lambda j: (0, 0))
    return pl.pallas_call(
        functools.partial(_proj_kernel, segments=tuple(rope_segments)),
        out_shape=(jax.ShapeDtypeStruct((rows, IN_PAD), F32), jax.ShapeDtypeStruct((D_MODEL, IN_PAD), BF16),
                   table, table),
        grid=(IN_PAD // W_IN_COLS,),
        in_specs=[
            pl.BlockSpec((rows, D_MODEL), lambda j: (0, 0)),
            pl.BlockSpec((1, D_MODEL), lambda j: (0, 0)),
            pl.BlockSpec((W_IN_COLS, D_MODEL), lambda j: (j, 0)),
            pl.BlockSpec((1, LANES), lambda j: (0, 0)),
        ],
        out_specs=(pl.BlockSpec((rows, W_IN_COLS), lambda j: (0, j)),
                   pl.BlockSpec((D_MODEL, W_IN_COLS), lambda j: (0, j)), table_spec, table_spec),
        scratch_shapes=[pltpu.VMEM((rows, D_MODEL), BF16)],
        compiler_params=pltpu.CompilerParams(dimension_semantics=("arbitrary",),
                                             vmem_limit_bytes=VMEM_LIMIT),
        name="in_proj",
    )(x, norm_w, w.T, invf2)


def _unit_lower_inverses(mats, c, tick=lambda: None, nilpotent=SUBLANES):
    ri = lax.broadcasted_iota(jnp.int32, (c, c), 0)
    ci = lax.broadcasted_iota(jnp.int32, (c, c), 1)
    eye = (ri == ci).astype(F32)
    diag_blk = (ri // SUBLANES) == (ci // SUBLANES)
    ad = [jnp.where(diag_blk, a, 0.0) for a in mats]
    assert 1 <= nilpotent <= SUBLANES
    n_factors = max(0, (nilpotent - 1).bit_length() - 1)
    t = [eye - x for x in ad]
    power = ad
    for i in range(2):
        if i < n_factors:
            power = [_mm(x, x) for x in power]
            t = [x + _mm(x, s) for x, s in zip(t, power)]
        tick()
    tick()
    s = SUBLANES
    while s < c:
        level = ((ri // (2 * s)) == (ci // (2 * s))) & ((ri // s) != (ci // s))
        off = [jnp.where(level, a, 0.0) for a in mats]
        lt = [_mm(o, x) for o, x in zip(off, t)]
        tick()
        t = [x - _mm(x, y) for x, y in zip(t, lt)]
        tick()
        s *= 2
    return t


XS_TOP = SUBLANES


def _mixer_init(sret0_ref, sgdn0_ref, cq0_ref, sret_ref, sgdn_ref, xs_ref, batch_minor=False):
    tail = CONV_GDN - 1
    sret_ref[...] = sret0_ref[...]
    sgdn_ref[...] = sgdn0_ref[...]
    if batch_minor:
        for b in range(xs_ref.shape[0]):
            xs_ref[b, XS_TOP - tail:XS_TOP, :] = cq0_ref[:, b, :]
    else:
        xs_ref[:, XS_TOP - tail:XS_TOP, :] = cq0_ref[...]


def _mixer_block(getp, put_mix, cos2, sin2, const_refs, sret_ref, sgdn_ref, cq_ref, xs_ref, *,
                 bb, c, nch, n_valid, tick=lambda: None, batch_minor=False):
    (tril_ref, dint_ref, qdec_ref, kdec_ref, cdec_ref, cw_ref, alog_ref, dtb_ref, nret_ref, ngdn_ref) = const_refs
    rows = nch * c
    tail = CONV_GDN - 1
    top = XS_TOP
    ri = lax.broadcasted_iota(jnp.int32, (c, c), 0)
    cj = lax.broadcasted_iota(jnp.int32, (c, c), 1)
    tri = ri >= cj
    strict = ri > cj
    tril_bf = tril_ref[...]
    scale = D_HEAD ** -0.5
    heads = range(N_HEADS)
    seqs = range(bb)
    chunks = [slice(j * c, (j + 1) * c) for j in range(nch)]
    join = lambda parts: parts[0] if len(parts) == 1 else jnp.concatenate(parts, axis=0)

    ret = {}
    for b in seqs:
        for h in heads:
            lo = h * D_HEAD
            q = getp(b, OFF_RQ + lo, OFF_RQ + lo + D_HEAD)
            k = getp(b, OFF_RK + lo, OFF_RK + lo + D_HEAD)
            v = getp(b, OFF_RV + lo, OFF_RV + lo + D_HEAD)
            qr = q * cos2 + pltpu.roll(q, D_HEAD // 2, 1) * sin2
            kr = (k * cos2 + pltpu.roll(k, D_HEAD // 2, 1) * sin2) * scale
            ret[b, h] = (qr, kr, v)
    tick()
    rtasks = [(b, h, j) for j in range(nch) for b in seqs for h in heads]
    scores = {t: _mm_nt(ret[t[0], t[1]][0][chunks[t[2]]], ret[t[0], t[1]][1][chunks[t[2]]]) * dint_ref[t[1]]
              for t in rtasks}
    tick()
    intra = {t: _mm(scores[t], ret[t[0], t[1]][2][chunks[t[2]]]) for t in rtasks}
    tick()
    kv = {t: _mm_tn(ret[t[0], t[1]][1][chunks[t[2]]] * kdec_ref[t[1]], ret[t[0], t[1]][2][chunks[t[2]]])
          for t in rtasks}
    tick()
    pairs = [(b, h) for b in seqs for h in heads]
    s_ret = {bh: sret_ref[bh[0], bh[1]] for bh in pairs}
    o_ret = {}
    for j in range(nch):
        for b, h in pairs:
            o_ret[b, h, j] = intra[b, h, j] + _mm(ret[b, h][0][chunks[j]] * qdec_ref[h], s_ret[b, h])
        for b, h in pairs:
            s_ret[b, h] = cdec_ref[h] * s_ret[b, h] + kv[b, h, j]
        tick()
    o_ret = {bh: join([o_ret[bh[0], bh[1], j] for j in range(nch)]) for bh in pairs}
    mu = {bh: jnp.mean(o_ret[bh], axis=-1, keepdims=True) for bh in pairs}
    cen = {bh: o_ret[bh] - mu[bh] for bh in pairs}
    var = {bh: jnp.mean(cen[bh] * cen[bh], axis=-1, keepdims=True) for bh in pairs}
    for b, h in pairs:
        lo = h * D_HEAD
        gate = getp(b, OFF_RG + lo, OFF_RG + lo + D_HEAD)
        o = cen[b, h] * lax.rsqrt(var[b, h] + EPS) * nret_ref[h:h + 1, :]
        put_mix(b, lo, o * _silu(gate))
        sret_ref[b, h] = s_ret[b, h]
    tick()

    qkvs, beta_alls, cum_cols, cum_rows, cum_tots = {}, {}, {}, {}, {}
    for b in seqs:
        assert CONV_GDN == 4
        w0, w1, w2, w3 = (cw_ref[i:i + 1, :] for i in range(CONV_GDN))
        u0 = getp(b, OFF_QKV, OFF_QKV + GDN_QKV)
        xs_ref[b, top:top + rows, :] = u0
        u1 = _shift_rows(u0, xs_ref[b, top - 1:top, :])
        p = u0 * w1 + u1 * w0
        p_lead = xs_ref[b, top - 2:top, :] * w1 + xs_ref[b, top - 3:top - 1, :] * w0
        conv = u0 * w3 + u1 * w2 + _shift_rows(p, p_lead)
        last = n_valid if nch == 1 else rows
        new_tail = xs_ref[b, top + last - tail:top + last, :]
        xs_ref[b, top - tail:top, :] = new_tail
        if batch_minor:
            cq_ref[:, b, :] = new_tail
        else:
            cq_ref[b] = new_tail
        qkvs[b] = _silu(conv)
        tick()

        ba = getp(b, OFF_BA, OFF_BA + LANES)
        beta_all = jax.nn.sigmoid(ba)
        z = ba + dtb_ref[...]
        softplus = jnp.maximum(z, 0.0) + jnp.log1p(jnp.exp(-jnp.abs(z)))
        g_all = -jnp.exp(alog_ref[...]) * softplus
        if n_valid < c:
            row = lax.broadcasted_iota(jnp.int32, (rows, LANES), 0)
            rowmask = (row < n_valid).astype(F32)
            beta_all = beta_all * rowmask
            g_all = g_all * rowmask
        beta_alls[b] = beta_all
        for j, rs in enumerate(chunks):
            parts = _split3(g_all[rs])
            cum_cols[b, j] = sum(jnp.dot(tril_bf, g, preferred_element_type=F32) for g in parts)
            cum_tots[b, j] = jnp.broadcast_to(cum_cols[b, j][c - 1:c, :], (D_HEAD, LANES))
            cum_rows[b, j] = cum_cols[b, j].T
    tick()

    tasks = [(j, b, h) for j in range(nch) for b in seqs for h in heads]
    qs, ks, vs, betas, ecums, kdecs, cdecs, amats, qkms = [], [], [], [], [], [], [], [], []
    raw = [(qkvs[b][chunks[j], h * D_HEAD:(h + 1) * D_HEAD],
            qkvs[b][chunks[j], D_GRP + h * D_HEAD:D_GRP + (h + 1) * D_HEAD]) for j, b, h in tasks]
    sumsq = [(jnp.sum(q * q, axis=-1, keepdims=True), jnp.sum(k * k, axis=-1, keepdims=True)) for q, k in raw]
    for i, (j, b, h) in enumerate(tasks):
        rs = chunks[j]
        lo = h * D_HEAD
        v = qkvs[b][rs, 2 * D_GRP + lo:2 * D_GRP + lo + D_HEAD]
        q = raw[i][0] * lax.rsqrt(sumsq[i][0] + EPS) * scale
        k = raw[i][1] * lax.rsqrt(sumsq[i][1] + EPS)
        beta = jnp.broadcast_to(beta_alls[b][rs, h:h + 1], (c, LANES))
        cum = jnp.broadcast_to(cum_cols[b, j][:, N_HEADS + h:N_HEADS + h + 1], (c, LANES))
        cum_row = cum_rows[b, j][N_HEADS + h:N_HEADS + h + 1, :]
        dmask = jnp.exp(jnp.where(tri, cum[:, :c] - cum_row, -jnp.inf))
        cum_last = jnp.broadcast_to(cum_tots[b, j][:, N_HEADS + h:N_HEADS + h + 1], (D_HEAD, LANES))
        kq = _mm_nt(jnp.concatenate([k, q], axis=0), k)
        amats.append(jnp.where(strict, beta[:, :c] * kq[:c] * dmask, 0.0))
        qkms.append(jnp.where(tri, kq[c:] * dmask, 0.0))
        qs.append(q)
        ks.append(k)
        vs.append(v)
        betas.append(beta)
        ecums.append(jnp.exp(cum))
        kdecs.append(jnp.exp(cum_last[:c] - cum))
        cdecs.append(jnp.exp(cum_last))
        if i % N_HEADS == N_HEADS - 1:
            tick()

    tinv = _unit_lower_inverses(amats, c, tick, nilpotent=min(SUBLANES, n_valid))
    eye = (ri == cj).astype(F32)
    sols = []
    for i in range(len(tasks)):
        rhs = jnp.concatenate([vs[i] * betas[i], ks[i] * (betas[i] * ecums[i])], axis=1)
        sols.append(rhs + _mm(tinv[i] - eye, rhs))
    tick()

    s_gdn = {bh: sgdn_ref[bh[0], bh[1]] for bh in pairs}
    o_gdn = {}
    for j in range(nch):
        idx = {tasks[i][1:]: i for i in range(len(tasks)) if tasks[i][0] == j}
        lhs = {bh: jnp.concatenate([sols[i][:, D_HEAD:], qs[i] * ecums[i]], axis=0) for bh, i in idx.items()}
        both = {bh: _mm(lhs[bh], s_gdn[bh]) for bh in idx}
        tick()
        w = {bh: sols[i][:, :D_HEAD] - both[bh][:c] for bh, i in idx.items()}
        for bh, i in idx.items():
            o_gdn[bh + (j,)] = both[bh][c:] + _mm(qkms[i], w[bh])
        tick()
        upd = {bh: _mm_tn(ks[i] * kdecs[i], w[bh]) for bh, i in idx.items()}
        for bh, i in idx.items():
            s_gdn[bh] = cdecs[i] * s_gdn[bh] + upd[bh]
        tick()
    o_gdn = {bh: join([o_gdn[bh + (j,)] for j in range(nch)]) for bh in pairs}
    msq = {bh: jnp.mean(o_gdn[bh] * o_gdn[bh], axis=-1, keepdims=True) for bh in pairs}
    for b, h in pairs:
        lo = h * D_HEAD
        gate = getp(b, OFF_GG + lo, OFF_GG + lo + D_HEAD)
        o = o_gdn[b, h] * lax.rsqrt(msq[b, h] + EPS) * ngdn_ref[...]
        put_mix(b, D_GRP + lo, o * _silu(gate))
        sgdn_ref[b, h] = s_gdn[b, h]


N_MIXER_CONSTS = 10


def _mixer_kernel(p_ref, sret0_ref, sgdn0_ref, cq0_ref, cos_ref, sin_ref, *rest, bb, c, nch, n_valid, stored,
                  batch_minor):
    const_refs, (mix_ref, sret_ref, sgdn_ref, cq_ref, xs_ref) = rest[:N_MIXER_CONSTS], rest[N_MIXER_CONSTS:]
    rows = nch * c

    @pl.when(pl.program_id(1) == 0)
    def _():
        _mixer_init(sret0_ref, sgdn0_ref, cq0_ref, sret_ref, sgdn_ref, xs_ref, batch_minor)

    def put_mix(b, lo, value):
        mix_ref[b, :, lo:lo + D_HEAD] = value.astype(mix_ref.dtype)

    def getp(b, lo, hi):
        blk = p_ref[b * stored:(b + 1) * stored, lo:hi]
        if stored < rows:
            blk = jnp.concatenate([blk, jnp.zeros((rows - stored, hi - lo), F32)], axis=0)
        return blk

    _mixer_block(getp, put_mix, cos_ref[...], sin_ref[...], const_refs, sret_ref, sgdn_ref, cq_ref, xs_ref,
                 bb=bb, c=c, nch=nch, n_valid=n_valid, batch_minor=batch_minor)


PROJ_PANEL = MXU_WIDTH


def _mixer_block_ticks(bb, c, nch):
    levels = (c // SUBLANES).bit_length() - 1
    retention = 4 + nch + 1
    gdn_prep = bb + 1 + bb * nch
    inverse = 3 + 2 * levels
    return retention + gdn_prep + inverse + 1 + 3 * nch


def _proj_mixer_kernel(x0_ref, xa_ref, xb_ref, nw_ref, w_ref, sret0_ref, sgdn0_ref, cq0_ref, cos_ref, sin_ref,
                       *rest, c, nch, steps_per_seq, n_cast):
    const_refs = rest[:N_MIXER_CONSTS]
    cast_in = rest[N_MIXER_CONSTS:N_MIXER_CONSTS + n_cast]
    mix_ref, sret_ref, sgdn_ref, cq_ref = rest[N_MIXER_CONSTS + n_cast:N_MIXER_CONSTS + n_cast + 4]
    cast_out = rest[N_MIXER_CONSTS + n_cast + 4:N_MIXER_CONSTS + 2 * n_cast + 4]
    pja_ref, pjb_ref, xs_ref = rest[N_MIXER_CONSTS + 2 * n_cast + 4:]
    step = pl.program_id(0)
    rows = nch * c

    for src_ref, dst_ref in zip(cast_in, cast_out):
        dst_ref[...] = src_ref[...].astype(dst_ref.dtype)

    def project(x_ref, dst_ref, n_ticks):
        h = _rms(x_ref[...], nw_ref[...]).astype(BF16)
        panels = [(lo, min(lo + PROJ_PANEL, IN_PAD)) for lo in range(0, IN_PAD, PROJ_PANEL)]
        n_panels = len(panels)
        calls = [0]

        def emit():
            lo, hi = panels.pop(0)
            dst_ref[:, lo:hi] = jnp.dot(h, w_ref[:, lo:hi], preferred_element_type=F32)

        def tick():
            calls[0] += 1
            due = min(n_panels, -(-calls[0] * n_panels // n_ticks))
            while n_panels - len(panels) < due:
                emit()

        def flush():
            assert n_ticks == 1 or calls[0] == n_ticks, (calls[0], n_ticks)
            while panels:
                emit()

        return tick, flush

    @pl.when(step == 0)
    def _():
        project(x0_ref, pja_ref, 1)[1]()

    @pl.when(lax.rem(step, steps_per_seq) == 0)
    def _():
        _mixer_init(sret0_ref, sgdn0_ref, cq0_ref, sret_ref, sgdn_ref, xs_ref)

    for half, (cur_ref, x_next_ref, nxt_ref) in enumerate(((pja_ref, xa_ref, pjb_ref), (pjb_ref, xb_ref, pja_ref))):
        r0 = half * rows
        tick, flush = project(x_next_ref, nxt_ref, _mixer_block_ticks(1, c, nch))

        def put_mix(b, lo, value, r0=r0):
            mix_ref[r0:r0 + rows, lo:lo + D_HEAD] = value.astype(mix_ref.dtype)

        _mixer_block(lambda b, lo, hi, cur_ref=cur_ref: cur_ref[:, lo:hi], put_mix,
                     cos_ref[r0:r0 + rows, :], sin_ref[r0:r0 + rows, :], const_refs,
                     sret_ref, sgdn_ref, cq_ref, xs_ref, bb=1, c=c, nch=nch, n_valid=c, tick=tick)
        flush()


def _retention_decay_tables(c, n_valid):
    f32 = np.float32
    lg = np.log1p(-np.power(f32(2.0), f32(-5.0) - np.arange(N_HEADS, dtype=f32)))[:, None].astype(f32)
    idx = np.arange(c, dtype=f32)
    diff = idx[:, None] - idx[None, :]
    dint = np.where(diff[None] >= 0, np.exp(lg[:, :, None] * np.maximum(diff[None], 0)), f32(0.0)).astype(f32)
    qdec = np.exp(lg * (idx + f32(1.0))).astype(f32)
    kdec = np.where(idx[None, :] < n_valid, np.exp(lg * np.minimum(f32(n_valid) - f32(1.0) - idx, c)), f32(0.0))
    cdec = np.exp(lg * f32(n_valid)).astype(f32)
    bc = lambda t: np.broadcast_to(t.astype(f32)[:, :, None], t.shape + (LANES,))
    return dint, bc(qdec), bc(kdec), np.broadcast_to(cdec[:, :, None], (N_HEADS, 1, LANES))


def _mixer_const_operands(c, n_valid, consts):
    cw, alog, dtb, nret, ngdn = consts
    dint, qdec, kdec, cdec = _retention_decay_tables(c, n_valid)
    idx = np.arange(c)
    tril = (idx[:, None] >= idx[None, :]).astype(np.float32)
    arrays = (jnp.asarray(tril, BF16), jnp.asarray(dint), jnp.asarray(qdec), jnp.asarray(kdec), jnp.asarray(cdec),
              cw, alog, dtb, nret, ngdn)
    assert len(arrays) == N_MIXER_CONSTS
    return arrays, [a.shape for a in arrays]


def _mixer(proj, row0, nb, length, sret0, sgdn0, cq0, rope, rope_row0, consts, *, bb, c, nch, n_valid,
           shared_init, stored=None, batch_minor=False):
    rows = nch * c
    assert nb % bb == 0 and length % rows == 0 and rope_row0 % rows == 0
    assert not shared_init or bb == 1
    assert n_valid == c or nch == 1
    assert bb == 1 or length == rows
    const_arrays, const_shapes = _mixer_const_operands(c, n_valid, consts)
    nsteps = length // rows
    stored = rows if stored is None else stored
    assert stored == rows or (nsteps == 1 and stored >= n_valid and (bb * stored) % SUBLANES == 0)
    assert row0 % (bb * stored) == 0
    blk0 = row0 // (bb * stored)
    init_idx = (lambda b, i: (0, 0, 0, 0)) if shared_init else (lambda b, i: (b, 0, 0, 0))
    init_idx3 = (lambda b, i: (0, 0, 0)) if shared_init else (lambda b, i: (b, 0, 0))
    whole = lambda shape: pl.BlockSpec(shape, lambda b, i: (0,) * len(shape))
    rope_spec = pl.BlockSpec((rows, LANES), lambda b, i: (rope_row0 // rows + i, 0))
    state_shape = (bb, N_HEADS, D_HEAD, D_HEAD)
    tail = CONV_GDN - 1
    assert not (batch_minor and shared_init)
    cq_shape, cq_block = (nb, tail, GDN_QKV), (bb, tail, GDN_QKV)
    cq_idx = lambda b, i: (b, 0, 0)
    if batch_minor:
        cq_shape, cq_block = (tail, nb, GDN_QKV), (tail, bb, GDN_QKV)
        cq_idx = init_idx3 = lambda b, i: (0, b, 0)
    kern = functools.partial(_mixer_kernel, bb=bb, c=c, nch=nch, n_valid=n_valid, stored=stored,
                             batch_minor=batch_minor)
    return pl.pallas_call(
        kern,
        out_shape=(
            jax.ShapeDtypeStruct((nb, length, D_MODEL), BF16),
            jax.ShapeDtypeStruct((nb, N_HEADS, D_HEAD, D_HEAD), F32),
            jax.ShapeDtypeStruct((nb, N_HEADS, D_HEAD, D_HEAD), F32),
            jax.ShapeDtypeStruct(cq_shape, F32),
        ),
        grid=(nb // bb, nsteps),
        in_specs=[
            pl.BlockSpec((bb * stored, IN_PAD), lambda b, i: (blk0 + b * nsteps + i, 0)),
            pl.BlockSpec(state_shape, init_idx),
            pl.BlockSpec(state_shape, init_idx),
            pl.BlockSpec(cq_block, init_idx3),
            rope_spec,
            rope_spec,
        ] + [whole(shape) for shape in const_shapes],
        out_specs=(
            pl.BlockSpec((bb, rows, D_MODEL), lambda b, i: (b, i, 0)),
            pl.BlockSpec(state_shape, lambda b, i: (b, 0, 0, 0)),
            pl.BlockSpec(state_shape, lambda b, i: (b, 0, 0, 0)),
            pl.BlockSpec(cq_block, cq_idx),
        ),
        scratch_shapes=[pltpu.VMEM((bb, XS_TOP + rows, GDN_QKV), F32)],
        compiler_params=pltpu.CompilerParams(dimension_semantics=("arbitrary", "arbitrary"),
                                             vmem_limit_bytes=VMEM_LIMIT),
        name="mixer",
    )(proj, sret0, sgdn0, cq0, rope[0], rope[1], *const_arrays)


def _row_slab(nrows, nsteps):
    for hold in (1, 2, 4, 8):
        slabs = nsteps // hold
        if nsteps % hold == 0 and nrows % slabs == 0 and (nrows // slabs) % (2 * SUBLANES) == 0:
            return nrows // slabs, hold
    raise ValueError((nrows, nsteps))


def _proj_mixer(x, norm_w, w_bf, nseq, sret0, sgdn0, cq0, rope, consts, to_bf16, *, c, nch):
    rows = nch * c
    total = x.shape[0]
    length = total // nseq
    assert total % nseq == 0 and length % (2 * rows) == 0
    nblk = total // rows
    steps_per_seq = length // (2 * rows)
    const_arrays, const_shapes = _mixer_const_operands(c, c, consts)
    whole = lambda shape, **kw: pl.BlockSpec(shape, lambda s: (0,) * len(shape), **kw)
    rope_spec = pl.BlockSpec((2 * rows, LANES), lambda s: (lax.rem(s, steps_per_seq), 0))
    state_shape = (1, N_HEADS, D_HEAD, D_HEAD)
    tail = CONV_GDN - 1
    nsteps = nblk // 2
    slabs = [_row_slab(w.shape[0], nsteps) for w in to_bf16]
    cast_specs = [pl.BlockSpec((r, w.shape[1]), lambda s, hold=hold: (s // hold, 0))
                  for w, (r, hold) in zip(to_bf16, slabs)]
    kern = functools.partial(_proj_mixer_kernel, c=c, nch=nch, steps_per_seq=steps_per_seq, n_cast=len(to_bf16))
    return pl.pallas_call(
        kern,
        out_shape=(
            jax.ShapeDtypeStruct((total, D_MODEL), BF16),
            jax.ShapeDtypeStruct((nseq, N_HEADS, D_HEAD, D_HEAD), F32),
            jax.ShapeDtypeStruct((nseq, N_HEADS, D_HEAD, D_HEAD), F32),
            jax.ShapeDtypeStruct((nseq, tail, GDN_QKV), F32),
        ) + tuple(jax.ShapeDtypeStruct(w.shape, BF16) for w in to_bf16),
        grid=(nsteps,),
        in_specs=[
            pl.BlockSpec((rows, D_MODEL), lambda s: (0, 0), pipeline_mode=pl.Buffered(1)),
            pl.BlockSpec((rows, D_MODEL), lambda s: (2 * s + 1, 0)),
            pl.BlockSpec((rows, D_MODEL), lambda s: (jnp.minimum(2 * s + 2, nblk - 1), 0)),
            whole((1, D_MODEL)),
            whole((D_MODEL, IN_PAD), pipeline_mode=pl.Buffered(1)),
            whole(state_shape),
            whole(state_shape),
            whole((1, tail, GDN_QKV)),
            rope_spec,
            rope_spec,
        ] + [whole(shape) for shape in const_shapes] + cast_specs,
        out_specs=(
            pl.BlockSpec((2 * rows, D_MODEL), lambda s: (s, 0)),
            pl.BlockSpec(state_shape, lambda s: (s // steps_per_seq, 0, 0, 0)),
            pl.BlockSpec(state_shape, lambda s: (s // steps_per_seq, 0, 0, 0)),
            pl.BlockSpec((1, tail, GDN_QKV), lambda s: (s // steps_per_seq, 0, 0)),
        ) + tuple(cast_specs),
        scratch_shapes=[pltpu.VMEM((rows, IN_PAD), F32), pltpu.VMEM((rows, IN_PAD), F32),
                        pltpu.VMEM((1, XS_TOP + rows, GDN_QKV), F32)],
        compiler_params=pltpu.CompilerParams(dimension_semantics=("arbitrary",),
                                             vmem_limit_bytes=VMEM_LIMIT),
        name="proj_mixer",
    )(x, x, x, norm_w, w_bf, sret0, sgdn0, cq0, rope[0], rope[1], *const_arrays, *to_bf16)


FFN_COL_CHUNK = D_FF // 11


def _ffn_kernel(x_ref, mix_ref, *rest, tm, stride, prefix):
    if prefix:
        (xm_ref, mixm_ref, wout_ref, nffn_ref, wup_ref, cw_ref, wdn_ref, nfin_ref,
         y_ref, tail_ref, full_ref, lead_ref) = rest
    else:
        tail0_ref, wout_ref, nffn_ref, wup_ref, cw_ref, wdn_ref, nfin_ref, y_ref, tail_ref, full_ref = rest
    t = pl.program_id(1)
    carry = (CONV_FFN - 1) * stride
    base = _round_up(carry, SUBLANES)

    def up_project(x, mix):
        x1 = x + jnp.dot(mix, wout_ref[...], preferred_element_type=F32)
        h = _rms(x1, nffn_ref[...]).astype(BF16)
        return x1, jnp.dot(h, wup_ref[...], preferred_element_type=F32)

    if prefix:
        @pl.when((pl.program_id(0) == 0) & (t == 0))
        def _():
            um = up_project(xm_ref[...], mixm_ref[...])[1]
            lead_ref[...] = um[um.shape[0] - carry:, :]

    @pl.when(t == 0)
    def _():
        full_ref[base - carry:base, :] = lead_ref[...] if prefix else tail0_ref[0]

    x1, up = up_project(x_ref[...], mix_ref[...])
    full_ref[base:base + tm, :] = up

    def conv_cols(lo):
        acc = full_ref[base - carry:base - carry + tm, lo:lo + FFN_COL_CHUNK] * cw_ref[0:1, lo:lo + FFN_COL_CHUNK]
        for i in range(1, CONV_FFN):
            r0 = base - carry + i * stride
            acc = acc + full_ref[r0:r0 + tm, lo:lo + FFN_COL_CHUNK] * cw_ref[i:i + 1, lo:lo + FFN_COL_CHUNK]
        return acc

    x2 = x1
    for j in range(D_FF // FFN_COL_CHUNK):
        lo = j * FFN_COL_CHUNK
        act = (_silu(conv_cols(lo)) * conv_cols(D_FF + lo)).astype(BF16)
        x2 = x2 + jnp.dot(act, wdn_ref[lo:lo + FFN_COL_CHUNK, :], preferred_element_type=F32)
    y_ref[...] = _rms(x2, nfin_ref[...])

    new_tail = full_ref[base + tm - carry:base + tm, :]
    full_ref[base - carry:base, :] = new_tail
    tail_ref[0] = new_tail


def _ffn(x, mix, lead, weights, *, nseq, tm, stride):
    wout, nffn, wup, cw, wdn, nfin = weights
    rows = x.shape[0]
    assert rows % (nseq * tm) == 0
    nt = rows // (nseq * tm)
    carry = (CONV_FFN - 1) * stride
    base = _round_up(carry, SUBLANES)
    assert tm >= carry
    resident = lambda shape: pl.BlockSpec(shape, lambda b, t: (0, 0), pipeline_mode=pl.Buffered(1))
    small = lambda shape: pl.BlockSpec(shape, lambda b, t: (0, 0))
    prefix = isinstance(lead, tuple)
    if prefix:
        assert stride == 1 and all(a.shape[0] >= carry for a in lead)
        lead_specs = [small(a.shape) for a in lead]
        lead_scratch = [pltpu.VMEM((carry, 2 * D_FF), F32)]
    else:
        lead = (lead,)
        lead_specs = [pl.BlockSpec((1, carry, 2 * D_FF), lambda b, t: (b, 0, 0))]
        lead_scratch = []
    kern = functools.partial(_ffn_kernel, tm=tm, stride=stride, prefix=prefix)
    return pl.pallas_call(
        kern,
        out_shape=(
            jax.ShapeDtypeStruct((rows, D_MODEL), F32),
            jax.ShapeDtypeStruct((nseq, carry, 2 * D_FF), F32),
        ),
        grid=(nseq, nt),
        in_specs=[
            pl.BlockSpec((tm, D_MODEL), lambda b, t: (b * nt + t, 0)),
            pl.BlockSpec((tm, D_MODEL), lambda b, t: (b * nt + t, 0)),
        ] + lead_specs + [
            resident((D_MODEL, D_MODEL)),
            small((1, D_MODEL)),
            resident((D_MODEL, 2 * D_FF)),
            small((CONV_FFN, 2 * D_FF)),
            resident((D_FF, D_MODEL)),
            small((1, D_MODEL)),
        ],
        out_specs=(
            pl.BlockSpec((tm, D_MODEL), lambda b, t: (b * nt + t, 0)),
            pl.BlockSpec((1, carry, 2 * D_FF), lambda b, t: (b, 0, 0)),
        ),
        scratch_shapes=[pltpu.VMEM((base + tm, 2 * D_FF), F32)] + lead_scratch,
        compiler_params=pltpu.CompilerParams(dimension_semantics=("arbitrary", "arbitrary"),
                                             vmem_limit_bytes=VMEM_LIMIT),
        name="out_ffn",
    )(x, mix, *lead, wout, nffn, wup, cw, wdn, nfin)


def kernel(x_prompt, x_sample, state_ret, state_gdn, state_conv_qkv, state_ffn_conv, meta_tokens, norm_mix,
           w_in, conv_gdn, gdn_a_log, gdn_dt_bias, norm_ret, norm_gdn, w_out, norm_ffn, w_up, conv_ffn,
           w_down, norm_final):
    depth = w_in.shape[0]
    assert depth == 1
    nbp, seq, _ = x_prompt.shape
    nbs, dec_seq, _ = x_sample.shape
    assert dec_seq <= SAMPLE_PAD and nbs % SAMPLE_GROUP == 0
    assert seq % (2 * PROMPT_CHUNK * PROMPT_CHUNKS_PER_STEP) == 0
    layer = 0

    row = lambda v: v.reshape(1, -1).astype(F32)
    pad_ba = lambda v: jnp.pad(v.astype(F32), (N_HEADS, LANES - 2 * N_HEADS)).reshape(1, LANES)
    mixer_consts = (conv_gdn[layer], pad_ba(gdn_a_log[layer]), pad_ba(gdn_dt_bias[layer]),
                    norm_ret[layer].reshape(N_HEADS, D_HEAD), row(norm_gdn[layer]))
    nmix = row(norm_mix[layer])

    assert seq % N_META == 0 and (seq + N_META) % SAMPLE_PAD == 0
    rope_meta_row0, rope_sample_row0 = seq, seq + N_META
    rope_segments = [(0, N_META), (rope_meta_row0, 0), (rope_sample_row0, PAST_LEN)]

    small_rows = jnp.concatenate([x_sample.reshape(nbs * dec_seq, D_MODEL), meta_tokens.astype(F32)], axis=0)
    proj_small, w_in_bf, *rope = _proj(small_rows, nmix, w_in[layer], seq + N_META + SAMPLE_PAD, rope_segments)
    meta_row0 = nbs * dec_seq

    zero_state = jnp.zeros((1, N_HEADS, D_HEAD, D_HEAD), F32)
    zero_cq = jnp.zeros((1, CONV_GDN - 1, GDN_QKV), F32)
    mix_m, sret_m, sgdn_m, cq_m = _mixer(proj_small, meta_row0, 1, N_META, zero_state, zero_state, zero_cq, rope,
                                         rope_meta_row0, mixer_consts, bb=1, c=N_META, nch=1, n_valid=N_META,
                                         shared_init=True)

    xp = x_prompt.reshape(nbp * seq, D_MODEL)
    mix_p, sret_p, sgdn_p, cq_p, w_out_bf, w_up_bf, w_down_bf = _proj_mixer(
        xp, nmix, w_in_bf, nbp, sret_m, sgdn_m, cq_m, rope, mixer_consts,
        (w_out[layer], w_up[layer], w_down[layer]), c=PROMPT_CHUNK, nch=PROMPT_CHUNKS_PER_STEP)
    ffn_weights = (w_out_bf, row(norm_ffn[layer]), w_up_bf, conv_ffn[layer], w_down_bf, row(norm_final))

    meta_lead = (meta_tokens.astype(F32), mix_m.reshape(N_META, D_MODEL))
    y_p, cf_p = _ffn(xp, mix_p, meta_lead, ffn_weights, nseq=nbp, tm=512, stride=1)
    y_prompt = y_p.reshape(nbp, seq, D_MODEL)

    mix_s, sret_s, sgdn_s, cq_s = _mixer(proj_small, 0, nbs, SAMPLE_PAD, state_ret[layer], state_gdn[layer],
                                         jnp.swapaxes(state_conv_qkv[layer], 0, 1), rope, rope_sample_row0,
                                         mixer_consts, bb=SAMPLE_SEQS_PER_STEP, c=SAMPLE_PAD, nch=1,
                                         n_valid=dec_seq, shared_init=False, stored=dec_seq, batch_minor=True)
    cq_s = jnp.swapaxes(cq_s, 0, 1)
    ng = nbs // SAMPLE_GROUP
    to_tmajor = lambda a: a.reshape(ng, SAMPLE_GROUP, a.shape[1], a.shape[2]).transpose(0, 2, 1, 3)
    xs_t = to_tmajor(x_sample).reshape(nbs * dec_seq, D_MODEL)
    mix_t = to_tmajor(mix_s[:, :dec_seq]).reshape(nbs * dec_seq, D_MODEL)
    cf0_t = to_tmajor(state_ffn_conv[layer]).reshape(ng, (CONV_FFN - 1) * SAMPLE_GROUP, 2 * D_FF)
    y_s_t, cf_s_t = _ffn(xs_t, mix_t, cf0_t, ffn_weights, nseq=ng, tm=dec_seq * SAMPLE_GROUP,
                         stride=SAMPLE_GROUP)
    y_sample = y_s_t.reshape(ng, dec_seq, SAMPLE_GROUP, D_MODEL).transpose(0, 2, 1, 3).reshape(nbs, dec_seq, D_MODEL)
    cf_s = cf_s_t.reshape(ng, CONV_FFN - 1, SAMPLE_GROUP, 2 * D_FF).transpose(0, 2, 1, 3).reshape(
        nbs, CONV_FFN - 1, 2 * D_FF)

    return (y_prompt, y_sample, sret_p[None], sgdn_p[None], cq_p[None], cf_p[None],
            sret_s[None], sgdn_s[None], cq_s[None], cf_s[None])
```

```python
import functools

import jax
import numpy as np
import jax.numpy as jnp
from jax import lax
from jax.experimental import pallas as pl
from jax.experimental.pallas import tpu as pltpu

F32 = jnp.float32
BF16 = jnp.bfloat16

D_MODEL = 1024
N_META = 16
PAST_LEN = 16384
N_HEADS = 4
D_HEAD = 128
D_GRP = N_HEADS * D_HEAD
GDN_QKV = 3 * D_GRP
CONV_GDN = 4
CONV_FFN = 3
D_FF = 2816
ROPE_THETA = 10000.0
EPS = 1e-6

OFF_RQ, OFF_RK, OFF_RV, OFF_RG = 0, D_GRP, 2 * D_GRP, 3 * D_GRP
OFF_QKV = 4 * D_GRP
OFF_GG = OFF_QKV + GDN_QKV
OFF_BA = OFF_GG + D_GRP
IN_WIDTH = OFF_BA + 2 * N_HEADS
LANES = 128
SUBLANES = 8
MXU_WIDTH = 256
V7X_VMEM_BYTES = 64 * 1024 * 1024
IN_PAD = OFF_BA + LANES

PROMPT_CHUNK = 128
PROMPT_CHUNKS_PER_STEP = 2
SAMPLE_PAD = 8
SAMPLE_SEQS_PER_STEP = 16
SAMPLE_STATE_SLOTS = 3
SAMPLE_GROUP = 64
VMEM_LIMIT = V7X_VMEM_BYTES * 7 // 8


def _round_up(n, m):
    return (n + m - 1) // m * m


def _mm(a, b):
    return jnp.dot(a.astype(BF16), b.astype(BF16), preferred_element_type=F32)


def _mm_nt(a, b):
    return lax.dot_general(a.astype(BF16), b.astype(BF16), (((1,), (1,)), ((), ())),
                           preferred_element_type=F32)


def _mm_tn(a, b):
    return lax.dot_general(a.astype(BF16), b.astype(BF16), (((0,), (0,)), ((), ())),
                           preferred_element_type=F32)


def _split3(x):
    p0 = x.astype(BF16)
    r = x - p0.astype(F32)
    p1 = r.astype(BF16)
    p2 = (r - p1.astype(F32)).astype(BF16)
    return p0, p1, p2


def _shift_rows(x, lead):
    k = lead.shape[0]
    row = lax.broadcasted_iota(jnp.int32, x.shape, 0)
    out = pltpu.roll(x, k, 0)
    for i in range(k):
        out = jnp.where(row == i, lead[i:i + 1, :], out)
    return out


def _silu(x):
    return x * jax.nn.sigmoid(x)


def _rms(x, w):
    return x * lax.rsqrt(jnp.mean(x * x, axis=-1, keepdims=True) + EPS) * w


def _rope_kernel(invf_ref, cos_ref, sin_ref, *, segments):
    shape = cos_ref.shape
    r = lax.broadcasted_iota(jnp.int32, shape, 0)
    pos = r + (segments[0][1] - segments[0][0])
    for row0, pos0 in segments[1:]:
        pos = jnp.where(r >= row0, r + (pos0 - row0), pos)
    ang = pos.astype(F32) * invf_ref[...]
    lane = lax.broadcasted_iota(jnp.int32, shape, 1)
    sin = jnp.sin(ang)
    cos_ref[...] = jnp.cos(ang)
    sin_ref[...] = jnp.where(lane < D_HEAD // 2, -sin, sin)


W_IN_COLS = 11 * LANES


def _proj_kernel(x_ref, nw_ref, wt_ref, invf_ref, o_ref, wbf_ref, cos_ref, sin_ref, h_ref, *, segments):
    first = pl.program_id(0) * W_IN_COLS

    @pl.when(first == 0)
    def _():
        h_ref[...] = _rms(x_ref[...], nw_ref[...]).astype(BF16)
        _rope_kernel(invf_ref, cos_ref, sin_ref, segments=segments)

    r = lax.broadcasted_iota(jnp.int32, wt_ref.shape, 0)
    w_blk = jnp.where(first + r < IN_WIDTH, wt_ref[...], 0.0).T.astype(BF16)
    wbf_ref[...] = w_blk
    o_ref[...] = jnp.dot(h_ref[...], w_blk, preferred_element_type=F32)


def _proj(x, norm_w, w, rope_rows, rope_segments):
    rows = x.shape[0]
    half = D_HEAD // 2
    inv_freq = ROPE_THETA ** (-jnp.arange(half, dtype=F32) / half)
    invf2 = jnp.concatenate([inv_freq, inv_freq]).reshape(1, LANES)
    table = jax.ShapeDtypeStruct((rope_rows, LANES), F32)
    table_spec = pl.BlockSpec((rope_rows, LANES), lambda j: (0, 0))
    return pl.pallas_call(
        functools.partial(_proj_kernel, segments=tuple(rope_segments)),
        out_shape=(jax.ShapeDtypeStruct((rows, IN_PAD), F32), jax.ShapeDtypeStruct((D_MODEL, IN_PAD), BF16),
                   table, table),
        grid=(IN_PAD // W_IN_COLS,),
        in_specs=[
            pl.BlockSpec((rows, D_MODEL), lambda j: (0, 0)),
            pl.BlockSpec((1, D_MODEL), lambda j: (0, 0)),
            pl.BlockSpec((W_IN_COLS, D_MODEL), lambda j: (j, 0)),
            pl.BlockSpec((1, LANES), lambda j: (0, 0)),
        ],
        out_specs=(pl.BlockSpec((rows, W_IN_COLS), lambda j: (0, j)),
                   pl.BlockSpec((D_MODEL, W_IN_COLS), lambda j: (0, j)), table_spec, table_spec),
        scratch_shapes=[pltpu.VMEM((rows, D_MODEL), BF16)],
        compiler_params=pltpu.CompilerParams(dimension_semantics=("arbitrary",),
                                             vmem_limit_bytes=VMEM_LIMIT),
        name="in_proj",
    )(x, norm_w, w.T, invf2)


def _unit_lower_inverses(mats, c, tick=lambda: None, nilpotent=SUBLANES):
    ri = lax.broadcasted_iota(jnp.int32, (c, c), 0)
    ci = lax.broadcasted_iota(jnp.int32, (c, c), 1)
    eye = (ri == ci).astype(F32)
    diag_blk = (ri // SUBLANES) == (ci // SUBLANES)
    ad = [jnp.where(diag_blk, a, 0.0) for a in mats]
    assert 1 <= nilpotent <= SUBLANES
    n_factors = max(0, (nilpotent - 1).bit_length() - 1)
    t = [eye - x for x in ad]
    power = ad
    for i in range(2):
        if i < n_factors:
            power = [_mm(x, x) for x in power]
            t = [x + _mm(x, s) for x, s in zip(t, power)]
        tick()
    tick()
    s = SUBLANES
    while s < c:
        level = ((ri // (2 * s)) == (ci // (2 * s))) & ((ri // s) != (ci // s))
        off = [jnp.where(level, a, 0.0) for a in mats]
        lt = [_mm(o, x) for o, x in zip(off, t)]
        tick()
        t = [x - _mm(x, y) for x, y in zip(t, lt)]
        tick()
        s *= 2
    return t


XS_TOP = SUBLANES


def _mixer_init(sret0_ref, sgdn0_ref, cq0_ref, sret_ref, sgdn_ref, xs_ref, batch_minor=False, copy_states=True):
    tail = CONV_GDN - 1
    if copy_states:
        sret_ref[...] = sret0_ref[...]
        sgdn_ref[...] = sgdn0_ref[...]
    if batch_minor:
        for b in range(xs_ref.shape[0]):
            xs_ref[b, XS_TOP - tail:XS_TOP, :] = cq0_ref[:, b, :]
    else:
        xs_ref[:, XS_TOP - tail:XS_TOP, :] = cq0_ref[...]


def _mixer_block(getp, put_mix, cos2, sin2, const_refs, sret_ref, sgdn_ref, cq_ref, xs_ref, *,
                 bb, c, nch, n_valid, tick=lambda: None, batch_minor=False, read_ret=None, read_gdn=None):
    (tril_ref, dint_ref, qdec_ref, kdec_ref, cdec_ref, cw_ref, alog_ref, dtb_ref, nret_ref, ngdn_ref) = const_refs
    rows = nch * c
    tail = CONV_GDN - 1
    top = XS_TOP
    ri = lax.broadcasted_iota(jnp.int32, (c, c), 0)
    cj = lax.broadcasted_iota(jnp.int32, (c, c), 1)
    tri = ri >= cj
    strict = ri > cj
    tril_bf = tril_ref[...]
    scale = D_HEAD ** -0.5
    heads = range(N_HEADS)
    seqs = range(bb)
    chunks = [slice(j * c, (j + 1) * c) for j in range(nch)]
    join = lambda parts: parts[0] if len(parts) == 1 else jnp.concatenate(parts, axis=0)

    ret = {}
    for b in seqs:
        for h in heads:
            lo = h * D_HEAD
            q = getp(b, OFF_RQ + lo, OFF_RQ + lo + D_HEAD)
            k = getp(b, OFF_RK + lo, OFF_RK + lo + D_HEAD)
            v = getp(b, OFF_RV + lo, OFF_RV + lo + D_HEAD)
            qr = q * cos2 + pltpu.roll(q, D_HEAD // 2, 1) * sin2
            kr = (k * cos2 + pltpu.roll(k, D_HEAD // 2, 1) * sin2) * scale
            ret[b, h] = (qr, kr, v)
    tick()
    rtasks = [(b, h, j) for j in range(nch) for b in seqs for h in heads]
    scores = {t: _mm_nt(ret[t[0], t[1]][0][chunks[t[2]]], ret[t[0], t[1]][1][chunks[t[2]]]) * dint_ref[t[1]]
              for t in rtasks}
    tick()
    intra = {t: _mm(scores[t], ret[t[0], t[1]][2][chunks[t[2]]]) for t in rtasks}
    tick()
    kv = {t: _mm_tn(ret[t[0], t[1]][1][chunks[t[2]]] * kdec_ref[t[1]], ret[t[0], t[1]][2][chunks[t[2]]])
          for t in rtasks}
    tick()
    pairs = [(b, h) for b in seqs for h in heads]
    read_ret = read_ret or (lambda b, h: sret_ref[b, h])
    read_gdn = read_gdn or (lambda b, h: sgdn_ref[b, h])
    s_ret = {bh: read_ret(*bh) for bh in pairs}
    o_ret = {}
    for j in range(nch):
        for b, h in pairs:
            o_ret[b, h, j] = intra[b, h, j] + _mm(ret[b, h][0][chunks[j]] * qdec_ref[h], s_ret[b, h])
        for b, h in pairs:
            s_ret[b, h] = cdec_ref[h] * s_ret[b, h] + kv[b, h, j]
        tick()
    o_ret = {bh: join([o_ret[bh[0], bh[1], j] for j in range(nch)]) for bh in pairs}
    mu = {bh: jnp.mean(o_ret[bh], axis=-1, keepdims=True) for bh in pairs}
    cen = {bh: o_ret[bh] - mu[bh] for bh in pairs}
    var = {bh: jnp.mean(cen[bh] * cen[bh], axis=-1, keepdims=True) for bh in pairs}
    for b, h in pairs:
        lo = h * D_HEAD
        gate = getp(b, OFF_RG + lo, OFF_RG + lo + D_HEAD)
        o = cen[b, h] * lax.rsqrt(var[b, h] + EPS) * nret_ref[h:h + 1, :]
        put_mix(b, lo, o * _silu(gate))
        sret_ref[b, h] = s_ret[b, h]
    tick()

    qkvs, beta_alls, cum_cols, cum_rows, cum_tots = {}, {}, {}, {}, {}
    for b in seqs:
        assert CONV_GDN == 4
        w0, w1, w2, w3 = (cw_ref[i:i + 1, :] for i in range(CONV_GDN))
        u0 = getp(b, OFF_QKV, OFF_QKV + GDN_QKV)
        xs_ref[b, top:top + rows, :] = u0
        u1 = _shift_rows(u0, xs_ref[b, top - 1:top, :])
        p = u0 * w1 + u1 * w0
        p_lead = xs_ref[b, top - 2:top, :] * w1 + xs_ref[b, top - 3:top - 1, :] * w0
        conv = u0 * w3 + u1 * w2 + _shift_rows(p, p_lead)
        last = n_valid if nch == 1 else rows
        new_tail = xs_ref[b, top + last - tail:top + last, :]
        xs_ref[b, top - tail:top, :] = new_tail
        if batch_minor:
            cq_ref[:, b, :] = new_tail
        else:
            cq_ref[b] = new_tail
        qkvs[b] = _silu(conv)
        tick()

        ba = getp(b, OFF_BA, OFF_BA + LANES)
        beta_all = jax.nn.sigmoid(ba)
        z = ba + dtb_ref[...]
        softplus = jnp.maximum(z, 0.0) + jnp.log1p(jnp.exp(-jnp.abs(z)))
        g_all = -jnp.exp(alog_ref[...]) * softplus
        if n_valid < c:
            row = lax.broadcasted_iota(jnp.int32, (rows, LANES), 0)
            rowmask = (row < n_valid).astype(F32)
            beta_all = beta_all * rowmask
            g_all = g_all * rowmask
        beta_alls[b] = beta_all
        for j, rs in enumerate(chunks):
            parts = _split3(g_all[rs])
            cum_cols[b, j] = sum(jnp.dot(tril_bf, g, preferred_element_type=F32) for g in parts)
            cum_tots[b, j] = jnp.broadcast_to(cum_cols[b, j][c - 1:c, :], (D_HEAD, LANES))
            cum_rows[b, j] = cum_cols[b, j].T
    tick()

    tasks = [(j, b, h) for j in range(nch) for b in seqs for h in heads]
    qs, ks, vs, betas, ecums, kdecs, cdecs, amats, qkms = [], [], [], [], [], [], [], [], []
    raw = [(qkvs[b][chunks[j], h * D_HEAD:(h + 1) * D_HEAD],
            qkvs[b][chunks[j], D_GRP + h * D_HEAD:D_GRP + (h + 1) * D_HEAD]) for j, b, h in tasks]
    sumsq = [(jnp.sum(q * q, axis=-1, keepdims=True), jnp.sum(k * k, axis=-1, keepdims=True)) for q, k in raw]
    for i, (j, b, h) in enumerate(tasks):
        rs = chunks[j]
        lo = h * D_HEAD
        v = qkvs[b][rs, 2 * D_GRP + lo:2 * D_GRP + lo + D_HEAD]
        q = raw[i][0] * lax.rsqrt(sumsq[i][0] + EPS) * scale
        k = raw[i][1] * lax.rsqrt(sumsq[i][1] + EPS)
        beta = jnp.broadcast_to(beta_alls[b][rs, h:h + 1], (c, LANES))
        cum = jnp.broadcast_to(cum_cols[b, j][:, N_HEADS + h:N_HEADS + h + 1], (c, LANES))
        cum_row = cum_rows[b, j][N_HEADS + h:N_HEADS + h + 1, :]
        dmask = jnp.exp(jnp.where(tri, cum[:, :c] - cum_row, -jnp.inf))
        cum_last = jnp.broadcast_to(cum_tots[b, j][:, N_HEADS + h:N_HEADS + h + 1], (D_HEAD, LANES))
        kq = _mm_nt(jnp.concatenate([k, q], axis=0), k)
        amats.append(jnp.where(strict, beta[:, :c] * kq[:c] * dmask, 0.0))
        qkms.append(jnp.where(tri, kq[c:] * dmask, 0.0))
        qs.append(q)
        ks.append(k)
        vs.append(v)
        betas.append(beta)
        ecums.append(jnp.exp(cum))
        kdecs.append(jnp.exp(cum_last[:c] - cum))
        cdecs.append(jnp.exp(cum_last))
        if i % N_HEADS == N_HEADS - 1:
            tick()

    tinv = _unit_lower_inverses(amats, c, tick, nilpotent=min(SUBLANES, n_valid))
    eye = (ri == cj).astype(F32)
    sols = []
    for i in range(len(tasks)):
        rhs = jnp.concatenate([vs[i] * betas[i], ks[i] * (betas[i] * ecums[i])], axis=1)
        sols.append(rhs + _mm(tinv[i] - eye, rhs))
    tick()

    s_gdn = {bh: read_gdn(*bh) for bh in pairs}
    o_gdn = {}
    for j in range(nch):
        idx = {tasks[i][1:]: i for i in range(len(tasks)) if tasks[i][0] == j}
        lhs = {bh: jnp.concatenate([sols[i][:, D_HEAD:], qs[i] * ecums[i]], axis=0) for bh, i in idx.items()}
        both = {bh: _mm(lhs[bh], s_gdn[bh]) for bh in idx}
        tick()
        w = {bh: sols[i][:, :D_HEAD] - both[bh][:c] for bh, i in idx.items()}
        for bh, i in idx.items():
            o_gdn[bh + (j,)] = both[bh][c:] + _mm(qkms[i], w[bh])
        tick()
        upd = {bh: _mm_tn(ks[i] * kdecs[i], w[bh]) for bh, i in idx.items()}
        for bh, i in idx.items():
            s_gdn[bh] = cdecs[i] * s_gdn[bh] + upd[bh]
        tick()
    o_gdn = {bh: join([o_gdn[bh + (j,)] for j in range(nch)]) for bh in pairs}
    msq = {bh: jnp.mean(o_gdn[bh] * o_gdn[bh], axis=-1, keepdims=True) for bh in pairs}
    for b, h in pairs:
        lo = h * D_HEAD
        gate = getp(b, OFF_GG + lo, OFF_GG + lo + D_HEAD)
        o = o_gdn[b, h] * lax.rsqrt(msq[b, h] + EPS) * ngdn_ref[...]
        put_mix(b, D_GRP + lo, o * _silu(gate))
        sgdn_ref[b, h] = s_gdn[b, h]


N_MIXER_CONSTS = 10


def _mixer_kernel(p_ref, sret0_ref, sgdn0_ref, cq0_ref, cos_ref, sin_ref, *rest, bb, c, nch, n_valid, stored,
                  batch_minor, ring):
    const_refs = rest[:N_MIXER_CONSTS]
    mix_ref, sret_ref, sgdn_ref, cq_ref, xs_ref = rest[N_MIXER_CONSTS:N_MIXER_CONSTS + 5]
    rows = nch * c
    read_ret = read_gdn = None
    if ring:
        ret_buf, gdn_buf, sems = rest[N_MIXER_CONSTS + 5:]
        step, nsteps = pl.program_id(0), pl.num_programs(0)

        def copies(s):
            slot, seqs = s % ring, pl.ds(s * bb, bb)
            return (pltpu.make_async_copy(sret0_ref.at[seqs], ret_buf.at[slot], sems.at[0, slot]),
                    pltpu.make_async_copy(sgdn0_ref.at[seqs], gdn_buf.at[slot], sems.at[1, slot]))

        @pl.when(step == 0)
        def _():
            for s in range(ring - 1):
                for cp in copies(s):
                    cp.start()

        @pl.when(step + (ring - 1) < nsteps)
        def _():
            for cp in copies(step + (ring - 1)):
                cp.start()

        for cp in copies(step):
            cp.wait()
        slot = step % ring
        read_ret = lambda b, h: ret_buf[slot, b, h]
        read_gdn = lambda b, h: gdn_buf[slot, b, h]

    @pl.when(pl.program_id(1) == 0)
    def _():
        _mixer_init(sret0_ref, sgdn0_ref, cq0_ref, sret_ref, sgdn_ref, xs_ref, batch_minor, copy_states=not ring)

    def put_mix(b, lo, value):
        mix_ref[b, :, lo:lo + D_HEAD] = value.astype(mix_ref.dtype)

    def getp(b, lo, hi):
        blk = p_ref[b * stored:(b + 1) * stored, lo:hi]
        if stored < rows:
            blk = jnp.concatenate([blk, jnp.zeros((rows - stored, hi - lo), F32)], axis=0)
        return blk

    _mixer_block(getp, put_mix, cos_ref[...], sin_ref[...], const_refs, sret_ref, sgdn_ref, cq_ref, xs_ref,
                 bb=bb, c=c, nch=nch, n_valid=n_valid, batch_minor=batch_minor, read_ret=read_ret,
                 read_gdn=read_gdn)


PROJ_PANEL = MXU_WIDTH


def _mixer_block_ticks(bb, c, nch):
    levels = (c // SUBLANES).bit_length() - 1
    retention = 4 + nch + 1
    gdn_prep = bb + 1 + bb * nch
    inverse = 3 + 2 * levels
    return retention + gdn_prep + inverse + 1 + 3 * nch


def _proj_mixer_kernel(x0_ref, xa_ref, xb_ref, nw_ref, w_ref, sret0_ref, sgdn0_ref, cq0_ref, cos_ref, sin_ref,
                       *rest, c, nch, steps_per_seq, n_cast):
    const_refs = rest[:N_MIXER_CONSTS]
    cast_in = rest[N_MIXER_CONSTS:N_MIXER_CONSTS + n_cast]
    mix_ref, sret_ref, sgdn_ref, cq_ref = rest[N_MIXER_CONSTS + n_cast:N_MIXER_CONSTS + n_cast + 4]
    cast_out = rest[N_MIXER_CONSTS + n_cast + 4:N_MIXER_CONSTS + 2 * n_cast + 4]
    pja_ref, pjb_ref, xs_ref = rest[N_MIXER_CONSTS + 2 * n_cast + 4:]
    step = pl.program_id(0)
    rows = nch * c

    for src_ref, dst_ref in zip(cast_in, cast_out):
        dst_ref[...] = src_ref[...].astype(dst_ref.dtype)

    def project(x_ref, dst_ref, n_ticks):
        h = _rms(x_ref[...], nw_ref[...]).astype(BF16)
        panels = [(lo, min(lo + PROJ_PANEL, IN_PAD)) for lo in range(0, IN_PAD, PROJ_PANEL)]
        n_panels = len(panels)
        calls = [0]

        def emit():
            lo, hi = panels.pop(0)
            dst_ref[:, lo:hi] = jnp.dot(h, w_ref[:, lo:hi], preferred_element_type=F32)

        def tick():
            calls[0] += 1
            due = min(n_panels, -(-calls[0] * n_panels // n_ticks))
            while n_panels - len(panels) < due:
                emit()

        def flush():
            assert n_ticks == 1 or calls[0] == n_ticks, (calls[0], n_ticks)
            while panels:
                emit()

        return tick, flush

    @pl.when(step == 0)
    def _():
        project(x0_ref, pja_ref, 1)[1]()

    @pl.when(lax.rem(step, steps_per_seq) == 0)
    def _():
        _mixer_init(sret0_ref, sgdn0_ref, cq0_ref, sret_ref, sgdn_ref, xs_ref)

    for half, (cur_ref, x_next_ref, nxt_ref) in enumerate(((pja_ref, xa_ref, pjb_ref), (pjb_ref, xb_ref, pja_ref))):
        r0 = half * rows
        tick, flush = project(x_next_ref, nxt_ref, _mixer_block_ticks(1, c, nch))

        def put_mix(b, lo, value, r0=r0):
            mix_ref[r0:r0 + rows, lo:lo + D_HEAD] = value.astype(mix_ref.dtype)

        _mixer_block(lambda b, lo, hi, cur_ref=cur_ref: cur_ref[:, lo:hi], put_mix,
                     cos_ref[r0:r0 + rows, :], sin_ref[r0:r0 + rows, :], const_refs,
                     sret_ref, sgdn_ref, cq_ref, xs_ref, bb=1, c=c, nch=nch, n_valid=c, tick=tick)
        flush()


def _retention_decay_tables(c, n_valid):
    f32 = np.float32
    lg = np.log1p(-np.power(f32(2.0), f32(-5.0) - np.arange(N_HEADS, dtype=f32)))[:, None].astype(f32)
    idx = np.arange(c, dtype=f32)
    diff = idx[:, None] - idx[None, :]
    dint = np.where(diff[None] >= 0, np.exp(lg[:, :, None] * np.maximum(diff[None], 0)), f32(0.0)).astype(f32)
    qdec = np.exp(lg * (idx + f32(1.0))).astype(f32)
    kdec = np.where(idx[None, :] < n_valid, np.exp(lg * np.minimum(f32(n_valid) - f32(1.0) - idx, c)), f32(0.0))
    cdec = np.exp(lg * f32(n_valid)).astype(f32)
    bc = lambda t: np.broadcast_to(t.astype(f32)[:, :, None], t.shape + (LANES,))
    return dint, bc(qdec), bc(kdec), np.broadcast_to(cdec[:, :, None], (N_HEADS, 1, LANES))


def _mixer_const_operands(c, n_valid, consts):
    cw, alog, dtb, nret, ngdn = consts
    dint, qdec, kdec, cdec = _retention_decay_tables(c, n_valid)
    idx = np.arange(c)
    tril = (idx[:, None] >= idx[None, :]).astype(np.float32)
    arrays = (jnp.asarray(tril, BF16), jnp.asarray(dint), jnp.asarray(qdec), jnp.asarray(kdec), jnp.asarray(cdec),
              cw, alog, dtb, nret, ngdn)
    assert len(arrays) == N_MIXER_CONSTS
    return arrays, [a.shape for a in arrays]


def _mixer(proj, row0, nb, length, sret0, sgdn0, cq0, rope, rope_row0, consts, *, bb, c, nch, n_valid,
           shared_init, stored=None, batch_minor=False, ring=0):
    rows = nch * c
    assert nb % bb == 0 and length % rows == 0 and rope_row0 % rows == 0
    assert not shared_init or bb == 1
    assert n_valid == c or nch == 1
    assert bb == 1 or length == rows
    const_arrays, const_shapes = _mixer_const_operands(c, n_valid, consts)
    nsteps = length // rows
    stored = rows if stored is None else stored
    assert stored == rows or (nsteps == 1 and stored >= n_valid and (bb * stored) % SUBLANES == 0)
    assert row0 % (bb * stored) == 0
    blk0 = row0 // (bb * stored)
    init_idx = (lambda b, i: (0, 0, 0, 0)) if shared_init else (lambda b, i: (b, 0, 0, 0))
    init_idx3 = (lambda b, i: (0, 0, 0)) if shared_init else (lambda b, i: (b, 0, 0))
    whole = lambda shape: pl.BlockSpec(shape, lambda b, i: (0,) * len(shape))
    rope_spec = pl.BlockSpec((rows, LANES), lambda b, i: (rope_row0 // rows + i, 0))
    state_shape = (bb, N_HEADS, D_HEAD, D_HEAD)
    tail = CONV_GDN - 1
    assert not (batch_minor and shared_init)
    cq_shape, cq_block = (nb, tail, GDN_QKV), (bb, tail, GDN_QKV)
    cq_idx = lambda b, i: (b, 0, 0)
    if batch_minor:
        cq_shape, cq_block = (tail, nb, GDN_QKV), (tail, bb, GDN_QKV)
        cq_idx = init_idx3 = lambda b, i: (0, b, 0)
    kern = functools.partial(_mixer_kernel, bb=bb, c=c, nch=nch, n_valid=n_valid, stored=stored,
                             batch_minor=batch_minor, ring=ring)
    state_spec = pl.BlockSpec(state_shape, init_idx)
    ring_scratch = []
    if ring:
        assert nsteps == 1 and not shared_init and nb // bb >= ring - 1
        state_spec = pl.BlockSpec(memory_space=pl.ANY)
        ring_scratch = [pltpu.VMEM((ring,) + state_shape, F32), pltpu.VMEM((ring,) + state_shape, F32),
                        pltpu.SemaphoreType.DMA((2, ring))]
    return pl.pallas_call(
        kern,
        out_shape=(
            jax.ShapeDtypeStruct((nb, length, D_MODEL), BF16),
            jax.ShapeDtypeStruct((nb, N_HEADS, D_HEAD, D_HEAD), F32),
            jax.ShapeDtypeStruct((nb, N_HEADS, D_HEAD, D_HEAD), F32),
            jax.ShapeDtypeStruct(cq_shape, F32),
        ),
        grid=(nb // bb, nsteps),
        in_specs=[
            pl.BlockSpec((bb * stored, IN_PAD), lambda b, i: (blk0 + b * nsteps + i, 0)),
            state_spec,
            state_spec,
            pl.BlockSpec(cq_block, init_idx3),
            rope_spec,
            rope_spec,
        ] + [whole(shape) for shape in const_shapes],
        out_specs=(
            pl.BlockSpec((bb, rows, D_MODEL), lambda b, i: (b, i, 0)),
            pl.BlockSpec(state_shape, lambda b, i: (b, 0, 0, 0)),
            pl.BlockSpec(state_shape, lambda b, i: (b, 0, 0, 0)),
            pl.BlockSpec(cq_block, cq_idx),
        ),
        scratch_shapes=[pltpu.VMEM((bb, XS_TOP + rows, GDN_QKV), F32)] + ring_scratch,
        compiler_params=pltpu.CompilerParams(dimension_semantics=("arbitrary", "arbitrary"),
                                             vmem_limit_bytes=VMEM_LIMIT),
        name="mixer",
    )(proj, sret0, sgdn0, cq0, rope[0], rope[1], *const_arrays)


def _row_slab(nrows, nsteps):
    for hold in (1, 2, 4, 8):
        slabs = nsteps // hold
        if nsteps % hold == 0 and nrows % slabs == 0 and (nrows // slabs) % (2 * SUBLANES) == 0:
            return nrows // slabs, hold
    raise ValueError((nrows, nsteps))


def _proj_mixer(x, norm_w, w_bf, nseq, sret0, sgdn0, cq0, rope, consts, to_bf16, *, c, nch):
    rows = nch * c
    total = x.shape[0]
    length = total // nseq
    assert total % nseq == 0 and length % (2 * rows) == 0
    nblk = total // rows
    steps_per_seq = length // (2 * rows)
    const_arrays, const_shapes = _mixer_const_operands(c, c, consts)
    whole = lambda shape, **kw: pl.BlockSpec(shape, lambda s: (0,) * len(shape), **kw)
    rope_spec = pl.BlockSpec((2 * rows, LANES), lambda s: (lax.rem(s, steps_per_seq), 0))
    state_shape = (1, N_HEADS, D_HEAD, D_HEAD)
    tail = CONV_GDN - 1
    nsteps = nblk // 2
    slabs = [_row_slab(w.shape[0], nsteps) for w in to_bf16]
    cast_specs = [pl.BlockSpec((r, w.shape[1]), lambda s, hold=hold: (s // hold, 0))
                  for w, (r, hold) in zip(to_bf16, slabs)]
    kern = functools.partial(_proj_mixer_kernel, c=c, nch=nch, steps_per_seq=steps_per_seq, n_cast=len(to_bf16))
    return pl.pallas_call(
        kern,
        out_shape=(
            jax.ShapeDtypeStruct((total, D_MODEL), BF16),
            jax.ShapeDtypeStruct((nseq, N_HEADS, D_HEAD, D_HEAD), F32),
            jax.ShapeDtypeStruct((nseq, N_HEADS, D_HEAD, D_HEAD), F32),
            jax.ShapeDtypeStruct((nseq, tail, GDN_QKV), F32),
        ) + tuple(jax.ShapeDtypeStruct(w.shape, BF16) for w in to_bf16),
        grid=(nsteps,),
        in_specs=[
            pl.BlockSpec((rows, D_MODEL), lambda s: (0, 0), pipeline_mode=pl.Buffered(1)),
            pl.BlockSpec((rows, D_MODEL), lambda s: (2 * s + 1, 0)),
            pl.BlockSpec((rows, D_MODEL), lambda s: (jnp.minimum(2 * s + 2, nblk - 1), 0)),
            whole((1, D_MODEL)),
            whole((D_MODEL, IN_PAD), pipeline_mode=pl.Buffered(1)),
            whole(state_shape),
            whole(state_shape),
            whole((1, tail, GDN_QKV)),
            rope_spec,
            rope_spec,
        ] + [whole(shape) for shape in const_shapes] + cast_specs,
        out_specs=(
            pl.BlockSpec((2 * rows, D_MODEL), lambda s: (s, 0)),
            pl.BlockSpec(state_shape, lambda s: (s // steps_per_seq, 0, 0, 0)),
            pl.BlockSpec(state_shape, lambda s: (s // steps_per_seq, 0, 0, 0)),
            pl.BlockSpec((1, tail, GDN_QKV), lambda s: (s // steps_per_seq, 0, 0)),
        ) + tuple(cast_specs),
        scratch_shapes=[pltpu.VMEM((rows, IN_PAD), F32), pltpu.VMEM((rows, IN_PAD), F32),
                        pltpu.VMEM((1, XS_TOP + rows, GDN_QKV), F32)],
        compiler_params=pltpu.CompilerParams(dimension_semantics=("arbitrary",),
                                             vmem_limit_bytes=VMEM_LIMIT),
        name="proj_mixer",
    )(x, x, x, norm_w, w_bf, sret0, sgdn0, cq0, rope[0], rope[1], *const_arrays, *to_bf16)


FFN_COL_CHUNK = D_FF // 11


def _ffn_kernel(x_ref, mix_ref, *rest, tm, stride, prefix):
    if prefix:
        (xm_ref, mixm_ref, wout_ref, nffn_ref, wup_ref, cw_ref, wdn_ref, nfin_ref,
         y_ref, tail_ref, full_ref, lead_ref) = rest
    else:
        tail0_ref, wout_ref, nffn_ref, wup_ref, cw_ref, wdn_ref, nfin_ref, y_ref, tail_ref, full_ref = rest
    t = pl.program_id(1)
    carry = (CONV_FFN - 1) * stride
    base = _round_up(carry, SUBLANES)

    def up_project(x, mix):
        x1 = x + jnp.dot(mix, wout_ref[...], preferred_element_type=F32)
        h = _rms(x1, nffn_ref[...]).astype(BF16)
        return x1, jnp.dot(h, wup_ref[...], preferred_element_type=F32)

    if prefix:
        @pl.when((pl.program_id(0) == 0) & (t == 0))
        def _():
            um = up_project(xm_ref[...], mixm_ref[...])[1]
            lead_ref[...] = um[um.shape[0] - carry:, :]

    @pl.when(t == 0)
    def _():
        full_ref[base - carry:base, :] = lead_ref[...] if prefix else tail0_ref[0]

    x1, up = up_project(x_ref[...], mix_ref[...])
    full_ref[base:base + tm, :] = up

    def conv_cols(lo):
        acc = full_ref[base - carry:base - carry + tm, lo:lo + FFN_COL_CHUNK] * cw_ref[0:1, lo:lo + FFN_COL_CHUNK]
        for i in range(1, CONV_FFN):
            r0 = base - carry + i * stride
            acc = acc + full_ref[r0:r0 + tm, lo:lo + FFN_COL_CHUNK] * cw_ref[i:i + 1, lo:lo + FFN_COL_CHUNK]
        return acc

    x2 = x1
    for j in range(D_FF // FFN_COL_CHUNK):
        lo = j * FFN_COL_CHUNK
        act = (_silu(conv_cols(lo)) * conv_cols(D_FF + lo)).astype(BF16)
        x2 = x2 + jnp.dot(act, wdn_ref[lo:lo + FFN_COL_CHUNK, :], preferred_element_type=F32)
    y_ref[...] = _rms(x2, nfin_ref[...])

    new_tail = full_ref[base + tm - carry:base + tm, :]
    full_ref[base - carry:base, :] = new_tail
    tail_ref[0] = new_tail


def _ffn(x, mix, lead, weights, *, nseq, tm, stride):
    wout, nffn, wup, cw, wdn, nfin = weights
    rows = x.shape[0]
    assert rows % (nseq * tm) == 0
    nt = rows // (nseq * tm)
    carry = (CONV_FFN - 1) * stride
    base = _round_up(carry, SUBLANES)
    assert tm >= carry
    resident = lambda shape: pl.BlockSpec(shape, lambda b, t: (0, 0), pipeline_mode=pl.Buffered(1))
    small = lambda shape: pl.BlockSpec(shape, lambda b, t: (0, 0))
    prefix = isinstance(lead, tuple)
    if prefix:
        assert stride == 1 and all(a.shape[0] >= carry for a in lead)
        lead_specs = [small(a.shape) for a in lead]
        lead_scratch = [pltpu.VMEM((carry, 2 * D_FF), F32)]
    else:
        lead = (lead,)
        lead_specs = [pl.BlockSpec((1, carry, 2 * D_FF), lambda b, t: (b, 0, 0))]
        lead_scratch = []
    kern = functools.partial(_ffn_kernel, tm=tm, stride=stride, prefix=prefix)
    return pl.pallas_call(
        kern,
        out_shape=(
            jax.ShapeDtypeStruct((rows, D_MODEL), F32),
            jax.ShapeDtypeStruct((nseq, carry, 2 * D_FF), F32),
        ),
        grid=(nseq, nt),
        in_specs=[
            pl.BlockSpec((tm, D_MODEL), lambda b, t: (b * nt + t, 0)),
            pl.BlockSpec((tm, D_MODEL), lambda b, t: (b * nt + t, 0)),
        ] + lead_specs + [
            resident((D_MODEL, D_MODEL)),
            small((1, D_MODEL)),
            resident((D_MODEL, 2 * D_FF)),
            small((CONV_FFN, 2 * D_FF)),
            resident((D_FF, D_MODEL)),
            small((1, D_MODEL)),
        ],
        out_specs=(
            pl.BlockSpec((tm, D_MODEL), lambda b, t: (b * nt + t, 0)),
            pl.BlockSpec((1, carry, 2 * D_FF), lambda b, t: (b, 0, 0)),
        ),
        scratch_shapes=[pltpu.VMEM((base + tm, 2 * D_FF), F32)] + lead_scratch,
        compiler_params=pltpu.CompilerParams(dimension_semantics=("arbitrary", "arbitrary"),
                                             vmem_limit_bytes=VMEM_LIMIT),
        name="out_ffn",
    )(x, mix, *lead, wout, nffn, wup, cw, wdn, nfin)


def kernel(x_prompt, x_sample, state_ret, state_gdn, state_conv_qkv, state_ffn_conv, meta_tokens, norm_mix,
           w_in, conv_gdn, gdn_a_log, gdn_dt_bias, norm_ret, norm_gdn, w_out, norm_ffn, w_up, conv_ffn,
           w_down, norm_final):
    depth = w_in.shape[0]
    assert depth == 1
    nbp, seq, _ = x_prompt.shape
    nbs, dec_seq, _ = x_sample.shape
    assert dec_seq <= SAMPLE_PAD and nbs % SAMPLE_GROUP == 0
    assert seq % (2 * PROMPT_CHUNK * PROMPT_CHUNKS_PER_STEP) == 0
    layer = 0

    row = lambda v: v.reshape(1, -1).astype(F32)
    pad_ba = lambda v: jnp.pad(v.astype(F32), (N_HEADS, LANES - 2 * N_HEADS)).reshape(1, LANES)
    mixer_consts = (conv_gdn[layer], pad_ba(gdn_a_log[layer]), pad_ba(gdn_dt_bias[layer]),
                    norm_ret[layer].reshape(N_HEADS, D_HEAD), row(norm_gdn[layer]))
    nmix = row(norm_mix[layer])

    assert seq % N_META == 0 and (seq + N_META) % SAMPLE_PAD == 0
    rope_meta_row0, rope_sample_row0 = seq, seq + N_META
    rope_segments = [(0, N_META), (rope_meta_row0, 0), (rope_sample_row0, PAST_LEN)]

    small_rows = jnp.concatenate([x_sample.reshape(nbs * dec_seq, D_MODEL), meta_tokens.astype(F32)], axis=0)
    proj_small, w_in_bf, *rope = _proj(small_rows, nmix, w_in[layer], seq + N_META + SAMPLE_PAD, rope_segments)
    meta_row0 = nbs * dec_seq

    zero_state = jnp.zeros((1, N_HEADS, D_HEAD, D_HEAD), F32)
    zero_cq = jnp.zeros((1, CONV_GDN - 1, GDN_QKV), F32)
    mix_m, sret_m, sgdn_m, cq_m = _mixer(proj_small, meta_row0, 1, N_META, zero_state, zero_state, zero_cq, rope,
                                         rope_meta_row0, mixer_consts, bb=1, c=N_META, nch=1, n_valid=N_META,
                                         shared_init=True)

    xp = x_prompt.reshape(nbp * seq, D_MODEL)
    mix_p, sret_p, sgdn_p, cq_p, w_out_bf, w_up_bf, w_down_bf = _proj_mixer(
        xp, nmix, w_in_bf, nbp, sret_m, sgdn_m, cq_m, rope, mixer_consts,
        (w_out[layer], w_up[layer], w_down[layer]), c=PROMPT_CHUNK, nch=PROMPT_CHUNKS_PER_STEP)
    ffn_weights = (w_out_bf, row(norm_ffn[layer]), w_up_bf, conv_ffn[layer], w_down_bf, row(norm_final))

    meta_lead = (meta_tokens.astype(F32), mix_m.reshape(N_META, D_MODEL))
    y_p, cf_p = _ffn(xp, mix_p, meta_lead, ffn_weights, nseq=nbp, tm=512, stride=1)
    y_prompt = y_p.reshape(nbp, seq, D_MODEL)

    mix_s, sret_s, sgdn_s, cq_s = _mixer(proj_small, 0, nbs, SAMPLE_PAD, state_ret[layer], state_gdn[layer],
                                         jnp.swapaxes(state_conv_qkv[layer], 0, 1), rope, rope_sample_row0,
                                         mixer_consts, bb=SAMPLE_SEQS_PER_STEP, c=SAMPLE_PAD, nch=1,
                                         n_valid=dec_seq, shared_init=False, stored=dec_seq, batch_minor=True,
                                         ring=SAMPLE_STATE_SLOTS)
    cq_s = jnp.swapaxes(cq_s, 0, 1)
    ng = nbs // SAMPLE_GROUP
    to_tmajor = lambda a: a.reshape(ng, SAMPLE_GROUP, a.shape[1], a.shape[2]).transpose(0, 2, 1, 3)
    xs_t = to_tmajor(x_sample).reshape(nbs * dec_seq, D_MODEL)
    mix_t = to_tmajor(mix_s[:, :dec_seq]).reshape(nbs * dec_seq, D_MODEL)
    cf0_t = to_tmajor(state_ffn_conv[layer]).reshape(ng, (CONV_FFN - 1) * SAMPLE_GROUP, 2 * D_FF)
    y_s_t, cf_s_t = _ffn(xs_t, mix_t, cf0_t, ffn_weights, nseq=ng, tm=dec_seq * SAMPLE_GROUP,
                         stride=SAMPLE_GROUP)
    y_sample = y_s_t.reshape(ng, dec_seq, SAMPLE_GROUP, D_MODEL).transpose(0, 2, 1, 3).reshape(nbs, dec_seq, D_MODEL)
    cf_s = cf_s_t.reshape(ng, CONV_FFN - 1, SAMPLE_GROUP, 2 * D_FF).transpose(0, 2, 1, 3).reshape(
        nbs, CONV_FFN - 1, 2 * D_FF)

    return (y_prompt, y_sample, sret_p[None], sgdn_p[None], cq_p[None], cf_p[None],
            sret_s[None], sgdn_s[None], cq_s[None], cf_s[None])
```

```python
import functools

import jax
import numpy as np
import jax.numpy as jnp
from jax import lax
from jax.experimental import pallas as pl
from jax.experimental.pallas import tpu as pltpu

F32 = jnp.float32
BF16 = jnp.bfloat16

D_MODEL = 1024
N_META = 16
PAST_LEN = 16384
N_HEADS = 4
D_HEAD = 128
D_GRP = N_HEADS * D_HEAD
GDN_QKV = 3 * D_GRP
CONV_GDN = 4
CONV_FFN = 3
D_FF = 2816
ROPE_THETA = 10000.0
EPS = 1e-6

OFF_RQ, OFF_RK, OFF_RV, OFF_RG = 0, D_GRP, 2 * D_GRP, 3 * D_GRP
OFF_QKV = 4 * D_GRP
OFF_GG = OFF_QKV + GDN_QKV
OFF_BA = OFF_GG + D_GRP
IN_WIDTH = OFF_BA + 2 * N_HEADS
LANES = 128
SUBLANES = 8
MXU_WIDTH = 256
V7X_VMEM_BYTES = 64 * 1024 * 1024
IN_PAD = OFF_BA + LANES

PROMPT_CHUNK = 128
PROMPT_CHUNKS_PER_STEP = 2
SAMPLE_PAD = 8
SAMPLE_SEQS_PER_STEP = 16
SAMPLE_STATE_SLOTS = 3
SAMPLE_GROUP = 64
VMEM_LIMIT = V7X_VMEM_BYTES * 7 // 8


def _round_up(n, m):
    return (n + m - 1) // m * m


def _mm(a, b):
    return jnp.dot(a.astype(BF16), b.astype(BF16), preferred_element_type=F32)


def _mm_nt(a, b):
    return lax.dot_general(a.astype(BF16), b.astype(BF16), (((1,), (1,)), ((), ())),
                           preferred_element_type=F32)


def _mm_tn(a, b):
    return lax.dot_general(a.astype(BF16), b.astype(BF16), (((0,), (0,)), ((), ())),
                           preferred_element_type=F32)


def _split3(x):
    p0 = x.astype(BF16)
    r = x - p0.astype(F32)
    p1 = r.astype(BF16)
    p2 = (r - p1.astype(F32)).astype(BF16)
    return p0, p1, p2


def _shift_rows(x, lead):
    k = lead.shape[0]
    row = lax.broadcasted_iota(jnp.int32, x.shape, 0)
    out = pltpu.roll(x, k, 0)
    for i in range(k):
        out = jnp.where(row == i, lead[i:i + 1, :], out)
    return out


def _silu(x):
    return x * jax.nn.sigmoid(x)


def _rms(x, w):
    return x * lax.rsqrt(jnp.mean(x * x, axis=-1, keepdims=True) + EPS) * w


def _rope_kernel(invf_ref, cos_ref, sin_ref, *, segments):
    shape = cos_ref.shape
    r = lax.broadcasted_iota(jnp.int32, shape, 0)
    pos = r + (segments[0][1] - segments[0][0])
    for row0, pos0 in segments[1:]:
        pos = jnp.where(r >= row0, r + (pos0 - row0), pos)
    ang = pos.astype(F32) * invf_ref[...]
    lane = lax.broadcasted_iota(jnp.int32, shape, 1)
    sin = jnp.sin(ang)
    cos_ref[...] = jnp.cos(ang)
    sin_ref[...] = jnp.where(lane < D_HEAD // 2, -sin, sin)


W_IN_COLS = 11 * LANES


def _proj_kernel(x_ref, nw_ref, wt_ref, invf_ref, o_ref, wbf_ref, cos_ref, sin_ref, h_ref, *, segments):
    first = pl.program_id(0) * W_IN_COLS

    @pl.when(first == 0)
    def _():
        h_ref[...] = _rms(x_ref[...], nw_ref[...]).astype(BF16)
        _rope_kernel(invf_ref, cos_ref, sin_ref, segments=segments)

    r = lax.broadcasted_iota(jnp.int32, wt_ref.shape, 0)
    w_blk = jnp.where(first + r < IN_WIDTH, wt_ref[...], 0.0).T.astype(BF16)
    wbf_ref[...] = w_blk
    o_ref[...] = jnp.dot(h_ref[...], w_blk, preferred_element_type=F32)


def _proj(x, norm_w, w, rope_rows, rope_segments):
    rows = x.shape[0]
    half = D_HEAD // 2
    inv_freq = ROPE_THETA ** (-jnp.arange(half, dtype=F32) / half)
    invf2 = jnp.concatenate([inv_freq, inv_freq]).reshape(1, LANES)
    table = jax.ShapeDtypeStruct((rope_rows, LANES), F32)
    table_spec = pl.BlockSpec((rope_rows, LANES), lambda j: (0, 0))
    return pl.pallas_call(
        functools.partial(_proj_kernel, segments=tuple(rope_segments)),
        out_shape=(jax.ShapeDtypeStruct((rows, IN_PAD), F32), jax.ShapeDtypeStruct((D_MODEL, IN_PAD), BF16),
                   table, table),
        grid=(IN_PAD // W_IN_COLS,),
        in_specs=[
            pl.BlockSpec((rows, D_MODEL), lambda j: (0, 0)),
            pl.BlockSpec((1, D_MODEL), lambda j: (0, 0)),
            pl.BlockSpec((W_IN_COLS, D_MODEL), lambda j: (j, 0)),
            pl.BlockSpec((1, LANES), lambda j: (0, 0)),
        ],
        out_specs=(pl.BlockSpec((rows, W_IN_COLS), lambda j: (0, j)),
                   pl.BlockSpec((D_MODEL, W_IN_COLS), lambda j: (0, j)), table_spec, table_spec),
        scratch_shapes=[pltpu.VMEM((rows, D_MODEL), BF16)],
        compiler_params=pltpu.CompilerParams(dimension_semantics=("arbitrary",),
                                             vmem_limit_bytes=VMEM_LIMIT),
        name="in_proj",
    )(x, norm_w, w.T, invf2)


def _unit_lower_inverses(mats, c, tick=lambda: None, nilpotent=SUBLANES):
    ri = lax.broadcasted_iota(jnp.int32, (c, c), 0)
    ci = lax.broadcasted_iota(jnp.int32, (c, c), 1)
    eye = (ri == ci).astype(F32)
    diag_blk = (ri // SUBLANES) == (ci // SUBLANES)
    ad = [jnp.where(diag_blk, a, 0.0) for a in mats]
    assert 1 <= nilpotent <= SUBLANES
    n_factors = max(0, (nilpotent - 1).bit_length() - 1)
    t = [eye - x for x in ad]
    power = ad
    for i in range(2):
        if i < n_factors:
            power = [_mm(x, x) for x in power]
            t = [x + _mm(x, s) for x, s in zip(t, power)]
        tick()
    tick()
    s = SUBLANES
    while s < c:
        level = ((ri // (2 * s)) == (ci // (2 * s))) & ((ri // s) != (ci // s))
        off = [jnp.where(level, a, 0.0) for a in mats]
        lt = [_mm(o, x) for o, x in zip(off, t)]
        tick()
        t = [x - _mm(x, y) for x, y in zip(t, lt)]
        tick()
        s *= 2
    return t


XS_TOP = SUBLANES


def _mixer_init(sret0_ref, sgdn0_ref, cq0_ref, sret_ref, sgdn_ref, xs_ref, batch_minor=False, copy_states=True):
    tail = CONV_GDN - 1
    if copy_states:
        sret_ref[...] = sret0_ref[...]
        sgdn_ref[...] = sgdn0_ref[...]
    if batch_minor:
        for b in range(xs_ref.shape[0]):
            xs_ref[b, XS_TOP - tail:XS_TOP, :] = cq0_ref[:, b, :]
    else:
        xs_ref[:, XS_TOP - tail:XS_TOP, :] = cq0_ref[...]


def _mixer_block(getp, put_mix, cos2, sin2, const_refs, sret_ref, sgdn_ref, cq_ref, xs_ref, *,
                 bb, c, nch, n_valid, tick=lambda: None, batch_minor=False, read_ret=None, read_gdn=None):
    (tril_ref, dint_ref, qdec_ref, kdec_ref, cdec_ref, cw_ref, alog_ref, dtb_ref, nret_ref, ngdn_ref) = const_refs
    rows = nch * c
    tail = CONV_GDN - 1
    top = XS_TOP
    ri = lax.broadcasted_iota(jnp.int32, (c, c), 0)
    cj = lax.broadcasted_iota(jnp.int32, (c, c), 1)
    tri = ri >= cj
    strict = ri > cj
    tril_bf = tril_ref[...]
    scale = D_HEAD ** -0.5
    heads = range(N_HEADS)
    seqs = range(bb)
    chunks = [slice(j * c, (j + 1) * c) for j in range(nch)]
    join = lambda parts: parts[0] if len(parts) == 1 else jnp.concatenate(parts, axis=0)

    ret = {}
    for b in seqs:
        for h in heads:
            lo = h * D_HEAD
            q = getp(b, OFF_RQ + lo, OFF_RQ + lo + D_HEAD)
            k = getp(b, OFF_RK + lo, OFF_RK + lo + D_HEAD)
            v = getp(b, OFF_RV + lo, OFF_RV + lo + D_HEAD)
            qr = q * cos2 + pltpu.roll(q, D_HEAD // 2, 1) * sin2
            kr = (k * cos2 + pltpu.roll(k, D_HEAD // 2, 1) * sin2) * scale
            ret[b, h] = (qr, kr, v)
    tick()
    rtasks = [(b, h, j) for j in range(nch) for b in seqs for h in heads]
    scores = {t: _mm_nt(ret[t[0], t[1]][0][chunks[t[2]]], ret[t[0], t[1]][1][chunks[t[2]]]) * dint_ref[t[1]]
              for t in rtasks}
    tick()
    intra = {t: _mm(scores[t], ret[t[0], t[1]][2][chunks[t[2]]]) for t in rtasks}
    tick()
    kv = {t: _mm_tn(ret[t[0], t[1]][1][chunks[t[2]]] * kdec_ref[t[1]], ret[t[0], t[1]][2][chunks[t[2]]])
          for t in rtasks}
    tick()
    pairs = [(b, h) for b in seqs for h in heads]
    read_ret = read_ret or (lambda b, h: sret_ref[b, h])
    read_gdn = read_gdn or (lambda b, h: sgdn_ref[b, h])
    s_ret = {bh: read_ret(*bh) for bh in pairs}
    o_ret = {}
    for j in range(nch):
        for b, h in pairs:
            o_ret[b, h, j] = intra[b, h, j] + _mm(ret[b, h][0][chunks[j]] * qdec_ref[h], s_ret[b, h])
        for b, h in pairs:
            s_ret[b, h] = cdec_ref[h] * s_ret[b, h] + kv[b, h, j]
        tick()
    o_ret = {bh: join([o_ret[bh[0], bh[1], j] for j in range(nch)]) for bh in pairs}
    mu = {bh: jnp.mean(o_ret[bh], axis=-1, keepdims=True) for bh in pairs}
    cen = {bh: o_ret[bh] - mu[bh] for bh in pairs}
    var = {bh: jnp.mean(cen[bh] * cen[bh], axis=-1, keepdims=True) for bh in pairs}
    for b, h in pairs:
        lo = h * D_HEAD
        gate = getp(b, OFF_RG + lo, OFF_RG + lo + D_HEAD)
        o = cen[b, h] * lax.rsqrt(var[b, h] + EPS) * nret_ref[h:h + 1, :]
        put_mix(b, lo, o * _silu(gate))
        sret_ref[b, h] = s_ret[b, h]
    tick()

    qkvs, beta_alls, cum_cols, cum_rows, cum_tots = {}, {}, {}, {}, {}
    for b in seqs:
        assert CONV_GDN == 4
        w0, w1, w2, w3 = (cw_ref[i:i + 1, :] for i in range(CONV_GDN))
        u0 = getp(b, OFF_QKV, OFF_QKV + GDN_QKV)
        xs_ref[b, top:top + rows, :] = u0
        u1 = _shift_rows(u0, xs_ref[b, top - 1:top, :])
        p = u0 * w1 + u1 * w0
        p_lead = xs_ref[b, top - 2:top, :] * w1 + xs_ref[b, top - 3:top - 1, :] * w0
        conv = u0 * w3 + u1 * w2 + _shift_rows(p, p_lead)
        last = n_valid if nch == 1 else rows
        new_tail = xs_ref[b, top + last - tail:top + last, :]
        xs_ref[b, top - tail:top, :] = new_tail
        if batch_minor:
            cq_ref[:, b, :] = new_tail
        else:
            cq_ref[b] = new_tail
        qkvs[b] = _silu(conv)
        tick()

        ba = getp(b, OFF_BA, OFF_BA + LANES)
        beta_all = jax.nn.sigmoid(ba)
        z = ba + dtb_ref[...]
        softplus = jnp.maximum(z, 0.0) + jnp.log1p(jnp.exp(-jnp.abs(z)))
        g_all = -jnp.exp(alog_ref[...]) * softplus
        if n_valid < c:
            row = lax.broadcasted_iota(jnp.int32, (rows, LANES), 0)
            rowmask = (row < n_valid).astype(F32)
            beta_all = beta_all * rowmask
            g_all = g_all * rowmask
        beta_alls[b] = beta_all
        for j, rs in enumerate(chunks):
            parts = _split3(g_all[rs])
            cum_cols[b, j] = sum(jnp.dot(tril_bf, g, preferred_element_type=F32) for g in parts)
            cum_tots[b, j] = jnp.broadcast_to(cum_cols[b, j][c - 1:c, :], (D_HEAD, LANES))
            cum_rows[b, j] = cum_cols[b, j].T
    tick()

    tasks = [(j, b, h) for j in range(nch) for b in seqs for h in heads]
    qs, ks, vs, betas, ecums, kdecs, cdecs, amats, qkms = [], [], [], [], [], [], [], [], []
    raw = [(qkvs[b][chunks[j], h * D_HEAD:(h + 1) * D_HEAD],
            qkvs[b][chunks[j], D_GRP + h * D_HEAD:D_GRP + (h + 1) * D_HEAD]) for j, b, h in tasks]
    sumsq = [(jnp.sum(q * q, axis=-1, keepdims=True), jnp.sum(k * k, axis=-1, keepdims=True)) for q, k in raw]
    for i, (j, b, h) in enumerate(tasks):
        rs = chunks[j]
        lo = h * D_HEAD
        v = qkvs[b][rs, 2 * D_GRP + lo:2 * D_GRP + lo + D_HEAD]
        q = raw[i][0] * lax.rsqrt(sumsq[i][0] + EPS) * scale
        k = raw[i][1] * lax.rsqrt(sumsq[i][1] + EPS)
        beta = jnp.broadcast_to(beta_alls[b][rs, h:h + 1], (c, LANES))
        cum = jnp.broadcast_to(cum_cols[b, j][:, N_HEADS + h:N_HEADS + h + 1], (c, LANES))
        cum_row = cum_rows[b, j][N_HEADS + h:N_HEADS + h + 1, :]
        dmask = jnp.exp(jnp.where(tri, cum[:, :c] - cum_row, -jnp.inf))
        cum_last = jnp.broadcast_to(cum_tots[b, j][:, N_HEADS + h:N_HEADS + h + 1], (D_HEAD, LANES))
        kq = _mm_nt(jnp.concatenate([k, q], axis=0), k)
        amats.append(jnp.where(strict, beta[:, :c] * kq[:c] * dmask, 0.0))
        qkms.append(jnp.where(tri, kq[c:] * dmask, 0.0))
        qs.append(q)
        ks.append(k)
        vs.append(v)
        betas.append(beta)
        ecums.append(jnp.exp(cum))
        kdecs.append(jnp.exp(cum_last[:c] - cum))
        cdecs.append(jnp.exp(cum_last))
        if i % N_HEADS == N_HEADS - 1:
            tick()

    tinv = _unit_lower_inverses(amats, c, tick, nilpotent=min(SUBLANES, n_valid))
    eye = (ri == cj).astype(F32)
    sols = []
    for i in range(len(tasks)):
        rhs = jnp.concatenate([vs[i] * betas[i], ks[i] * (betas[i] * ecums[i])], axis=1)
        sols.append(rhs + _mm(tinv[i] - eye, rhs))
    tick()

    s_gdn = {bh: read_gdn(*bh) for bh in pairs}
    o_gdn = {}
    for j in range(nch):
        idx = {tasks[i][1:]: i for i in range(len(tasks)) if tasks[i][0] == j}
        lhs = {bh: jnp.concatenate([sols[i][:, D_HEAD:], qs[i] * ecums[i]], axis=0) for bh, i in idx.items()}
        both = {bh: _mm(lhs[bh], s_gdn[bh]) for bh in idx}
        tick()
        w = {bh: sols[i][:, :D_HEAD] - both[bh][:c] for bh, i in idx.items()}
        for bh, i in idx.items():
            o_gdn[bh + (j,)] = both[bh][c:] + _mm(qkms[i], w[bh])
        tick()
        upd = {bh: _mm_tn(ks[i] * kdecs[i], w[bh]) for bh, i in idx.items()}
        for bh, i in idx.items():
            s_gdn[bh] = cdecs[i] * s_gdn[bh] + upd[bh]
        tick()
    o_gdn = {bh: join([o_gdn[bh + (j,)] for j in range(nch)]) for bh in pairs}
    msq = {bh: jnp.mean(o_gdn[bh] * o_gdn[bh], axis=-1, keepdims=True) for bh in pairs}
    for b, h in pairs:
        lo = h * D_HEAD
        gate = getp(b, OFF_GG + lo, OFF_GG + lo + D_HEAD)
        o = o_gdn[b, h] * lax.rsqrt(msq[b, h] + EPS) * ngdn_ref[...]
        put_mix(b, D_GRP + lo, o * _silu(gate))
        sgdn_ref[b, h] = s_gdn[b, h]


N_MIXER_CONSTS = 10


def _mixer_kernel(p_ref, sret0_ref, sgdn0_ref, cq0_ref, cos_ref, sin_ref, *rest, bb, c, nch, n_valid, stored,
                  batch_minor, ring):
    const_refs = rest[:N_MIXER_CONSTS]
    mix_ref, sret_ref, sgdn_ref, cq_ref, xs_ref = rest[N_MIXER_CONSTS:N_MIXER_CONSTS + 5]
    rows = nch * c
    read_ret = read_gdn = None
    if ring:
        ret_buf, gdn_buf, sems = rest[N_MIXER_CONSTS + 5:]
        step, nsteps = pl.program_id(0), pl.num_programs(0)

        def copies(s):
            slot, seqs = s % ring, pl.ds(s * bb, bb)
            return (pltpu.make_async_copy(sret0_ref.at[seqs], ret_buf.at[slot], sems.at[0, slot]),
                    pltpu.make_async_copy(sgdn0_ref.at[seqs], gdn_buf.at[slot], sems.at[1, slot]))

        @pl.when(step == 0)
        def _():
            for s in range(ring - 1):
                for cp in copies(s):
                    cp.start()

        @pl.when(step + (ring - 1) < nsteps)
        def _():
            for cp in copies(step + (ring - 1)):
                cp.start()

        for cp in copies(step):
            cp.wait()
        slot = step % ring
        read_ret = lambda b, h: ret_buf[slot, b, h]
        read_gdn = lambda b, h: gdn_buf[slot, b, h]

    @pl.when(pl.program_id(1) == 0)
    def _():
        _mixer_init(sret0_ref, sgdn0_ref, cq0_ref, sret_ref, sgdn_ref, xs_ref, batch_minor, copy_states=not ring)

    def put_mix(b, lo, value):
        mix_ref[b, :, lo:lo + D_HEAD] = value.astype(mix_ref.dtype)

    def getp(b, lo, hi):
        blk = p_ref[b * stored:(b + 1) * stored, lo:hi]
        if stored < rows:
            blk = jnp.concatenate([blk, jnp.zeros((rows - stored, hi - lo), F32)], axis=0)
        return blk

    _mixer_block(getp, put_mix, cos_ref[...], sin_ref[...], const_refs, sret_ref, sgdn_ref, cq_ref, xs_ref,
                 bb=bb, c=c, nch=nch, n_valid=n_valid, batch_minor=batch_minor, read_ret=read_ret,
                 read_gdn=read_gdn)


PROJ_PANEL = MXU_WIDTH


def _mixer_block_ticks(bb, c, nch):
    levels = (c // SUBLANES).bit_length() - 1
    retention = 4 + nch + 1
    gdn_prep = bb + 1 + bb * nch
    inverse = 3 + 2 * levels
    return retention + gdn_prep + inverse + 1 + 3 * nch


def _proj_mixer_kernel(x0_ref, xa_ref, xb_ref, nw_ref, w_ref, sret0_ref, sgdn0_ref, cq0_ref, cos_ref, sin_ref,
                       *rest, c, nch, steps_per_seq, n_cast):
    const_refs = rest[:N_MIXER_CONSTS]
    cast_in = rest[N_MIXER_CONSTS:N_MIXER_CONSTS + n_cast]
    mix_ref, sret_ref, sgdn_ref, cq_ref = rest[N_MIXER_CONSTS + n_cast:N_MIXER_CONSTS + n_cast + 4]
    cast_out = rest[N_MIXER_CONSTS + n_cast + 4:N_MIXER_CONSTS + 2 * n_cast + 4]
    pja_ref, pjb_ref, xs_ref = rest[N_MIXER_CONSTS + 2 * n_cast + 4:]
    step = pl.program_id(0)
    rows = nch * c

    for src_ref, dst_ref in zip(cast_in, cast_out):
        dst_ref[...] = src_ref[...].astype(dst_ref.dtype)

    def project(x_ref, dst_ref, n_ticks):
        h = _rms(x_ref[...], nw_ref[...]).astype(BF16)
        panels = [(lo, min(lo + PROJ_PANEL, IN_PAD)) for lo in range(0, IN_PAD, PROJ_PANEL)]
        n_panels = len(panels)
        calls = [0]

        def emit():
            lo, hi = panels.pop(0)
            dst_ref[:, lo:hi] = jnp.dot(h, w_ref[:, lo:hi], preferred_element_type=F32)

        def tick():
            calls[0] += 1
            due = min(n_panels, -(-calls[0] * n_panels // n_ticks))
            while n_panels - len(panels) < due:
                emit()

        def flush():
            assert n_ticks == 1 or calls[0] == n_ticks, (calls[0], n_ticks)
            while panels:
                emit()

        return tick, flush

    @pl.when(step == 0)
    def _():
        project(x0_ref, pja_ref, 1)[1]()

    @pl.when(lax.rem(step, steps_per_seq) == 0)
    def _():
        _mixer_init(sret0_ref, sgdn0_ref, cq0_ref, sret_ref, sgdn_ref, xs_ref)

    for half, (cur_ref, x_next_ref, nxt_ref) in enumerate(((pja_ref, xa_ref, pjb_ref), (pjb_ref, xb_ref, pja_ref))):
        r0 = half * rows
        tick, flush = project(x_next_ref, nxt_ref, _mixer_block_ticks(1, c, nch))

        def put_mix(b, lo, value, r0=r0):
            mix_ref[r0:r0 + rows, lo:lo + D_HEAD] = value.astype(mix_ref.dtype)

        _mixer_block(lambda b, lo, hi, cur_ref=cur_ref: cur_ref[:, lo:hi], put_mix,
                     cos_ref[r0:r0 + rows, :], sin_ref[r0:r0 + rows, :], const_refs,
                     sret_ref, sgdn_ref, cq_ref, xs_ref, bb=1, c=c, nch=nch, n_valid=c, tick=tick)
        flush()


def _retention_decay_tables(c, n_valid):
    f32 = np.float32
    lg = np.log1p(-np.power(f32(2.0), f32(-5.0) - np.arange(N_HEADS, dtype=f32)))[:, None].astype(f32)
    idx = np.arange(c, dtype=f32)
    diff = idx[:, None] - idx[None, :]
    dint = np.where(diff[None] >= 0, np.exp(lg[:, :, None] * np.maximum(diff[None], 0)), f32(0.0)).astype(f32)
    qdec = np.exp(lg * (idx + f32(1.0))).astype(f32)
    kdec = np.where(idx[None, :] < n_valid, np.exp(lg * np.minimum(f32(n_valid) - f32(1.0) - idx, c)), f32(0.0))
    cdec = np.exp(lg * f32(n_valid)).astype(f32)
    bc = lambda t: np.ascontiguousarray(np.broadcast_to(t.astype(f32)[:, :, None], t.shape + (LANES,)))
    return dint, bc(qdec), bc(kdec), bc(cdec)


def _mixer_const_operands(c, n_valid, consts):
    cw, alog, dtb, nret, ngdn = consts
    dint, qdec, kdec, cdec = _retention_decay_tables(c, n_valid)
    idx = np.arange(c)
    tril = (idx[:, None] >= idx[None, :]).astype(np.float32)
    arrays = (jnp.asarray(tril, BF16), jnp.asarray(dint), jnp.asarray(qdec), jnp.asarray(kdec), jnp.asarray(cdec),
              cw, alog, dtb, nret, ngdn)
    assert len(arrays) == N_MIXER_CONSTS
    return arrays, [a.shape for a in arrays]


def _mixer(proj, row0, nb, length, sret0, sgdn0, cq0, rope, rope_row0, consts, *, bb, c, nch, n_valid,
           shared_init, stored=None, batch_minor=False, ring=0):
    rows = nch * c
    assert nb % bb == 0 and length % rows == 0 and rope_row0 % rows == 0
    assert not shared_init or bb == 1
    assert n_valid == c or nch == 1
    assert bb == 1 or length == rows
    const_arrays, const_shapes = _mixer_const_operands(c, n_valid, consts)
    nsteps = length // rows
    stored = rows if stored is None else stored
    assert stored == rows or (nsteps == 1 and stored >= n_valid and (bb * stored) % SUBLANES == 0)
    assert row0 % (bb * stored) == 0
    blk0 = row0 // (bb * stored)
    init_idx = (lambda b, i: (0, 0, 0, 0)) if shared_init else (lambda b, i: (b, 0, 0, 0))
    init_idx3 = (lambda b, i: (0, 0, 0)) if shared_init else (lambda b, i: (b, 0, 0))
    whole = lambda shape: pl.BlockSpec(shape, lambda b, i: (0,) * len(shape))
    rope_spec = pl.BlockSpec((rows, LANES), lambda b, i: (rope_row0 // rows + i, 0))
    state_shape = (bb, N_HEADS, D_HEAD, D_HEAD)
    tail = CONV_GDN - 1
    assert not (batch_minor and shared_init)
    cq_shape, cq_block = (nb, tail, GDN_QKV), (bb, tail, GDN_QKV)
    cq_idx = lambda b, i: (b, 0, 0)
    if batch_minor:
        cq_shape, cq_block = (tail, nb, GDN_QKV), (tail, bb, GDN_QKV)
        cq_idx = init_idx3 = lambda b, i: (0, b, 0)
    kern = functools.partial(_mixer_kernel, bb=bb, c=c, nch=nch, n_valid=n_valid, stored=stored,
                             batch_minor=batch_minor, ring=ring)
    state_spec = pl.BlockSpec(state_shape, init_idx)
    ring_scratch = []
    if ring:
        assert nsteps == 1 and not shared_init and nb // bb >= ring - 1
        state_spec = pl.BlockSpec(memory_space=pl.ANY)
        ring_scratch = [pltpu.VMEM((ring,) + state_shape, F32), pltpu.VMEM((ring,) + state_shape, F32),
                        pltpu.SemaphoreType.DMA((2, ring))]
    return pl.pallas_call(
        kern,
        out_shape=(
            jax.ShapeDtypeStruct((nb, length, D_MODEL), BF16),
            jax.ShapeDtypeStruct((nb, N_HEADS, D_HEAD, D_HEAD), F32),
            jax.ShapeDtypeStruct((nb, N_HEADS, D_HEAD, D_HEAD), F32),
            jax.ShapeDtypeStruct(cq_shape, F32),
        ),
        grid=(nb // bb, nsteps),
        in_specs=[
            pl.BlockSpec((bb * stored, IN_PAD), lambda b, i: (blk0 + b * nsteps + i, 0)),
            state_spec,
            state_spec,
            pl.BlockSpec(cq_block, init_idx3),
            rope_spec,
            rope_spec,
        ] + [whole(shape) for shape in const_shapes],
        out_specs=(
            pl.BlockSpec((bb, rows, D_MODEL), lambda b, i: (b, i, 0)),
            pl.BlockSpec(state_shape, lambda b, i: (b, 0, 0, 0)),
            pl.BlockSpec(state_shape, lambda b, i: (b, 0, 0, 0)),
            pl.BlockSpec(cq_block, cq_idx),
        ),
        scratch_shapes=[pltpu.VMEM((bb, XS_TOP + rows, GDN_QKV), F32)] + ring_scratch,
        compiler_params=pltpu.CompilerParams(dimension_semantics=("arbitrary", "arbitrary"),
                                             vmem_limit_bytes=VMEM_LIMIT),
        name="mixer",
    )(proj, sret0, sgdn0, cq0, rope[0], rope[1], *const_arrays)


def _row_slab(nrows, nsteps):
    for hold in (1, 2, 4, 8):
        slabs = nsteps // hold
        if nsteps % hold == 0 and nrows % slabs == 0 and (nrows // slabs) % (2 * SUBLANES) == 0:
            return nrows // slabs, hold
    raise ValueError((nrows, nsteps))


def _proj_mixer(x, norm_w, w_bf, nseq, sret0, sgdn0, cq0, rope, consts, to_bf16, *, c, nch):
    rows = nch * c
    total = x.shape[0]
    length = total // nseq
    assert total % nseq == 0 and length % (2 * rows) == 0
    nblk = total // rows
    steps_per_seq = length // (2 * rows)
    const_arrays, const_shapes = _mixer_const_operands(c, c, consts)
    whole = lambda shape, **kw: pl.BlockSpec(shape, lambda s: (0,) * len(shape), **kw)
    rope_spec = pl.BlockSpec((2 * rows, LANES), lambda s: (lax.rem(s, steps_per_seq), 0))
    state_shape = (1, N_HEADS, D_HEAD, D_HEAD)
    tail = CONV_GDN - 1
    nsteps = nblk // 2
    slabs = [_row_slab(w.shape[0], nsteps) for w in to_bf16]
    cast_specs = [pl.BlockSpec((r, w.shape[1]), lambda s, hold=hold: (s // hold, 0))
                  for w, (r, hold) in zip(to_bf16, slabs)]
    kern = functools.partial(_proj_mixer_kernel, c=c, nch=nch, steps_per_seq=steps_per_seq, n_cast=len(to_bf16))
    return pl.pallas_call(
        kern,
        out_shape=(
            jax.ShapeDtypeStruct((total, D_MODEL), BF16),
            jax.ShapeDtypeStruct((nseq, N_HEADS, D_HEAD, D_HEAD), F32),
            jax.ShapeDtypeStruct((nseq, N_HEADS, D_HEAD, D_HEAD), F32),
            jax.ShapeDtypeStruct((nseq, tail, GDN_QKV), F32),
        ) + tuple(jax.ShapeDtypeStruct(w.shape, BF16) for w in to_bf16),
        grid=(nsteps,),
        in_specs=[
            pl.BlockSpec((rows, D_MODEL), lambda s: (0, 0), pipeline_mode=pl.Buffered(1)),
            pl.BlockSpec((rows, D_MODEL), lambda s: (2 * s + 1, 0)),
            pl.BlockSpec((rows, D_MODEL), lambda s: (jnp.minimum(2 * s + 2, nblk - 1), 0)),
            whole((1, D_MODEL)),
            whole((D_MODEL, IN_PAD), pipeline_mode=pl.Buffered(1)),
            whole(state_shape),
            whole(state_shape),
            whole((1, tail, GDN_QKV)),
            rope_spec,
            rope_spec,
        ] + [whole(shape) for shape in const_shapes] + cast_specs,
        out_specs=(
            pl.BlockSpec((2 * rows, D_MODEL), lambda s: (s, 0)),
            pl.BlockSpec(state_shape, lambda s: (s // steps_per_seq, 0, 0, 0)),
            pl.BlockSpec(state_shape, lambda s: (s // steps_per_seq, 0, 0, 0)),
            pl.BlockSpec((1, tail, GDN_QKV), lambda s: (s // steps_per_seq, 0, 0)),
        ) + tuple(cast_specs),
        scratch_shapes=[pltpu.VMEM((rows, IN_PAD), F32), pltpu.VMEM((rows, IN_PAD), F32),
                        pltpu.VMEM((1, XS_TOP + rows, GDN_QKV), F32)],
        compiler_params=pltpu.CompilerParams(dimension_semantics=("arbitrary",),
                                             vmem_limit_bytes=VMEM_LIMIT),
        name="proj_mixer",
    )(x, x, x, norm_w, w_bf, sret0, sgdn0, cq0, rope[0], rope[1], *const_arrays, *to_bf16)


FFN_COL_CHUNK = D_FF // 11


def _ffn_kernel(x_ref, mix_ref, *rest, tm, stride, prefix):
    if prefix:
        (xm_ref, mixm_ref, wout_ref, nffn_ref, wup_ref, cw_ref, wdn_ref, nfin_ref,
         y_ref, tail_ref, full_ref, lead_ref) = rest
    else:
        tail0_ref, wout_ref, nffn_ref, wup_ref, cw_ref, wdn_ref, nfin_ref, y_ref, tail_ref, full_ref = rest
    t = pl.program_id(1)
    carry = (CONV_FFN - 1) * stride
    base = _round_up(carry, SUBLANES)

    def up_project(x, mix):
        x1 = x + jnp.dot(mix, wout_ref[...], preferred_element_type=F32)
        h = _rms(x1, nffn_ref[...]).astype(BF16)
        return x1, jnp.dot(h, wup_ref[...], preferred_element_type=F32)

    if prefix:
        @pl.when((pl.program_id(0) == 0) & (t == 0))
        def _():
            um = up_project(xm_ref[...], mixm_ref[...])[1]
            lead_ref[...] = um[um.shape[0] - carry:, :]

    @pl.when(t == 0)
    def _():
        full_ref[base - carry:base, :] = lead_ref[...] if prefix else tail0_ref[0]

    x1, up = up_project(x_ref[...], mix_ref[...])
    full_ref[base:base + tm, :] = up

    def conv_cols(lo):
        acc = full_ref[base - carry:base - carry + tm, lo:lo + FFN_COL_CHUNK] * cw_ref[0:1, lo:lo + FFN_COL_CHUNK]
        for i in range(1, CONV_FFN):
            r0 = base - carry + i * stride
            acc = acc + full_ref[r0:r0 + tm, lo:lo + FFN_COL_CHUNK] * cw_ref[i:i + 1, lo:lo + FFN_COL_CHUNK]
        return acc

    x2 = x1
    for j in range(D_FF // FFN_COL_CHUNK):
        lo = j * FFN_COL_CHUNK
        act = (_silu(conv_cols(lo)) * conv_cols(D_FF + lo)).astype(BF16)
        x2 = x2 + jnp.dot(act, wdn_ref[lo:lo + FFN_COL_CHUNK, :], preferred_element_type=F32)
    y_ref[...] = _rms(x2, nfin_ref[...])

    new_tail = full_ref[base + tm - carry:base + tm, :]
    full_ref[base - carry:base, :] = new_tail
    tail_ref[0] = new_tail


def _ffn(x, mix, lead, weights, *, nseq, tm, stride):
    wout, nffn, wup, cw, wdn, nfin = weights
    rows = x.shape[0]
    assert rows % (nseq * tm) == 0
    nt = rows // (nseq * tm)
    carry = (CONV_FFN - 1) * stride
    base = _round_up(carry, SUBLANES)
    assert tm >= carry
    resident = lambda shape: pl.BlockSpec(shape, lambda b, t: (0, 0), pipeline_mode=pl.Buffered(1))
    small = lambda shape: pl.BlockSpec(shape, lambda b, t: (0, 0))
    prefix = isinstance(lead, tuple)
    if prefix:
        assert stride == 1 and all(a.shape[0] >= carry for a in lead)
        lead_specs = [small(a.shape) for a in lead]
        lead_scratch = [pltpu.VMEM((carry, 2 * D_FF), F32)]
    else:
        lead = (lead,)
        lead_specs = [pl.BlockSpec((1, carry, 2 * D_FF), lambda b, t: (b, 0, 0))]
        lead_scratch = []
    kern = functools.partial(_ffn_kernel, tm=tm, stride=stride, prefix=prefix)
    return pl.pallas_call(
        kern,
        out_shape=(
            jax.ShapeDtypeStruct((rows, D_MODEL), F32),
            jax.ShapeDtypeStruct((nseq, carry, 2 * D_FF), F32),
        ),
        grid=(nseq, nt),
        in_specs=[
            pl.BlockSpec((tm, D_MODEL), lambda b, t: (b * nt + t, 0)),
            pl.BlockSpec((tm, D_MODEL), lambda b, t: (b * nt + t, 0)),
        ] + lead_specs + [
            resident((D_MODEL, D_MODEL)),
            small((1, D_MODEL)),
            resident((D_MODEL, 2 * D_FF)),
            small((CONV_FFN, 2 * D_FF)),
            resident((D_FF, D_MODEL)),
            small((1, D_MODEL)),
        ],
        out_specs=(
            pl.BlockSpec((tm, D_MODEL), lambda b, t: (b * nt + t, 0)),
            pl.BlockSpec((1, carry, 2 * D_FF), lambda b, t: (b, 0, 0)),
        ),
        scratch_shapes=[pltpu.VMEM((base + tm, 2 * D_FF), F32)] + lead_scratch,
        compiler_params=pltpu.CompilerParams(dimension_semantics=("arbitrary", "arbitrary"),
                                             vmem_limit_bytes=VMEM_LIMIT),
        name="out_ffn",
    )(x, mix, *lead, wout, nffn, wup, cw, wdn, nfin)


def kernel(x_prompt, x_sample, state_ret, state_gdn, state_conv_qkv, state_ffn_conv, meta_tokens, norm_mix,
           w_in, conv_gdn, gdn_a_log, gdn_dt_bias, norm_ret, norm_gdn, w_out, norm_ffn, w_up, conv_ffn,
           w_down, norm_final):
    depth = w_in.shape[0]
    assert depth == 1
    nbp, seq, _ = x_prompt.shape
    nbs, dec_seq, _ = x_sample.shape
    assert dec_seq <= SAMPLE_PAD and nbs % SAMPLE_GROUP == 0
    assert seq % (2 * PROMPT_CHUNK * PROMPT_CHUNKS_PER_STEP) == 0
    layer = 0

    row = lambda v: v.reshape(1, -1).astype(F32)
    pad_ba = lambda v: jnp.pad(v.astype(F32), (N_HEADS, LANES - 2 * N_HEADS)).reshape(1, LANES)
    mixer_consts = (conv_gdn[layer], pad_ba(gdn_a_log[layer]), pad_ba(gdn_dt_bias[layer]),
                    norm_ret[layer].reshape(N_HEADS, D_HEAD), row(norm_gdn[layer]))
    nmix = row(norm_mix[layer])

    assert seq % N_META == 0 and (seq + N_META) % SAMPLE_PAD == 0
    rope_meta_row0, rope_sample_row0 = seq, seq + N_META
    rope_segments = [(0, N_META), (rope_meta_row0, 0), (rope_sample_row0, PAST_LEN)]

    small_rows = jnp.concatenate([x_sample.reshape(nbs * dec_seq, D_MODEL), meta_tokens.astype(F32)], axis=0)
    proj_small, w_in_bf, *rope = _proj(small_rows, nmix, w_in[layer], seq + N_META + SAMPLE_PAD, rope_segments)
    meta_row0 = nbs * dec_seq

    zero_state = jnp.zeros((1, N_HEADS, D_HEAD, D_HEAD), F32)
    zero_cq = jnp.zeros((1, CONV_GDN - 1, GDN_QKV), F32)
    mix_m, sret_m, sgdn_m, cq_m = _mixer(proj_small, meta_row0, 1, N_META, zero_state, zero_state, zero_cq, rope,
                                         rope_meta_row0, mixer_consts, bb=1, c=N_META, nch=1, n_valid=N_META,
                                         shared_init=True)

    xp = x_prompt.reshape(nbp * seq, D_MODEL)
    mix_p, sret_p, sgdn_p, cq_p, w_out_bf, w_up_bf, w_down_bf = _proj_mixer(
        xp, nmix, w_in_bf, nbp, sret_m, sgdn_m, cq_m, rope, mixer_consts,
        (w_out[layer], w_up[layer], w_down[layer]), c=PROMPT_CHUNK, nch=PROMPT_CHUNKS_PER_STEP)
    ffn_weights = (w_out_bf, row(norm_ffn[layer]), w_up_bf, conv_ffn[layer], w_down_bf, row(norm_final))

    meta_lead = (meta_tokens.astype(F32), mix_m.reshape(N_META, D_MODEL))
    y_p, cf_p = _ffn(xp, mix_p, meta_lead, ffn_weights, nseq=nbp, tm=512, stride=1)
    y_prompt = y_p.reshape(nbp, seq, D_MODEL)

    mix_s, sret_s, sgdn_s, cq_s = _mixer(proj_small, 0, nbs, SAMPLE_PAD, state_ret[layer], state_gdn[layer],
                                         jnp.swapaxes(state_conv_qkv[layer], 0, 1), rope, rope_sample_row0,
                                         mixer_consts, bb=SAMPLE_SEQS_PER_STEP, c=SAMPLE_PAD, nch=1,
                                         n_valid=dec_seq, shared_init=False, stored=dec_seq, batch_minor=True,
                                         ring=SAMPLE_STATE_SLOTS)
    cq_s = jnp.swapaxes(cq_s, 0, 1)
    ng = nbs // SAMPLE_GROUP
    to_tmajor = lambda a: a.reshape(ng, SAMPLE_GROUP, a.shape[1], a.shape[2]).transpose(0, 2, 1, 3)
    xs_t = to_tmajor(x_sample).reshape(nbs * dec_seq, D_MODEL)
    mix_t = to_tmajor(mix_s[:, :dec_seq]).reshape(nbs * dec_seq, D_MODEL)
    cf0_t = to_tmajor(state_ffn_conv[layer]).reshape(ng, (CONV_FFN - 1) * SAMPLE_GROUP, 2 * D_FF)
    y_s_t, cf_s_t = _ffn(xs_t, mix_t, cf0_t, ffn_weights, nseq=ng, tm=dec_seq * SAMPLE_GROUP,
                         stride=SAMPLE_GROUP)
    y_sample = y_s_t.reshape(ng, dec_seq, SAMPLE_GROUP, D_MODEL).transpose(0, 2, 1, 3).reshape(nbs, dec_seq, D_MODEL)
    cf_s = cf_s_t.reshape(ng, CONV_FFN - 1, SAMPLE_GROUP, 2 * D_FF).transpose(0, 2, 1, 3).reshape(
        nbs, CONV_FFN - 1, 2 * D_FF)

    return (y_prompt, y_sample, sret_p[None], sgdn_p[None], cq_p[None], cf_p[None],
            sret_s[None], sgdn_s[None], cq_s[None], cf_s[None])
```

```python
import functools

import jax
import numpy as np
import jax.numpy as jnp
from jax import lax
from jax.experimental import pallas as pl
from jax.experimental.pallas import tpu as pltpu

F32 = jnp.float32
BF16 = jnp.bfloat16

D_MODEL = 1024
N_META = 16
PAST_LEN = 16384
N_HEADS = 4
D_HEAD = 128
D_GRP = N_HEADS * D_HEAD
GDN_QKV = 3 * D_GRP
CONV_GDN = 4
CONV_FFN = 3
D_FF = 2816
ROPE_THETA = 10000.0
EPS = 1e-6

OFF_RQ, OFF_RK, OFF_RV, OFF_RG = 0, D_GRP, 2 * D_GRP, 3 * D_GRP
OFF_QKV = 4 * D_GRP
OFF_GG = OFF_QKV + GDN_QKV
OFF_BA = OFF_GG + D_GRP
IN_WIDTH = OFF_BA + 2 * N_HEADS
LANES = 128
SUBLANES = 8
MXU_WIDTH = 256
V7X_VMEM_BYTES = 64 * 1024 * 1024
IN_PAD = OFF_BA + LANES

PROMPT_CHUNK = 128
PROMPT_CHUNKS_PER_STEP = 2
SAMPLE_PAD = 8
SAMPLE_SEQS_PER_STEP = 16
SAMPLE_STATE_SLOTS = 3
SAMPLE_GROUP = 64
VMEM_LIMIT = V7X_VMEM_BYTES * 7 // 8


def _round_up(n, m):
    return (n + m - 1) // m * m


def _mm(a, b):
    return jnp.dot(a.astype(BF16), b.astype(BF16), preferred_element_type=F32)


def _mm_nt(a, b):
    return lax.dot_general(a.astype(BF16), b.astype(BF16), (((1,), (1,)), ((), ())),
                           preferred_element_type=F32)


def _mm_tn(a, b):
    return lax.dot_general(a.astype(BF16), b.astype(BF16), (((0,), (0,)), ((), ())),
                           preferred_element_type=F32)


def _split3(x):
    p0 = x.astype(BF16)
    r = x - p0.astype(F32)
    p1 = r.astype(BF16)
    p2 = (r - p1.astype(F32)).astype(BF16)
    return p0, p1, p2


def _shift_rows(x, lead):
    k = lead.shape[0]
    row = lax.broadcasted_iota(jnp.int32, x.shape, 0)
    out = pltpu.roll(x, k, 0)
    for i in range(k):
        out = jnp.where(row == i, lead[i:i + 1, :], out)
    return out


def _silu(x):
    return x * jax.nn.sigmoid(x)


def _rms(x, w):
    return x * lax.rsqrt(jnp.mean(x * x, axis=-1, keepdims=True) + EPS) * w


def _rope_kernel(invf_ref, cos_ref, sin_ref, *, segments):
    shape = cos_ref.shape
    r = lax.broadcasted_iota(jnp.int32, shape, 0)
    pos = r + (segments[0][1] - segments[0][0])
    for row0, pos0 in segments[1:]:
        pos = jnp.where(r >= row0, r + (pos0 - row0), pos)
    ang = pos.astype(F32) * invf_ref[...]
    lane = lax.broadcasted_iota(jnp.int32, shape, 1)
    sin = jnp.sin(ang)
    cos_ref[...] = jnp.cos(ang)
    sin_ref[...] = jnp.where(lane < D_HEAD // 2, -sin, sin)


W_IN_COLS = 11 * LANES


def _proj_kernel(*refs, n_x, segments):
    x_refs, (nw_ref, wt_ref, invf_ref, o_ref, wbf_ref, cos_ref, sin_ref, h_ref) = refs[:n_x], refs[n_x:]
    first = pl.program_id(0) * W_IN_COLS

    @pl.when(first == 0)
    def _():
        row0 = 0
        for x_ref in x_refs:
            h_ref[row0:row0 + x_ref.shape[0], :] = _rms(x_ref[...], nw_ref[...]).astype(BF16)
            row0 += x_ref.shape[0]
        _rope_kernel(invf_ref, cos_ref, sin_ref, segments=segments)

    r = lax.broadcasted_iota(jnp.int32, wt_ref.shape, 0)
    w_blk = jnp.where(first + r < IN_WIDTH, wt_ref[...], 0.0).T.astype(BF16)
    wbf_ref[...] = w_blk
    o_ref[...] = jnp.dot(h_ref[...], w_blk, preferred_element_type=F32)


def _proj(xs, norm_w, w, rope_rows, rope_segments):
    rows = sum(x.shape[0] for x in xs)
    assert all(x.shape[0] % (2 * SUBLANES) == 0 for x in xs)
    half = D_HEAD // 2
    inv_freq = ROPE_THETA ** (-jnp.arange(half, dtype=F32) / half)
    invf2 = jnp.concatenate([inv_freq, inv_freq]).reshape(1, LANES)
    table = jax.ShapeDtypeStruct((rope_rows, LANES), F32)
    table_spec = pl.BlockSpec((rope_rows, LANES), lambda j: (0, 0))
    return pl.pallas_call(
        functools.partial(_proj_kernel, n_x=len(xs), segments=tuple(rope_segments)),
        out_shape=(jax.ShapeDtypeStruct((rows, IN_PAD), F32), jax.ShapeDtypeStruct((D_MODEL, IN_PAD), BF16),
                   table, table),
        grid=(IN_PAD // W_IN_COLS,),
        in_specs=[pl.BlockSpec(x.shape, lambda j: (0, 0)) for x in xs] + [
            pl.BlockSpec((1, D_MODEL), lambda j: (0, 0)),
            pl.BlockSpec((W_IN_COLS, D_MODEL), lambda j: (j, 0)),
            pl.BlockSpec((1, LANES), lambda j: (0, 0)),
        ],
        out_specs=(pl.BlockSpec((rows, W_IN_COLS), lambda j: (0, j)),
                   pl.BlockSpec((D_MODEL, W_IN_COLS), lambda j: (0, j)), table_spec, table_spec),
        scratch_shapes=[pltpu.VMEM((rows, D_MODEL), BF16)],
        compiler_params=pltpu.CompilerParams(dimension_semantics=("arbitrary",),
                                             vmem_limit_bytes=VMEM_LIMIT),
        name="in_proj",
    )(*xs, norm_w, w.T, invf2)


def _unit_lower_inverses(mats, c, tick=lambda: None, nilpotent=SUBLANES):
    ri = lax.broadcasted_iota(jnp.int32, (c, c), 0)
    ci = lax.broadcasted_iota(jnp.int32, (c, c), 1)
    eye = (ri == ci).astype(F32)
    diag_blk = (ri // SUBLANES) == (ci // SUBLANES)
    ad = [jnp.where(diag_blk, a, 0.0) for a in mats]
    assert 1 <= nilpotent <= SUBLANES
    n_factors = max(0, (nilpotent - 1).bit_length() - 1)
    t = [eye - x for x in ad]
    power = ad
    for i in range(2):
        if i < n_factors:
            power = [_mm(x, x) for x in power]
            t = [x + _mm(x, s) for x, s in zip(t, power)]
        tick()
    tick()
    s = SUBLANES
    while s < c:
        level = ((ri // (2 * s)) == (ci // (2 * s))) & ((ri // s) != (ci // s))
        off = [jnp.where(level, a, 0.0) for a in mats]
        lt = [_mm(o, x) for o, x in zip(off, t)]
        tick()
        t = [x - _mm(x, y) for x, y in zip(t, lt)]
        tick()
        s *= 2
    return t


XS_TOP = SUBLANES


def _mixer_init(sret0_ref, sgdn0_ref, cq0_ref, sret_ref, sgdn_ref, xs_ref, batch_minor=False, copy_states=True):
    tail = CONV_GDN - 1
    if copy_states:
        sret_ref[...] = sret0_ref[...]
        sgdn_ref[...] = sgdn0_ref[...]
    if batch_minor:
        for b in range(xs_ref.shape[0]):
            xs_ref[b, XS_TOP - tail:XS_TOP, :] = cq0_ref[:, b, :]
    else:
        xs_ref[:, XS_TOP - tail:XS_TOP, :] = cq0_ref[...]


def _mixer_block(getp, put_mix, cos2, sin2, const_refs, sret_ref, sgdn_ref, cq_ref, xs_ref, *,
                 bb, c, nch, n_valid, tick=lambda: None, batch_minor=False, read_ret=None, read_gdn=None):
    (tril_ref, dint_ref, qdec_ref, kdec_ref, cdec_ref, cw_ref, alog_ref, dtb_ref, nret_ref, ngdn_ref) = const_refs
    rows = nch * c
    tail = CONV_GDN - 1
    top = XS_TOP
    ri = lax.broadcasted_iota(jnp.int32, (c, c), 0)
    cj = lax.broadcasted_iota(jnp.int32, (c, c), 1)
    tri = ri >= cj
    strict = ri > cj
    tril_bf = tril_ref[...]
    scale = D_HEAD ** -0.5
    heads = range(N_HEADS)
    seqs = range(bb)
    chunks = [slice(j * c, (j + 1) * c) for j in range(nch)]
    join = lambda parts: parts[0] if len(parts) == 1 else jnp.concatenate(parts, axis=0)

    ret = {}
    for b in seqs:
        for h in heads:
            lo = h * D_HEAD
            q = getp(b, OFF_RQ + lo, OFF_RQ + lo + D_HEAD)
            k = getp(b, OFF_RK + lo, OFF_RK + lo + D_HEAD)
            v = getp(b, OFF_RV + lo, OFF_RV + lo + D_HEAD)
            qr = q * cos2 + pltpu.roll(q, D_HEAD // 2, 1) * sin2
            kr = (k * cos2 + pltpu.roll(k, D_HEAD // 2, 1) * sin2) * scale
            ret[b, h] = (qr, kr, v)
    tick()
    rtasks = [(b, h, j) for j in range(nch) for b in seqs for h in heads]
    scores = {t: _mm_nt(ret[t[0], t[1]][0][chunks[t[2]]], ret[t[0], t[1]][1][chunks[t[2]]]) * dint_ref[t[1]]
              for t in rtasks}
    tick()
    intra = {t: _mm(scores[t], ret[t[0], t[1]][2][chunks[t[2]]]) for t in rtasks}
    tick()
    kv = {t: _mm_tn(ret[t[0], t[1]][1][chunks[t[2]]] * kdec_ref[t[1]], ret[t[0], t[1]][2][chunks[t[2]]])
          for t in rtasks}
    tick()
    pairs = [(b, h) for b in seqs for h in heads]
    read_ret = read_ret or (lambda b, h: sret_ref[b, h])
    read_gdn = read_gdn or (lambda b, h: sgdn_ref[b, h])
    s_ret = {bh: read_ret(*bh) for bh in pairs}
    o_ret = {}
    for j in range(nch):
        for b, h in pairs:
            o_ret[b, h, j] = intra[b, h, j] + _mm(ret[b, h][0][chunks[j]] * qdec_ref[h], s_ret[b, h])
        for b, h in pairs:
            s_ret[b, h] = cdec_ref[h] * s_ret[b, h] + kv[b, h, j]
        tick()
    o_ret = {bh: join([o_ret[bh[0], bh[1], j] for j in range(nch)]) for bh in pairs}
    mu = {bh: jnp.mean(o_ret[bh], axis=-1, keepdims=True) for bh in pairs}
    cen = {bh: o_ret[bh] - mu[bh] for bh in pairs}
    var = {bh: jnp.mean(cen[bh] * cen[bh], axis=-1, keepdims=True) for bh in pairs}
    for b, h in pairs:
        lo = h * D_HEAD
        gate = getp(b, OFF_RG + lo, OFF_RG + lo + D_HEAD)
        o = cen[b, h] * lax.rsqrt(var[b, h] + EPS) * nret_ref[h:h + 1, :]
        put_mix(b, lo, o * _silu(gate))
        sret_ref[b, h] = s_ret[b, h]
    tick()

    qkvs, beta_alls, cum_cols, cum_rows, cum_tots = {}, {}, {}, {}, {}
    for b in seqs:
        assert CONV_GDN == 4
        w0, w1, w2, w3 = (cw_ref[i:i + 1, :] for i in range(CONV_GDN))
        u0 = getp(b, OFF_QKV, OFF_QKV + GDN_QKV)
        xs_ref[b, top:top + rows, :] = u0
        u1 = _shift_rows(u0, xs_ref[b, top - 1:top, :])
        p = u0 * w1 + u1 * w0
        p_lead = xs_ref[b, top - 2:top, :] * w1 + xs_ref[b, top - 3:top - 1, :] * w0
        conv = u0 * w3 + u1 * w2 + _shift_rows(p, p_lead)
        last = n_valid if nch == 1 else rows
        new_tail = xs_ref[b, top + last - tail:top + last, :]
        xs_ref[b, top - tail:top, :] = new_tail
        if batch_minor:
            cq_ref[:, b, :] = new_tail
        else:
            cq_ref[b] = new_tail
        qkvs[b] = _silu(conv)
        tick()

        ba = getp(b, OFF_BA, OFF_BA + LANES)
        beta_all = jax.nn.sigmoid(ba)
        z = ba + dtb_ref[...]
        softplus = jnp.maximum(z, 0.0) + jnp.log1p(jnp.exp(-jnp.abs(z)))
        g_all = -jnp.exp(alog_ref[...]) * softplus
        if n_valid < c:
            row = lax.broadcasted_iota(jnp.int32, (rows, LANES), 0)
            rowmask = (row < n_valid).astype(F32)
            beta_all = beta_all * rowmask
            g_all = g_all * rowmask
        beta_alls[b] = beta_all
        for j, rs in enumerate(chunks):
            parts = _split3(g_all[rs])
            cum_cols[b, j] = sum(jnp.dot(tril_bf, g, preferred_element_type=F32) for g in parts)
            cum_tots[b, j] = jnp.broadcast_to(cum_cols[b, j][c - 1:c, :], (D_HEAD, LANES))
            cum_rows[b, j] = cum_cols[b, j].T
    tick()

    tasks = [(j, b, h) for j in range(nch) for b in seqs for h in heads]
    qs, ks, vs, betas, ecums, kdecs, cdecs, amats, qkms = [], [], [], [], [], [], [], [], []
    raw = [(qkvs[b][chunks[j], h * D_HEAD:(h + 1) * D_HEAD],
            qkvs[b][chunks[j], D_GRP + h * D_HEAD:D_GRP + (h + 1) * D_HEAD]) for j, b, h in tasks]
    sumsq = [(jnp.sum(q * q, axis=-1, keepdims=True), jnp.sum(k * k, axis=-1, keepdims=True)) for q, k in raw]
    for i, (j, b, h) in enumerate(tasks):
        rs = chunks[j]
        lo = h * D_HEAD
        v = qkvs[b][rs, 2 * D_GRP + lo:2 * D_GRP + lo + D_HEAD]
        q = raw[i][0] * lax.rsqrt(sumsq[i][0] + EPS) * scale
        k = raw[i][1] * lax.rsqrt(sumsq[i][1] + EPS)
        beta = jnp.broadcast_to(beta_alls[b][rs, h:h + 1], (c, LANES))
        cum = jnp.broadcast_to(cum_cols[b, j][:, N_HEADS + h:N_HEADS + h + 1], (c, LANES))
        cum_row = cum_rows[b, j][N_HEADS + h:N_HEADS + h + 1, :]
        dmask = jnp.exp(jnp.where(tri, cum[:, :c] - cum_row, -jnp.inf))
        cum_last = jnp.broadcast_to(cum_tots[b, j][:, N_HEADS + h:N_HEADS + h + 1], (D_HEAD, LANES))
        kq = _mm_nt(jnp.concatenate([k, q], axis=0), k)
        amats.append(jnp.where(strict, beta[:, :c] * kq[:c] * dmask, 0.0))
        qkms.append(jnp.where(tri, kq[c:] * dmask, 0.0))
        qs.append(q)
        ks.append(k)
        vs.append(v)
        betas.append(beta)
        ecums.append(jnp.exp(cum))
        kdecs.append(jnp.exp(cum_last[:c] - cum))
        cdecs.append(jnp.exp(cum_last))
        if i % N_HEADS == N_HEADS - 1:
            tick()

    tinv = _unit_lower_inverses(amats, c, tick, nilpotent=min(SUBLANES, n_valid))
    eye = (ri == cj).astype(F32)
    sols = []
    for i in range(len(tasks)):
        rhs = jnp.concatenate([vs[i] * betas[i], ks[i] * (betas[i] * ecums[i])], axis=1)
        sols.append(rhs + _mm(tinv[i] - eye, rhs))
    tick()

    s_gdn = {bh: read_gdn(*bh) for bh in pairs}
    o_gdn = {}
    for j in range(nch):
        idx = {tasks[i][1:]: i for i in range(len(tasks)) if tasks[i][0] == j}
        lhs = {bh: jnp.concatenate([sols[i][:, D_HEAD:], qs[i] * ecums[i]], axis=0) for bh, i in idx.items()}
        both = {bh: _mm(lhs[bh], s_gdn[bh]) for bh in idx}
        tick()
        w = {bh: sols[i][:, :D_HEAD] - both[bh][:c] for bh, i in idx.items()}
        for bh, i in idx.items():
            o_gdn[bh + (j,)] = both[bh][c:] + _mm(qkms[i], w[bh])
        tick()
        upd = {bh: _mm_tn(ks[i] * kdecs[i], w[bh]) for bh, i in idx.items()}
        for bh, i in idx.items():
            s_gdn[bh] = cdecs[i] * s_gdn[bh] + upd[bh]
        tick()
    o_gdn = {bh: join([o_gdn[bh + (j,)] for j in range(nch)]) for bh in pairs}
    msq = {bh: jnp.mean(o_gdn[bh] * o_gdn[bh], axis=-1, keepdims=True) for bh in pairs}
    for b, h in pairs:
        lo = h * D_HEAD
        gate = getp(b, OFF_GG + lo, OFF_GG + lo + D_HEAD)
        o = o_gdn[b, h] * lax.rsqrt(msq[b, h] + EPS) * ngdn_ref[...]
        put_mix(b, D_GRP + lo, o * _silu(gate))
        sgdn_ref[b, h] = s_gdn[b, h]


N_MIXER_CONSTS = 10


def _mixer_kernel(p_ref, sret0_ref, sgdn0_ref, cq0_ref, cos_ref, sin_ref, *rest, bb, c, nch, n_valid, stored,
                  batch_minor, ring):
    const_refs = rest[:N_MIXER_CONSTS]
    mix_ref, sret_ref, sgdn_ref, cq_ref, xs_ref = rest[N_MIXER_CONSTS:N_MIXER_CONSTS + 5]
    rows = nch * c
    read_ret = read_gdn = None
    if ring:
        ret_buf, gdn_buf, sems = rest[N_MIXER_CONSTS + 5:]
        step, nsteps = pl.program_id(0), pl.num_programs(0)

        def copies(s):
            slot, seqs = s % ring, pl.ds(s * bb, bb)
            return (pltpu.make_async_copy(sret0_ref.at[seqs], ret_buf.at[slot], sems.at[0, slot]),
                    pltpu.make_async_copy(sgdn0_ref.at[seqs], gdn_buf.at[slot], sems.at[1, slot]))

        @pl.when(step == 0)
        def _():
            for s in range(ring - 1):
                for cp in copies(s):
                    cp.start()

        @pl.when(step + (ring - 1) < nsteps)
        def _():
            for cp in copies(step + (ring - 1)):
                cp.start()

        for cp in copies(step):
            cp.wait()
        slot = step % ring
        read_ret = lambda b, h: ret_buf[slot, b, h]
        read_gdn = lambda b, h: gdn_buf[slot, b, h]

    @pl.when(pl.program_id(1) == 0)
    def _():
        _mixer_init(sret0_ref, sgdn0_ref, cq0_ref, sret_ref, sgdn_ref, xs_ref, batch_minor, copy_states=not ring)

    def put_mix(b, lo, value):
        mix_ref[b, :, lo:lo + D_HEAD] = value.astype(mix_ref.dtype)

    def getp(b, lo, hi):
        blk = p_ref[b * stored:(b + 1) * stored, lo:hi]
        if stored < rows:
            blk = jnp.concatenate([blk, jnp.zeros((rows - stored, hi - lo), F32)], axis=0)
        return blk

    _mixer_block(getp, put_mix, cos_ref[...], sin_ref[...], const_refs, sret_ref, sgdn_ref, cq_ref, xs_ref,
                 bb=bb, c=c, nch=nch, n_valid=n_valid, batch_minor=batch_minor, read_ret=read_ret,
                 read_gdn=read_gdn)


PROJ_PANEL = MXU_WIDTH


def _mixer_block_ticks(bb, c, nch):
    levels = (c // SUBLANES).bit_length() - 1
    retention = 4 + nch + 1
    gdn_prep = bb + 1 + bb * nch
    inverse = 3 + 2 * levels
    return retention + gdn_prep + inverse + 1 + 3 * nch


def _proj_mixer_kernel(x0_ref, xa_ref, xb_ref, nw_ref, w_ref, sret0_ref, sgdn0_ref, cq0_ref, cos_ref, sin_ref,
                       *rest, c, nch, steps_per_seq, n_cast):
    const_refs = rest[:N_MIXER_CONSTS]
    cast_in = rest[N_MIXER_CONSTS:N_MIXER_CONSTS + n_cast]
    mix_ref, sret_ref, sgdn_ref, cq_ref = rest[N_MIXER_CONSTS + n_cast:N_MIXER_CONSTS + n_cast + 4]
    cast_out = rest[N_MIXER_CONSTS + n_cast + 4:N_MIXER_CONSTS + 2 * n_cast + 4]
    pja_ref, pjb_ref, xs_ref = rest[N_MIXER_CONSTS + 2 * n_cast + 4:]
    step = pl.program_id(0)
    rows = nch * c

    for src_ref, dst_ref in zip(cast_in, cast_out):
        dst_ref[...] = src_ref[...].astype(dst_ref.dtype)

    def project(x_ref, dst_ref, n_ticks):
        h = _rms(x_ref[...], nw_ref[...]).astype(BF16)
        panels = [(lo, min(lo + PROJ_PANEL, IN_PAD)) for lo in range(0, IN_PAD, PROJ_PANEL)]
        n_panels = len(panels)
        calls = [0]

        def emit():
            lo, hi = panels.pop(0)
            dst_ref[:, lo:hi] = jnp.dot(h, w_ref[:, lo:hi], preferred_element_type=F32)

        def tick():
            calls[0] += 1
            due = min(n_panels, -(-calls[0] * n_panels // n_ticks))
            while n_panels - len(panels) < due:
                emit()

        def flush():
            assert n_ticks == 1 or calls[0] == n_ticks, (calls[0], n_ticks)
            while panels:
                emit()

        return tick, flush

    @pl.when(step == 0)
    def _():
        project(x0_ref, pja_ref, 1)[1]()

    @pl.when(lax.rem(step, steps_per_seq) == 0)
    def _():
        _mixer_init(sret0_ref, sgdn0_ref, cq0_ref, sret_ref, sgdn_ref, xs_ref)

    for half, (cur_ref, x_next_ref, nxt_ref) in enumerate(((pja_ref, xa_ref, pjb_ref), (pjb_ref, xb_ref, pja_ref))):
        r0 = half * rows
        tick, flush = project(x_next_ref, nxt_ref, _mixer_block_ticks(1, c, nch))

        def put_mix(b, lo, value, r0=r0):
            mix_ref[r0:r0 + rows, lo:lo + D_HEAD] = value.astype(mix_ref.dtype)

        _mixer_block(lambda b, lo, hi, cur_ref=cur_ref: cur_ref[:, lo:hi], put_mix,
                     cos_ref[r0:r0 + rows, :], sin_ref[r0:r0 + rows, :], const_refs,
                     sret_ref, sgdn_ref, cq_ref, xs_ref, bb=1, c=c, nch=nch, n_valid=c, tick=tick)
        flush()


def _retention_decay_tables(c, n_valid):
    f32 = np.float32
    lg = np.log1p(-np.power(f32(2.0), f32(-5.0) - np.arange(N_HEADS, dtype=f32)))[:, None].astype(f32)
    idx = np.arange(c, dtype=f32)
    diff = idx[:, None] - idx[None, :]
    dint = np.where(diff[None] >= 0, np.exp(lg[:, :, None] * np.maximum(diff[None], 0)), f32(0.0)).astype(f32)
    qdec = np.exp(lg * (idx + f32(1.0))).astype(f32)
    kdec = np.where(idx[None, :] < n_valid, np.exp(lg * np.minimum(f32(n_valid) - f32(1.0) - idx, c)), f32(0.0))
    cdec = np.exp(lg * f32(n_valid)).astype(f32)
    bc = lambda t: np.ascontiguousarray(np.broadcast_to(t.astype(f32)[:, :, None], t.shape + (LANES,)))
    return dint, bc(qdec), bc(kdec), bc(cdec)


def _mixer_const_operands(c, n_valid, consts):
    cw, alog, dtb, nret, ngdn = consts
    dint, qdec, kdec, cdec = _retention_decay_tables(c, n_valid)
    idx = np.arange(c)
    tril = (idx[:, None] >= idx[None, :]).astype(np.float32)
    arrays = (jnp.asarray(tril, BF16), jnp.asarray(dint), jnp.asarray(qdec), jnp.asarray(kdec), jnp.asarray(cdec),
              cw, alog, dtb, nret, ngdn)
    assert len(arrays) == N_MIXER_CONSTS
    return arrays, [a.shape for a in arrays]


def _mixer(proj, row0, nb, length, sret0, sgdn0, cq0, rope, rope_row0, consts, *, bb, c, nch, n_valid,
           shared_init, stored=None, batch_minor=False, ring=0):
    rows = nch * c
    assert nb % bb == 0 and length % rows == 0 and rope_row0 % rows == 0
    assert not shared_init or bb == 1
    assert n_valid == c or nch == 1
    assert bb == 1 or length == rows
    const_arrays, const_shapes = _mixer_const_operands(c, n_valid, consts)
    nsteps = length // rows
    stored = rows if stored is None else stored
    assert stored == rows or (nsteps == 1 and stored >= n_valid and (bb * stored) % SUBLANES == 0)
    assert row0 % (bb * stored) == 0
    blk0 = row0 // (bb * stored)
    init_idx = (lambda b, i: (0, 0, 0, 0)) if shared_init else (lambda b, i: (b, 0, 0, 0))
    init_idx3 = (lambda b, i: (0, 0, 0)) if shared_init else (lambda b, i: (b, 0, 0))
    whole = lambda shape: pl.BlockSpec(shape, lambda b, i: (0,) * len(shape))
    rope_spec = pl.BlockSpec((rows, LANES), lambda b, i: (rope_row0 // rows + i, 0))
    state_shape = (bb, N_HEADS, D_HEAD, D_HEAD)
    tail = CONV_GDN - 1
    assert not (batch_minor and shared_init)
    cq_shape, cq_block = (nb, tail, GDN_QKV), (bb, tail, GDN_QKV)
    cq_idx = lambda b, i: (b, 0, 0)
    if batch_minor:
        cq_shape, cq_block = (tail, nb, GDN_QKV), (tail, bb, GDN_QKV)
        cq_idx = init_idx3 = lambda b, i: (0, b, 0)
    kern = functools.partial(_mixer_kernel, bb=bb, c=c, nch=nch, n_valid=n_valid, stored=stored,
                             batch_minor=batch_minor, ring=ring)
    state_spec = pl.BlockSpec(state_shape, init_idx)
    ring_scratch = []
    if ring:
        assert nsteps == 1 and not shared_init and nb // bb >= ring - 1
        state_spec = pl.BlockSpec(memory_space=pl.ANY)
        ring_scratch = [pltpu.VMEM((ring,) + state_shape, F32), pltpu.VMEM((ring,) + state_shape, F32),
                        pltpu.SemaphoreType.DMA((2, ring))]
    return pl.pallas_call(
        kern,
        out_shape=(
            jax.ShapeDtypeStruct((nb, length, D_MODEL), BF16),
            jax.ShapeDtypeStruct((nb, N_HEADS, D_HEAD, D_HEAD), F32),
            jax.ShapeDtypeStruct((nb, N_HEADS, D_HEAD, D_HEAD), F32),
            jax.ShapeDtypeStruct(cq_shape, F32),
        ),
        grid=(nb // bb, nsteps),
        in_specs=[
            pl.BlockSpec((bb * stored, IN_PAD), lambda b, i: (blk0 + b * nsteps + i, 0)),
            state_spec,
            state_spec,
            pl.BlockSpec(cq_block, init_idx3),
            rope_spec,
            rope_spec,
        ] + [whole(shape) for shape in const_shapes],
        out_specs=(
            pl.BlockSpec((bb, rows, D_MODEL), lambda b, i: (b, i, 0)),
            pl.BlockSpec(state_shape, lambda b, i: (b, 0, 0, 0)),
            pl.BlockSpec(state_shape, lambda b, i: (b, 0, 0, 0)),
            pl.BlockSpec(cq_block, cq_idx),
        ),
        scratch_shapes=[pltpu.VMEM((bb, XS_TOP + rows, GDN_QKV), F32)] + ring_scratch,
        compiler_params=pltpu.CompilerParams(dimension_semantics=("arbitrary", "arbitrary"),
                                             vmem_limit_bytes=VMEM_LIMIT),
        name="mixer",
    )(proj, sret0, sgdn0, cq0, rope[0], rope[1], *const_arrays)


def _row_slab(nrows, nsteps):
    for hold in (1, 2, 4, 8):
        slabs = nsteps // hold
        if nsteps % hold == 0 and nrows % slabs == 0 and (nrows // slabs) % (2 * SUBLANES) == 0:
            return nrows // slabs, hold
    raise ValueError((nrows, nsteps))


def _proj_mixer(x, norm_w, w_bf, nseq, sret0, sgdn0, cq0, rope, consts, to_bf16, *, c, nch):
    rows = nch * c
    total = x.shape[0]
    length = total // nseq
    assert total % nseq == 0 and length % (2 * rows) == 0
    nblk = total // rows
    steps_per_seq = length // (2 * rows)
    const_arrays, const_shapes = _mixer_const_operands(c, c, consts)
    whole = lambda shape, **kw: pl.BlockSpec(shape, lambda s: (0,) * len(shape), **kw)
    rope_spec = pl.BlockSpec((2 * rows, LANES), lambda s: (lax.rem(s, steps_per_seq), 0))
    state_shape = (1, N_HEADS, D_HEAD, D_HEAD)
    tail = CONV_GDN - 1
    nsteps = nblk // 2
    slabs = [_row_slab(w.shape[0], nsteps) for w in to_bf16]
    cast_specs = [pl.BlockSpec((r, w.shape[1]), lambda s, hold=hold: (s // hold, 0))
                  for w, (r, hold) in zip(to_bf16, slabs)]
    kern = functools.partial(_proj_mixer_kernel, c=c, nch=nch, steps_per_seq=steps_per_seq, n_cast=len(to_bf16))
    return pl.pallas_call(
        kern,
        out_shape=(
            jax.ShapeDtypeStruct((total, D_MODEL), BF16),
            jax.ShapeDtypeStruct((nseq, N_HEADS, D_HEAD, D_HEAD), F32),
            jax.ShapeDtypeStruct((nseq, N_HEADS, D_HEAD, D_HEAD), F32),
            jax.ShapeDtypeStruct((nseq, tail, GDN_QKV), F32),
        ) + tuple(jax.ShapeDtypeStruct(w.shape, BF16) for w in to_bf16),
        grid=(nsteps,),
        in_specs=[
            pl.BlockSpec((rows, D_MODEL), lambda s: (0, 0), pipeline_mode=pl.Buffered(1)),
            pl.BlockSpec((rows, D_MODEL), lambda s: (2 * s + 1, 0)),
            pl.BlockSpec((rows, D_MODEL), lambda s: (jnp.minimum(2 * s + 2, nblk - 1), 0)),
            whole((1, D_MODEL)),
            whole((D_MODEL, IN_PAD), pipeline_mode=pl.Buffered(1)),
            whole(state_shape),
            whole(state_shape),
            whole((1, tail, GDN_QKV)),
            rope_spec,
            rope_spec,
        ] + [whole(shape) for shape in const_shapes] + cast_specs,
        out_specs=(
            pl.BlockSpec((2 * rows, D_MODEL), lambda s: (s, 0)),
            pl.BlockSpec(state_shape, lambda s: (s // steps_per_seq, 0, 0, 0)),
            pl.BlockSpec(state_shape, lambda s: (s // steps_per_seq, 0, 0, 0)),
            pl.BlockSpec((1, tail, GDN_QKV), lambda s: (s // steps_per_seq, 0, 0)),
        ) + tuple(cast_specs),
        scratch_shapes=[pltpu.VMEM((rows, IN_PAD), F32), pltpu.VMEM((rows, IN_PAD), F32),
                        pltpu.VMEM((1, XS_TOP + rows, GDN_QKV), F32)],
        compiler_params=pltpu.CompilerParams(dimension_semantics=("arbitrary",),
                                             vmem_limit_bytes=VMEM_LIMIT),
        name="proj_mixer",
    )(x, x, x, norm_w, w_bf, sret0, sgdn0, cq0, rope[0], rope[1], *const_arrays, *to_bf16)


FFN_COL_CHUNK = D_FF // 11


def _ffn_kernel(x_ref, mix_ref, *rest, tm, stride, prefix):
    if prefix:
        (xm_ref, mixm_ref, wout_ref, nffn_ref, wup_ref, cw_ref, wdn_ref, nfin_ref,
         y_ref, tail_ref, full_ref, lead_ref) = rest
    else:
        tail0_ref, wout_ref, nffn_ref, wup_ref, cw_ref, wdn_ref, nfin_ref, y_ref, tail_ref, full_ref = rest
    t = pl.program_id(1)
    carry = (CONV_FFN - 1) * stride
    base = _round_up(carry, SUBLANES)

    def up_project(x, mix):
        x1 = x + jnp.dot(mix, wout_ref[...], preferred_element_type=F32)
        h = _rms(x1, nffn_ref[...]).astype(BF16)
        return x1, jnp.dot(h, wup_ref[...], preferred_element_type=F32)

    if prefix:
        @pl.when((pl.program_id(0) == 0) & (t == 0))
        def _():
            um = up_project(xm_ref[...], mixm_ref[...])[1]
            lead_ref[...] = um[um.shape[0] - carry:, :]

    @pl.when(t == 0)
    def _():
        full_ref[base - carry:base, :] = lead_ref[...] if prefix else tail0_ref[0]

    x1, up = up_project(x_ref[...], mix_ref[...])
    full_ref[base:base + tm, :] = up

    def conv_cols(lo):
        acc = full_ref[base - carry:base - carry + tm, lo:lo + FFN_COL_CHUNK] * cw_ref[0:1, lo:lo + FFN_COL_CHUNK]
        for i in range(1, CONV_FFN):
            r0 = base - carry + i * stride
            acc = acc + full_ref[r0:r0 + tm, lo:lo + FFN_COL_CHUNK] * cw_ref[i:i + 1, lo:lo + FFN_COL_CHUNK]
        return acc

    x2 = x1
    for j in range(D_FF // FFN_COL_CHUNK):
        lo = j * FFN_COL_CHUNK
        act = (_silu(conv_cols(lo)) * conv_cols(D_FF + lo)).astype(BF16)
        x2 = x2 + jnp.dot(act, wdn_ref[lo:lo + FFN_COL_CHUNK, :], preferred_element_type=F32)
    y_ref[...] = _rms(x2, nfin_ref[...])

    new_tail = full_ref[base + tm - carry:base + tm, :]
    full_ref[base - carry:base, :] = new_tail
    tail_ref[0] = new_tail


def _ffn(x, mix, lead, weights, *, nseq, tm, stride):
    wout, nffn, wup, cw, wdn, nfin = weights
    rows = x.shape[0]
    assert rows % (nseq * tm) == 0
    nt = rows // (nseq * tm)
    carry = (CONV_FFN - 1) * stride
    base = _round_up(carry, SUBLANES)
    assert tm >= carry
    resident = lambda shape: pl.BlockSpec(shape, lambda b, t: (0, 0), pipeline_mode=pl.Buffered(1))
    small = lambda shape: pl.BlockSpec(shape, lambda b, t: (0, 0))
    prefix = isinstance(lead, tuple)
    if prefix:
        assert stride == 1 and all(a.shape[0] >= carry for a in lead)
        lead_specs = [small(a.shape) for a in lead]
        lead_scratch = [pltpu.VMEM((carry, 2 * D_FF), F32)]
    else:
        lead = (lead,)
        lead_specs = [pl.BlockSpec((1, carry, 2 * D_FF), lambda b, t: (b, 0, 0))]
        lead_scratch = []
    kern = functools.partial(_ffn_kernel, tm=tm, stride=stride, prefix=prefix)
    return pl.pallas_call(
        kern,
        out_shape=(
            jax.ShapeDtypeStruct((rows, D_MODEL), F32),
            jax.ShapeDtypeStruct((nseq, carry, 2 * D_FF), F32),
        ),
        grid=(nseq, nt),
        in_specs=[
            pl.BlockSpec((tm, D_MODEL), lambda b, t: (b * nt + t, 0)),
            pl.BlockSpec((tm, D_MODEL), lambda b, t: (b * nt + t, 0)),
        ] + lead_specs + [
            resident((D_MODEL, D_MODEL)),
            small((1, D_MODEL)),
            resident((D_MODEL, 2 * D_FF)),
            small((CONV_FFN, 2 * D_FF)),
            resident((D_FF, D_MODEL)),
            small((1, D_MODEL)),
        ],
        out_specs=(
            pl.BlockSpec((tm, D_MODEL), lambda b, t: (b * nt + t, 0)),
            pl.BlockSpec((1, carry, 2 * D_FF), lambda b, t: (b, 0, 0)),
        ),
        scratch_shapes=[pltpu.VMEM((base + tm, 2 * D_FF), F32)] + lead_scratch,
        compiler_params=pltpu.CompilerParams(dimension_semantics=("arbitrary", "arbitrary"),
                                             vmem_limit_bytes=VMEM_LIMIT),
        name="out_ffn",
    )(x, mix, *lead, wout, nffn, wup, cw, wdn, nfin)


def kernel(x_prompt, x_sample, state_ret, state_gdn, state_conv_qkv, state_ffn_conv, meta_tokens, norm_mix,
           w_in, conv_gdn, gdn_a_log, gdn_dt_bias, norm_ret, norm_gdn, w_out, norm_ffn, w_up, conv_ffn,
           w_down, norm_final):
    depth = w_in.shape[0]
    assert depth == 1
    nbp, seq, _ = x_prompt.shape
    nbs, dec_seq, _ = x_sample.shape
    assert dec_seq <= SAMPLE_PAD and nbs % SAMPLE_GROUP == 0
    assert seq % (2 * PROMPT_CHUNK * PROMPT_CHUNKS_PER_STEP) == 0
    layer = 0

    row = lambda v: v.reshape(1, -1).astype(F32)
    pad_ba = lambda v: jnp.pad(v.astype(F32), (N_HEADS, LANES - 2 * N_HEADS)).reshape(1, LANES)
    mixer_consts = (conv_gdn[layer], pad_ba(gdn_a_log[layer]), pad_ba(gdn_dt_bias[layer]),
                    norm_ret[layer].reshape(N_HEADS, D_HEAD), row(norm_gdn[layer]))
    nmix = row(norm_mix[layer])

    assert seq % N_META == 0 and (seq + N_META) % SAMPLE_PAD == 0
    rope_meta_row0, rope_sample_row0 = seq, seq + N_META
    rope_segments = [(0, N_META), (rope_meta_row0, 0), (rope_sample_row0, PAST_LEN)]

    small_rows = (x_sample.reshape(nbs * dec_seq, D_MODEL), meta_tokens.astype(F32))
    proj_small, w_in_bf, *rope = _proj(small_rows, nmix, w_in[layer], seq + N_META + SAMPLE_PAD, rope_segments)
    meta_row0 = nbs * dec_seq

    zero_state = jnp.zeros((1, N_HEADS, D_HEAD, D_HEAD), F32)
    zero_cq = jnp.zeros((1, CONV_GDN - 1, GDN_QKV), F32)
    mix_m, sret_m, sgdn_m, cq_m = _mixer(proj_small, meta_row0, 1, N_META, zero_state, zero_state, zero_cq, rope,
                                         rope_meta_row0, mixer_consts, bb=1, c=N_META, nch=1, n_valid=N_META,
                                         shared_init=True)

    xp = x_prompt.reshape(nbp * seq, D_MODEL)
    mix_p, sret_p, sgdn_p, cq_p, w_out_bf, w_up_bf, w_down_bf = _proj_mixer(
        xp, nmix, w_in_bf, nbp, sret_m, sgdn_m, cq_m, rope, mixer_consts,
        (w_out[layer], w_up[layer], w_down[layer]), c=PROMPT_CHUNK, nch=PROMPT_CHUNKS_PER_STEP)
    ffn_weights = (w_out_bf, row(norm_ffn[layer]), w_up_bf, conv_ffn[layer], w_down_bf, row(norm_final))

    meta_lead = (meta_tokens.astype(F32), mix_m.reshape(N_META, D_MODEL))
    y_p, cf_p = _ffn(xp, mix_p, meta_lead, ffn_weights, nseq=nbp, tm=512, stride=1)
    y_prompt = y_p.reshape(nbp, seq, D_MODEL)

    mix_s, sret_s, sgdn_s, cq_s = _mixer(proj_small, 0, nbs, SAMPLE_PAD, state_ret[layer], state_gdn[layer],
                                         jnp.swapaxes(state_conv_qkv[layer], 0, 1), rope, rope_sample_row0,
                                         mixer_consts, bb=SAMPLE_SEQS_PER_STEP, c=SAMPLE_PAD, nch=1,
                                         n_valid=dec_seq, shared_init=False, stored=dec_seq, batch_minor=True,
                                         ring=SAMPLE_STATE_SLOTS)
    cq_s = jnp.swapaxes(cq_s, 0, 1)
    ng = nbs // SAMPLE_GROUP
    to_tmajor = lambda a: a.reshape(ng, SAMPLE_GROUP, a.shape[1], a.shape[2]).transpose(0, 2, 1, 3)
    xs_t = to_tmajor(x_sample).reshape(nbs * dec_seq, D_MODEL)
    mix_t = to_tmajor(mix_s[:, :dec_seq]).reshape(nbs * dec_seq, D_MODEL)
    cf0_t = to_tmajor(state_ffn_conv[layer]).reshape(ng, (CONV_FFN - 1) * SAMPLE_GROUP, 2 * D_FF)
    y_s_t, cf_s_t = _ffn(xs_t, mix_t, cf0_t, ffn_weights, nseq=ng, tm=dec_seq * SAMPLE_GROUP,
                         stride=SAMPLE_GROUP)
    y_sample = y_s_t.reshape(ng, dec_seq, SAMPLE_GROUP, D_MODEL).transpose(0, 2, 1, 3).reshape(nbs, dec_seq, D_MODEL)
    cf_s = cf_s_t.reshape(ng, CONV_FFN - 1, SAMPLE_GROUP, 2 * D_FF).transpose(0, 2, 1, 3).reshape(
        nbs, CONV_FFN - 1, 2 * D_FF)

    return (y_prompt, y_sample, sret_p[None], sgdn_p[None], cq_p[None], cf_p[None],
            sret_s[None], sgdn_s[None], cq_s[None], cf_s[None])
```
